```python
import math
import jax, jax.numpy as jnp
from jax import lax
import numpy as np

D_MODEL = 1024
BATCH = 4
SEQ = 4096
DEPTH = 1

EPS = 1e-6
D_CONV = D_MODEL // 2
CONV_K = 3
HEAD_DIM = 64
N_HEADS = (D_MODEL // 2) // HEAD_DIM
N_KV_HEADS = 2
GQA_GROUP = N_HEADS // N_KV_HEADS
D_ATTN = N_HEADS * HEAD_DIM
D_KV = N_KV_HEADS * HEAD_DIM
WINDOW = 128
BLK = 128
D_MIX = D_CONV + D_ATTN
D_IN_PROJ = 3 * D_CONV + D_ATTN + 2 * D_KV
N_EXPERTS = 64
TOP_K = 8
N_GROUPS = 8
TOPK_GROUPS = 4
D_EXPERT = D_MODEL // 4
ROUTED_SCALE = 2.5
MOE_BLK = 128

kernel_name = "hymba_conv_swa_sink_alibi_moe_block"


def _alibi_slopes(n_heads):
    return np.array([2.0 ** (-8.0 * (h + 1) / n_heads) for h in range(n_heads)], dtype=np.float32)


def rms_norm(x, w):
    xf = x.astype(jnp.float32)
    y = xf * lax.rsqrt(jnp.mean(xf * xf, axis=-1, keepdims=True) + EPS)
    return (y * w.astype(jnp.float32)).astype(x.dtype)


def short_conv_mixer(b_gate, c_gate, h, conv_w):
    S = h.shape[1]
    u = c_gate * h
    u_pad = jnp.pad(u, ((0, 0), (CONV_K - 1, 0), (0, 0)))
    v = conv_w[0] * u_pad[:, 0:S]
    for k in range(1, CONV_K):
        v = v + conv_w[k] * u_pad[:, k:k + S]
    return b_gate * v


def sliding_window_attention(q, k, v, sinks):
    Bsz, S = q.shape[0], q.shape[1]
    nb = S // BLK
    qb = q.reshape(Bsz, nb, BLK, N_KV_HEADS, GQA_GROUP, HEAD_DIM)
    kb = k.reshape(Bsz, nb, BLK, N_KV_HEADS, HEAD_DIM)
    vb = v.reshape(Bsz, nb, BLK, N_KV_HEADS, HEAD_DIM)
    pad = ((0, 0), (1, 0), (0, 0), (0, 0), (0, 0))
    kk = jnp.concatenate([jnp.pad(kb, pad)[:, :-1], kb], axis=2)
    vv = jnp.concatenate([jnp.pad(vb, pad)[:, :-1], vb], axis=2)
    s = jnp.einsum('bnqhgd,bnkhd->bhgnqk', qb, kk,
                   preferred_element_type=jnp.float32) * (HEAD_DIM ** -0.5)
    qi = jnp.arange(BLK)[:, None]
    kj = jnp.arange(2 * BLK)[None, :]
    dist = qi - kj + BLK
    key_pos = jnp.arange(nb)[:, None, None] * BLK - BLK + kj[None]
    mask = (dist >= 0)[None] & (dist < WINDOW)[None] & (key_pos >= 0)
    slopes = jnp.asarray(_alibi_slopes(N_HEADS)).reshape(N_KV_HEADS, GQA_GROUP)
    bias = -slopes[:, :, None, None, None] * dist.astype(jnp.float32)[None, None, None]
    s = jnp.where(mask, s + bias, -jnp.inf)
    sink = jnp.broadcast_to(
        sinks.astype(jnp.float32).reshape(1, N_KV_HEADS, GQA_GROUP, 1, 1, 1),
        s.shape[:-1] + (1,))
    p = jax.nn.softmax(jnp.concatenate([s, sink], axis=-1), axis=-1)[..., :-1]
    o = jnp.einsum('bhgnqk,bnkhd->bnqhgd', p.astype(vv.dtype), vv)
    return o.reshape(Bsz, S, N_HEADS * HEAD_DIM)


def swiglu(x, w_gate, w_up, w_down):
    return (jax.nn.silu(x @ w_gate) * (x @ w_up)) @ w_down


def moe_ffn(xf, w_router, router_bias, w_eg, w_eu, w_ed, w_sg, w_su, w_sd):
    N, D = xf.shape
    scores = jax.nn.sigmoid(xf.astype(jnp.float32) @ w_router.astype(jnp.float32))
    biased = scores + router_bias.astype(jnp.float32)
    grp = biased.reshape(N, N_GROUPS, N_EXPERTS // N_GROUPS)
    grp_score = lax.top_k(grp, 2)[0].sum(-1)
    _, top_groups = lax.top_k(grp_score, TOPK_GROUPS)
    group_mask = jax.nn.one_hot(top_groups, N_GROUPS, dtype=jnp.float32).sum(1) > 0
    expert_mask = jnp.repeat(group_mask, N_EXPERTS // N_GROUPS, axis=1)
    _, top_idx = lax.top_k(jnp.where(expert_mask, biased, -jnp.inf), TOP_K)
    top_w = jnp.take_along_axis(scores, top_idx, axis=1)
    top_w = top_w / jnp.sum(top_w, axis=-1, keepdims=True) * ROUTED_SCALE

    A = N * TOP_K
    flat_e = top_idx.reshape(-1)
    flat_tok = jnp.repeat(jnp.arange(N, dtype=jnp.int32), TOP_K)
    flat_w = top_w.reshape(-1)
    order = jnp.argsort(flat_e)
    sorted_e = flat_e[order]
    counts = jnp.bincount(flat_e, length=N_EXPERTS)
    starts = jnp.cumsum(counts) - counts
    padded = (counts + MOE_BLK - 1) // MOE_BLK * MOE_BLK
    pad_end = jnp.cumsum(padded)
    pad_start = pad_end - padded
    dest = pad_start[sorted_e] + (jnp.arange(A) - starts[sorted_e])
    n_blocks = -(-A // MOE_BLK) + N_EXPERTS
    P = n_blocks * MOE_BLK
    buf_tok = jnp.full((P,), N, dtype=jnp.int32).at[dest].set(flat_tok[order])
    buf_w = jnp.zeros((P,), jnp.float32).at[dest].set(flat_w[order])
    block_e = jnp.clip(jnp.searchsorted(pad_end, jnp.arange(n_blocks) * MOE_BLK, side='right'),
                       0, N_EXPERTS - 1)
    x_pad = jnp.concatenate([xf, jnp.zeros((1, D), xf.dtype)], axis=0)

    def expert_block(args):
        tok, e = args
        xb = x_pad[tok]
        return swiglu(xb, w_eg[e], w_eu[e], w_ed[e])

    y = lax.map(expert_block, (buf_tok.reshape(n_blocks, MOE_BLK), block_e)).reshape(P, D)
    y = y * buf_w[:, None].astype(y.dtype)
    routed = jnp.zeros((N + 1, D), y.dtype).at[buf_tok].add(y)[:N]
    return routed + swiglu(xf, w_sg, w_su, w_sd)


def setup_inputs(seed: int = 0) -> dict:
    key = jax.random.key(seed)
    ks = jax.random.split(key, 20)
    f32 = jnp.float32
    nrm = lambda k, shape, scale: jax.random.normal(k, shape, f32) * scale
    gain = lambda k, shape: 1.0 + 0.02 * jax.random.normal(k, shape, f32)
    L = DEPTH
    return {
        "x": jax.random.normal(ks[0], (BATCH, SEQ, D_MODEL), f32),
        "mix_norm_w": gain(ks[1], (L, D_MODEL)),
        "w_in": nrm(ks[2], (L, D_MODEL, D_IN_PROJ), D_MODEL ** -0.5),
        "conv_w": nrm(ks[3], (L, CONV_K, D_CONV), CONV_K ** -0.5),
        "q_norm_w": gain(ks[4], (L, HEAD_DIM)),
        "k_norm_w": gain(ks[5], (L, HEAD_DIM)),
        "sinks": nrm(ks[6], (L, N_HEADS), 0.5),
        "conv_out_norm_w": gain(ks[7], (L, D_CONV)),
        "attn_out_norm_w": gain(ks[8], (L, D_ATTN)),
        "w_out": nrm(ks[9], (L, D_MIX, D_MODEL), D_MIX ** -0.5),
        "ffn_norm_w": gain(ks[10], (L, D_MODEL)),
        "w_router": nrm(ks[11], (L, D_MODEL, N_EXPERTS), D_MODEL ** -0.5),
        "router_bias": nrm(ks[12], (L, N_EXPERTS), 0.01),
        "w_exp_gate": nrm(ks[13], (L, N_EXPERTS, D_MODEL, D_EXPERT), D_MODEL ** -0.5),
        "w_exp_up": nrm(ks[14], (L, N_EXPERTS, D_MODEL, D_EXPERT), D_MODEL ** -0.5),
        "w_exp_down": nrm(ks[15], (L, N_EXPERTS, D_EXPERT, D_MODEL), D_EXPERT ** -0.5),
        "w_sh_gate": nrm(ks[16], (L, D_MODEL, D_EXPERT), D_MODEL ** -0.5),
        "w_sh_up": nrm(ks[17], (L, D_MODEL, D_EXPERT), D_MODEL ** -0.5),
        "w_sh_down": nrm(ks[18], (L, D_EXPERT, D_MODEL), D_EXPERT ** -0.5),
    }


def reference(x, mix_norm_w, w_in, conv_w, q_norm_w, k_norm_w, sinks, conv_out_norm_w,
              attn_out_norm_w, w_out, ffn_norm_w, w_router, router_bias, w_exp_gate,
              w_exp_up, w_exp_down, w_sh_gate, w_sh_up, w_sh_down):
    Bsz, S, D = x.shape
    for l in range(DEPTH):
        u = rms_norm(x, mix_norm_w[l])
        proj = u @ w_in[l]
        b_gate, c_gate, h, q, k, v = jnp.split(
            proj, np.cumsum([D_CONV, D_CONV, D_CONV, D_ATTN, D_KV]).tolist(), axis=-1)
        y_conv = short_conv_mixer(b_gate, c_gate, h, conv_w[l])
        q = rms_norm(q.reshape(Bsz, S, N_HEADS, HEAD_DIM), q_norm_w[l])
        k = rms_norm(k.reshape(Bsz, S, N_KV_HEADS, HEAD_DIM), k_norm_w[l])
        v = v.reshape(Bsz, S, N_KV_HEADS, HEAD_DIM)
        y_attn = sliding_window_attention(q, k, v, sinks[l])
        y_mix = jnp.concatenate([rms_norm(y_conv, conv_out_norm_w[l]),
                                 rms_norm(y_attn, attn_out_norm_w[l])], axis=-1)
        x = x + y_mix @ w_out[l]
        xn = rms_norm(x, ffn_norm_w[l]).reshape(Bsz * S, D)
        x = x + moe_ffn(xn, w_router[l], router_bias[l], w_exp_gate[l], w_exp_up[l],
                        w_exp_down[l], w_sh_gate[l], w_sh_up[l], w_sh_down[l]).reshape(Bsz, S, D)
    return x
```

```python
import functools

import numpy as np
import jax
import jax.numpy as jnp
from jax import lax
from jax.experimental import pallas as pl
from jax.experimental.pallas import tpu as pltpu

D_MODEL = 1024
EPS = 1e-6
D_CONV = 512
CONV_K = 3
HEAD_DIM = 64
N_HEADS = 8
N_KV_HEADS = 2
GQA_GROUP = 4
D_ATTN = 512
D_KV = 128
WINDOW = 128
BLK = 128
D_IN_PROJ = 2304
N_EXPERTS = 64
TOP_K = 8
N_GROUPS = 8
GROUP_SIZE = 8
TOPK_GROUPS = 4
D_EXPERT = 256
ROUTED_SCALE = 2.5

LANES = 128
MIX_TILE = 512
EXPERT_BLOCK = 256
ROW_TILE = 256
NEG_BIG = -1e30
VMEM_LIMIT = 58 * 1024 * 1024


def _alibi_slopes():
    return np.array([2.0 ** (-8.0 * (h + 1) / N_HEADS) for h in range(N_HEADS)], dtype=np.float32)


def _attn_bias_table():
    qi = np.arange(BLK)[:, None]
    kj = np.arange(2 * BLK)[None, :]
    dist = qi - kj + BLK
    inwin = (dist >= 0) & (dist < WINDOW)
    slopes = _alibi_slopes()
    out = np.zeros((N_KV_HEADS, GQA_GROUP * BLK, 2 * BLK), np.float32)
    for g in range(N_KV_HEADS):
        for i in range(GQA_GROUP):
            h = g * GQA_GROUP + i
            out[g, i * BLK:(i + 1) * BLK] = np.where(inwin, -slopes[h] * dist.astype(np.float32), NEG_BIG)
    return out


def _rms(x, w):
    ms = jnp.mean(x * x, axis=-1, keepdims=True)
    return x * lax.rsqrt(ms + EPS) * w


def _head_rms(x, w_tiled, lane_lo):
    outs = []
    for c in range(x.shape[1] // LANES):
        xc = x[:, c * LANES:(c + 1) * LANES]
        sq = xc * xc
        s_lo = jnp.sum(jnp.where(lane_lo, sq, 0.0), axis=-1, keepdims=True)
        s_hi = jnp.sum(jnp.where(lane_lo, 0.0, sq), axis=-1, keepdims=True)
        inv = jnp.where(lane_lo, lax.rsqrt(s_lo / HEAD_DIM + EPS), lax.rsqrt(s_hi / HEAD_DIM + EPS))
        outs.append(xc * inv)
    return jnp.concatenate(outs, axis=1) * w_tiled


def _dot(a, b):
    return jnp.dot(a, b, preferred_element_type=jnp.float32)


def _dot_nt(a, b):
    return lax.dot_general(a, b, (((1,), (1,)), ((), ())), preferred_element_type=jnp.float32)


def _mixer_kernel(x_ref, mixw_ref, win_ref, convw_ref, qw_ref, kw_ref, sink_ref, bias_ref,
                  cnw_ref, anw_ref, wout_ref, fnw_ref, wsg_ref, wsu_ref, wsd_ref,
                  wrh_ref, wrl_ref, rbias_ref, tri_ref,
                  x2_ref, xn_ref, idx_ref, wts_ref, pos_ref, cnt_ref,
                  kc_ref, vc_ref, cc_ref, run_ref):
    b = pl.program_id(0)
    j = pl.program_id(1)
    T = MIX_TILE
    bf16 = jnp.bfloat16
    f32 = jnp.float32

    @pl.when(j == 0)
    def _():
        kc_ref[...] = jnp.zeros_like(kc_ref)
        vc_ref[...] = jnp.zeros_like(vc_ref)
        cc_ref[...] = jnp.zeros_like(cc_ref)

    @pl.when(jnp.logical_and(b == 0, j == 0))
    def _():
        run_ref[...] = jnp.zeros_like(run_ref)

    x = x_ref[0]
    u = _rms(x, mixw_ref[...]).astype(bf16)
    proj = _dot(u, win_ref[...])
    b_gate = proj[:, 0:512]
    c_gate = proj[:, 512:1024]
    hh = proj[:, 1024:1536]
    q = proj[:, 1536:2048]
    k = proj[:, 2048:2176]
    v = proj[:, 2176:2304]

    ch = c_gate * hh
    prev = cc_ref[...]
    p6 = prev[6:7, :]
    p7 = prev[7:8, :]
    row = lax.broadcasted_iota(jnp.int32, (T, D_CONV), 0)
    ch_m1 = jnp.where(row == 0, p7, pltpu.roll(ch, 1, axis=0))
    ch_m2 = jnp.where(row == 0, p6, jnp.where(row == 1, p7, pltpu.roll(ch, 2, axis=0)))
    cw = convw_ref[...]
    y_conv = b_gate * (cw[0:1, :] * ch_m2 + cw[1:2, :] * ch_m1 + cw[2:3, :] * ch)
    cc_ref[...] = ch[T - 8:T, :]

    lane = lax.broadcasted_iota(jnp.int32, (1, LANES), 1)
    lane_lo = lane < HEAD_DIM
    qn = _head_rms(q, qw_ref[...], lane_lo) * (HEAD_DIM ** -0.5)
    kn = _head_rms(k, kw_ref[...], lane_lo)
    kfull = jnp.concatenate([kc_ref[...], kn], axis=0)
    vfull = jnp.concatenate([vc_ref[...], v], axis=0)
    kc_ref[...] = kn[T - BLK:T, :]
    vc_ref[...] = v[T - BLK:T, :]

    def _rep(a, g):
        r = pltpu.roll(a, HEAD_DIM, axis=1)
        two = jnp.where(lane_lo, a, r) if g == 0 else jnp.where(lane_lo, r, a)
        return jnp.concatenate([two, two], axis=1).astype(bf16)

    k_rep = [_rep(kfull, g) for g in range(N_KV_HEADS)]
    v_rep = [_rep(vfull, g) for g in range(N_KV_HEADS)]

    lane256 = lax.broadcasted_iota(jnp.int32, (1, 2 * LANES), 1)
    head_of_lane = lane256 // HEAD_DIM
    first_f = (j == 0).astype(f32)
    prev_key_mask = jnp.where(lane256 < BLK, first_f * NEG_BIG, 0.0)

    attn_rows = []
    for i in range(T // BLK):
        grp_out = []
        for g in range(N_KV_HEADS):
            qg = qn[i * BLK:(i + 1) * BLK, g * 256:(g + 1) * 256]
            qst = jnp.concatenate(
                [jnp.where(head_of_lane == hi, qg, 0.0) for hi in range(GQA_GROUP)], axis=0).astype(bf16)
            kk = k_rep[g][i * BLK:i * BLK + 2 * BLK, :]
            vv = v_rep[g][i * BLK:i * BLK + 2 * BLK, :]
            s = _dot_nt(qst, kk) + bias_ref[g]
            if i == 0:
                s = s + prev_key_mask
            sink = sink_ref[g]
            m = jnp.maximum(jnp.max(s, axis=-1, keepdims=True), sink)
            e = jnp.exp(s - m)
            denom = jnp.sum(e, axis=-1, keepdims=True) + jnp.exp(sink - m)
            p = (e / denom).astype(bf16)
            r = _dot(p, vv)
            o = jnp.where(head_of_lane == 0, r[0:BLK], 0.0)
            for hi in range(1, GQA_GROUP):
                o = jnp.where(head_of_lane == hi, r[hi * BLK:(hi + 1) * BLK], o)
            grp_out.append(o)
        attn_rows.append(jnp.concatenate(grp_out, axis=1))
    y_attn = jnp.concatenate(attn_rows, axis=0)

    y_mix = jnp.concatenate([_rms(y_conv, cnw_ref[...]), _rms(y_attn, anw_ref[...])], axis=1)
    x1 = x + _dot(y_mix.astype(bf16), wout_ref[...])

    xn = _rms(x1, fnw_ref[...])
    xn_ref[...] = xn
    xh = xn.astype(bf16)
    gs = _dot(xh, wsg_ref[...])
    us = _dot(xh, wsu_ref[...])
    hs = (gs * jax.nn.sigmoid(gs) * us).astype(bf16)
    x2_ref[0] = x1 + _dot(hs, wsd_ref[...])

    xl = (xn - xh.astype(f32)).astype(bf16)
    logits = _dot_nt(wrh_ref[...], xh) + _dot_nt(wrh_ref[...], xl) + _dot_nt(wrl_ref[...], xh)
    scores = jax.nn.sigmoid(logits)
    biased = scores + rbias_ref[...]

    sub8 = lax.broadcasted_iota(jnp.int32, (GROUP_SIZE, T), 0)
    gscore = []
    for g in range(N_GROUPS):
        blk = biased[g * GROUP_SIZE:(g + 1) * GROUP_SIZE, :]
        m1 = jnp.max(blk, axis=0, keepdims=True)
        first = jnp.min(jnp.where(blk == m1, sub8, GROUP_SIZE), axis=0, keepdims=True)
        m2 = jnp.max(jnp.where(sub8 == first, -jnp.inf, blk), axis=0, keepdims=True)
        gscore.append(m1 + m2)
    masked_blocks = []
    for g in range(N_GROUPS):
        rank = jnp.zeros((1, T), jnp.int32)
        for o_ in range(N_GROUPS):
            if o_ == g:
                continue
            if o_ < g:
                ahead = gscore[o_] >= gscore[g]
            else:
                ahead = gscore[o_] > gscore[g]
            rank = rank + ahead.astype(jnp.int32)
        keep = rank < TOPK_GROUPS
        blk = biased[g * GROUP_SIZE:(g + 1) * GROUP_SIZE, :]
        masked_blocks.append(jnp.where(keep, blk, -jnp.inf))
    cur = jnp.concatenate(masked_blocks, axis=0)

    eiota = lax.broadcasted_iota(jnp.int32, (N_EXPERTS, T), 0)
    row8 = lax.broadcasted_iota(jnp.int32, (TOP_K, T), 0)
    idx_out = jnp.zeros((TOP_K, T), jnp.int32)
    w_out = jnp.zeros((TOP_K, T), f32)
    sel_dense = jnp.zeros((N_EXPERTS, T), f32)
    onehots = []
    for kk_ in range(TOP_K):
        mx = jnp.max(cur, axis=0, keepdims=True)
        sel_idx = jnp.min(jnp.where(cur == mx, eiota, N_EXPERTS), axis=0, keepdims=True)
        onehot = eiota == sel_idx
        w_k = jnp.sum(jnp.where(onehot, scores, 0.0), axis=0, keepdims=True)
        cur = jnp.where(onehot, -jnp.inf, cur)
        sel_dense = jnp.where(onehot, 1.0, sel_dense)
        idx_out = jnp.where(row8 == kk_, sel_idx, idx_out)
        w_out = jnp.where(row8 == kk_, w_k, w_out)
        onehots.append(onehot)
    wsum = jnp.sum(w_out, axis=0, keepdims=True)
    wts_ref[...] = w_out / wsum * ROUTED_SCALE
    idx_ref[...] = idx_out

    cum = _dot(sel_dense.astype(bf16), tri_ref[...])
    run = run_ref[...]
    pos_dense = run[:, 0:1] + cum
    pos_out = jnp.zeros((TOP_K, T), f32)
    for kk_ in range(TOP_K):
        p_k = jnp.sum(jnp.where(onehots[kk_], pos_dense, 0.0), axis=0, keepdims=True)
        pos_out = jnp.where(row8 == kk_, p_k, pos_out)
    pos_ref[...] = pos_out.astype(jnp.int32)
    run_new = run + jnp.sum(sel_dense, axis=1, keepdims=True)
    run_ref[...] = run_new
    cnt_ref[...] = run_new.astype(jnp.int32)


def _mixer_call(x, mixw, win, convw, qw, kw, sink_col, bias_tab, cnw, anw, wout, fnw,
                wsg, wsu, wsd, wrh, wrl, rbias, tri):
    B, S, D = x.shape
    T = MIX_TILE
    nt = S // T
    N = B * S

    def full(a):
        nd = a.ndim
        return pl.BlockSpec(a.shape, lambda b, j, _nd=nd: (0,) * _nd)

    tok_spec = pl.BlockSpec((TOP_K, T), lambda b, j: (0, b * nt + j))
    in_arrays = [mixw, win, convw, qw, kw, sink_col, bias_tab, cnw, anw, wout, fnw,
                 wsg, wsu, wsd, wrh, wrl, rbias, tri]
    return pl.pallas_call(
        _mixer_kernel,
        grid=(B, nt),
        in_specs=[pl.BlockSpec((1, T, D), lambda b, j: (b, j, 0))] + [full(a) for a in in_arrays],
        out_specs=[
            pl.BlockSpec((1, T, D), lambda b, j: (b, j, 0)),
            pl.BlockSpec((T, D), lambda b, j: (b * nt + j, 0)),
            tok_spec, tok_spec, tok_spec,
            pl.BlockSpec((N_EXPERTS, LANES), lambda b, j: (0, 0)),
        ],
        out_shape=[
            jax.ShapeDtypeStruct((B, S, D), jnp.float32),
            jax.ShapeDtypeStruct((N, D), jnp.float32),
            jax.ShapeDtypeStruct((TOP_K, N), jnp.int32),
            jax.ShapeDtypeStruct((TOP_K, N), jnp.float32),
            jax.ShapeDtypeStruct((TOP_K, N), jnp.int32),
            jax.ShapeDtypeStruct((N_EXPERTS, LANES), jnp.int32),
        ],
        scratch_shapes=[
            pltpu.VMEM((BLK, D_KV), jnp.float32),
            pltpu.VMEM((BLK, D_KV), jnp.float32),
            pltpu.VMEM((8, D_CONV), jnp.float32),
            pltpu.VMEM((N_EXPERTS, LANES), jnp.float32),
        ],
        compiler_params=pltpu.CompilerParams(
            dimension_semantics=("arbitrary", "arbitrary"), vmem_limit_bytes=VMEM_LIMIT),
        name="mixer_router",
    )(x, *in_arrays)


def _row_copy(src, s, dst, d, sem):
    return pltpu.make_async_copy(src.at[pl.ds(s, 1)], dst.at[pl.ds(d, 1)], sem)


def _dispatch_kernel(pstart_ref, idx_ref, pos_ref, xn_ref, xs_in_ref, xs_ref, sem):
    del xs_in_ref
    n = TOP_K * ROW_TILE

    def body(i, c):
        d = pstart_ref[idx_ref[0, 0, i]] + pos_ref[0, 0, i]
        t = lax.rem(i, ROW_TILE)
        _row_copy(xn_ref, t, xs_ref, d, sem).start()
        return c

    lax.fori_loop(0, n, body, 0, unroll=8)
    for _ in range(TOP_K):
        pltpu.make_async_copy(xn_ref, xs_ref.at[pl.ds(0, ROW_TILE)], sem).wait()


def _dispatch_call(pstart, idx_t, pos_t, xn, xs_zero):
    N, D = xn.shape
    nt = N // ROW_TILE
    n = TOP_K * ROW_TILE
    smem_blk = pl.BlockSpec((1, 1, n), lambda i: (i, 0, 0), memory_space=pltpu.SMEM)
    return pl.pallas_call(
        _dispatch_kernel,
        grid=(nt,),
        in_specs=[
            pl.BlockSpec(memory_space=pltpu.SMEM),
            smem_blk, smem_blk,
            pl.BlockSpec((ROW_TILE, D), lambda i: (i, 0)),
            pl.BlockSpec(memory_space=pl.ANY),
        ],
        out_specs=pl.BlockSpec(memory_space=pl.ANY),
        out_shape=jax.ShapeDtypeStruct(xs_zero.shape, xs_zero.dtype),
        scratch_shapes=[pltpu.SemaphoreType.DMA],
        input_output_aliases={4: 0},
        compiler_params=pltpu.CompilerParams(dimension_semantics=("arbitrary",)),
        name="moe_dispatch",
    )(pstart, idx_t, pos_t, xn, xs_zero)


def _expert_kernel(be_ref, nv_ref, xs_ref, wg_ref, wu_ref, wd_ref, ys_ref):
    i = pl.program_id(0)
    bf16 = jnp.bfloat16

    @pl.when(i < nv_ref[0])
    def _():
        xb = xs_ref[...].astype(bf16)
        g = _dot(xb, wg_ref[0].astype(bf16))
        u = _dot(xb, wu_ref[0].astype(bf16))
        h = (g * jax.nn.sigmoid(g) * u).astype(bf16)
        ys_ref[...] = _dot(h, wd_ref[0].astype(bf16))

    @pl.when(i >= nv_ref[0])
    def _():
        ys_ref[...] = jnp.zeros_like(ys_ref)


def _expert_call(block_expert, nvalid, xs, w_eg, w_eu, w_ed):
    P, D = xs.shape
    BM = EXPERT_BLOCK
    nb = P // BM

    def row_map(i, be, nv):
        return (jnp.minimum(i, nv[0] - 1), 0)

    def w_map(i, be, nv):
        return (be[i], 0, 0)

    grid_spec = pltpu.PrefetchScalarGridSpec(
        num_scalar_prefetch=2,
        grid=(nb,),
        in_specs=[
            pl.BlockSpec((BM, D), row_map),
            pl.BlockSpec((1, D, D_EXPERT), w_map),
            pl.BlockSpec((1, D, D_EXPERT), w_map),
            pl.BlockSpec((1, D_EXPERT, D), w_map),
        ],
        out_specs=pl.BlockSpec((BM, D), lambda i, be, nv: (i, 0)),
    )
    return pl.pallas_call(
        _expert_kernel,
        grid_spec=grid_spec,
        out_shape=jax.ShapeDtypeStruct((P, D), jnp.float32),
        compiler_params=pltpu.CompilerParams(
            dimension_semantics=("arbitrary",), vmem_limit_bytes=VMEM_LIMIT),
        name="moe_experts",
    )(block_expert, nvalid, xs, w_eg, w_eu, w_ed)


def _combine_kernel(pstart_ref, idx_ref, pos_ref, wts_ref, x2_ref, ys_ref, out_ref, buf_ref, sem):
    n = TOP_K * ROW_TILE

    def body(i, c):
        d = pstart_ref[idx_ref[0, 0, i]] + pos_ref[0, 0, i]
        _row_copy(ys_ref, d, buf_ref, i, sem).start()
        return c

    lax.fori_loop(0, n, body, 0, unroll=8)
    for kk_ in range(TOP_K):
        pltpu.make_async_copy(ys_ref.at[pl.ds(0, ROW_TILE)],
                              buf_ref.at[pl.ds(kk_ * ROW_TILE, ROW_TILE)], sem).wait()
    w = wts_ref[...]
    acc = x2_ref[...]
    for kk_ in range(TOP_K):
        acc = acc + w[:, kk_:kk_ + 1] * buf_ref[kk_ * ROW_TILE:(kk_ + 1) * ROW_TILE, :]
    out_ref[...] = acc


def _combine_call(pstart, idx_t, pos_t, wts_rows, x2, ys):
    N, D = x2.shape
    nt = N // ROW_TILE
    n = TOP_K * ROW_TILE
    smem_blk = pl.BlockSpec((1, 1, n), lambda i: (i, 0, 0), memory_space=pltpu.SMEM)
    return pl.pallas_call(
        _combine_kernel,
        grid=(nt,),
        in_specs=[
            pl.BlockSpec(memory_space=pltpu.SMEM),
            smem_blk, smem_blk,
            pl.BlockSpec((ROW_TILE, TOP_K), lambda i: (i, 0)),
            pl.BlockSpec((ROW_TILE, D), lambda i: (i, 0)),
            pl.BlockSpec(memory_space=pl.ANY),
        ],
        out_specs=pl.BlockSpec((ROW_TILE, D), lambda i: (i, 0)),
        out_shape=jax.ShapeDtypeStruct((N, D), jnp.float32),
        scratch_shapes=[pltpu.VMEM((n, D), jnp.float32), pltpu.SemaphoreType.DMA],
        compiler_params=pltpu.CompilerParams(
            dimension_semantics=("arbitrary",), vmem_limit_bytes=VMEM_LIMIT),
        name="moe_combine",
    )(pstart, idx_t, pos_t, wts_rows, x2, ys)


def _layer(x, mix_norm_w, w_in, conv_w, q_norm_w, k_norm_w, sinks, conv_out_norm_w,
           attn_out_norm_w, w_out, ffn_norm_w, w_router, router_bias, w_eg, w_eu, w_ed,
           w_sg, w_su, w_sd):
    B, S, D = x.shape
    N = B * S
    bf16 = jnp.bfloat16
    f32 = jnp.float32

    wr_t = w_router.astype(f32).T
    wr_hi = wr_t.astype(bf16)
    wr_lo = (wr_t - wr_hi.astype(f32)).astype(bf16)
    sink_col = jnp.repeat(sinks.astype(f32), BLK).reshape(N_KV_HEADS, GQA_GROUP * BLK, 1)
    tri = jnp.asarray(np.triu(np.ones((MIX_TILE, MIX_TILE), np.float32), k=1), dtype=bf16)
    bias_tab = jnp.asarray(_attn_bias_table())

    x2, xn, idx, wts, pos, cnt = _mixer_call(
        x, mix_norm_w.reshape(1, D), w_in.astype(bf16), conv_w,
        jnp.tile(q_norm_w, N_HEADS).reshape(1, D_ATTN), jnp.tile(k_norm_w, N_KV_HEADS).reshape(1, D_KV),
        sink_col, bias_tab, conv_out_norm_w.reshape(1, D_CONV), attn_out_norm_w.reshape(1, D_ATTN),
        w_out.astype(bf16), ffn_norm_w.reshape(1, D), w_sg.astype(bf16), w_su.astype(bf16),
        w_sd.astype(bf16), wr_hi, wr_lo, router_bias.astype(f32).reshape(N_EXPERTS, 1), tri)

    BM = EXPERT_BLOCK
    nb = (N * TOP_K) // BM + N_EXPERTS
    counts = cnt[:, 0]
    padded = (counts + BM - 1) // BM * BM
    pad_end = jnp.cumsum(padded)
    pstart = (pad_end - padded).astype(jnp.int32)
    nvalid = (pad_end[-1] // BM).astype(jnp.int32).reshape(1)
    block_expert = jnp.clip(
        jnp.searchsorted(pad_end, jnp.arange(nb, dtype=jnp.int32) * BM, side='right'),
        0, N_EXPERTS - 1).astype(jnp.int32)

    nt = N // ROW_TILE

    def tile_major(a):
        return a.reshape(TOP_K, nt, ROW_TILE).transpose(1, 0, 2).reshape(nt, 1, TOP_K * ROW_TILE)

    idx_t = tile_major(idx)
    pos_t = tile_major(pos)
    xs = _dispatch_call(pstart, idx_t, pos_t, xn, jnp.zeros((nb * BM, D), f32))
    ys = _expert_call(block_expert, nvalid, xs, w_eg, w_eu, w_ed)
    out = _combine_call(pstart, idx_t, pos_t, wts.T, x2.reshape(N, D), ys)
    return out.reshape(B, S, D)


def kernel(x, mix_norm_w, w_in, conv_w, q_norm_w, k_norm_w, sinks, conv_out_norm_w, attn_out_norm_w, w_out, ffn_norm_w, w_router, router_bias, w_exp_gate, w_exp_up, w_exp_down, w_sh_gate, w_sh_up, w_sh_down):
    return _layer(x, mix_norm_w[0], w_in[0], conv_w[0], q_norm_w[0], k_norm_w[0], sinks[0],
                  conv_out_norm_w[0], attn_out_norm_w[0], w_out[0], ffn_norm_w[0], w_router[0],
                  router_bias[0], w_exp_gate[0], w_exp_up[0], w_exp_down[0], w_sh_gate[0],
                  w_sh_up[0], w_sh_down[0])
```

```python
import numpy as np
import jax
import jax.numpy as jnp
from jax import lax
from jax.experimental import pallas as pl
from jax.experimental.pallas import tpu as pltpu

D_MODEL = 1024
EPS = 1e-6
D_CONV = 512
CONV_K = 3
HEAD_DIM = 64
N_HEADS = 8
N_KV_HEADS = 2
GQA_GROUP = 4
D_ATTN = 512
D_KV = 128
WINDOW = 128
BLK = 128
D_IN_PROJ = 2304
N_EXPERTS = 64
TOP_K = 8
N_GROUPS = 8
GROUP_SIZE = 8
TOPK_GROUPS = 4
D_EXPERT = 256
ROUTED_SCALE = 2.5

LANES = 128
ROW_SUB = 8
MIX_TILE = 512
TILE_ROWS = TOP_K * MIX_TILE
EXPERT_BLOCK = 256
SEG_CHUNK_LOG2 = 5
NEG_BIG = -1e30
VMEM_LIMIT = 58 * 1024 * 1024


def _alibi_slopes():
    return np.array([2.0 ** (-8.0 * (h + 1) / N_HEADS) for h in range(N_HEADS)], dtype=np.float32)


def _attn_bias_table():
    qi = np.arange(BLK)[:, None]
    kj = np.arange(2 * BLK)[None, :]
    dist = qi - kj + BLK
    inwin = (dist >= 0) & (dist < WINDOW)
    slopes = _alibi_slopes()
    out = np.zeros((N_KV_HEADS, GQA_GROUP * BLK, 2 * BLK), np.float32)
    for g in range(N_KV_HEADS):
        for i in range(GQA_GROUP):
            h = g * GQA_GROUP + i
            out[g, i * BLK:(i + 1) * BLK] = np.where(inwin, -slopes[h] * dist.astype(np.float32), NEG_BIG)
    return out


def _rms(x, w):
    ms = jnp.mean(x * x, axis=-1, keepdims=True)
    return x * lax.rsqrt(ms + EPS) * w


def _head_rms(x, w_tiled, lane_lo):
    outs = []
    for c in range(x.shape[1] // LANES):
        xc = x[:, c * LANES:(c + 1) * LANES]
        sq = xc * xc
        s_lo = jnp.sum(jnp.where(lane_lo, sq, 0.0), axis=-1, keepdims=True)
        s_hi = jnp.sum(jnp.where(lane_lo, 0.0, sq), axis=-1, keepdims=True)
        inv = jnp.where(lane_lo, lax.rsqrt(s_lo / HEAD_DIM + EPS), lax.rsqrt(s_hi / HEAD_DIM + EPS))
        outs.append(xc * inv)
    return jnp.concatenate(outs, axis=1) * w_tiled


def _dot(a, b):
    return jnp.dot(a, b, preferred_element_type=jnp.float32)


def _dot_nt(a, b):
    return lax.dot_general(a, b, (((1,), (1,)), ((), ())), preferred_element_type=jnp.float32)


def _mixer_kernel(x_ref, mixw_ref, win_ref, convw_ref, qw_ref, kw_ref, sink_ref, bias_ref,
                  cnw_ref, anw_ref, wout_ref, fnw_ref, wsg_ref, wsu_ref, wsd_ref,
                  wrh_ref, wrl_ref, rbias_ref, tri_ref,
                  x2_ref, xn_ref, wts_ref, ld_ref, cnt_ref,
                  kc_ref, vc_ref, cc_ref):
    j = pl.program_id(1)
    T = MIX_TILE
    bf16 = jnp.bfloat16
    f32 = jnp.float32

    @pl.when(j == 0)
    def _():
        kc_ref[...] = jnp.zeros_like(kc_ref)
        vc_ref[...] = jnp.zeros_like(vc_ref)
        cc_ref[...] = jnp.zeros_like(cc_ref)

    x = x_ref[0]
    u = _rms(x, mixw_ref[...]).astype(bf16)
    proj = _dot(u, win_ref[...])
    b_gate = proj[:, 0:512]
    c_gate = proj[:, 512:1024]
    hh = proj[:, 1024:1536]
    q = proj[:, 1536:2048]
    k = proj[:, 2048:2176]
    v = proj[:, 2176:2304]

    ch = c_gate * hh
    prev = cc_ref[...]
    p6 = prev[6:7, :]
    p7 = prev[7:8, :]
    row = lax.broadcasted_iota(jnp.int32, (T, D_CONV), 0)
    ch_m1 = jnp.where(row == 0, p7, pltpu.roll(ch, 1, axis=0))
    ch_m2 = jnp.where(row == 0, p6, jnp.where(row == 1, p7, pltpu.roll(ch, 2, axis=0)))
    cw = convw_ref[...]
    y_conv = b_gate * (cw[0:1, :] * ch_m2 + cw[1:2, :] * ch_m1 + cw[2:3, :] * ch)
    cc_ref[...] = ch[T - 8:T, :]

    lane = lax.broadcasted_iota(jnp.int32, (1, LANES), 1)
    lane_lo = lane < HEAD_DIM
    qn = _head_rms(q, qw_ref[...], lane_lo) * (HEAD_DIM ** -0.5)
    kn = _head_rms(k, kw_ref[...], lane_lo)
    kfull = jnp.concatenate([kc_ref[...], kn], axis=0)
    vfull = jnp.concatenate([vc_ref[...], v], axis=0)
    kc_ref[...] = kn[T - BLK:T, :]
    vc_ref[...] = v[T - BLK:T, :]

    def _rep(a, g):
        r = pltpu.roll(a, HEAD_DIM, axis=1)
        two = jnp.where(lane_lo, a, r) if g == 0 else jnp.where(lane_lo, r, a)
        return jnp.concatenate([two, two], axis=1).astype(bf16)

    k_rep = [_rep(kfull, g) for g in range(N_KV_HEADS)]
    v_rep = [_rep(vfull, g) for g in range(N_KV_HEADS)]

    lane256 = lax.broadcasted_iota(jnp.int32, (1, 2 * LANES), 1)
    head_of_lane = lane256 // HEAD_DIM
    first_f = (j == 0).astype(f32)
    prev_key_mask = jnp.where(lane256 < BLK, first_f * NEG_BIG, 0.0)

    attn_rows = []
    for i in range(T // BLK):
        grp_out = []
        for g in range(N_KV_HEADS):
            qg = qn[i * BLK:(i + 1) * BLK, g * 256:(g + 1) * 256]
            qst = jnp.concatenate(
                [jnp.where(head_of_lane == hi, qg, 0.0) for hi in range(GQA_GROUP)], axis=0).astype(bf16)
            kk = k_rep[g][i * BLK:i * BLK + 2 * BLK, :]
            vv = v_rep[g][i * BLK:i * BLK + 2 * BLK, :]
            s = _dot_nt(qst, kk) + bias_ref[g]
            if i == 0:
                s = s + prev_key_mask
            sink = sink_ref[g]
            m = jnp.maximum(jnp.max(s, axis=-1, keepdims=True), sink)
            e = jnp.exp(s - m)
            denom = jnp.sum(e, axis=-1, keepdims=True) + jnp.exp(sink - m)
            p = (e / denom).astype(bf16)
            r = _dot(p, vv)
            o = jnp.where(head_of_lane == 0, r[0:BLK], 0.0)
            for hi in range(1, GQA_GROUP):
                o = jnp.where(head_of_lane == hi, r[hi * BLK:(hi + 1) * BLK], o)
            grp_out.append(o)
        attn_rows.append(jnp.concatenate(grp_out, axis=1))
    y_attn = jnp.concatenate(attn_rows, axis=0)

    y_mix = jnp.concatenate([_rms(y_conv, cnw_ref[...]), _rms(y_attn, anw_ref[...])], axis=1)
    x1 = x + _dot(y_mix.astype(bf16), wout_ref[...])

    xn = _rms(x1, fnw_ref[...])
    for c in range(ROW_SUB):
        xn_ref[pl.ds(c, T, stride=ROW_SUB), :] = xn[:, c * LANES:(c + 1) * LANES]
    xh = xn.astype(bf16)
    gs = _dot(xh, wsg_ref[...])
    us = _dot(xh, wsu_ref[...])
    hs = (gs * jax.nn.sigmoid(gs) * us).astype(bf16)
    x2_ref[0] = x1 + _dot(hs, wsd_ref[...])

    xl = (xn - xh.astype(f32)).astype(bf16)
    logits = _dot_nt(wrh_ref[...], xh) + _dot_nt(wrh_ref[...], xl) + _dot_nt(wrl_ref[...], xh)
    scores = jax.nn.sigmoid(logits)
    biased = scores + rbias_ref[...]

    sub8 = lax.broadcasted_iota(jnp.int32, (GROUP_SIZE, T), 0)
    gscore = []
    for g in range(N_GROUPS):
        blk = biased[g * GROUP_SIZE:(g + 1) * GROUP_SIZE, :]
        m1 = jnp.max(blk, axis=0, keepdims=True)
        first = jnp.min(jnp.where(blk == m1, sub8, GROUP_SIZE), axis=0, keepdims=True)
        m2 = jnp.max(jnp.where(sub8 == first, -jnp.inf, blk), axis=0, keepdims=True)
        gscore.append(m1 + m2)
    masked_blocks = []
    for g in range(N_GROUPS):
        rank = jnp.zeros((1, T), jnp.int32)
        for o_ in range(N_GROUPS):
            if o_ == g:
                continue
            if o_ < g:
                ahead = gscore[o_] >= gscore[g]
            else:
                ahead = gscore[o_] > gscore[g]
            rank = rank + ahead.astype(jnp.int32)
        keep = rank < TOPK_GROUPS
        blk = biased[g * GROUP_SIZE:(g + 1) * GROUP_SIZE, :]
        masked_blocks.append(jnp.where(keep, blk, -jnp.inf))
    cur = jnp.concatenate(masked_blocks, axis=0)

    eiota = lax.broadcasted_iota(jnp.int32, (N_EXPERTS, T), 0)
    row8 = lax.broadcasted_iota(jnp.int32, (TOP_K, T), 0)
    w_out = jnp.zeros((TOP_K, T), f32)
    sel_dense = jnp.zeros((N_EXPERTS, T), f32)
    onehots = []
    for kk_ in range(TOP_K):
        mx = jnp.max(cur, axis=0, keepdims=True)
        sel_idx = jnp.min(jnp.where(cur == mx, eiota, N_EXPERTS), axis=0, keepdims=True)
        onehot = eiota == sel_idx
        w_k = jnp.sum(jnp.where(onehot, scores, 0.0), axis=0, keepdims=True)
        cur = jnp.where(onehot, -jnp.inf, cur)
        sel_dense = jnp.where(onehot, 1.0, sel_dense)
        w_out = jnp.where(row8 == kk_, w_k, w_out)
        onehots.append(onehot)
    wsum = jnp.sum(w_out, axis=0, keepdims=True)
    wts_ref[...] = w_out / wsum * ROUTED_SCALE

    cum = _dot(sel_dense.astype(bf16), tri_ref[...])
    cnt = jnp.broadcast_to(jnp.sum(sel_dense, axis=1, keepdims=True), (N_EXPERTS, LANES))
    erow = lax.broadcasted_iota(jnp.int32, (N_EXPERTS, LANES), 0)
    incl = cnt
    step = 1
    while step < N_EXPERTS:
        incl = incl + jnp.where(erow >= step, pltpu.roll(incl, step, axis=0), 0.0)
        step *= 2
    lstart = incl - cnt
    ld_dense = lstart[:, 0:1] + cum
    ld_out = jnp.zeros((TOP_K, T), f32)
    for kk_ in range(TOP_K):
        p_k = jnp.sum(jnp.where(onehots[kk_], ld_dense, 0.0), axis=0, keepdims=True)
        ld_out = jnp.where(row8 == kk_, p_k, ld_out)
    ld_ref[...] = ld_out.astype(jnp.int32)
    cnt_ref[...] = cnt.astype(jnp.int32)


def _mixer_call(x, mixw, win, convw, qw, kw, sink_col, bias_tab, cnw, anw, wout, fnw,
                wsg, wsu, wsd, wrh, wrl, rbias, tri):
    B, S, D = x.shape
    T = MIX_TILE
    nt = S // T
    N = B * S

    def full(a):
        nd = a.ndim
        return pl.BlockSpec(a.shape, lambda b, j, _nd=nd: (0,) * _nd)

    tok_spec = pl.BlockSpec((TOP_K, T), lambda b, j: (0, b * nt + j))
    in_arrays = [mixw, win, convw, qw, kw, sink_col, bias_tab, cnw, anw, wout, fnw,
                 wsg, wsu, wsd, wrh, wrl, rbias, tri]
    return pl.pallas_call(
        _mixer_kernel,
        grid=(B, nt),
        in_specs=[pl.BlockSpec((1, T, D), lambda b, j: (b, j, 0))] + [full(a) for a in in_arrays],
        out_specs=[
            pl.BlockSpec((1, T, D), lambda b, j: (b, j, 0)),
            pl.BlockSpec((T * ROW_SUB, LANES), lambda b, j: (b * nt + j, 0)),
            tok_spec, tok_spec,
            pl.BlockSpec((N_EXPERTS, LANES), lambda b, j: (b * nt + j, 0)),
        ],
        out_shape=[
            jax.ShapeDtypeStruct((B, S, D), jnp.float32),
            jax.ShapeDtypeStruct((N * ROW_SUB, LANES), jnp.float32),
            jax.ShapeDtypeStruct((TOP_K, N), jnp.float32),
            jax.ShapeDtypeStruct((TOP_K, N), jnp.int32),
            jax.ShapeDtypeStruct((B * nt * N_EXPERTS, LANES), jnp.int32),
        ],
        scratch_shapes=[
            pltpu.VMEM((BLK, D_KV), jnp.float32),
            pltpu.VMEM((BLK, D_KV), jnp.float32),
            pltpu.VMEM((8, D_CONV), jnp.float32),
        ],
        compiler_params=pltpu.CompilerParams(
            dimension_semantics=("arbitrary", "arbitrary"), vmem_limit_bytes=VMEM_LIMIT),
        name="mixer_router",
    )(x, *in_arrays)


def _rows(ref, row, nrows):
    return ref.at[pl.ds(pl.multiple_of(row * ROW_SUB, ROW_SUB), nrows * ROW_SUB)]


def _segment_copies(src, src_row, dst, dst_row, count, sem):
    chunk = 1 << SEG_CHUNK_LOG2

    def big(i, c):
        o = i * chunk
        pltpu.make_async_copy(_rows(src, src_row + o, chunk), _rows(dst, dst_row + o, chunk), sem).start()
        return c

    lax.fori_loop(0, count >> SEG_CHUNK_LOG2, big, 0)
    for bit in range(SEG_CHUNK_LOG2 - 1, -1, -1):
        o = (count >> (bit + 1)) << (bit + 1)

        @pl.when(((count >> bit) & 1) == 1)
        def _():
            n = 1 << bit
            pltpu.make_async_copy(_rows(src, src_row + o, n), _rows(dst, dst_row + o, n), sem).start()


def _dispatch_kernel(lsrc_ref, gdst_ref, cnt_ref, ld_ref, xn_ref, xs_ref, stag_ref, sem):
    b = pl.program_id(0)
    T = MIX_TILE

    def row(i, c):
        r = ld_ref[0, 0, i]
        t = i & (T - 1)
        stag_ref[pl.ds(pl.multiple_of(r * ROW_SUB, ROW_SUB), ROW_SUB), :] = (
            xn_ref[pl.ds(pl.multiple_of(t * ROW_SUB, ROW_SUB), ROW_SUB), :])
        return c

    lax.fori_loop(0, TILE_ROWS, row, 0, unroll=8)

    def seg(e, c):
        s = b * N_EXPERTS + e
        _segment_copies(stag_ref, lsrc_ref[s], xs_ref, gdst_ref[s], cnt_ref[s], sem)
        return c

    lax.fori_loop(0, N_EXPERTS, seg, 0)
    pltpu.make_async_copy(stag_ref, _rows(xs_ref, 0, TILE_ROWS), sem).wait()


def _dispatch_call(lsrc, gdst, cnt, ld_t, xn_rp):
    nt = ld_t.shape[0]
    T = MIX_TILE
    smem = pl.BlockSpec(memory_space=pltpu.SMEM)
    return pl.pallas_call(
        _dispatch_kernel,
        grid=(nt,),
        in_specs=[
            smem, smem, smem,
            pl.BlockSpec((1, 1, TILE_ROWS), lambda i: (i, 0, 0), memory_space=pltpu.SMEM),
            pl.BlockSpec((T * ROW_SUB, LANES), lambda i: (i, 0)),
        ],
        out_specs=pl.BlockSpec(memory_space=pl.ANY),
        out_shape=jax.ShapeDtypeStruct((nt * TILE_ROWS * ROW_SUB, LANES), jnp.float32),
        scratch_shapes=[pltpu.VMEM((TILE_ROWS * ROW_SUB, LANES), jnp.float32), pltpu.SemaphoreType.DMA],
        compiler_params=pltpu.CompilerParams(
            dimension_semantics=("arbitrary",), vmem_limit_bytes=VMEM_LIMIT),
        name="moe_dispatch",
    )(lsrc, gdst, cnt, ld_t, xn_rp)


def _expert_kernel(blk_ref, exp_ref, lo_ref, hi_ref, first_ref, xs_ref, wg_ref, wu_ref, wd_ref, ys_ref):
    i = pl.program_id(0)
    bf16 = jnp.bfloat16
    BM = EXPERT_BLOCK
    lo = lo_ref[i]
    hi = hi_ref[i]

    @pl.when(hi > lo)
    def _():
        xb = jnp.concatenate(
            [xs_ref[pl.ds(c, BM, stride=ROW_SUB), :] for c in range(ROW_SUB)], axis=1).astype(bf16)
        g = _dot(xb, wg_ref[0].astype(bf16))
        u = _dot(xb, wu_ref[0].astype(bf16))
        h = (g * jax.nn.sigmoid(g) * u).astype(bf16)
        y = _dot(h, wd_ref[0].astype(bf16))

        @pl.when(first_ref[i] == 1)
        def _():
            for c in range(ROW_SUB):
                ys_ref[pl.ds(c, BM, stride=ROW_SUB), :] = y[:, c * LANES:(c + 1) * LANES]

        @pl.when(first_ref[i] == 0)
        def _():
            rows = lax.broadcasted_iota(jnp.int32, (BM, LANES), 0)
            keep = jnp.logical_and(rows >= lo, rows < hi)
            for c in range(ROW_SUB):
                old = ys_ref[pl.ds(c, BM, stride=ROW_SUB), :]
                ys_ref[pl.ds(c, BM, stride=ROW_SUB), :] = jnp.where(keep, y[:, c * LANES:(c + 1) * LANES], old)


def _expert_call(item_blk, item_exp, item_lo, item_hi, item_first, xs_rp, w_eg, w_eu, w_ed):
    BM = EXPERT_BLOCK
    D = D_MODEL
    n_items = item_blk.shape[0]

    def row_map(i, blk, exp, lo, hi, first):
        return (blk[i], 0)

    def w_map(i, blk, exp, lo, hi, first):
        return (exp[i], 0, 0)

    grid_spec = pltpu.PrefetchScalarGridSpec(
        num_scalar_prefetch=5,
        grid=(n_items,),
        in_specs=[
            pl.BlockSpec((BM * ROW_SUB, LANES), row_map),
            pl.BlockSpec((1, D, D_EXPERT), w_map),
            pl.BlockSpec((1, D, D_EXPERT), w_map),
            pl.BlockSpec((1, D_EXPERT, D), w_map),
        ],
        out_specs=pl.BlockSpec((BM * ROW_SUB, LANES), row_map),
    )
    return pl.pallas_call(
        _expert_kernel,
        grid_spec=grid_spec,
        out_shape=jax.ShapeDtypeStruct(xs_rp.shape, jnp.float32),
        compiler_params=pltpu.CompilerParams(
            dimension_semantics=("arbitrary",), vmem_limit_bytes=VMEM_LIMIT),
        name="moe_experts",
    )(item_blk, item_exp, item_lo, item_hi, item_first, xs_rp, w_eg, w_eu, w_ed)


def _combine_kernel(lsrc_ref, gdst_ref, cnt_ref, ld_ref, w_ref, x2_ref, ys_ref, out_ref,
                    stag_ref, acc_ref, sem):
    b = pl.program_id(0)
    T = MIX_TILE

    def seg(e, c):
        s = b * N_EXPERTS + e
        _segment_copies(ys_ref, gdst_ref[s], stag_ref, lsrc_ref[s], cnt_ref[s], sem)
        return c

    lax.fori_loop(0, N_EXPERTS, seg, 0)
    for c in range(ROW_SUB):
        acc_ref[pl.ds(c, T, stride=ROW_SUB), :] = x2_ref[:, c * LANES:(c + 1) * LANES]
    pltpu.make_async_copy(_rows(ys_ref, 0, TILE_ROWS), stag_ref, sem).wait()

    def tok(t, c):
        at = pl.ds(pl.multiple_of(t * ROW_SUB, ROW_SUB), ROW_SUB)
        a = acc_ref[at, :]
        for kk_ in range(TOP_K):
            r = ld_ref[0, 0, kk_ * T + t]
            w = w_ref[0, 0, kk_ * T + t]
            a = a + w * stag_ref[pl.ds(pl.multiple_of(r * ROW_SUB, ROW_SUB), ROW_SUB), :]
        acc_ref[at, :] = a
        return c

    lax.fori_loop(0, T, tok, 0, unroll=2)
    for c in range(ROW_SUB):
        out_ref[:, c * LANES:(c + 1) * LANES] = acc_ref[pl.ds(c, T, stride=ROW_SUB), :]


def _combine_call(lsrc, gdst, cnt, ld_t, w_t, x2, ys_rp):
    N, D = x2.shape
    nt = ld_t.shape[0]
    T = MIX_TILE
    smem = pl.BlockSpec(memory_space=pltpu.SMEM)
    smem_blk = pl.BlockSpec((1, 1, TILE_ROWS), lambda i: (i, 0, 0), memory_space=pltpu.SMEM)
    return pl.pallas_call(
        _combine_kernel,
        grid=(nt,),
        in_specs=[
            smem, smem, smem, smem_blk, smem_blk,
            pl.BlockSpec((T, D), lambda i: (i, 0)),
            pl.BlockSpec(memory_space=pl.ANY),
        ],
        out_specs=pl.BlockSpec((T, D), lambda i: (i, 0)),
        out_shape=jax.ShapeDtypeStruct((N, D), jnp.float32),
        scratch_shapes=[
            pltpu.VMEM((TILE_ROWS * ROW_SUB, LANES), jnp.float32),
            pltpu.VMEM((T * ROW_SUB, LANES), jnp.float32),
            pltpu.SemaphoreType.DMA,
        ],
        compiler_params=pltpu.CompilerParams(
            dimension_semantics=("arbitrary",), vmem_limit_bytes=VMEM_LIMIT),
        name="moe_combine",
    )(lsrc, gdst, cnt, ld_t, w_t, x2, ys_rp)


def _work_items(gstart, totals, n_items):
    BM = EXPERT_BLOCK
    i32 = jnp.int32
    gend = gstart + totals
    first_blk = gstart // BM
    last_blk = jnp.maximum(gend - 1, gstart) // BM
    n_e = jnp.where(totals > 0, last_blk - first_blk + 1, 0)
    item_end = jnp.cumsum(n_e)
    item_start = item_end - n_e
    n_real = item_end[-1]
    j = jnp.minimum(jnp.arange(n_items, dtype=i32), n_real - 1)
    exp = jnp.sum((item_end[None, :] <= j[:, None]).astype(i32), axis=1)
    blk = first_blk[exp] + (j - item_start[exp])
    lo = jnp.maximum(gstart[exp], blk * BM) - blk * BM
    hi = jnp.minimum(gend[exp], (blk + 1) * BM) - blk * BM
    real = jnp.arange(n_items, dtype=i32) < n_real
    hi = jnp.where(real, hi, lo)
    prev_blk = jnp.concatenate([jnp.full((1,), -1, i32), blk[:-1]])
    first = jnp.logical_and(real, blk != prev_blk).astype(i32)
    return blk.astype(i32), exp.astype(i32), lo.astype(i32), hi.astype(i32), first


def _layer(x, mix_norm_w, w_in, conv_w, q_norm_w, k_norm_w, sinks, conv_out_norm_w,
           attn_out_norm_w, w_out, ffn_norm_w, w_router, router_bias, w_eg, w_eu, w_ed,
           w_sg, w_su, w_sd):
    B, S, D = x.shape
    N = B * S
    T = MIX_TILE
    nt = N // T
    bf16 = jnp.bfloat16
    f32 = jnp.float32
    i32 = jnp.int32

    wr_t = w_router.astype(f32).T
    wr_hi = wr_t.astype(bf16)
    wr_lo = (wr_t - wr_hi.astype(f32)).astype(bf16)
    sink_col = jnp.repeat(sinks.astype(f32), BLK).reshape(N_KV_HEADS, GQA_GROUP * BLK, 1)
    tri = jnp.asarray(np.triu(np.ones((T, T), np.float32), k=1), dtype=bf16)
    bias_tab = jnp.asarray(_attn_bias_table())

    x2, xn_rp, wts, ld, cnt = _mixer_call(
        x, mix_norm_w.reshape(1, D), w_in.astype(bf16), conv_w,
        jnp.tile(q_norm_w, N_HEADS).reshape(1, D_ATTN), jnp.tile(k_norm_w, N_KV_HEADS).reshape(1, D_KV),
        sink_col, bias_tab, conv_out_norm_w.reshape(1, D_CONV), attn_out_norm_w.reshape(1, D_ATTN),
        w_out.astype(bf16), ffn_norm_w.reshape(1, D), w_sg.astype(bf16), w_su.astype(bf16),
        w_sd.astype(bf16), wr_hi, wr_lo, router_bias.astype(f32).reshape(N_EXPERTS, 1), tri)

    cnt_te = cnt[:, 0].reshape(nt, N_EXPERTS)
    lsrc = jnp.cumsum(cnt_te, axis=1) - cnt_te
    before = jnp.cumsum(cnt_te, axis=0) - cnt_te
    totals = jnp.sum(cnt_te, axis=0)
    gstart = jnp.cumsum(totals) - totals
    gdst = gstart[None, :] + before
    n_items = (N * TOP_K) // EXPERT_BLOCK + N_EXPERTS
    items = _work_items(gstart.astype(i32), totals.astype(i32), n_items)

    def tile_major(a):
        return a.reshape(TOP_K, nt, T).transpose(1, 0, 2).reshape(nt, 1, TILE_ROWS)

    ld_t = tile_major(ld)
    seg_tabs = (lsrc.reshape(-1).astype(i32), gdst.reshape(-1).astype(i32), cnt_te.reshape(-1).astype(i32))
    xs_rp = _dispatch_call(*seg_tabs, ld_t, xn_rp)
    ys_rp = _expert_call(*items, xs_rp, w_eg, w_eu, w_ed)
    out = _combine_call(*seg_tabs, ld_t, tile_major(wts), x2.reshape(N, D), ys_rp)
    return out.reshape(B, S, D)


def kernel(x, mix_norm_w, w_in, conv_w, q_norm_w, k_norm_w, sinks, conv_out_norm_w, attn_out_norm_w, w_out, ffn_norm_w, w_router, router_bias, w_exp_gate, w_exp_up, w_exp_down, w_sh_gate, w_sh_up, w_sh_down):
    return _layer(x, mix_norm_w[0], w_in[0], conv_w[0], q_norm_w[0], k_norm_w[0], sinks[0],
                  conv_out_norm_w[0], attn_out_norm_w[0], w_out[0], ffn_norm_w[0], w_router[0],
                  router_bias[0], w_exp_gate[0], w_exp_up[0], w_exp_down[0], w_sh_gate[0],
                  w_sh_up[0], w_sh_down[0])
```

```python
import numpy as np
import jax
import jax.numpy as jnp
from jax import lax
from jax.experimental import pallas as pl
from jax.experimental.pallas import tpu as pltpu

D_MODEL = 1024
EPS = 1e-6
D_CONV = 512
CONV_K = 3
HEAD_DIM = 64
N_HEADS = 8
N_KV_HEADS = 2
GQA_GROUP = 4
D_ATTN = 512
D_KV = 128
WINDOW = 128
BLK = 128
D_IN_PROJ = 2304
N_EXPERTS = 64
TOP_K = 8
N_GROUPS = 8
GROUP_SIZE = 8
TOPK_GROUPS = 4
D_EXPERT = 256
ROUTED_SCALE = 2.5

LANES = 128
ROW_SUB = 8
MIX_TILE = 512
TILE_ROWS = TOP_K * MIX_TILE
EXPERT_BLOCK = 256
SEG_CHUNK_LOG2 = 5
NEG_BIG = -1e30
VMEM_LIMIT = 58 * 1024 * 1024


def _alibi_slopes():
    return np.array([2.0 ** (-8.0 * (h + 1) / N_HEADS) for h in range(N_HEADS)], dtype=np.float32)


def _attn_bias_table():
    qi = np.arange(BLK)[:, None]
    kj = np.arange(2 * BLK)[None, :]
    dist = qi - kj + BLK
    inwin = (dist >= 0) & (dist < WINDOW)
    slopes = _alibi_slopes()
    out = np.zeros((N_KV_HEADS, GQA_GROUP * BLK, 2 * BLK), np.float32)
    for g in range(N_KV_HEADS):
        for i in range(GQA_GROUP):
            h = g * GQA_GROUP + i
            out[g, i * BLK:(i + 1) * BLK] = np.where(inwin, -slopes[h] * dist.astype(np.float32), NEG_BIG)
    return out


def _rms(x, w):
    ms = jnp.mean(x * x, axis=-1, keepdims=True)
    return x * lax.rsqrt(ms + EPS) * w


def _head_rms(x, w_tiled, lane_lo):
    outs = []
    for c in range(x.shape[1] // LANES):
        xc = x[:, c * LANES:(c + 1) * LANES]
        sq = xc * xc
        s_lo = jnp.sum(jnp.where(lane_lo, sq, 0.0), axis=-1, keepdims=True)
        s_hi = jnp.sum(jnp.where(lane_lo, 0.0, sq), axis=-1, keepdims=True)
        inv = jnp.where(lane_lo, lax.rsqrt(s_lo / HEAD_DIM + EPS), lax.rsqrt(s_hi / HEAD_DIM + EPS))
        outs.append(xc * inv)
    return jnp.concatenate(outs, axis=1) * w_tiled


def _dot(a, b):
    return jnp.dot(a, b, preferred_element_type=jnp.float32)


def _dot_nt(a, b):
    return lax.dot_general(a, b, (((1,), (1,)), ((), ())), preferred_element_type=jnp.float32)


def _mixer_kernel(x_ref, mixw_ref, win_ref, convw_ref, qw_ref, kw_ref, sink_ref, bias_ref,
                  cnw_ref, anw_ref, wout_ref, fnw_ref, wsg_ref, wsu_ref, wsd_ref,
                  wrh_ref, wrl_ref, rbias_ref, tri_ref,
                  x2_ref, xn_ref, wts_ref, ld_ref, cnt_ref,
                  kc_ref, vc_ref, cc_ref):
    j = pl.program_id(1)
    T = MIX_TILE
    bf16 = jnp.bfloat16
    f32 = jnp.float32

    @pl.when(j == 0)
    def _():
        kc_ref[...] = jnp.zeros_like(kc_ref)
        vc_ref[...] = jnp.zeros_like(vc_ref)
        cc_ref[...] = jnp.zeros_like(cc_ref)

    x = x_ref[0]
    u = _rms(x, mixw_ref[...]).astype(bf16)
    proj = _dot(u, win_ref[...])
    b_gate = proj[:, 0:512]
    c_gate = proj[:, 512:1024]
    hh = proj[:, 1024:1536]
    q = proj[:, 1536:2048]
    k = proj[:, 2048:2176]
    v = proj[:, 2176:2304]

    ch = c_gate * hh
    prev = cc_ref[...]
    p6 = prev[6:7, :]
    p7 = prev[7:8, :]
    row = lax.broadcasted_iota(jnp.int32, (T, D_CONV), 0)
    ch_m1 = jnp.where(row == 0, p7, pltpu.roll(ch, 1, axis=0))
    ch_m2 = jnp.where(row == 0, p6, jnp.where(row == 1, p7, pltpu.roll(ch, 2, axis=0)))
    cw = convw_ref[...]
    y_conv = b_gate * (cw[0:1, :] * ch_m2 + cw[1:2, :] * ch_m1 + cw[2:3, :] * ch)
    cc_ref[...] = ch[T - 8:T, :]

    lane = lax.broadcasted_iota(jnp.int32, (1, LANES), 1)
    lane_lo = lane < HEAD_DIM
    qn = _head_rms(q, qw_ref[...], lane_lo) * (HEAD_DIM ** -0.5)
    kn = _head_rms(k, kw_ref[...], lane_lo)
    kfull = jnp.concatenate([kc_ref[...], kn], axis=0)
    vfull = jnp.concatenate([vc_ref[...], v], axis=0)
    kc_ref[...] = kn[T - BLK:T, :]
    vc_ref[...] = v[T - BLK:T, :]

    def _rep(a, g):
        r = pltpu.roll(a, HEAD_DIM, axis=1)
        two = jnp.where(lane_lo, a, r) if g == 0 else jnp.where(lane_lo, r, a)
        return jnp.concatenate([two, two], axis=1).astype(bf16)

    k_rep = [_rep(kfull, g) for g in range(N_KV_HEADS)]
    v_rep = [_rep(vfull, g) for g in range(N_KV_HEADS)]

    lane256 = lax.broadcasted_iota(jnp.int32, (1, 2 * LANES), 1)
    head_of_lane = lane256 // HEAD_DIM
    first_f = (j == 0).astype(f32)
    prev_key_mask = jnp.where(lane256 < BLK, first_f * NEG_BIG, 0.0)

    attn_rows = []
    for i in range(T // BLK):
        grp_out = []
        for g in range(N_KV_HEADS):
            qg = qn[i * BLK:(i + 1) * BLK, g * 256:(g + 1) * 256]
            qst = jnp.concatenate(
                [jnp.where(head_of_lane == hi, qg, 0.0) for hi in range(GQA_GROUP)], axis=0).astype(bf16)
            kk = k_rep[g][i * BLK:i * BLK + 2 * BLK, :]
            vv = v_rep[g][i * BLK:i * BLK + 2 * BLK, :]
            s = _dot_nt(qst, kk) + bias_ref[g]
            if i == 0:
                s = s + prev_key_mask
            sink = sink_ref[g]
            m = jnp.maximum(jnp.max(s, axis=-1, keepdims=True), sink)
            e = jnp.exp(s - m)
            denom = jnp.sum(e, axis=-1, keepdims=True) + jnp.exp(sink - m)
            p = (e / denom).astype(bf16)
            r = _dot(p, vv)
            o = jnp.where(head_of_lane == 0, r[0:BLK], 0.0)
            for hi in range(1, GQA_GROUP):
                o = jnp.where(head_of_lane == hi, r[hi * BLK:(hi + 1) * BLK], o)
            grp_out.append(o)
        attn_rows.append(jnp.concatenate(grp_out, axis=1))
    y_attn = jnp.concatenate(attn_rows, axis=0)

    y_mix = jnp.concatenate([_rms(y_conv, cnw_ref[...]), _rms(y_attn, anw_ref[...])], axis=1)
    x1 = x + _dot(y_mix.astype(bf16), wout_ref[...])

    xn = _rms(x1, fnw_ref[...])
    for c in range(ROW_SUB):
        xn_ref[pl.ds(c, T, stride=ROW_SUB), :] = xn[:, c * LANES:(c + 1) * LANES]
    xh = xn.astype(bf16)
    gs = _dot(xh, wsg_ref[...])
    us = _dot(xh, wsu_ref[...])
    hs = (gs * jax.nn.sigmoid(gs) * us).astype(bf16)
    x2_ref[0] = x1 + _dot(hs, wsd_ref[...])

    xl = (xn - xh.astype(f32)).astype(bf16)
    logits = _dot_nt(wrh_ref[...], xh) + _dot_nt(wrh_ref[...], xl) + _dot_nt(wrl_ref[...], xh)
    scores = jax.nn.sigmoid(logits)
    biased = scores + rbias_ref[...]

    sub8 = lax.broadcasted_iota(jnp.int32, (GROUP_SIZE, T), 0)
    gscore = []
    for g in range(N_GROUPS):
        blk = biased[g * GROUP_SIZE:(g + 1) * GROUP_SIZE, :]
        m1 = jnp.max(blk, axis=0, keepdims=True)
        first = jnp.min(jnp.where(blk == m1, sub8, GROUP_SIZE), axis=0, keepdims=True)
        m2 = jnp.max(jnp.where(sub8 == first, -jnp.inf, blk), axis=0, keepdims=True)
        gscore.append(m1 + m2)
    masked_blocks = []
    for g in range(N_GROUPS):
        rank = jnp.zeros((1, T), jnp.int32)
        for o_ in range(N_GROUPS):
            if o_ == g:
                continue
            if o_ < g:
                ahead = gscore[o_] >= gscore[g]
            else:
                ahead = gscore[o_] > gscore[g]
            rank = rank + ahead.astype(jnp.int32)
        keep = rank < TOPK_GROUPS
        blk = biased[g * GROUP_SIZE:(g + 1) * GROUP_SIZE, :]
        masked_blocks.append(jnp.where(keep, blk, -jnp.inf))
    cur = jnp.concatenate(masked_blocks, axis=0)

    eiota = lax.broadcasted_iota(jnp.int32, (N_EXPERTS, T), 0)
    row8 = lax.broadcasted_iota(jnp.int32, (TOP_K, T), 0)
    w_out = jnp.zeros((TOP_K, T), f32)
    sel_dense = jnp.zeros((N_EXPERTS, T), f32)
    onehots = []
    for kk_ in range(TOP_K):
        mx = jnp.max(cur, axis=0, keepdims=True)
        sel_idx = jnp.min(jnp.where(cur == mx, eiota, N_EXPERTS), axis=0, keepdims=True)
        onehot = eiota == sel_idx
        w_k = jnp.sum(jnp.where(onehot, scores, 0.0), axis=0, keepdims=True)
        cur = jnp.where(onehot, -jnp.inf, cur)
        sel_dense = jnp.where(onehot, 1.0, sel_dense)
        w_out = jnp.where(row8 == kk_, w_k, w_out)
        onehots.append(onehot)
    wsum = jnp.sum(w_out, axis=0, keepdims=True)
    wts_ref[...] = w_out / wsum * ROUTED_SCALE

    cum = _dot(sel_dense.astype(bf16), tri_ref[...])
    cnt = jnp.broadcast_to(jnp.sum(sel_dense, axis=1, keepdims=True), (N_EXPERTS, LANES))
    erow = lax.broadcasted_iota(jnp.int32, (N_EXPERTS, LANES), 0)
    incl = cnt
    step = 1
    while step < N_EXPERTS:
        incl = incl + jnp.where(erow >= step, pltpu.roll(incl, step, axis=0), 0.0)
        step *= 2
    lstart = incl - cnt
    ld_dense = lstart[:, 0:1] + cum
    ld_out = jnp.zeros((TOP_K, T), f32)
    for kk_ in range(TOP_K):
        p_k = jnp.sum(jnp.where(onehots[kk_], ld_dense, 0.0), axis=0, keepdims=True)
        ld_out = jnp.where(row8 == kk_, p_k, ld_out)
    ld_ref[...] = ld_out.astype(jnp.int32)
    cnt_ref[...] = cnt.astype(jnp.int32)


def _mixer_call(x, mixw, win, convw, qw, kw, sink_col, bias_tab, cnw, anw, wout, fnw,
                wsg, wsu, wsd, wrh, wrl, rbias, tri):
    B, S, D = x.shape
    T = MIX_TILE
    nt = S // T
    N = B * S

    def full(a):
        nd = a.ndim
        return pl.BlockSpec(a.shape, lambda b, j, _nd=nd: (0,) * _nd)

    tok_spec = pl.BlockSpec((TOP_K, T), lambda b, j: (0, b * nt + j))
    in_arrays = [mixw, win, convw, qw, kw, sink_col, bias_tab, cnw, anw, wout, fnw,
                 wsg, wsu, wsd, wrh, wrl, rbias, tri]
    return pl.pallas_call(
        _mixer_kernel,
        grid=(B, nt),
        in_specs=[pl.BlockSpec((1, T, D), lambda b, j: (b, j, 0))] + [full(a) for a in in_arrays],
        out_specs=[
            pl.BlockSpec((1, T, D), lambda b, j: (b, j, 0)),
            pl.BlockSpec((T * ROW_SUB, LANES), lambda b, j: (b * nt + j, 0)),
            tok_spec, tok_spec,
            pl.BlockSpec((N_EXPERTS, LANES), lambda b, j: (b * nt + j, 0)),
        ],
        out_shape=[
            jax.ShapeDtypeStruct((B, S, D), jnp.float32),
            jax.ShapeDtypeStruct((N * ROW_SUB, LANES), jnp.float32),
            jax.ShapeDtypeStruct((TOP_K, N), jnp.float32),
            jax.ShapeDtypeStruct((TOP_K, N), jnp.int32),
            jax.ShapeDtypeStruct((B * nt * N_EXPERTS, LANES), jnp.int32),
        ],
        scratch_shapes=[
            pltpu.VMEM((BLK, D_KV), jnp.float32),
            pltpu.VMEM((BLK, D_KV), jnp.float32),
            pltpu.VMEM((8, D_CONV), jnp.float32),
        ],
        compiler_params=pltpu.CompilerParams(
            dimension_semantics=("arbitrary", "arbitrary"), vmem_limit_bytes=VMEM_LIMIT),
        name="mixer_router",
    )(x, *in_arrays)


def _rows(ref, row, nrows):
    return ref.at[pl.ds(pl.multiple_of(row * ROW_SUB, ROW_SUB), nrows * ROW_SUB)]


def _segment_copies(src, src_row, dst, dst_row, count, sem):
    chunk = 1 << SEG_CHUNK_LOG2

    def big(i, c):
        o = i * chunk
        pltpu.make_async_copy(_rows(src, src_row + o, chunk), _rows(dst, dst_row + o, chunk), sem).start()
        return c

    lax.fori_loop(0, count >> SEG_CHUNK_LOG2, big, 0)
    for bit in range(SEG_CHUNK_LOG2 - 1, -1, -1):
        o = (count >> (bit + 1)) << (bit + 1)

        @pl.when(((count >> bit) & 1) == 1)
        def _():
            n = 1 << bit
            pltpu.make_async_copy(_rows(src, src_row + o, n), _rows(dst, dst_row + o, n), sem).start()


def _wait_tile(hbm_ref, stag_ref, slot_row, sem, to_hbm):
    vm = _rows(stag_ref, slot_row, TILE_ROWS)
    hb = _rows(hbm_ref, 0, TILE_ROWS)
    (pltpu.make_async_copy(vm, hb, sem) if to_hbm else pltpu.make_async_copy(hb, vm, sem)).wait()


def _dispatch_kernel(lsrc_ref, gdst_ref, cnt_ref, ld_ref, xn_ref, xs_ref, stag_ref, sems):
    b = pl.program_id(0)
    nb = pl.num_programs(0)
    T = MIX_TILE
    slot = b & 1
    base = slot * TILE_ROWS

    for kk_ in range(TOP_K):
        def row(t, c, kk_=kk_):
            r = base + ld_ref[0, 0, kk_ * T + t]
            stag_ref[pl.ds(pl.multiple_of(r * ROW_SUB, ROW_SUB), ROW_SUB), :] = (
                xn_ref[pl.ds(pl.multiple_of(t * ROW_SUB, ROW_SUB), ROW_SUB), :])
            return c

        lax.fori_loop(0, T, row, 0, unroll=8)

    def seg(e, c):
        s = b * N_EXPERTS + e
        _segment_copies(stag_ref, base + lsrc_ref[s], xs_ref, gdst_ref[s], cnt_ref[s], sems.at[slot])
        return c

    lax.fori_loop(0, N_EXPERTS, seg, 0)

    @pl.when(b > 0)
    def _():
        _wait_tile(xs_ref, stag_ref, (1 - slot) * TILE_ROWS, sems.at[1 - slot], True)

    @pl.when(b == nb - 1)
    def _():
        _wait_tile(xs_ref, stag_ref, base, sems.at[slot], True)


def _dispatch_call(lsrc, gdst, cnt, ld_t, xn_rp):
    nt = ld_t.shape[0]
    T = MIX_TILE
    smem = pl.BlockSpec(memory_space=pltpu.SMEM)
    return pl.pallas_call(
        _dispatch_kernel,
        grid=(nt,),
        in_specs=[
            smem, smem, smem,
            pl.BlockSpec((1, 1, TILE_ROWS), lambda i: (i, 0, 0), memory_space=pltpu.SMEM),
            pl.BlockSpec((T * ROW_SUB, LANES), lambda i: (i, 0)),
        ],
        out_specs=pl.BlockSpec(memory_space=pl.ANY),
        out_shape=jax.ShapeDtypeStruct((nt * TILE_ROWS * ROW_SUB, LANES), jnp.float32),
        scratch_shapes=[pltpu.VMEM((2 * TILE_ROWS * ROW_SUB, LANES), jnp.float32),
                        pltpu.SemaphoreType.DMA((2,))],
        compiler_params=pltpu.CompilerParams(
            dimension_semantics=("arbitrary",), vmem_limit_bytes=VMEM_LIMIT),
        name="moe_dispatch",
    )(lsrc, gdst, cnt, ld_t, xn_rp)


def _expert_kernel(blk_ref, exp_ref, lo_ref, hi_ref, first_ref, newexp_ref,
                   xs_ref, wg_ref, wu_ref, wd_ref, ys_ref, wgb_ref, wub_ref, wdb_ref):
    i = pl.program_id(0)
    bf16 = jnp.bfloat16
    BM = EXPERT_BLOCK
    lo = lo_ref[i]
    hi = hi_ref[i]

    @pl.when(newexp_ref[i] == 1)
    def _():
        wgb_ref[...] = wg_ref[0].astype(bf16)
        wub_ref[...] = wu_ref[0].astype(bf16)
        wdb_ref[...] = wd_ref[0].astype(bf16)

    @pl.when(hi > lo)
    def _():
        xb = jnp.concatenate(
            [xs_ref[pl.ds(c, BM, stride=ROW_SUB), :] for c in range(ROW_SUB)], axis=1).astype(bf16)
        g = _dot(xb, wgb_ref[...])
        u = _dot(xb, wub_ref[...])
        h = (g * jax.nn.sigmoid(g) * u).astype(bf16)
        y = _dot(h, wdb_ref[...])

        @pl.when(first_ref[i] == 1)
        def _():
            for c in range(ROW_SUB):
                ys_ref[pl.ds(c, BM, stride=ROW_SUB), :] = y[:, c * LANES:(c + 1) * LANES]

        @pl.when(first_ref[i] == 0)
        def _():
            rows = lax.broadcasted_iota(jnp.int32, (BM, LANES), 0)
            keep = jnp.logical_and(rows >= lo, rows < hi)
            for c in range(ROW_SUB):
                old = ys_ref[pl.ds(c, BM, stride=ROW_SUB), :]
                ys_ref[pl.ds(c, BM, stride=ROW_SUB), :] = jnp.where(keep, y[:, c * LANES:(c + 1) * LANES], old)


def _expert_call(item_blk, item_exp, item_lo, item_hi, item_first, item_newexp, xs_rp, w_eg, w_eu, w_ed):
    BM = EXPERT_BLOCK
    D = D_MODEL
    n_items = item_blk.shape[0]

    def row_map(i, blk, exp, lo, hi, first, newexp):
        return (blk[i], 0)

    def w_map(i, blk, exp, lo, hi, first, newexp):
        return (exp[i], 0, 0)

    grid_spec = pltpu.PrefetchScalarGridSpec(
        num_scalar_prefetch=6,
        grid=(n_items,),
        in_specs=[
            pl.BlockSpec((BM * ROW_SUB, LANES), row_map),
            pl.BlockSpec((1, D, D_EXPERT), w_map),
            pl.BlockSpec((1, D, D_EXPERT), w_map),
            pl.BlockSpec((1, D_EXPERT, D), w_map),
        ],
        out_specs=pl.BlockSpec((BM * ROW_SUB, LANES), row_map),
        scratch_shapes=[
            pltpu.VMEM((D, D_EXPERT), jnp.bfloat16),
            pltpu.VMEM((D, D_EXPERT), jnp.bfloat16),
            pltpu.VMEM((D_EXPERT, D), jnp.bfloat16),
        ],
    )
    return pl.pallas_call(
        _expert_kernel,
        grid_spec=grid_spec,
        out_shape=jax.ShapeDtypeStruct(xs_rp.shape, jnp.float32),
        compiler_params=pltpu.CompilerParams(
            dimension_semantics=("arbitrary",), vmem_limit_bytes=VMEM_LIMIT),
        name="moe_experts",
    )(item_blk, item_exp, item_lo, item_hi, item_first, item_newexp, xs_rp, w_eg, w_eu, w_ed)


def _combine_kernel(lsrc_ref, gdst_ref, cnt_ref, ld_ref, w_ref, x2_ref, ys_ref, out_ref,
                    stag_ref, acc_ref, sems):
    b = pl.program_id(0)
    nb = pl.num_programs(0)
    T = MIX_TILE
    slot = b & 1
    base = slot * TILE_ROWS

    def fetch(tile, slot_):
        def seg(e, c):
            s = tile * N_EXPERTS + e
            _segment_copies(ys_ref, gdst_ref[s], stag_ref, slot_ * TILE_ROWS + lsrc_ref[s], cnt_ref[s],
                            sems.at[slot_])
            return c

        lax.fori_loop(0, N_EXPERTS, seg, 0)

    @pl.when(b == 0)
    def _():
        fetch(b, slot)

    @pl.when(b + 1 < nb)
    def _():
        fetch(b + 1, 1 - slot)

    for c in range(ROW_SUB):
        acc_ref[pl.ds(c, T, stride=ROW_SUB), :] = x2_ref[:, c * LANES:(c + 1) * LANES]
    _wait_tile(ys_ref, stag_ref, base, sems.at[slot], False)

    def tok(t, c):
        at = pl.ds(pl.multiple_of(t * ROW_SUB, ROW_SUB), ROW_SUB)
        a = acc_ref[at, :]
        for kk_ in range(TOP_K):
            r = base + ld_ref[0, 0, kk_ * T + t]
            w = w_ref[0, 0, kk_ * T + t]
            a = a + w * stag_ref[pl.ds(pl.multiple_of(r * ROW_SUB, ROW_SUB), ROW_SUB), :]
        acc_ref[at, :] = a
        return c

    lax.fori_loop(0, T, tok, 0, unroll=2)
    for c in range(ROW_SUB):
        out_ref[:, c * LANES:(c + 1) * LANES] = acc_ref[pl.ds(c, T, stride=ROW_SUB), :]


def _combine_call(lsrc, gdst, cnt, ld_t, w_t, x2, ys_rp):
    N, D = x2.shape
    nt = ld_t.shape[0]
    T = MIX_TILE
    smem = pl.BlockSpec(memory_space=pltpu.SMEM)
    smem_blk = pl.BlockSpec((1, 1, TILE_ROWS), lambda i: (i, 0, 0), memory_space=pltpu.SMEM)
    return pl.pallas_call(
        _combine_kernel,
        grid=(nt,),
        in_specs=[
            smem, smem, smem, smem_blk, smem_blk,
            pl.BlockSpec((T, D), lambda i: (i, 0)),
            pl.BlockSpec(memory_space=pl.ANY),
        ],
        out_specs=pl.BlockSpec((T, D), lambda i: (i, 0)),
        out_shape=jax.ShapeDtypeStruct((N, D), jnp.float32),
        scratch_shapes=[
            pltpu.VMEM((2 * TILE_ROWS * ROW_SUB, LANES), jnp.float32),
            pltpu.VMEM((T * ROW_SUB, LANES), jnp.float32),
            pltpu.SemaphoreType.DMA((2,)),
        ],
        compiler_params=pltpu.CompilerParams(
            dimension_semantics=("arbitrary",), vmem_limit_bytes=VMEM_LIMIT),
        name="moe_combine",
    )(lsrc, gdst, cnt, ld_t, w_t, x2, ys_rp)


def _work_items(gstart, totals, n_items):
    BM = EXPERT_BLOCK
    i32 = jnp.int32
    gend = gstart + totals
    first_blk = gstart // BM
    last_blk = jnp.maximum(gend - 1, gstart) // BM
    n_e = jnp.where(totals > 0, last_blk - first_blk + 1, 0)
    item_end = jnp.cumsum(n_e)
    item_start = item_end - n_e
    n_real = item_end[-1]
    j = jnp.minimum(jnp.arange(n_items, dtype=i32), n_real - 1)
    exp = jnp.sum((item_end[None, :] <= j[:, None]).astype(i32), axis=1)
    onehot = exp[:, None] == jnp.arange(N_EXPERTS, dtype=i32)[None, :]

    def of_exp(v):
        return jnp.sum(jnp.where(onehot, v[None, :], 0), axis=1)

    blk = of_exp(first_blk) + (j - of_exp(item_start))
    lo = jnp.maximum(of_exp(gstart), blk * BM) - blk * BM
    hi = jnp.minimum(of_exp(gend), (blk + 1) * BM) - blk * BM
    real = jnp.arange(n_items, dtype=i32) < n_real
    hi = jnp.where(real, hi, lo)
    prev_blk = jnp.concatenate([jnp.full((1,), -1, i32), blk[:-1]])
    prev_exp = jnp.concatenate([jnp.full((1,), -1, i32), exp[:-1]])
    first = jnp.logical_and(real, blk != prev_blk).astype(i32)
    newexp = jnp.logical_and(real, exp != prev_exp).astype(i32)
    return blk.astype(i32), exp.astype(i32), lo.astype(i32), hi.astype(i32), first, newexp


def _layer(x, mix_norm_w, w_in, conv_w, q_norm_w, k_norm_w, sinks, conv_out_norm_w,
           attn_out_norm_w, w_out, ffn_norm_w, w_router, router_bias, w_eg, w_eu, w_ed,
           w_sg, w_su, w_sd):
    B, S, D = x.shape
    N = B * S
    T = MIX_TILE
    nt = N // T
    bf16 = jnp.bfloat16
    f32 = jnp.float32
    i32 = jnp.int32

    wr_t = w_router.astype(f32).T
    wr_hi = wr_t.astype(bf16)
    wr_lo = (wr_t - wr_hi.astype(f32)).astype(bf16)
    sink_col = jnp.repeat(sinks.astype(f32), BLK).reshape(N_KV_HEADS, GQA_GROUP * BLK, 1)
    tri = jnp.asarray(np.triu(np.ones((T, T), np.float32), k=1), dtype=bf16)
    bias_tab = jnp.asarray(_attn_bias_table())

    x2, xn_rp, wts, ld, cnt = _mixer_call(
        x, mix_norm_w.reshape(1, D), w_in.astype(bf16), conv_w,
        jnp.tile(q_norm_w, N_HEADS).reshape(1, D_ATTN), jnp.tile(k_norm_w, N_KV_HEADS).reshape(1, D_KV),
        sink_col, bias_tab, conv_out_norm_w.reshape(1, D_CONV), attn_out_norm_w.reshape(1, D_ATTN),
        w_out.astype(bf16), ffn_norm_w.reshape(1, D), w_sg.astype(bf16), w_su.astype(bf16),
        w_sd.astype(bf16), wr_hi, wr_lo, router_bias.astype(f32).reshape(N_EXPERTS, 1), tri)

    cnt_te = cnt[:, 0].reshape(nt, N_EXPERTS)
    lsrc = jnp.cumsum(cnt_te, axis=1) - cnt_te
    before = jnp.cumsum(cnt_te, axis=0) - cnt_te
    totals = jnp.sum(cnt_te, axis=0)
    gstart = jnp.cumsum(totals) - totals
    gdst = gstart[None, :] + before
    n_items = (N * TOP_K) // EXPERT_BLOCK + N_EXPERTS
    items = _work_items(gstart.astype(i32), totals.astype(i32), n_items)

    def tile_major(a):
        return a.reshape(TOP_K, nt, T).transpose(1, 0, 2).reshape(nt, 1, TILE_ROWS)

    ld_t = tile_major(ld)
    seg_tabs = (lsrc.reshape(-1).astype(i32), gdst.reshape(-1).astype(i32), cnt_te.reshape(-1).astype(i32))
    xs_rp = _dispatch_call(*seg_tabs, ld_t, xn_rp)
    ys_rp = _expert_call(*items, xs_rp, w_eg, w_eu, w_ed)
    out = _combine_call(*seg_tabs, ld_t, tile_major(wts), x2.reshape(N, D), ys_rp)
    return out.reshape(B, S, D)


def kernel(x, mix_norm_w, w_in, conv_w, q_norm_w, k_norm_w, sinks, conv_out_norm_w, attn_out_norm_w, w_out, ffn_norm_w, w_router, router_bias, w_exp_gate, w_exp_up, w_exp_down, w_sh_gate, w_sh_up, w_sh_down):
    return _layer(x, mix_norm_w[0], w_in[0], conv_w[0], q_norm_w[0], k_norm_w[0], sinks[0],
                  conv_out_norm_w[0], attn_out_norm_w[0], w_out[0], ffn_norm_w[0], w_router[0],
                  router_bias[0], w_exp_gate[0], w_exp_up[0], w_exp_down[0], w_sh_gate[0],
                  w_sh_up[0], w_sh_down[0])
```

```python
import numpy as np
import jax
import jax.numpy as jnp
from jax import lax
from jax.experimental import pallas as pl
from jax.experimental.pallas import tpu as pltpu

D_MODEL = 1024
EPS = 1e-6
D_CONV = 512
CONV_K = 3
HEAD_DIM = 64
N_HEADS = 8
N_KV_HEADS = 2
GQA_GROUP = 4
D_ATTN = 512
D_KV = 128
WINDOW = 128
BLK = 128
D_IN_PROJ = 2304
N_EXPERTS = 64
TOP_K = 8
N_GROUPS = 8
GROUP_SIZE = 8
TOPK_GROUPS = 4
D_EXPERT = 256
ROUTED_SCALE = 2.5

LANES = 128
ROW_SUB = 8
MIX_TILE = 512
TILE_ROWS = TOP_K * MIX_TILE
EXPERT_BLOCK = 1024
EXPERT_SUB = 256
SEG_CHUNK_LOG2 = 5
NEG_BIG = -1e30
VMEM_LIMIT = 58 * 1024 * 1024


def _alibi_slopes():
    return np.array([2.0 ** (-8.0 * (h + 1) / N_HEADS) for h in range(N_HEADS)], dtype=np.float32)


def _attn_bias_table():
    qi = np.arange(BLK)[:, None]
    kj = np.arange(2 * BLK)[None, :]
    dist = qi - kj + BLK
    inwin = (dist >= 0) & (dist < WINDOW)
    slopes = _alibi_slopes()
    out = np.zeros((N_KV_HEADS, GQA_GROUP * BLK, 2 * BLK), np.float32)
    for g in range(N_KV_HEADS):
        for i in range(GQA_GROUP):
            h = g * GQA_GROUP + i
            out[g, i * BLK:(i + 1) * BLK] = np.where(inwin, -slopes[h] * dist.astype(np.float32), NEG_BIG)
    return out


def _rms(x, w):
    ms = jnp.mean(x * x, axis=-1, keepdims=True)
    return x * lax.rsqrt(ms + EPS) * w


def _head_rms(x, w_tiled, lane_lo):
    outs = []
    for c in range(x.shape[1] // LANES):
        xc = x[:, c * LANES:(c + 1) * LANES]
        sq = xc * xc
        s_lo = jnp.sum(jnp.where(lane_lo, sq, 0.0), axis=-1, keepdims=True)
        s_hi = jnp.sum(jnp.where(lane_lo, 0.0, sq), axis=-1, keepdims=True)
        inv = jnp.where(lane_lo, lax.rsqrt(s_lo / HEAD_DIM + EPS), lax.rsqrt(s_hi / HEAD_DIM + EPS))
        outs.append(xc * inv)
    return jnp.concatenate(outs, axis=1) * w_tiled


def _dot(a, b):
    return jnp.dot(a, b, preferred_element_type=jnp.float32)


def _dot_nt(a, b):
    return lax.dot_general(a, b, (((1,), (1,)), ((), ())), preferred_element_type=jnp.float32)


def _mixer_kernel(x_ref, mixw_ref, win_ref, convw_ref, qw_ref, kw_ref, sink_ref, bias_ref,
                  cnw_ref, anw_ref, wout_ref, fnw_ref, wsg_ref, wsu_ref, wsd_ref,
                  wrh_ref, wrl_ref, rbias_ref, tri_ref,
                  x2_ref, xn_ref, wts_ref, ld_ref, cnt_ref,
                  kc_ref, vc_ref, cc_ref):
    j = pl.program_id(1)
    T = MIX_TILE
    bf16 = jnp.bfloat16
    f32 = jnp.float32

    @pl.when(j == 0)
    def _():
        kc_ref[...] = jnp.zeros_like(kc_ref)
        vc_ref[...] = jnp.zeros_like(vc_ref)
        cc_ref[...] = jnp.zeros_like(cc_ref)

    x = x_ref[0]
    u = _rms(x, mixw_ref[...]).astype(bf16)
    proj = _dot(u, win_ref[...])
    b_gate = proj[:, 0:512]
    c_gate = proj[:, 512:1024]
    hh = proj[:, 1024:1536]
    q = proj[:, 1536:2048]
    k = proj[:, 2048:2176]
    v = proj[:, 2176:2304]

    ch = c_gate * hh
    prev = cc_ref[...]
    p6 = prev[6:7, :]
    p7 = prev[7:8, :]
    row = lax.broadcasted_iota(jnp.int32, (T, D_CONV), 0)
    ch_m1 = jnp.where(row == 0, p7, pltpu.roll(ch, 1, axis=0))
    ch_m2 = jnp.where(row == 0, p6, jnp.where(row == 1, p7, pltpu.roll(ch, 2, axis=0)))
    cw = convw_ref[...]
    y_conv = b_gate * (cw[0:1, :] * ch_m2 + cw[1:2, :] * ch_m1 + cw[2:3, :] * ch)
    cc_ref[...] = ch[T - 8:T, :]

    lane = lax.broadcasted_iota(jnp.int32, (1, LANES), 1)
    lane_lo = lane < HEAD_DIM
    qn = _head_rms(q, qw_ref[...], lane_lo) * (HEAD_DIM ** -0.5)
    kn = _head_rms(k, kw_ref[...], lane_lo)
    kfull = jnp.concatenate([kc_ref[...], kn], axis=0)
    vfull = jnp.concatenate([vc_ref[...], v], axis=0)
    kc_ref[...] = kn[T - BLK:T, :]
    vc_ref[...] = v[T - BLK:T, :]

    def _rep(a, g):
        r = pltpu.roll(a, HEAD_DIM, axis=1)
        two = jnp.where(lane_lo, a, r) if g == 0 else jnp.where(lane_lo, r, a)
        return jnp.concatenate([two, two], axis=1).astype(bf16)

    k_rep = [_rep(kfull, g) for g in range(N_KV_HEADS)]
    v_rep = [_rep(vfull, g) for g in range(N_KV_HEADS)]

    lane256 = lax.broadcasted_iota(jnp.int32, (1, 2 * LANES), 1)
    head_of_lane = lane256 // HEAD_DIM
    first_f = (j == 0).astype(f32)
    prev_key_mask = jnp.where(lane256 < BLK, first_f * NEG_BIG, 0.0)

    attn_rows = []
    for i in range(T // BLK):
        grp_out = []
        for g in range(N_KV_HEADS):
            qg = qn[i * BLK:(i + 1) * BLK, g * 256:(g + 1) * 256]
            qst = jnp.concatenate(
                [jnp.where(head_of_lane == hi, qg, 0.0) for hi in range(GQA_GROUP)], axis=0).astype(bf16)
            kk = k_rep[g][i * BLK:i * BLK + 2 * BLK, :]
            vv = v_rep[g][i * BLK:i * BLK + 2 * BLK, :]
            s = _dot_nt(qst, kk) + bias_ref[g]
            if i == 0:
                s = s + prev_key_mask
            sink = sink_ref[g]
            m = jnp.maximum(jnp.max(s, axis=-1, keepdims=True), sink)
            e = jnp.exp(s - m)
            denom = jnp.sum(e, axis=-1, keepdims=True) + jnp.exp(sink - m)
            p = (e / denom).astype(bf16)
            r = _dot(p, vv)
            o = jnp.where(head_of_lane == 0, r[0:BLK], 0.0)
            for hi in range(1, GQA_GROUP):
                o = jnp.where(head_of_lane == hi, r[hi * BLK:(hi + 1) * BLK], o)
            grp_out.append(o)
        attn_rows.append(jnp.concatenate(grp_out, axis=1))
    y_attn = jnp.concatenate(attn_rows, axis=0)

    y_mix = jnp.concatenate([_rms(y_conv, cnw_ref[...]), _rms(y_attn, anw_ref[...])], axis=1)
    x1 = x + _dot(y_mix.astype(bf16), wout_ref[...])

    xn = _rms(x1, fnw_ref[...])
    for c in range(ROW_SUB):
        xn_ref[pl.ds(c, T, stride=ROW_SUB), :] = xn[:, c * LANES:(c + 1) * LANES]
    xh = xn.astype(bf16)
    gs = _dot(xh, wsg_ref[...])
    us = _dot(xh, wsu_ref[...])
    hs = (gs * jax.nn.sigmoid(gs) * us).astype(bf16)
    x2_ref[0] = x1 + _dot(hs, wsd_ref[...])

    xl = (xn - xh.astype(f32)).astype(bf16)
    logits = _dot_nt(wrh_ref[...], xh) + _dot_nt(wrh_ref[...], xl) + _dot_nt(wrl_ref[...], xh)
    scores = jax.nn.sigmoid(logits)
    biased = scores + rbias_ref[...]

    sub8 = lax.broadcasted_iota(jnp.int32, (GROUP_SIZE, T), 0)
    gscore = []
    for g in range(N_GROUPS):
        blk = biased[g * GROUP_SIZE:(g + 1) * GROUP_SIZE, :]
        m1 = jnp.max(blk, axis=0, keepdims=True)
        first = jnp.min(jnp.where(blk == m1, sub8, GROUP_SIZE), axis=0, keepdims=True)
        m2 = jnp.max(jnp.where(sub8 == first, -jnp.inf, blk), axis=0, keepdims=True)
        gscore.append(m1 + m2)
    masked_blocks = []
    for g in range(N_GROUPS):
        rank = jnp.zeros((1, T), jnp.int32)
        for o_ in range(N_GROUPS):
            if o_ == g:
                continue
            if o_ < g:
                ahead = gscore[o_] >= gscore[g]
            else:
                ahead = gscore[o_] > gscore[g]
            rank = rank + ahead.astype(jnp.int32)
        keep = rank < TOPK_GROUPS
        blk = biased[g * GROUP_SIZE:(g + 1) * GROUP_SIZE, :]
        masked_blocks.append(jnp.where(keep, blk, -jnp.inf))
    cur = jnp.concatenate(masked_blocks, axis=0)

    eiota = lax.broadcasted_iota(jnp.int32, (N_EXPERTS, T), 0)
    row8 = lax.broadcasted_iota(jnp.int32, (TOP_K, T), 0)
    w_out = jnp.zeros((TOP_K, T), f32)
    sel_dense = jnp.zeros((N_EXPERTS, T), f32)
    onehots = []
    for kk_ in range(TOP_K):
        mx = jnp.max(cur, axis=0, keepdims=True)
        sel_idx = jnp.min(jnp.where(cur == mx, eiota, N_EXPERTS), axis=0, keepdims=True)
        onehot = eiota == sel_idx
        w_k = jnp.sum(jnp.where(onehot, scores, 0.0), axis=0, keepdims=True)
        cur = jnp.where(onehot, -jnp.inf, cur)
        sel_dense = jnp.where(onehot, 1.0, sel_dense)
        w_out = jnp.where(row8 == kk_, w_k, w_out)
        onehots.append(onehot)
    wsum = jnp.sum(w_out, axis=0, keepdims=True)
    wts_ref[...] = w_out / wsum * ROUTED_SCALE

    cum = _dot(sel_dense.astype(bf16), tri_ref[...])
    cnt = jnp.broadcast_to(jnp.sum(sel_dense, axis=1, keepdims=True), (N_EXPERTS, LANES))
    erow = lax.broadcasted_iota(jnp.int32, (N_EXPERTS, LANES), 0)
    incl = cnt
    step = 1
    while step < N_EXPERTS:
        incl = incl + jnp.where(erow >= step, pltpu.roll(incl, step, axis=0), 0.0)
        step *= 2
    lstart = incl - cnt
    ld_dense = lstart[:, 0:1] + cum
    ld_out = jnp.zeros((TOP_K, T), f32)
    for kk_ in range(TOP_K):
        p_k = jnp.sum(jnp.where(onehots[kk_], ld_dense, 0.0), axis=0, keepdims=True)
        ld_out = jnp.where(row8 == kk_, p_k, ld_out)
    ld_ref[...] = ld_out.astype(jnp.int32)
    cnt_ref[...] = cnt.astype(jnp.int32)


def _mixer_call(x, mixw, win, convw, qw, kw, sink_col, bias_tab, cnw, anw, wout, fnw,
                wsg, wsu, wsd, wrh, wrl, rbias, tri):
    B, S, D = x.shape
    T = MIX_TILE
    nt = S // T
    N = B * S

    def full(a):
        nd = a.ndim
        return pl.BlockSpec(a.shape, lambda b, j, _nd=nd: (0,) * _nd)

    tok_spec = pl.BlockSpec((TOP_K, T), lambda b, j: (0, b * nt + j))
    in_arrays = [mixw, win, convw, qw, kw, sink_col, bias_tab, cnw, anw, wout, fnw,
                 wsg, wsu, wsd, wrh, wrl, rbias, tri]
    return pl.pallas_call(
        _mixer_kernel,
        grid=(B, nt),
        in_specs=[pl.BlockSpec((1, T, D), lambda b, j: (b, j, 0))] + [full(a) for a in in_arrays],
        out_specs=[
            pl.BlockSpec((1, T, D), lambda b, j: (b, j, 0)),
            pl.BlockSpec((T * ROW_SUB, LANES), lambda b, j: (b * nt + j, 0)),
            tok_spec, tok_spec,
            pl.BlockSpec((N_EXPERTS, LANES), lambda b, j: (b * nt + j, 0)),
        ],
        out_shape=[
            jax.ShapeDtypeStruct((B, S, D), jnp.float32),
            jax.ShapeDtypeStruct((N * ROW_SUB, LANES), jnp.float32),
            jax.ShapeDtypeStruct((TOP_K, N), jnp.float32),
            jax.ShapeDtypeStruct((TOP_K, N), jnp.int32),
            jax.ShapeDtypeStruct((B * nt * N_EXPERTS, LANES), jnp.int32),
        ],
        scratch_shapes=[
            pltpu.VMEM((BLK, D_KV), jnp.float32),
            pltpu.VMEM((BLK, D_KV), jnp.float32),
            pltpu.VMEM((8, D_CONV), jnp.float32),
        ],
        compiler_params=pltpu.CompilerParams(
            dimension_semantics=("arbitrary", "arbitrary"), vmem_limit_bytes=VMEM_LIMIT),
        name="mixer_router",
    )(x, *in_arrays)


def _rows(ref, row, nrows):
    return ref.at[pl.ds(pl.multiple_of(row * ROW_SUB, ROW_SUB), nrows * ROW_SUB)]


def _segment_copies(src, src_row, dst, dst_row, count, sem):
    chunk = 1 << SEG_CHUNK_LOG2

    def big(i, c):
        o = i * chunk
        pltpu.make_async_copy(_rows(src, src_row + o, chunk), _rows(dst, dst_row + o, chunk), sem).start()
        return c

    lax.fori_loop(0, count >> SEG_CHUNK_LOG2, big, 0)
    for bit in range(SEG_CHUNK_LOG2 - 1, -1, -1):
        o = (count >> (bit + 1)) << (bit + 1)

        @pl.when(((count >> bit) & 1) == 1)
        def _():
            n = 1 << bit
            pltpu.make_async_copy(_rows(src, src_row + o, n), _rows(dst, dst_row + o, n), sem).start()


def _wait_tile(hbm_ref, stag_ref, slot_row, sem, to_hbm):
    vm = _rows(stag_ref, slot_row, TILE_ROWS)
    hb = _rows(hbm_ref, 0, TILE_ROWS)
    (pltpu.make_async_copy(vm, hb, sem) if to_hbm else pltpu.make_async_copy(hb, vm, sem)).wait()


def _dispatch_kernel(lsrc_ref, gdst_ref, cnt_ref, ld_ref, xn_ref, xs_ref, stag_ref, sems):
    b = pl.program_id(0)
    nb = pl.num_programs(0)
    T = MIX_TILE
    slot = b & 1
    base = slot * TILE_ROWS

    for kk_ in range(TOP_K):
        def row(t, c, kk_=kk_):
            r = base + ld_ref[0, 0, kk_ * T + t]
            stag_ref[pl.ds(pl.multiple_of(r * ROW_SUB, ROW_SUB), ROW_SUB), :] = (
                xn_ref[pl.ds(pl.multiple_of(t * ROW_SUB, ROW_SUB), ROW_SUB), :])
            return c

        lax.fori_loop(0, T, row, 0, unroll=8)

    def seg(e, c):
        s = b * N_EXPERTS + e
        _segment_copies(stag_ref, base + lsrc_ref[s], xs_ref, gdst_ref[s], cnt_ref[s], sems.at[slot])
        return c

    lax.fori_loop(0, N_EXPERTS, seg, 0)

    @pl.when(b > 0)
    def _():
        _wait_tile(xs_ref, stag_ref, (1 - slot) * TILE_ROWS, sems.at[1 - slot], True)

    @pl.when(b == nb - 1)
    def _():
        _wait_tile(xs_ref, stag_ref, base, sems.at[slot], True)


def _dispatch_call(lsrc, gdst, cnt, ld_t, xn_rp):
    nt = ld_t.shape[0]
    T = MIX_TILE
    smem = pl.BlockSpec(memory_space=pltpu.SMEM)
    return pl.pallas_call(
        _dispatch_kernel,
        grid=(nt,),
        in_specs=[
            smem, smem, smem,
            pl.BlockSpec((1, 1, TILE_ROWS), lambda i: (i, 0, 0), memory_space=pltpu.SMEM),
            pl.BlockSpec((T * ROW_SUB, LANES), lambda i: (i, 0)),
        ],
        out_specs=pl.BlockSpec(memory_space=pl.ANY),
        out_shape=jax.ShapeDtypeStruct((nt * TILE_ROWS * ROW_SUB, LANES), jnp.float32),
        scratch_shapes=[pltpu.VMEM((2 * TILE_ROWS * ROW_SUB, LANES), jnp.float32),
                        pltpu.SemaphoreType.DMA((2,))],
        compiler_params=pltpu.CompilerParams(
            dimension_semantics=("arbitrary",), vmem_limit_bytes=VMEM_LIMIT),
        name="moe_dispatch",
    )(lsrc, gdst, cnt, ld_t, xn_rp)


def _expert_kernel(blk_ref, exp_ref, lo_ref, hi_ref, newexp_ref,
                   xs_ref, wg_ref, wu_ref, wd_ref, ys_ref, wgb_ref, wub_ref, wdb_ref):
    i = pl.program_id(0)
    bf16 = jnp.bfloat16
    SB = EXPERT_SUB
    lo = lo_ref[i]
    hi = hi_ref[i]

    @pl.when(newexp_ref[i] == 1)
    def _():
        wgb_ref[...] = wg_ref[0].astype(bf16)
        wub_ref[...] = wu_ref[0].astype(bf16)
        wdb_ref[...] = wd_ref[0].astype(bf16)

    for q in range(EXPERT_BLOCK // SB):
        r0 = q * SB

        def chunk(c, r0=r0):
            return pl.ds(r0 * ROW_SUB + c, SB, stride=ROW_SUB)

        @pl.when(jnp.minimum(hi, r0 + SB) > jnp.maximum(lo, r0))
        def _(r0=r0, chunk=chunk):
            xb = jnp.concatenate([xs_ref[chunk(c), :] for c in range(ROW_SUB)], axis=1).astype(bf16)
            g = _dot(xb, wgb_ref[...])
            u = _dot(xb, wub_ref[...])
            h = (g * jax.nn.sigmoid(g) * u).astype(bf16)
            y = _dot(h, wdb_ref[...])

            @pl.when(lo <= r0)
            def _():
                for c in range(ROW_SUB):
                    ys_ref[chunk(c), :] = y[:, c * LANES:(c + 1) * LANES]

            @pl.when(lo > r0)
            def _():
                rows = r0 + lax.broadcasted_iota(jnp.int32, (SB, LANES), 0)
                keep = jnp.logical_and(rows >= lo, rows < hi)
                for c in range(ROW_SUB):
                    old = ys_ref[chunk(c), :]
                    ys_ref[chunk(c), :] = jnp.where(keep, y[:, c * LANES:(c + 1) * LANES], old)


def _expert_call(item_blk, item_exp, item_lo, item_hi, item_newexp, xs_rp, w_eg, w_eu, w_ed):
    BM = EXPERT_BLOCK
    D = D_MODEL
    n_items = item_blk.shape[0]

    def row_map(i, blk, exp, lo, hi, newexp):
        return (blk[i], 0)

    def w_map(i, blk, exp, lo, hi, newexp):
        return (exp[i], 0, 0)

    grid_spec = pltpu.PrefetchScalarGridSpec(
        num_scalar_prefetch=5,
        grid=(n_items,),
        in_specs=[
            pl.BlockSpec((BM * ROW_SUB, LANES), row_map),
            pl.BlockSpec((1, D, D_EXPERT), w_map),
            pl.BlockSpec((1, D, D_EXPERT), w_map),
            pl.BlockSpec((1, D_EXPERT, D), w_map),
        ],
        out_specs=pl.BlockSpec((BM * ROW_SUB, LANES), row_map),
        scratch_shapes=[
            pltpu.VMEM((D, D_EXPERT), jnp.bfloat16),
            pltpu.VMEM((D, D_EXPERT), jnp.bfloat16),
            pltpu.VMEM((D_EXPERT, D), jnp.bfloat16),
        ],
    )
    return pl.pallas_call(
        _expert_kernel,
        grid_spec=grid_spec,
        out_shape=jax.ShapeDtypeStruct(xs_rp.shape, jnp.float32),
        compiler_params=pltpu.CompilerParams(
            dimension_semantics=("arbitrary",), vmem_limit_bytes=VMEM_LIMIT),
        name="moe_experts",
    )(item_blk, item_exp, item_lo, item_hi, item_newexp, xs_rp, w_eg, w_eu, w_ed)


def _combine_kernel(lsrc_ref, gdst_ref, cnt_ref, ld_ref, w_ref, x2_ref, ys_ref, out_ref,
                    stag_ref, acc_ref, sems):
    b = pl.program_id(0)
    nb = pl.num_programs(0)
    T = MIX_TILE
    slot = b & 1
    base = slot * TILE_ROWS

    def fetch(tile, slot_):
        def seg(e, c):
            s = tile * N_EXPERTS + e
            _segment_copies(ys_ref, gdst_ref[s], stag_ref, slot_ * TILE_ROWS + lsrc_ref[s], cnt_ref[s],
                            sems.at[slot_])
            return c

        lax.fori_loop(0, N_EXPERTS, seg, 0)

    @pl.when(b == 0)
    def _():
        fetch(b, slot)

    @pl.when(b + 1 < nb)
    def _():
        fetch(b + 1, 1 - slot)

    for c in range(ROW_SUB):
        acc_ref[pl.ds(c, T, stride=ROW_SUB), :] = x2_ref[:, c * LANES:(c + 1) * LANES]
    _wait_tile(ys_ref, stag_ref, base, sems.at[slot], False)

    def tok(t, c):
        at = pl.ds(pl.multiple_of(t * ROW_SUB, ROW_SUB), ROW_SUB)
        a = acc_ref[at, :]
        for kk_ in range(TOP_K):
            r = base + ld_ref[0, 0, kk_ * T + t]
            w = w_ref[0, 0, kk_ * T + t]
            a = a + w * stag_ref[pl.ds(pl.multiple_of(r * ROW_SUB, ROW_SUB), ROW_SUB), :]
        acc_ref[at, :] = a
        return c

    lax.fori_loop(0, T, tok, 0, unroll=2)
    for c in range(ROW_SUB):
        out_ref[:, c * LANES:(c + 1) * LANES] = acc_ref[pl.ds(c, T, stride=ROW_SUB), :]


def _combine_call(lsrc, gdst, cnt, ld_t, w_t, x2, ys_rp):
    N, D = x2.shape
    nt = ld_t.shape[0]
    T = MIX_TILE
    smem = pl.BlockSpec(memory_space=pltpu.SMEM)
    smem_blk = pl.BlockSpec((1, 1, TILE_ROWS), lambda i: (i, 0, 0), memory_space=pltpu.SMEM)
    return pl.pallas_call(
        _combine_kernel,
        grid=(nt,),
        in_specs=[
            smem, smem, smem, smem_blk, smem_blk,
            pl.BlockSpec((T, D), lambda i: (i, 0)),
            pl.BlockSpec(memory_space=pl.ANY),
        ],
        out_specs=pl.BlockSpec((T, D), lambda i: (i, 0)),
        out_shape=jax.ShapeDtypeStruct((N, D), jnp.float32),
        scratch_shapes=[
            pltpu.VMEM((2 * TILE_ROWS * ROW_SUB, LANES), jnp.float32),
            pltpu.VMEM((T * ROW_SUB, LANES), jnp.float32),
            pltpu.SemaphoreType.DMA((2,)),
        ],
        compiler_params=pltpu.CompilerParams(
            dimension_semantics=("arbitrary",), vmem_limit_bytes=VMEM_LIMIT),
        name="moe_combine",
    )(lsrc, gdst, cnt, ld_t, w_t, x2, ys_rp)


def _work_items(gstart, totals, n_items):
    BM = EXPERT_BLOCK
    i32 = jnp.int32
    gend = gstart + totals
    first_blk = gstart // BM
    last_blk = jnp.maximum(gend - 1, gstart) // BM
    n_e = jnp.where(totals > 0, last_blk - first_blk + 1, 0)
    item_end = jnp.cumsum(n_e)
    item_start = item_end - n_e
    n_real = item_end[-1]
    j = jnp.minimum(jnp.arange(n_items, dtype=i32), n_real - 1)
    exp = jnp.sum((item_end[None, :] <= j[:, None]).astype(i32), axis=1)
    onehot = exp[:, None] == jnp.arange(N_EXPERTS, dtype=i32)[None, :]

    def of_exp(v):
        return jnp.sum(jnp.where(onehot, v[None, :], 0), axis=1)

    blk = of_exp(first_blk) + (j - of_exp(item_start))
    lo = jnp.maximum(of_exp(gstart), blk * BM) - blk * BM
    hi = jnp.minimum(of_exp(gend), (blk + 1) * BM) - blk * BM
    real = jnp.arange(n_items, dtype=i32) < n_real
    hi = jnp.where(real, hi, lo)
    prev_exp = jnp.concatenate([jnp.full((1,), -1, i32), exp[:-1]])
    newexp = jnp.logical_and(real, exp != prev_exp).astype(i32)
    return blk.astype(i32), exp.astype(i32), lo.astype(i32), hi.astype(i32), newexp


def _layer(x, mix_norm_w, w_in, conv_w, q_norm_w, k_norm_w, sinks, conv_out_norm_w,
           attn_out_norm_w, w_out, ffn_norm_w, w_router, router_bias, w_eg, w_eu, w_ed,
           w_sg, w_su, w_sd):
    B, S, D = x.shape
    N = B * S
    T = MIX_TILE
    nt = N // T
    bf16 = jnp.bfloat16
    f32 = jnp.float32
    i32 = jnp.int32

    wr_t = w_router.astype(f32).T
    wr_hi = wr_t.astype(bf16)
    wr_lo = (wr_t - wr_hi.astype(f32)).astype(bf16)
    sink_col = jnp.repeat(sinks.astype(f32), BLK).reshape(N_KV_HEADS, GQA_GROUP * BLK, 1)
    tri = jnp.asarray(np.triu(np.ones((T, T), np.float32), k=1), dtype=bf16)
    bias_tab = jnp.asarray(_attn_bias_table())

    x2, xn_rp, wts, ld, cnt = _mixer_call(
        x, mix_norm_w.reshape(1, D), w_in.astype(bf16), conv_w,
        jnp.tile(q_norm_w, N_HEADS).reshape(1, D_ATTN), jnp.tile(k_norm_w, N_KV_HEADS).reshape(1, D_KV),
        sink_col, bias_tab, conv_out_norm_w.reshape(1, D_CONV), attn_out_norm_w.reshape(1, D_ATTN),
        w_out.astype(bf16), ffn_norm_w.reshape(1, D), w_sg.astype(bf16), w_su.astype(bf16),
        w_sd.astype(bf16), wr_hi, wr_lo, router_bias.astype(f32).reshape(N_EXPERTS, 1), tri)

    cnt_te = cnt[:, 0].reshape(nt, N_EXPERTS)
    lsrc = jnp.cumsum(cnt_te, axis=1) - cnt_te
    before = jnp.cumsum(cnt_te, axis=0) - cnt_te
    totals = jnp.sum(cnt_te, axis=0)
    gstart = jnp.cumsum(totals) - totals
    gdst = gstart[None, :] + before
    n_items = (N * TOP_K) // EXPERT_BLOCK + N_EXPERTS
    items = _work_items(gstart.astype(i32), totals.astype(i32), n_items)

    def tile_major(a):
        return a.reshape(TOP_K, nt, T).transpose(1, 0, 2).reshape(nt, 1, TILE_ROWS)

    ld_t = tile_major(ld)
    seg_tabs = (lsrc.reshape(-1).astype(i32), gdst.reshape(-1).astype(i32), cnt_te.reshape(-1).astype(i32))
    xs_rp = _dispatch_call(*seg_tabs, ld_t, xn_rp)
    ys_rp = _expert_call(*items, xs_rp, w_eg, w_eu, w_ed)
    out = _combine_call(*seg_tabs, ld_t, tile_major(wts), x2.reshape(N, D), ys_rp)
    return out.reshape(B, S, D)


def kernel(x, mix_norm_w, w_in, conv_w, q_norm_w, k_norm_w, sinks, conv_out_norm_w, attn_out_norm_w, w_out, ffn_norm_w, w_router, router_bias, w_exp_gate, w_exp_up, w_exp_down, w_sh_gate, w_sh_up, w_sh_down):
    return _layer(x, mix_norm_w[0], w_in[0], conv_w[0], q_norm_w[0], k_norm_w[0], sinks[0],
                  conv_out_norm_w[0], attn_out_norm_w[0], w_out[0], ffn_norm_w[0], w_router[0],
                  router_bias[0], w_exp_gate[0], w_exp_up[0], w_exp_down[0], w_sh_gate[0],
                  w_sh_up[0], w_sh_down[0])
```

```python
import numpy as np
import jax
import jax.numpy as jnp
from jax import lax
from jax.experimental import pallas as pl
from jax.experimental.pallas import tpu as pltpu

D_MODEL = 1024
EPS = 1e-6
D_CONV = 512
CONV_K = 3
HEAD_DIM = 64
N_HEADS = 8
N_KV_HEADS = 2
GQA_GROUP = 4
D_ATTN = 512
D_KV = 128
WINDOW = 128
BLK = 128
D_IN_PROJ = 2304
N_EXPERTS = 64
TOP_K = 8
N_GROUPS = 8
GROUP_SIZE = 8
TOPK_GROUPS = 4
D_EXPERT = 256
ROUTED_SCALE = 2.5

LANES = 128
ROW_SUB = 8
MIX_TILE = 512
TILE_ROWS = TOP_K * MIX_TILE
EXPERT_BLOCK = 1024
EXPERT_SUB = 256
SEG_CHUNK_LOG2 = 5
NEG_BIG = -1e30
VMEM_LIMIT = 58 * 1024 * 1024


def _alibi_slopes():
    return np.array([2.0 ** (-8.0 * (h + 1) / N_HEADS) for h in range(N_HEADS)], dtype=np.float32)


def _attn_bias_table():
    qi = np.arange(BLK)[:, None]
    kj = np.arange(2 * BLK)[None, :]
    dist = qi - kj + BLK
    inwin = (dist >= 0) & (dist < WINDOW)
    slopes = _alibi_slopes()
    out = np.zeros((N_KV_HEADS, GQA_GROUP * BLK, 2 * BLK), np.float32)
    for g in range(N_KV_HEADS):
        for i in range(GQA_GROUP):
            h = g * GQA_GROUP + i
            out[g, i * BLK:(i + 1) * BLK] = np.where(inwin, -slopes[h] * dist.astype(np.float32), NEG_BIG)
    return out


def _rms(x, w):
    ms = jnp.mean(x * x, axis=-1, keepdims=True)
    return x * lax.rsqrt(ms + EPS) * w


def _head_rms(x, w_tiled, head_ones):
    C = x.shape[1]
    sq = x * x
    hi = sq.astype(jnp.bfloat16)
    lo = (sq - hi.astype(jnp.float32)).astype(jnp.bfloat16)
    ones = head_ones[0:C, 0:C]
    ssq = _dot(hi, ones) + _dot(lo, ones)
    return x * lax.rsqrt(ssq * (1.0 / HEAD_DIM) + EPS) * w_tiled


def _dot(a, b):
    return jnp.dot(a, b, preferred_element_type=jnp.float32)


def _dot_nt(a, b):
    return lax.dot_general(a, b, (((1,), (1,)), ((), ())), preferred_element_type=jnp.float32)


def _mixer_kernel(x_ref, mixw_ref, win_ref, convw_ref, qw_ref, kw_ref, sink_ref, bias_ref,
                  cnw_ref, anw_ref, wout_ref, fnw_ref, wsg_ref, wsu_ref, wsd_ref,
                  wrh_ref, wrl_ref, rbias_ref, tri_ref, hones_ref,
                  x2_ref, xn_ref, wts_ref, ld_ref, cnt_ref,
                  kc_ref, vc_ref, cc_ref):
    j = pl.program_id(1)
    T = MIX_TILE
    bf16 = jnp.bfloat16
    f32 = jnp.float32

    @pl.when(j == 0)
    def _():
        kc_ref[...] = jnp.zeros_like(kc_ref)
        vc_ref[...] = jnp.zeros_like(vc_ref)
        cc_ref[...] = jnp.zeros_like(cc_ref)

    x = x_ref[0]
    u = _rms(x, mixw_ref[...]).astype(bf16)
    proj = _dot(u, win_ref[...])
    b_gate = proj[:, 0:512]
    c_gate = proj[:, 512:1024]
    hh = proj[:, 1024:1536]
    q = proj[:, 1536:2048]
    k = proj[:, 2048:2176]
    v = proj[:, 2176:2304]

    ch = c_gate * hh
    prev = cc_ref[...]
    p6 = prev[6:7, :]
    p7 = prev[7:8, :]
    row = lax.broadcasted_iota(jnp.int32, (T, D_CONV), 0)
    ch_m1 = jnp.where(row == 0, p7, pltpu.roll(ch, 1, axis=0))
    ch_m2 = jnp.where(row == 0, p6, jnp.where(row == 1, p7, pltpu.roll(ch, 2, axis=0)))
    cw = convw_ref[...]
    y_conv = b_gate * (cw[0:1, :] * ch_m2 + cw[1:2, :] * ch_m1 + cw[2:3, :] * ch)
    cc_ref[...] = ch[T - 8:T, :]

    lane = lax.broadcasted_iota(jnp.int32, (1, LANES), 1)
    lane_lo = lane < HEAD_DIM
    qn = _head_rms(q, qw_ref[...], hones_ref) * (HEAD_DIM ** -0.5)
    kn = _head_rms(k, kw_ref[...], hones_ref)
    kfull = jnp.concatenate([kc_ref[...], kn], axis=0)
    vfull = jnp.concatenate([vc_ref[...], v], axis=0)
    kc_ref[...] = kn[T - BLK:T, :]
    vc_ref[...] = v[T - BLK:T, :]

    def _rep(a, g):
        r = pltpu.roll(a, HEAD_DIM, axis=1)
        two = jnp.where(lane_lo, a, r) if g == 0 else jnp.where(lane_lo, r, a)
        return jnp.concatenate([two, two], axis=1).astype(bf16)

    k_rep = [_rep(kfull, g) for g in range(N_KV_HEADS)]
    v_rep = [_rep(vfull, g) for g in range(N_KV_HEADS)]

    lane256 = lax.broadcasted_iota(jnp.int32, (1, 2 * LANES), 1)
    head_of_lane = lane256 // HEAD_DIM
    first_f = (j == 0).astype(f32)
    prev_key_mask = jnp.where(lane256 < BLK, first_f * NEG_BIG, 0.0)
    ones_cols = jnp.ones((2 * BLK, LANES), bf16)

    def lanes2(a):
        return jnp.concatenate([a, a], axis=1)

    attn_rows = []
    for i in range(T // BLK):
        grp_out = []
        for g in range(N_KV_HEADS):
            qg = qn[i * BLK:(i + 1) * BLK, g * 256:(g + 1) * 256]
            qst = jnp.concatenate(
                [jnp.where(head_of_lane == hi, qg, 0.0) for hi in range(GQA_GROUP)], axis=0).astype(bf16)
            kk = k_rep[g][i * BLK:i * BLK + 2 * BLK, :]
            vv = jnp.concatenate([v_rep[g][i * BLK:i * BLK + 2 * BLK, :], ones_cols], axis=1)
            s = _dot_nt(qst, kk) + bias_ref[g]
            if i == 0:
                s = s + prev_key_mask
            sink = sink_ref[g]
            m = jnp.maximum(jnp.max(s, axis=-1, keepdims=True), sink)
            e = jnp.exp(s - lanes2(m)).astype(bf16)
            r = _dot(e, vv)
            inv = 1.0 / (r[:, 2 * LANES:3 * LANES] + jnp.exp(sink - m))
            r = r[:, 0:2 * LANES] * lanes2(inv)
            o = jnp.where(head_of_lane == 0, r[0:BLK], 0.0)
            for hi in range(1, GQA_GROUP):
                o = jnp.where(head_of_lane == hi, r[hi * BLK:(hi + 1) * BLK], o)
            grp_out.append(o)
        attn_rows.append(jnp.concatenate(grp_out, axis=1))
    y_attn = jnp.concatenate(attn_rows, axis=0)

    y_mix = jnp.concatenate([_rms(y_conv, cnw_ref[...]), _rms(y_attn, anw_ref[...])], axis=1)
    x1 = x + _dot(y_mix.astype(bf16), wout_ref[...])

    xn = _rms(x1, fnw_ref[...])
    for c in range(ROW_SUB):
        xn_ref[pl.ds(c, T, stride=ROW_SUB), :] = xn[:, c * LANES:(c + 1) * LANES]
    xh = xn.astype(bf16)
    gs = _dot(xh, wsg_ref[...])
    us = _dot(xh, wsu_ref[...])
    hs = (gs * jax.nn.sigmoid(gs) * us).astype(bf16)
    x2_ref[0] = x1 + _dot(hs, wsd_ref[...])

    xl = (xn - xh.astype(f32)).astype(bf16)
    logits = _dot_nt(wrh_ref[...], xh) + _dot_nt(wrh_ref[...], xl) + _dot_nt(wrl_ref[...], xh)
    scores = jax.nn.sigmoid(logits)
    biased = scores + rbias_ref[...]

    sub8 = lax.broadcasted_iota(jnp.int32, (GROUP_SIZE, T), 0)
    gscore = []
    for g in range(N_GROUPS):
        blk = biased[g * GROUP_SIZE:(g + 1) * GROUP_SIZE, :]
        m1 = jnp.max(blk, axis=0, keepdims=True)
        first = jnp.min(jnp.where(blk == m1, sub8, GROUP_SIZE), axis=0, keepdims=True)
        m2 = jnp.max(jnp.where(sub8 == first, -jnp.inf, blk), axis=0, keepdims=True)
        gscore.append(m1 + m2)
    masked_blocks = []
    for g in range(N_GROUPS):
        rank = jnp.zeros((1, T), jnp.int32)
        for o_ in range(N_GROUPS):
            if o_ == g:
                continue
            if o_ < g:
                ahead = gscore[o_] >= gscore[g]
            else:
                ahead = gscore[o_] > gscore[g]
            rank = rank + ahead.astype(jnp.int32)
        keep = rank < TOPK_GROUPS
        blk = biased[g * GROUP_SIZE:(g + 1) * GROUP_SIZE, :]
        masked_blocks.append(jnp.where(keep, blk, -jnp.inf))
    cur = jnp.concatenate(masked_blocks, axis=0)

    eiota = lax.broadcasted_iota(jnp.int32, (N_EXPERTS, T), 0)
    row8 = lax.broadcasted_iota(jnp.int32, (TOP_K, T), 0)
    w_out = jnp.zeros((TOP_K, T), f32)
    sel_dense = jnp.zeros((N_EXPERTS, T), f32)
    onehots = []
    for kk_ in range(TOP_K):
        mx = jnp.max(cur, axis=0, keepdims=True)
        sel_idx = jnp.min(jnp.where(cur == mx, eiota, N_EXPERTS), axis=0, keepdims=True)
        onehot = eiota == sel_idx
        w_k = jnp.sum(jnp.where(onehot, scores, 0.0), axis=0, keepdims=True)
        cur = jnp.where(onehot, -jnp.inf, cur)
        sel_dense = jnp.where(onehot, 1.0, sel_dense)
        w_out = jnp.where(row8 == kk_, w_k, w_out)
        onehots.append(onehot)
    wsum = jnp.sum(w_out, axis=0, keepdims=True)
    wts_ref[...] = w_out / wsum * ROUTED_SCALE

    cum = _dot(sel_dense.astype(bf16), tri_ref[...])
    cnt = jnp.broadcast_to(jnp.sum(sel_dense, axis=1, keepdims=True), (N_EXPERTS, LANES))
    erow = lax.broadcasted_iota(jnp.int32, (N_EXPERTS, LANES), 0)
    incl = cnt
    step = 1
    while step < N_EXPERTS:
        incl = incl + jnp.where(erow >= step, pltpu.roll(incl, step, axis=0), 0.0)
        step *= 2
    lstart = incl - cnt
    ld_dense = lstart[:, 0:1] + cum
    ld_out = jnp.zeros((TOP_K, T), f32)
    for kk_ in range(TOP_K):
        p_k = jnp.sum(jnp.where(onehots[kk_], ld_dense, 0.0), axis=0, keepdims=True)
        ld_out = jnp.where(row8 == kk_, p_k, ld_out)
    ld_ref[...] = ld_out.astype(jnp.int32)
    cnt_ref[...] = cnt.astype(jnp.int32)


def _mixer_call(x, mixw, win, convw, qw, kw, sink_col, bias_tab, cnw, anw, wout, fnw,
                wsg, wsu, wsd, wrh, wrl, rbias, tri, head_ones):
    B, S, D = x.shape
    T = MIX_TILE
    nt = S // T
    N = B * S

    def full(a):
        nd = a.ndim
        return pl.BlockSpec(a.shape, lambda b, j, _nd=nd: (0,) * _nd)

    tok_spec = pl.BlockSpec((TOP_K, T), lambda b, j: (0, b * nt + j))
    in_arrays = [mixw, win, convw, qw, kw, sink_col, bias_tab, cnw, anw, wout, fnw,
                 wsg, wsu, wsd, wrh, wrl, rbias, tri, head_ones]
    return pl.pallas_call(
        _mixer_kernel,
        grid=(B, nt),
        in_specs=[pl.BlockSpec((1, T, D), lambda b, j: (b, j, 0))] + [full(a) for a in in_arrays],
        out_specs=[
            pl.BlockSpec((1, T, D), lambda b, j: (b, j, 0)),
            pl.BlockSpec((T * ROW_SUB, LANES), lambda b, j: (b * nt + j, 0)),
            tok_spec, tok_spec,
            pl.BlockSpec((N_EXPERTS, LANES), lambda b, j: (b * nt + j, 0)),
        ],
        out_shape=[
            jax.ShapeDtypeStruct((B, S, D), jnp.float32),
            jax.ShapeDtypeStruct((N * ROW_SUB, LANES), jnp.float32),
            jax.ShapeDtypeStruct((TOP_K, N), jnp.float32),
            jax.ShapeDtypeStruct((TOP_K, N), jnp.int32),
            jax.ShapeDtypeStruct((B * nt * N_EXPERTS, LANES), jnp.int32),
        ],
        scratch_shapes=[
            pltpu.VMEM((BLK, D_KV), jnp.float32),
            pltpu.VMEM((BLK, D_KV), jnp.float32),
            pltpu.VMEM((8, D_CONV), jnp.float32),
        ],
        compiler_params=pltpu.CompilerParams(
            dimension_semantics=("arbitrary", "arbitrary"), vmem_limit_bytes=VMEM_LIMIT),
        name="mixer_router",
    )(x, *in_arrays)


def _rows(ref, row, nrows):
    return ref.at[pl.ds(pl.multiple_of(row * ROW_SUB, ROW_SUB), nrows * ROW_SUB)]


def _segment_copies(src, src_row, dst, dst_row, count, sem):
    chunk = 1 << SEG_CHUNK_LOG2

    def big(i, c):
        o = i * chunk
        pltpu.make_async_copy(_rows(src, src_row + o, chunk), _rows(dst, dst_row + o, chunk), sem).start()
        return c

    lax.fori_loop(0, count >> SEG_CHUNK_LOG2, big, 0)
    for bit in range(SEG_CHUNK_LOG2 - 1, -1, -1):
        o = (count >> (bit + 1)) << (bit + 1)

        @pl.when(((count >> bit) & 1) == 1)
        def _():
            n = 1 << bit
            pltpu.make_async_copy(_rows(src, src_row + o, n), _rows(dst, dst_row + o, n), sem).start()


def _wait_tile(hbm_ref, stag_ref, slot_row, sem, to_hbm):
    vm = _rows(stag_ref, slot_row, TILE_ROWS)
    hb = _rows(hbm_ref, 0, TILE_ROWS)
    (pltpu.make_async_copy(vm, hb, sem) if to_hbm else pltpu.make_async_copy(hb, vm, sem)).wait()


def _dispatch_kernel(lsrc_ref, gdst_ref, cnt_ref, ld_ref, xn_ref, xs_ref, stag_ref, sems):
    b = pl.program_id(0)
    nb = pl.num_programs(0)
    T = MIX_TILE
    slot = b & 1
    base = slot * TILE_ROWS

    for kk_ in range(TOP_K):
        def row(t, c, kk_=kk_):
            r = base + ld_ref[0, 0, kk_ * T + t]
            stag_ref[pl.ds(pl.multiple_of(r * ROW_SUB, ROW_SUB), ROW_SUB), :] = (
                xn_ref[pl.ds(pl.multiple_of(t * ROW_SUB, ROW_SUB), ROW_SUB), :])
            return c

        lax.fori_loop(0, T, row, 0, unroll=8)

    def seg(e, c):
        s = b * N_EXPERTS + e
        _segment_copies(stag_ref, base + lsrc_ref[s], xs_ref, gdst_ref[s], cnt_ref[s], sems.at[slot])
        return c

    lax.fori_loop(0, N_EXPERTS, seg, 0)

    @pl.when(b > 0)
    def _():
        _wait_tile(xs_ref, stag_ref, (1 - slot) * TILE_ROWS, sems.at[1 - slot], True)

    @pl.when(b == nb - 1)
    def _():
        _wait_tile(xs_ref, stag_ref, base, sems.at[slot], True)


def _dispatch_call(lsrc, gdst, cnt, ld_t, xn_rp):
    nt = ld_t.shape[0]
    T = MIX_TILE
    smem = pl.BlockSpec(memory_space=pltpu.SMEM)
    return pl.pallas_call(
        _dispatch_kernel,
        grid=(nt,),
        in_specs=[
            smem, smem, smem,
            pl.BlockSpec((1, 1, TILE_ROWS), lambda i: (i, 0, 0), memory_space=pltpu.SMEM),
            pl.BlockSpec((T * ROW_SUB, LANES), lambda i: (i, 0)),
        ],
        out_specs=pl.BlockSpec(memory_space=pl.ANY),
        out_shape=jax.ShapeDtypeStruct((nt * TILE_ROWS * ROW_SUB, LANES), jnp.float32),
        scratch_shapes=[pltpu.VMEM((2 * TILE_ROWS * ROW_SUB, LANES), jnp.float32),
                        pltpu.SemaphoreType.DMA((2,))],
        compiler_params=pltpu.CompilerParams(
            dimension_semantics=("arbitrary",), vmem_limit_bytes=VMEM_LIMIT),
        name="moe_dispatch",
    )(lsrc, gdst, cnt, ld_t, xn_rp)


def _expert_kernel(blk_ref, exp_ref, lo_ref, hi_ref, newexp_ref,
                   xs_ref, wg_ref, wu_ref, wd_ref, ys_ref, wgb_ref, wub_ref, wdb_ref):
    i = pl.program_id(0)
    bf16 = jnp.bfloat16
    SB = EXPERT_SUB
    lo = lo_ref[i]
    hi = hi_ref[i]

    @pl.when(newexp_ref[i] == 1)
    def _():
        wgb_ref[...] = wg_ref[0].astype(bf16)
        wub_ref[...] = wu_ref[0].astype(bf16)
        wdb_ref[...] = wd_ref[0].astype(bf16)

    def ffn(chunk):
        xb = jnp.concatenate([xs_ref[chunk(c), :] for c in range(ROW_SUB)], axis=1).astype(bf16)
        g = _dot(xb, wgb_ref[...])
        u = _dot(xb, wub_ref[...])
        h = (g * jax.nn.sigmoid(g) * u).astype(bf16)
        return _dot(h, wdb_ref[...])

    whole = jnp.logical_and(lo == 0, hi == EXPERT_BLOCK)

    @pl.when(whole)
    def _():
        def chunk(c):
            return pl.ds(c, EXPERT_BLOCK, stride=ROW_SUB)

        y = ffn(chunk)
        for c in range(ROW_SUB):
            ys_ref[chunk(c), :] = y[:, c * LANES:(c + 1) * LANES]

    for q in range(EXPERT_BLOCK // SB):
        r0 = q * SB

        def chunk(c, r0=r0):
            return pl.ds(r0 * ROW_SUB + c, SB, stride=ROW_SUB)

        @pl.when(jnp.logical_and(jnp.logical_not(whole), jnp.minimum(hi, r0 + SB) > jnp.maximum(lo, r0)))
        def _(r0=r0, chunk=chunk):
            y = ffn(chunk)

            @pl.when(lo <= r0)
            def _():
                for c in range(ROW_SUB):
                    ys_ref[chunk(c), :] = y[:, c * LANES:(c + 1) * LANES]

            @pl.when(lo > r0)
            def _():
                rows = r0 + lax.broadcasted_iota(jnp.int32, (SB, LANES), 0)
                keep = jnp.logical_and(rows >= lo, rows < hi)
                for c in range(ROW_SUB):
                    old = ys_ref[chunk(c), :]
                    ys_ref[chunk(c), :] = jnp.where(keep, y[:, c * LANES:(c + 1) * LANES], old)


def _expert_call(item_blk, item_exp, item_lo, item_hi, item_newexp, xs_rp, w_eg, w_eu, w_ed):
    BM = EXPERT_BLOCK
    D = D_MODEL
    n_items = item_blk.shape[0]

    def row_map(i, blk, exp, lo, hi, newexp):
        return (blk[i], 0)

    def w_map(i, blk, exp, lo, hi, newexp):
        return (exp[i], 0, 0)

    grid_spec = pltpu.PrefetchScalarGridSpec(
        num_scalar_prefetch=5,
        grid=(n_items,),
        in_specs=[
            pl.BlockSpec((BM * ROW_SUB, LANES), row_map),
            pl.BlockSpec((1, D, D_EXPERT), w_map),
            pl.BlockSpec((1, D, D_EXPERT), w_map),
            pl.BlockSpec((1, D_EXPERT, D), w_map),
        ],
        out_specs=pl.BlockSpec((BM * ROW_SUB, LANES), row_map),
        scratch_shapes=[
            pltpu.VMEM((D, D_EXPERT), jnp.bfloat16),
            pltpu.VMEM((D, D_EXPERT), jnp.bfloat16),
            pltpu.VMEM((D_EXPERT, D), jnp.bfloat16),
        ],
    )
    return pl.pallas_call(
        _expert_kernel,
        grid_spec=grid_spec,
        out_shape=jax.ShapeDtypeStruct(xs_rp.shape, jnp.float32),
        compiler_params=pltpu.CompilerParams(
            dimension_semantics=("arbitrary",), vmem_limit_bytes=VMEM_LIMIT),
        name="moe_experts",
    )(item_blk, item_exp, item_lo, item_hi, item_newexp, xs_rp, w_eg, w_eu, w_ed)


def _combine_kernel(lsrc_ref, gdst_ref, cnt_ref, ld_ref, w_ref, x2_ref, ys_ref, out_ref,
                    stag_ref, acc_ref, sems):
    b = pl.program_id(0)
    nb = pl.num_programs(0)
    T = MIX_TILE
    slot = b & 1
    base = slot * TILE_ROWS

    def fetch(tile, slot_):
        def seg(e, c):
            s = tile * N_EXPERTS + e
            _segment_copies(ys_ref, gdst_ref[s], stag_ref, slot_ * TILE_ROWS + lsrc_ref[s], cnt_ref[s],
                            sems.at[slot_])
            return c

        lax.fori_loop(0, N_EXPERTS, seg, 0)

    @pl.when(b == 0)
    def _():
        fetch(b, slot)

    @pl.when(b + 1 < nb)
    def _():
        fetch(b + 1, 1 - slot)

    for c in range(ROW_SUB):
        acc_ref[pl.ds(c, T, stride=ROW_SUB), :] = x2_ref[:, c * LANES:(c + 1) * LANES]
    _wait_tile(ys_ref, stag_ref, base, sems.at[slot], False)

    def tok(t, c):
        at = pl.ds(pl.multiple_of(t * ROW_SUB, ROW_SUB), ROW_SUB)
        a = acc_ref[at, :]
        for kk_ in range(TOP_K):
            r = base + ld_ref[0, 0, kk_ * T + t]
            w = w_ref[0, 0, kk_ * T + t]
            a = a + w * stag_ref[pl.ds(pl.multiple_of(r * ROW_SUB, ROW_SUB), ROW_SUB), :]
        acc_ref[at, :] = a
        return c

    lax.fori_loop(0, T, tok, 0, unroll=2)
    for c in range(ROW_SUB):
        out_ref[:, c * LANES:(c + 1) * LANES] = acc_ref[pl.ds(c, T, stride=ROW_SUB), :]


def _combine_call(lsrc, gdst, cnt, ld_t, w_t, x2, ys_rp):
    N, D = x2.shape
    nt = ld_t.shape[0]
    T = MIX_TILE
    smem = pl.BlockSpec(memory_space=pltpu.SMEM)
    smem_blk = pl.BlockSpec((1, 1, TILE_ROWS), lambda i: (i, 0, 0), memory_space=pltpu.SMEM)
    return pl.pallas_call(
        _combine_kernel,
        grid=(nt,),
        in_specs=[
            smem, smem, smem, smem_blk, smem_blk,
            pl.BlockSpec((T, D), lambda i: (i, 0)),
            pl.BlockSpec(memory_space=pl.ANY),
        ],
        out_specs=pl.BlockSpec((T, D), lambda i: (i, 0)),
        out_shape=jax.ShapeDtypeStruct((N, D), jnp.float32),
        scratch_shapes=[
            pltpu.VMEM((2 * TILE_ROWS * ROW_SUB, LANES), jnp.float32),
            pltpu.VMEM((T * ROW_SUB, LANES), jnp.float32),
            pltpu.SemaphoreType.DMA((2,)),
        ],
        compiler_params=pltpu.CompilerParams(
            dimension_semantics=("arbitrary",), vmem_limit_bytes=VMEM_LIMIT),
        name="moe_combine",
    )(lsrc, gdst, cnt, ld_t, w_t, x2, ys_rp)


def _work_items(gstart, totals, n_items):
    BM = EXPERT_BLOCK
    i32 = jnp.int32
    gend = gstart + totals
    first_blk = gstart // BM
    last_blk = jnp.maximum(gend - 1, gstart) // BM
    n_e = jnp.where(totals > 0, last_blk - first_blk + 1, 0)
    item_end = jnp.cumsum(n_e)
    item_start = item_end - n_e
    n_real = item_end[-1]
    j = jnp.minimum(jnp.arange(n_items, dtype=i32), n_real - 1)
    exp = jnp.sum((item_end[None, :] <= j[:, None]).astype(i32), axis=1)
    onehot = exp[:, None] == jnp.arange(N_EXPERTS, dtype=i32)[None, :]

    def of_exp(v):
        return jnp.sum(jnp.where(onehot, v[None, :], 0), axis=1)

    blk = of_exp(first_blk) + (j - of_exp(item_start))
    lo = jnp.maximum(of_exp(gstart), blk * BM) - blk * BM
    hi = jnp.minimum(of_exp(gend), (blk + 1) * BM) - blk * BM
    real = jnp.arange(n_items, dtype=i32) < n_real
    hi = jnp.where(real, hi, lo)
    prev_exp = jnp.concatenate([jnp.full((1,), -1, i32), exp[:-1]])
    newexp = jnp.logical_and(real, exp != prev_exp).astype(i32)
    return blk.astype(i32), exp.astype(i32), lo.astype(i32), hi.astype(i32), newexp


def _layer(x, mix_norm_w, w_in, conv_w, q_norm_w, k_norm_w, sinks, conv_out_norm_w,
           attn_out_norm_w, w_out, ffn_norm_w, w_router, router_bias, w_eg, w_eu, w_ed,
           w_sg, w_su, w_sd):
    B, S, D = x.shape
    N = B * S
    T = MIX_TILE
    nt = N // T
    bf16 = jnp.bfloat16
    f32 = jnp.float32
    i32 = jnp.int32

    wr_t = w_router.astype(f32).T
    wr_hi = wr_t.astype(bf16)
    wr_lo = (wr_t - wr_hi.astype(f32)).astype(bf16)
    sink_col = jnp.broadcast_to(
        jnp.repeat(sinks.astype(f32), BLK).reshape(N_KV_HEADS, GQA_GROUP * BLK, 1),
        (N_KV_HEADS, GQA_GROUP * BLK, LANES))
    tri = jnp.asarray(np.triu(np.ones((T, T), np.float32), k=1), dtype=bf16)
    head_ones = jnp.asarray(np.kron(np.eye(D_ATTN // HEAD_DIM, dtype=np.float32),
                                    np.ones((HEAD_DIM, HEAD_DIM), np.float32)), dtype=bf16)
    bias_tab = jnp.asarray(_attn_bias_table())

    x2, xn_rp, wts, ld, cnt = _mixer_call(
        x, mix_norm_w.reshape(1, D), w_in.astype(bf16), conv_w,
        jnp.tile(q_norm_w, N_HEADS).reshape(1, D_ATTN), jnp.tile(k_norm_w, N_KV_HEADS).reshape(1, D_KV),
        sink_col, bias_tab, conv_out_norm_w.reshape(1, D_CONV), attn_out_norm_w.reshape(1, D_ATTN),
        w_out.astype(bf16), ffn_norm_w.reshape(1, D), w_sg.astype(bf16), w_su.astype(bf16),
        w_sd.astype(bf16), wr_hi, wr_lo, router_bias.astype(f32).reshape(N_EXPERTS, 1), tri, head_ones)

    cnt_te = cnt[:, 0].reshape(nt, N_EXPERTS)
    lsrc = jnp.cumsum(cnt_te, axis=1) - cnt_te
    before = jnp.cumsum(cnt_te, axis=0) - cnt_te
    totals = jnp.sum(cnt_te, axis=0)
    gstart = jnp.cumsum(totals) - totals
    gdst = gstart[None, :] + before
    n_items = (N * TOP_K) // EXPERT_BLOCK + N_EXPERTS
    items = _work_items(gstart.astype(i32), totals.astype(i32), n_items)

    def tile_major(a):
        return a.reshape(TOP_K, nt, T).transpose(1, 0, 2).reshape(nt, 1, TILE_ROWS)

    ld_t = tile_major(ld)
    seg_tabs = (lsrc.reshape(-1).astype(i32), gdst.reshape(-1).astype(i32), cnt_te.reshape(-1).astype(i32))
    xs_rp = _dispatch_call(*seg_tabs, ld_t, xn_rp)
    ys_rp = _expert_call(*items, xs_rp, w_eg, w_eu, w_ed)
    out = _combine_call(*seg_tabs, ld_t, tile_major(wts), x2.reshape(N, D), ys_rp)
    return out.reshape(B, S, D)


def kernel(x, mix_norm_w, w_in, conv_w, q_norm_w, k_norm_w, sinks, conv_out_norm_w, attn_out_norm_w, w_out, ffn_norm_w, w_router, router_bias, w_exp_gate, w_exp_up, w_exp_down, w_sh_gate, w_sh_up, w_sh_down):
    return _layer(x, mix_norm_w[0], w_in[0], conv_w[0], q_norm_w[0], k_norm_w[0], sinks[0],
                  conv_out_norm_w[0], attn_out_norm_w[0], w_out[0], ffn_norm_w[0], w_router[0],
                  router_bias[0], w_exp_gate[0], w_exp_up[0], w_exp_down[0], w_sh_gate[0],
                  w_sh_up[0], w_sh_down[0])
```

```python
import numpy as np
import jax
import jax.numpy as jnp
from jax import lax
from jax.experimental import pallas as pl
from jax.experimental.pallas import tpu as pltpu

D_MODEL = 1024
EPS = 1e-6
D_CONV = 512
CONV_K = 3
HEAD_DIM = 64
N_HEADS = 8
N_KV_HEADS = 2
GQA_GROUP = 4
D_ATTN = 512
D_KV = 128
WINDOW = 128
BLK = 128
D_IN_PROJ = 2304
N_EXPERTS = 64
TOP_K = 8
N_GROUPS = 8
GROUP_SIZE = 8
TOPK_GROUPS = 4
D_EXPERT = 256
ROUTED_SCALE = 2.5

LANES = 128
ROW_SUB = 8
MIX_TILE = 512
TILE_ROWS = TOP_K * MIX_TILE
EXPERT_BLOCK = 1024
EXPERT_SUB = 256
SEG_CHUNK_LOG2 = 5
NEG_BIG = -1e30
VMEM_LIMIT = 58 * 1024 * 1024


def _alibi_slopes():
    return np.array([2.0 ** (-8.0 * (h + 1) / N_HEADS) for h in range(N_HEADS)], dtype=np.float32)


def _attn_bias_table():
    qi = np.arange(BLK)[:, None]
    kj = np.arange(2 * BLK)[None, :]
    dist = qi - kj + BLK
    inwin = (dist >= 0) & (dist < WINDOW)
    slopes = _alibi_slopes()
    out = np.zeros((N_KV_HEADS, GQA_GROUP * BLK, 2 * BLK), np.float32)
    for g in range(N_KV_HEADS):
        for i in range(GQA_GROUP):
            h = g * GQA_GROUP + i
            out[g, i * BLK:(i + 1) * BLK] = np.where(inwin, -slopes[h] * dist.astype(np.float32), NEG_BIG)
    return out


def _rms(x, w):
    ms = jnp.mean(x * x, axis=-1, keepdims=True)
    return x * lax.rsqrt(ms + EPS) * w


def _head_rms(x, w_tiled, head_ones):
    C = x.shape[1]
    sq = x * x
    hi = sq.astype(jnp.bfloat16)
    lo = (sq - hi.astype(jnp.float32)).astype(jnp.bfloat16)
    ones = head_ones[0:C, 0:C]
    ssq = _dot(hi, ones) + _dot(lo, ones)
    return x * lax.rsqrt(ssq * (1.0 / HEAD_DIM) + EPS) * w_tiled


def _dot(a, b):
    return jnp.dot(a, b, preferred_element_type=jnp.float32)


def _dot_nt(a, b):
    return lax.dot_general(a, b, (((1,), (1,)), ((), ())), preferred_element_type=jnp.float32)


def _mixer_kernel(x_ref, mixw_ref, win_ref, convw_ref, qw_ref, kw_ref, sink_ref, bias_ref,
                  cnw_ref, anw_ref, wout_ref, fnw_ref, wsg_ref, wsu_ref, wsd_ref,
                  wrh_ref, wrl_ref, rbias_ref, tri_ref, hones_ref,
                  x2_ref, xn_ref, wts_ref, ld_ref, cnt_ref,
                  kc_ref, vc_ref, cc_ref):
    j = pl.program_id(1)
    T = MIX_TILE
    bf16 = jnp.bfloat16
    f32 = jnp.float32

    @pl.when(j == 0)
    def _():
        kc_ref[...] = jnp.zeros_like(kc_ref)
        vc_ref[...] = jnp.zeros_like(vc_ref)
        cc_ref[...] = jnp.zeros_like(cc_ref)

    x = x_ref[0]
    u = _rms(x, mixw_ref[...]).astype(bf16)
    proj = _dot(u, win_ref[...])
    b_gate = proj[:, 0:512]
    c_gate = proj[:, 512:1024]
    hh = proj[:, 1024:1536]
    q = proj[:, 1536:2048]
    k = proj[:, 2048:2176]
    v = proj[:, 2176:2304]

    ch = c_gate * hh
    prev = cc_ref[...]
    p6 = prev[6:7, :]
    p7 = prev[7:8, :]
    row = lax.broadcasted_iota(jnp.int32, (T, D_CONV), 0)
    ch_m1 = jnp.where(row == 0, p7, pltpu.roll(ch, 1, axis=0))
    ch_m2 = jnp.where(row == 0, p6, jnp.where(row == 1, p7, pltpu.roll(ch, 2, axis=0)))
    cw = convw_ref[...]
    y_conv = b_gate * (cw[0:1, :] * ch_m2 + cw[1:2, :] * ch_m1 + cw[2:3, :] * ch)
    cc_ref[...] = ch[T - 8:T, :]

    lane = lax.broadcasted_iota(jnp.int32, (1, LANES), 1)
    lane_lo = lane < HEAD_DIM
    qn = _head_rms(q, qw_ref[...], hones_ref) * (HEAD_DIM ** -0.5)
    kn = _head_rms(k, kw_ref[...], hones_ref)
    kfull = jnp.concatenate([kc_ref[...], kn], axis=0)
    vfull = jnp.concatenate([vc_ref[...], v], axis=0)
    kc_ref[...] = kn[T - BLK:T, :]
    vc_ref[...] = v[T - BLK:T, :]

    def _rep(a, g):
        r = pltpu.roll(a, HEAD_DIM, axis=1)
        two = jnp.where(lane_lo, a, r) if g == 0 else jnp.where(lane_lo, r, a)
        return jnp.concatenate([two, two], axis=1).astype(bf16)

    k_rep = [_rep(kfull, g) for g in range(N_KV_HEADS)]
    v_rep = [_rep(vfull, g) for g in range(N_KV_HEADS)]

    lane256 = lax.broadcasted_iota(jnp.int32, (1, 2 * LANES), 1)
    head_of_lane = lane256 // HEAD_DIM
    first_f = (j == 0).astype(f32)
    prev_key_mask = jnp.where(lane256 < BLK, first_f * NEG_BIG, 0.0)
    ones_cols = jnp.ones((2 * BLK, LANES), bf16)

    def lanes2(a):
        return jnp.concatenate([a, a], axis=1)

    attn_rows = []
    for i in range(T // BLK):
        grp_out = []
        for g in range(N_KV_HEADS):
            qg = qn[i * BLK:(i + 1) * BLK, g * 256:(g + 1) * 256]
            qst = jnp.concatenate(
                [jnp.where(head_of_lane == hi, qg, 0.0) for hi in range(GQA_GROUP)], axis=0).astype(bf16)
            kk = k_rep[g][i * BLK:i * BLK + 2 * BLK, :]
            vv = jnp.concatenate([v_rep[g][i * BLK:i * BLK + 2 * BLK, :], ones_cols], axis=1)
            s = _dot_nt(qst, kk) + bias_ref[g]
            if i == 0:
                s = s + prev_key_mask
            sink = sink_ref[g]
            m = jnp.maximum(jnp.max(s, axis=-1, keepdims=True), sink)
            e = jnp.exp(s - lanes2(m)).astype(bf16)
            r = _dot(e, vv)
            inv = 1.0 / (r[:, 2 * LANES:3 * LANES] + jnp.exp(sink - m))
            r = r[:, 0:2 * LANES] * lanes2(inv)
            o = jnp.where(head_of_lane == 0, r[0:BLK], 0.0)
            for hi in range(1, GQA_GROUP):
                o = jnp.where(head_of_lane == hi, r[hi * BLK:(hi + 1) * BLK], o)
            grp_out.append(o)
        attn_rows.append(jnp.concatenate(grp_out, axis=1))
    y_attn = jnp.concatenate(attn_rows, axis=0)

    y_mix = jnp.concatenate([_rms(y_conv, cnw_ref[...]), _rms(y_attn, anw_ref[...])], axis=1)
    x1 = x + _dot(y_mix.astype(bf16), wout_ref[...])

    xn = _rms(x1, fnw_ref[...])
    for c in range(ROW_SUB):
        xn_ref[pl.ds(c, T, stride=ROW_SUB), :] = xn[:, c * LANES:(c + 1) * LANES]
    xh = xn.astype(bf16)
    gs = _dot(xh, wsg_ref[...])
    us = _dot(xh, wsu_ref[...])
    hs = (gs * jax.nn.sigmoid(gs) * us).astype(bf16)
    x2 = x1 + _dot(hs, wsd_ref[...])
    for c in range(ROW_SUB):
        x2_ref[pl.ds(c, T, stride=ROW_SUB), :] = x2[:, c * LANES:(c + 1) * LANES]

    xl = (xn - xh.astype(f32)).astype(bf16)
    logits = _dot_nt(wrh_ref[...], xh) + _dot_nt(wrh_ref[...], xl) + _dot_nt(wrl_ref[...], xh)
    scores = jax.nn.sigmoid(logits)
    biased = scores + rbias_ref[...]

    sub8 = lax.broadcasted_iota(jnp.int32, (GROUP_SIZE, T), 0)
    gscore = []
    for g in range(N_GROUPS):
        blk = biased[g * GROUP_SIZE:(g + 1) * GROUP_SIZE, :]
        m1 = jnp.max(blk, axis=0, keepdims=True)
        first = jnp.min(jnp.where(blk == m1, sub8, GROUP_SIZE), axis=0, keepdims=True)
        m2 = jnp.max(jnp.where(sub8 == first, -jnp.inf, blk), axis=0, keepdims=True)
        gscore.append(m1 + m2)
    masked_blocks = []
    for g in range(N_GROUPS):
        rank = jnp.zeros((1, T), jnp.int32)
        for o_ in range(N_GROUPS):
            if o_ == g:
                continue
            if o_ < g:
                ahead = gscore[o_] >= gscore[g]
            else:
                ahead = gscore[o_] > gscore[g]
            rank = rank + ahead.astype(jnp.int32)
        keep = rank < TOPK_GROUPS
        blk = biased[g * GROUP_SIZE:(g + 1) * GROUP_SIZE, :]
        masked_blocks.append(jnp.where(keep, blk, -jnp.inf))
    cur = jnp.concatenate(masked_blocks, axis=0)

    eiota = lax.broadcasted_iota(jnp.int32, (N_EXPERTS, T), 0)
    row8 = lax.broadcasted_iota(jnp.int32, (TOP_K, T), 0)
    w_out = jnp.zeros((TOP_K, T), f32)
    sel_dense = jnp.zeros((N_EXPERTS, T), f32)
    onehots = []
    for kk_ in range(TOP_K):
        mx = jnp.max(cur, axis=0, keepdims=True)
        sel_idx = jnp.min(jnp.where(cur == mx, eiota, N_EXPERTS), axis=0, keepdims=True)
        onehot = eiota == sel_idx
        w_k = jnp.sum(jnp.where(onehot, scores, 0.0), axis=0, keepdims=True)
        cur = jnp.where(onehot, -jnp.inf, cur)
        sel_dense = jnp.where(onehot, 1.0, sel_dense)
        w_out = jnp.where(row8 == kk_, w_k, w_out)
        onehots.append(onehot)
    wsum = jnp.sum(w_out, axis=0, keepdims=True)
    wts_ref[...] = w_out / wsum * ROUTED_SCALE

    cum = _dot(sel_dense.astype(bf16), tri_ref[...])
    cnt = jnp.broadcast_to(jnp.sum(sel_dense, axis=1, keepdims=True), (N_EXPERTS, LANES))
    erow = lax.broadcasted_iota(jnp.int32, (N_EXPERTS, LANES), 0)
    incl = cnt
    step = 1
    while step < N_EXPERTS:
        incl = incl + jnp.where(erow >= step, pltpu.roll(incl, step, axis=0), 0.0)
        step *= 2
    lstart = incl - cnt
    ld_dense = lstart[:, 0:1] + cum
    ld_out = jnp.zeros((TOP_K, T), f32)
    for kk_ in range(TOP_K):
        p_k = jnp.sum(jnp.where(onehots[kk_], ld_dense, 0.0), axis=0, keepdims=True)
        ld_out = jnp.where(row8 == kk_, p_k, ld_out)
    tile_idx = pl.program_id(0) * pl.num_programs(1) + j
    slot_base = ((tile_idx & 1) * TILE_ROWS).astype(f32)
    ld_ref[...] = ((ld_out + slot_base) * ROW_SUB).astype(jnp.int32)
    cnt_ref[...] = cnt.astype(jnp.int32)


def _mixer_call(x, mixw, win, convw, qw, kw, sink_col, bias_tab, cnw, anw, wout, fnw,
                wsg, wsu, wsd, wrh, wrl, rbias, tri, head_ones):
    B, S, D = x.shape
    T = MIX_TILE
    nt = S // T
    N = B * S

    def full(a):
        nd = a.ndim
        return pl.BlockSpec(a.shape, lambda b, j, _nd=nd: (0,) * _nd)

    tok_spec = pl.BlockSpec((TOP_K, T), lambda b, j: (0, b * nt + j))
    in_arrays = [mixw, win, convw, qw, kw, sink_col, bias_tab, cnw, anw, wout, fnw,
                 wsg, wsu, wsd, wrh, wrl, rbias, tri, head_ones]
    return pl.pallas_call(
        _mixer_kernel,
        grid=(B, nt),
        in_specs=[pl.BlockSpec((1, T, D), lambda b, j: (b, j, 0))] + [full(a) for a in in_arrays],
        out_specs=[
            pl.BlockSpec((T * ROW_SUB, LANES), lambda b, j: (b * nt + j, 0)),
            pl.BlockSpec((T * ROW_SUB, LANES), lambda b, j: (b * nt + j, 0)),
            tok_spec, tok_spec,
            pl.BlockSpec((N_EXPERTS, LANES), lambda b, j: (b * nt + j, 0)),
        ],
        out_shape=[
            jax.ShapeDtypeStruct((N * ROW_SUB, LANES), jnp.float32),
            jax.ShapeDtypeStruct((N * ROW_SUB, LANES), jnp.float32),
            jax.ShapeDtypeStruct((TOP_K, N), jnp.float32),
            jax.ShapeDtypeStruct((TOP_K, N), jnp.int32),
            jax.ShapeDtypeStruct((B * nt * N_EXPERTS, LANES), jnp.int32),
        ],
        scratch_shapes=[
            pltpu.VMEM((BLK, D_KV), jnp.float32),
            pltpu.VMEM((BLK, D_KV), jnp.float32),
            pltpu.VMEM((8, D_CONV), jnp.float32),
        ],
        compiler_params=pltpu.CompilerParams(
            dimension_semantics=("arbitrary", "arbitrary"), vmem_limit_bytes=VMEM_LIMIT),
        name="mixer_router",
    )(x, *in_arrays)


def _rows(ref, row, nrows):
    return ref.at[pl.ds(pl.multiple_of(row * ROW_SUB, ROW_SUB), nrows * ROW_SUB)]


def _segment_copies(src, src_row, dst, dst_row, count, sem):
    chunk = 1 << SEG_CHUNK_LOG2

    def big(i, c):
        o = i * chunk
        pltpu.make_async_copy(_rows(src, src_row + o, chunk), _rows(dst, dst_row + o, chunk), sem).start()
        return c

    lax.fori_loop(0, count >> SEG_CHUNK_LOG2, big, 0)
    for bit in range(SEG_CHUNK_LOG2 - 1, -1, -1):
        o = (count >> (bit + 1)) << (bit + 1)

        @pl.when(((count >> bit) & 1) == 1)
        def _():
            n = 1 << bit
            pltpu.make_async_copy(_rows(src, src_row + o, n), _rows(dst, dst_row + o, n), sem).start()


def _wait_tile(hbm_ref, stag_ref, slot_row, sem, to_hbm):
    vm = _rows(stag_ref, slot_row, TILE_ROWS)
    hb = _rows(hbm_ref, 0, TILE_ROWS)
    (pltpu.make_async_copy(vm, hb, sem) if to_hbm else pltpu.make_async_copy(hb, vm, sem)).wait()


def _dispatch_kernel(lsrc_ref, gdst_ref, cnt_ref, ld_ref, xn_ref, xs_ref, stag_ref, sems):
    b = pl.program_id(0)
    nb = pl.num_programs(0)
    T = MIX_TILE
    slot = b & 1
    base = slot * TILE_ROWS

    for kk_ in range(TOP_K):
        def row(t, c, kk_=kk_):
            r8 = ld_ref[0, 0, kk_ * T + t]
            stag_ref[pl.ds(pl.multiple_of(r8, ROW_SUB), ROW_SUB), :] = (
                xn_ref[pl.ds(pl.multiple_of(t * ROW_SUB, ROW_SUB), ROW_SUB), :])
            return c

        lax.fori_loop(0, T, row, 0, unroll=32)

    def seg(e, c):
        s = b * N_EXPERTS + e
        _segment_copies(stag_ref, base + lsrc_ref[s], xs_ref, gdst_ref[s], cnt_ref[s], sems.at[slot])
        return c

    lax.fori_loop(0, N_EXPERTS, seg, 0)

    @pl.when(b > 0)
    def _():
        _wait_tile(xs_ref, stag_ref, (1 - slot) * TILE_ROWS, sems.at[1 - slot], True)

    @pl.when(b == nb - 1)
    def _():
        _wait_tile(xs_ref, stag_ref, base, sems.at[slot], True)


def _dispatch_call(lsrc, gdst, cnt, ld_t, xn_rp):
    nt = ld_t.shape[0]
    T = MIX_TILE
    smem = pl.BlockSpec(memory_space=pltpu.SMEM)
    return pl.pallas_call(
        _dispatch_kernel,
        grid=(nt,),
        in_specs=[
            smem, smem, smem,
            pl.BlockSpec((1, 1, TILE_ROWS), lambda i: (i, 0, 0), memory_space=pltpu.SMEM),
            pl.BlockSpec((T * ROW_SUB, LANES), lambda i: (i, 0)),
        ],
        out_specs=pl.BlockSpec(memory_space=pl.ANY),
        out_shape=jax.ShapeDtypeStruct((nt * TILE_ROWS * ROW_SUB, LANES), jnp.float32),
        scratch_shapes=[pltpu.VMEM((2 * TILE_ROWS * ROW_SUB, LANES), jnp.float32),
                        pltpu.SemaphoreType.DMA((2,))],
        compiler_params=pltpu.CompilerParams(
            dimension_semantics=("arbitrary",), vmem_limit_bytes=VMEM_LIMIT),
        name="moe_dispatch",
    )(lsrc, gdst, cnt, ld_t, xn_rp)


def _expert_kernel(blk_ref, exp_ref, lo_ref, hi_ref, newexp_ref,
                   xs_ref, wg_ref, wu_ref, wd_ref, ys_ref, wgb_ref, wub_ref, wdb_ref):
    i = pl.program_id(0)
    bf16 = jnp.bfloat16
    SB = EXPERT_SUB
    lo = lo_ref[i]
    hi = hi_ref[i]

    @pl.when(newexp_ref[i] == 1)
    def _():
        wgb_ref[...] = wg_ref[0].astype(bf16)
        wub_ref[...] = wu_ref[0].astype(bf16)
        wdb_ref[...] = wd_ref[0].astype(bf16)

    def ffn(chunk):
        xb = jnp.concatenate([xs_ref[chunk(c), :] for c in range(ROW_SUB)], axis=1).astype(bf16)
        g = _dot(xb, wgb_ref[...])
        u = _dot(xb, wub_ref[...])
        h = (g * jax.nn.sigmoid(g) * u).astype(bf16)
        return _dot(h, wdb_ref[...])

    whole = jnp.logical_and(lo == 0, hi == EXPERT_BLOCK)

    @pl.when(whole)
    def _():
        def chunk(c):
            return pl.ds(c, EXPERT_BLOCK, stride=ROW_SUB)

        y = ffn(chunk)
        for c in range(ROW_SUB):
            ys_ref[chunk(c), :] = y[:, c * LANES:(c + 1) * LANES]

    for q in range(EXPERT_BLOCK // SB):
        r0 = q * SB

        def chunk(c, r0=r0):
            return pl.ds(r0 * ROW_SUB + c, SB, stride=ROW_SUB)

        @pl.when(jnp.logical_and(jnp.logical_not(whole), jnp.minimum(hi, r0 + SB) > jnp.maximum(lo, r0)))
        def _(r0=r0, chunk=chunk):
            y = ffn(chunk)

            @pl.when(lo <= r0)
            def _():
                for c in range(ROW_SUB):
                    ys_ref[chunk(c), :] = y[:, c * LANES:(c + 1) * LANES]

            @pl.when(lo > r0)
            def _():
                rows = r0 + lax.broadcasted_iota(jnp.int32, (SB, LANES), 0)
                keep = jnp.logical_and(rows >= lo, rows < hi)
                for c in range(ROW_SUB):
                    old = ys_ref[chunk(c), :]
                    ys_ref[chunk(c), :] = jnp.where(keep, y[:, c * LANES:(c + 1) * LANES], old)


def _expert_call(item_blk, item_exp, item_lo, item_hi, item_newexp, xs_rp, w_eg, w_eu, w_ed):
    BM = EXPERT_BLOCK
    D = D_MODEL
    n_items = item_blk.shape[0]

    def row_map(i, blk, exp, lo, hi, newexp):
        return (blk[i], 0)

    def w_map(i, blk, exp, lo, hi, newexp):
        return (exp[i], 0, 0)

    grid_spec = pltpu.PrefetchScalarGridSpec(
        num_scalar_prefetch=5,
        grid=(n_items,),
        in_specs=[
            pl.BlockSpec((BM * ROW_SUB, LANES), row_map),
            pl.BlockSpec((1, D, D_EXPERT), w_map),
            pl.BlockSpec((1, D, D_EXPERT), w_map),
            pl.BlockSpec((1, D_EXPERT, D), w_map),
        ],
        out_specs=pl.BlockSpec((BM * ROW_SUB, LANES), row_map),
        scratch_shapes=[
            pltpu.VMEM((D, D_EXPERT), jnp.bfloat16),
            pltpu.VMEM((D, D_EXPERT), jnp.bfloat16),
            pltpu.VMEM((D_EXPERT, D), jnp.bfloat16),
        ],
    )
    return pl.pallas_call(
        _expert_kernel,
        grid_spec=grid_spec,
        out_shape=jax.ShapeDtypeStruct(xs_rp.shape, jnp.float32),
        compiler_params=pltpu.CompilerParams(
            dimension_semantics=("arbitrary",), vmem_limit_bytes=VMEM_LIMIT),
        name="moe_experts",
    )(item_blk, item_exp, item_lo, item_hi, item_newexp, xs_rp, w_eg, w_eu, w_ed)


def _combine_kernel(lsrc_ref, gdst_ref, cnt_ref, ld_ref, w_ref, x2_ref, ys_ref, out_ref,
                    stag_ref, acc_ref, sems):
    b = pl.program_id(0)
    nb = pl.num_programs(0)
    T = MIX_TILE
    slot = b & 1
    base = slot * TILE_ROWS

    def fetch(tile, slot_):
        def seg(e, c):
            s = tile * N_EXPERTS + e
            _segment_copies(ys_ref, gdst_ref[s], stag_ref, slot_ * TILE_ROWS + lsrc_ref[s], cnt_ref[s],
                            sems.at[slot_])
            return c

        lax.fori_loop(0, N_EXPERTS, seg, 0)

    @pl.when(b == 0)
    def _():
        fetch(b, slot)

    @pl.when(b + 1 < nb)
    def _():
        fetch(b + 1, 1 - slot)

    _wait_tile(ys_ref, stag_ref, base, sems.at[slot], False)

    def tok(t, c):
        at = pl.ds(pl.multiple_of(t * ROW_SUB, ROW_SUB), ROW_SUB)
        a = x2_ref[at, :]
        for kk_ in range(TOP_K):
            r8 = ld_ref[0, 0, kk_ * T + t]
            w = w_ref[0, 0, kk_ * T + t]
            a = a + w * stag_ref[pl.ds(pl.multiple_of(r8, ROW_SUB), ROW_SUB), :]
        acc_ref[at, :] = a
        return c

    lax.fori_loop(0, T, tok, 0, unroll=4)
    for c in range(ROW_SUB):
        out_ref[:, c * LANES:(c + 1) * LANES] = acc_ref[pl.ds(c, T, stride=ROW_SUB), :]


def _combine_call(lsrc, gdst, cnt, ld_t, w_t, x2, ys_rp):
    D = D_MODEL
    N = x2.shape[0] // ROW_SUB
    nt = ld_t.shape[0]
    T = MIX_TILE
    smem = pl.BlockSpec(memory_space=pltpu.SMEM)
    smem_blk = pl.BlockSpec((1, 1, TILE_ROWS), lambda i: (i, 0, 0), memory_space=pltpu.SMEM)
    return pl.pallas_call(
        _combine_kernel,
        grid=(nt,),
        in_specs=[
            smem, smem, smem, smem_blk, smem_blk,
            pl.BlockSpec((T * ROW_SUB, LANES), lambda i: (i, 0)),
            pl.BlockSpec(memory_space=pl.ANY),
        ],
        out_specs=pl.BlockSpec((T, D), lambda i: (i, 0)),
        out_shape=jax.ShapeDtypeStruct((N, D), jnp.float32),
        scratch_shapes=[
            pltpu.VMEM((2 * TILE_ROWS * ROW_SUB, LANES), jnp.float32),
            pltpu.VMEM((T * ROW_SUB, LANES), jnp.float32),
            pltpu.SemaphoreType.DMA((2,)),
        ],
        compiler_params=pltpu.CompilerParams(
            dimension_semantics=("arbitrary",), vmem_limit_bytes=VMEM_LIMIT),
        name="moe_combine",
    )(lsrc, gdst, cnt, ld_t, w_t, x2, ys_rp)


def _work_items(gstart, totals, n_items):
    BM = EXPERT_BLOCK
    i32 = jnp.int32
    gend = gstart + totals
    first_blk = gstart // BM
    last_blk = jnp.maximum(gend - 1, gstart) // BM
    n_e = jnp.where(totals > 0, last_blk - first_blk + 1, 0)
    item_end = jnp.cumsum(n_e)
    item_start = item_end - n_e
    n_real = item_end[-1]
    j = jnp.minimum(jnp.arange(n_items, dtype=i32), n_real - 1)
    exp = jnp.sum((item_end[None, :] <= j[:, None]).astype(i32), axis=1)
    onehot = exp[:, None] == jnp.arange(N_EXPERTS, dtype=i32)[None, :]

    def of_exp(v):
        return jnp.sum(jnp.where(onehot, v[None, :], 0), axis=1)

    blk = of_exp(first_blk) + (j - of_exp(item_start))
    lo = jnp.maximum(of_exp(gstart), blk * BM) - blk * BM
    hi = jnp.minimum(of_exp(gend), (blk + 1) * BM) - blk * BM
    real = jnp.arange(n_items, dtype=i32) < n_real
    hi = jnp.where(real, hi, lo)
    prev_exp = jnp.concatenate([jnp.full((1,), -1, i32), exp[:-1]])
    newexp = jnp.logical_and(real, exp != prev_exp).astype(i32)
    return blk.astype(i32), exp.astype(i32), lo.astype(i32), hi.astype(i32), newexp


def _layer(x, mix_norm_w, w_in, conv_w, q_norm_w, k_norm_w, sinks, conv_out_norm_w,
           attn_out_norm_w, w_out, ffn_norm_w, w_router, router_bias, w_eg, w_eu, w_ed,
           w_sg, w_su, w_sd):
    B, S, D = x.shape
    N = B * S
    T = MIX_TILE
    nt = N // T
    bf16 = jnp.bfloat16
    f32 = jnp.float32
    i32 = jnp.int32

    wr_t = w_router.astype(f32).T
    wr_hi = wr_t.astype(bf16)
    wr_lo = (wr_t - wr_hi.astype(f32)).astype(bf16)
    sink_col = jnp.broadcast_to(
        jnp.repeat(sinks.astype(f32), BLK).reshape(N_KV_HEADS, GQA_GROUP * BLK, 1),
        (N_KV_HEADS, GQA_GROUP * BLK, LANES))
    tri = jnp.asarray(np.triu(np.ones((T, T), np.float32), k=1), dtype=bf16)
    head_ones = jnp.asarray(np.kron(np.eye(D_ATTN // HEAD_DIM, dtype=np.float32),
                                    np.ones((HEAD_DIM, HEAD_DIM), np.float32)), dtype=bf16)
    bias_tab = jnp.asarray(_attn_bias_table())

    x2, xn_rp, wts, ld, cnt = _mixer_call(
        x, mix_norm_w.reshape(1, D), w_in.astype(bf16), conv_w,
        jnp.tile(q_norm_w, N_HEADS).reshape(1, D_ATTN), jnp.tile(k_norm_w, N_KV_HEADS).reshape(1, D_KV),
        sink_col, bias_tab, conv_out_norm_w.reshape(1, D_CONV), attn_out_norm_w.reshape(1, D_ATTN),
        w_out.astype(bf16), ffn_norm_w.reshape(1, D), w_sg.astype(bf16), w_su.astype(bf16),
        w_sd.astype(bf16), wr_hi, wr_lo, router_bias.astype(f32).reshape(N_EXPERTS, 1), tri, head_ones)

    cnt_te = cnt[:, 0].reshape(nt, N_EXPERTS)
    lsrc = jnp.cumsum(cnt_te, axis=1) - cnt_te
    before = jnp.cumsum(cnt_te, axis=0) - cnt_te
    totals = jnp.sum(cnt_te, axis=0)
    gstart = jnp.cumsum(totals) - totals
    gdst = gstart[None, :] + before
    n_items = (N * TOP_K) // EXPERT_BLOCK + N_EXPERTS
    items = _work_items(gstart.astype(i32), totals.astype(i32), n_items)

    def tile_major(a):
        return a.reshape(TOP_K, nt, T).transpose(1, 0, 2).reshape(nt, 1, TILE_ROWS)

    ld_t = tile_major(ld)
    seg_tabs = (lsrc.reshape(-1).astype(i32), gdst.reshape(-1).astype(i32), cnt_te.reshape(-1).astype(i32))
    xs_rp = _dispatch_call(*seg_tabs, ld_t, xn_rp)
    ys_rp = _expert_call(*items, xs_rp, w_eg, w_eu, w_ed)
    out = _combine_call(*seg_tabs, ld_t, tile_major(wts), x2, ys_rp)
    return out.reshape(B, S, D)


def kernel(x, mix_norm_w, w_in, conv_w, q_norm_w, k_norm_w, sinks, conv_out_norm_w, attn_out_norm_w, w_out, ffn_norm_w, w_router, router_bias, w_exp_gate, w_exp_up, w_exp_down, w_sh_gate, w_sh_up, w_sh_down):
    return _layer(x, mix_norm_w[0], w_in[0], conv_w[0], q_norm_w[0], k_norm_w[0], sinks[0],
                  conv_out_norm_w[0], attn_out_norm_w[0], w_out[0], ffn_norm_w[0], w_router[0],
                  router_bias[0], w_exp_gate[0], w_exp_up[0], w_exp_down[0], w_sh_gate[0],
                  w_sh_up[0], w_sh_down[0])
```

```python
import numpy as np
import jax
import jax.numpy as jnp
from jax import lax
from jax.experimental import pallas as pl
from jax.experimental.pallas import tpu as pltpu

D_MODEL = 1024
EPS = 1e-6
D_CONV = 512
CONV_K = 3
HEAD_DIM = 64
N_HEADS = 8
N_KV_HEADS = 2
GQA_GROUP = 4
D_ATTN = 512
D_KV = 128
WINDOW = 128
BLK = 128
D_IN_PROJ = 2304
N_EXPERTS = 64
TOP_K = 8
N_GROUPS = 8
GROUP_SIZE = 8
TOPK_GROUPS = 4
D_EXPERT = 256
ROUTED_SCALE = 2.5

LANES = 128
ROW_SUB = 8
MIX_TILE = 512
TILE_ROWS = TOP_K * MIX_TILE
EXPERT_BLOCK = 1024
EXPERT_SUB = 256
EXPERT_IN_SLOTS = 3
EXPERT_OUT_SLOTS = 2
SEG_CHUNK_LOG2 = 5
NEG_BIG = -1e30
VMEM_LIMIT = 58 * 1024 * 1024


def _alibi_slopes():
    return np.array([2.0 ** (-8.0 * (h + 1) / N_HEADS) for h in range(N_HEADS)], dtype=np.float32)


def _attn_bias_table():
    qi = np.arange(BLK)[:, None]
    kj = np.arange(2 * BLK)[None, :]
    dist = qi - kj + BLK
    inwin = (dist >= 0) & (dist < WINDOW)
    slopes = _alibi_slopes()
    out = np.zeros((N_KV_HEADS, GQA_GROUP * BLK, 2 * BLK), np.float32)
    for g in range(N_KV_HEADS):
        for i in range(GQA_GROUP):
            h = g * GQA_GROUP + i
            out[g, i * BLK:(i + 1) * BLK] = np.where(inwin, -slopes[h] * dist.astype(np.float32), NEG_BIG)
    return out


def _rms(x, w):
    ms = jnp.mean(x * x, axis=-1, keepdims=True)
    return x * lax.rsqrt(ms + EPS) * w


def _head_rms(x, w_tiled, head_ones):
    C = x.shape[1]
    sq = x * x
    hi = sq.astype(jnp.bfloat16)
    lo = (sq - hi.astype(jnp.float32)).astype(jnp.bfloat16)
    ones = head_ones[0:C, 0:C]
    ssq = _dot(hi, ones) + _dot(lo, ones)
    return x * lax.rsqrt(ssq * (1.0 / HEAD_DIM) + EPS) * w_tiled


def _dot(a, b):
    return jnp.dot(a, b, preferred_element_type=jnp.float32)


def _dot_nt(a, b):
    return lax.dot_general(a, b, (((1,), (1,)), ((), ())), preferred_element_type=jnp.float32)


def _mixer_kernel(x_ref, mixw_ref, win_ref, convw_ref, qw_ref, kw_ref, sink_ref, bias_ref,
                  cnw_ref, anw_ref, wout_ref, fnw_ref, wsg_ref, wsu_ref, wsd_ref,
                  wrh_ref, wrl_ref, rbias_ref, tri_ref, hones_ref,
                  x2_ref, xn_ref, wts_ref, ld_ref, cnt_ref,
                  kc_ref, vc_ref, cc_ref):
    j = pl.program_id(1)
    T = MIX_TILE
    bf16 = jnp.bfloat16
    f32 = jnp.float32

    @pl.when(j == 0)
    def _():
        kc_ref[...] = jnp.zeros_like(kc_ref)
        vc_ref[...] = jnp.zeros_like(vc_ref)
        cc_ref[...] = jnp.zeros_like(cc_ref)

    x = x_ref[0]
    u = _rms(x, mixw_ref[...]).astype(bf16)
    proj = _dot(u, win_ref[...])
    b_gate = proj[:, 0:512]
    c_gate = proj[:, 512:1024]
    hh = proj[:, 1024:1536]
    q = proj[:, 1536:2048]
    k = proj[:, 2048:2176]
    v = proj[:, 2176:2304]

    ch = c_gate * hh
    prev = cc_ref[...]
    p6 = prev[6:7, :]
    p7 = prev[7:8, :]
    row = lax.broadcasted_iota(jnp.int32, (T, D_CONV), 0)
    ch_m1 = jnp.where(row == 0, p7, pltpu.roll(ch, 1, axis=0))
    ch_m2 = jnp.where(row == 0, p6, jnp.where(row == 1, p7, pltpu.roll(ch, 2, axis=0)))
    cw = convw_ref[...]
    y_conv = b_gate * (cw[0:1, :] * ch_m2 + cw[1:2, :] * ch_m1 + cw[2:3, :] * ch)
    cc_ref[...] = ch[T - 8:T, :]

    lane = lax.broadcasted_iota(jnp.int32, (1, LANES), 1)
    lane_lo = lane < HEAD_DIM
    qn = _head_rms(q, qw_ref[...], hones_ref) * (HEAD_DIM ** -0.5)
    kn = _head_rms(k, kw_ref[...], hones_ref)
    kfull = jnp.concatenate([kc_ref[...], kn], axis=0)
    vfull = jnp.concatenate([vc_ref[...], v], axis=0)
    kc_ref[...] = kn[T - BLK:T, :]
    vc_ref[...] = v[T - BLK:T, :]

    def _rep(a, g):
        r = pltpu.roll(a, HEAD_DIM, axis=1)
        two = jnp.where(lane_lo, a, r) if g == 0 else jnp.where(lane_lo, r, a)
        return jnp.concatenate([two, two], axis=1).astype(bf16)

    k_rep = [_rep(kfull, g) for g in range(N_KV_HEADS)]
    v_rep = [_rep(vfull, g) for g in range(N_KV_HEADS)]

    lane256 = lax.broadcasted_iota(jnp.int32, (1, 2 * LANES), 1)
    head_of_lane = lane256 // HEAD_DIM
    first_f = (j == 0).astype(f32)
    prev_key_mask = jnp.where(lane256 < BLK, first_f * NEG_BIG, 0.0)
    ones_cols = jnp.ones((2 * BLK, LANES), bf16)

    def lanes2(a):
        return jnp.concatenate([a, a], axis=1)

    attn_rows = []
    for i in range(T // BLK):
        grp_out = []
        for g in range(N_KV_HEADS):
            qg = qn[i * BLK:(i + 1) * BLK, g * 256:(g + 1) * 256]
            qst = jnp.concatenate(
                [jnp.where(head_of_lane == hi, qg, 0.0) for hi in range(GQA_GROUP)], axis=0).astype(bf16)
            kk = k_rep[g][i * BLK:i * BLK + 2 * BLK, :]
            vv = jnp.concatenate([v_rep[g][i * BLK:i * BLK + 2 * BLK, :], ones_cols], axis=1)
            s = _dot_nt(qst, kk) + bias_ref[g]
            if i == 0:
                s = s + prev_key_mask
            sink = sink_ref[g]
            m = jnp.maximum(jnp.max(s, axis=-1, keepdims=True), sink)
            e = jnp.exp(s - lanes2(m)).astype(bf16)
            r = _dot(e, vv)
            inv = 1.0 / (r[:, 2 * LANES:3 * LANES] + jnp.exp(sink - m))
            r = r[:, 0:2 * LANES] * lanes2(inv)
            o = jnp.where(head_of_lane == 0, r[0:BLK], 0.0)
            for hi in range(1, GQA_GROUP):
                o = jnp.where(head_of_lane == hi, r[hi * BLK:(hi + 1) * BLK], o)
            grp_out.append(o)
        attn_rows.append(jnp.concatenate(grp_out, axis=1))
    y_attn = jnp.concatenate(attn_rows, axis=0)

    y_mix = jnp.concatenate([_rms(y_conv, cnw_ref[...]), _rms(y_attn, anw_ref[...])], axis=1)
    x1 = x + _dot(y_mix.astype(bf16), wout_ref[...])

    xn = _rms(x1, fnw_ref[...])
    for c in range(ROW_SUB):
        xn_ref[pl.ds(c, T, stride=ROW_SUB), :] = xn[:, c * LANES:(c + 1) * LANES]
    xh = xn.astype(bf16)
    gs = _dot(xh, wsg_ref[...])
    us = _dot(xh, wsu_ref[...])
    hs = (gs * jax.nn.sigmoid(gs) * us).astype(bf16)
    x2 = x1 + _dot(hs, wsd_ref[...])
    for c in range(ROW_SUB):
        x2_ref[pl.ds(c, T, stride=ROW_SUB), :] = x2[:, c * LANES:(c + 1) * LANES]

    xl = (xn - xh.astype(f32)).astype(bf16)
    logits = _dot_nt(wrh_ref[...], xh) + _dot_nt(wrh_ref[...], xl) + _dot_nt(wrl_ref[...], xh)
    scores = jax.nn.sigmoid(logits)
    biased = scores + rbias_ref[...]

    sub8 = lax.broadcasted_iota(jnp.int32, (GROUP_SIZE, T), 0)
    gscore = []
    for g in range(N_GROUPS):
        blk = biased[g * GROUP_SIZE:(g + 1) * GROUP_SIZE, :]
        m1 = jnp.max(blk, axis=0, keepdims=True)
        first = jnp.min(jnp.where(blk == m1, sub8, GROUP_SIZE), axis=0, keepdims=True)
        m2 = jnp.max(jnp.where(sub8 == first, -jnp.inf, blk), axis=0, keepdims=True)
        gscore.append(m1 + m2)
    masked_blocks = []
    for g in range(N_GROUPS):
        rank = jnp.zeros((1, T), jnp.int32)
        for o_ in range(N_GROUPS):
            if o_ == g:
                continue
            if o_ < g:
                ahead = gscore[o_] >= gscore[g]
            else:
                ahead = gscore[o_] > gscore[g]
            rank = rank + ahead.astype(jnp.int32)
        keep = rank < TOPK_GROUPS
        blk = biased[g * GROUP_SIZE:(g + 1) * GROUP_SIZE, :]
        masked_blocks.append(jnp.where(keep, blk, -jnp.inf))
    cur = jnp.concatenate(masked_blocks, axis=0)

    eiota = lax.broadcasted_iota(jnp.int32, (N_EXPERTS, T), 0)
    row8 = lax.broadcasted_iota(jnp.int32, (TOP_K, T), 0)
    w_out = jnp.zeros((TOP_K, T), f32)
    sel_dense = jnp.zeros((N_EXPERTS, T), f32)
    onehots = []
    for kk_ in range(TOP_K):
        mx = jnp.max(cur, axis=0, keepdims=True)
        sel_idx = jnp.min(jnp.where(cur == mx, eiota, N_EXPERTS), axis=0, keepdims=True)
        onehot = eiota == sel_idx
        w_k = jnp.sum(jnp.where(onehot, scores, 0.0), axis=0, keepdims=True)
        cur = jnp.where(onehot, -jnp.inf, cur)
        sel_dense = jnp.where(onehot, 1.0, sel_dense)
        w_out = jnp.where(row8 == kk_, w_k, w_out)
        onehots.append(onehot)
    wsum = jnp.sum(w_out, axis=0, keepdims=True)
    wts_ref[...] = w_out / wsum * ROUTED_SCALE

    cum = _dot(sel_dense.astype(bf16), tri_ref[...])
    cnt = jnp.broadcast_to(jnp.sum(sel_dense, axis=1, keepdims=True), (N_EXPERTS, LANES))
    erow = lax.broadcasted_iota(jnp.int32, (N_EXPERTS, LANES), 0)
    incl = cnt
    step = 1
    while step < N_EXPERTS:
        incl = incl + jnp.where(erow >= step, pltpu.roll(incl, step, axis=0), 0.0)
        step *= 2
    lstart = incl - cnt
    ld_dense = lstart[:, 0:1] + cum
    ld_out = jnp.zeros((TOP_K, T), f32)
    for kk_ in range(TOP_K):
        p_k = jnp.sum(jnp.where(onehots[kk_], ld_dense, 0.0), axis=0, keepdims=True)
        ld_out = jnp.where(row8 == kk_, p_k, ld_out)
    tile_idx = pl.program_id(0) * pl.num_programs(1) + j
    slot_base = ((tile_idx & 1) * TILE_ROWS).astype(f32)
    ld_ref[...] = ((ld_out + slot_base) * ROW_SUB).astype(jnp.int32)
    cnt_ref[...] = cnt.astype(jnp.int32)


def _mixer_call(x, mixw, win, convw, qw, kw, sink_col, bias_tab, cnw, anw, wout, fnw,
                wsg, wsu, wsd, wrh, wrl, rbias, tri, head_ones):
    B, S, D = x.shape
    T = MIX_TILE
    nt = S // T
    N = B * S

    def full(a):
        nd = a.ndim
        return pl.BlockSpec(a.shape, lambda b, j, _nd=nd: (0,) * _nd)

    tok_spec = pl.BlockSpec((TOP_K, T), lambda b, j: (0, b * nt + j))
    in_arrays = [mixw, win, convw, qw, kw, sink_col, bias_tab, cnw, anw, wout, fnw,
                 wsg, wsu, wsd, wrh, wrl, rbias, tri, head_ones]
    return pl.pallas_call(
        _mixer_kernel,
        grid=(B, nt),
        in_specs=[pl.BlockSpec((1, T, D), lambda b, j: (b, j, 0))] + [full(a) for a in in_arrays],
        out_specs=[
            pl.BlockSpec((T * ROW_SUB, LANES), lambda b, j: (b * nt + j, 0)),
            pl.BlockSpec((T * ROW_SUB, LANES), lambda b, j: (b * nt + j, 0)),
            tok_spec, tok_spec,
            pl.BlockSpec((N_EXPERTS, LANES), lambda b, j: (b * nt + j, 0)),
        ],
        out_shape=[
            jax.ShapeDtypeStruct((N * ROW_SUB, LANES), jnp.float32),
            jax.ShapeDtypeStruct((N * ROW_SUB, LANES), jnp.float32),
            jax.ShapeDtypeStruct((TOP_K, N), jnp.float32),
            jax.ShapeDtypeStruct((TOP_K, N), jnp.int32),
            jax.ShapeDtypeStruct((B * nt * N_EXPERTS, LANES), jnp.int32),
        ],
        scratch_shapes=[
            pltpu.VMEM((BLK, D_KV), jnp.float32),
            pltpu.VMEM((BLK, D_KV), jnp.float32),
            pltpu.VMEM((8, D_CONV), jnp.float32),
        ],
        compiler_params=pltpu.CompilerParams(
            dimension_semantics=("arbitrary", "arbitrary"), vmem_limit_bytes=VMEM_LIMIT),
        name="mixer_router",
    )(x, *in_arrays)


def _rows(ref, row, nrows):
    return ref.at[pl.ds(pl.multiple_of(row * ROW_SUB, ROW_SUB), nrows * ROW_SUB)]


def _segment_copies(src, src_row, dst, dst_row, count, sem):
    chunk = 1 << SEG_CHUNK_LOG2

    def big(i, c):
        o = i * chunk
        pltpu.make_async_copy(_rows(src, src_row + o, chunk), _rows(dst, dst_row + o, chunk), sem).start()
        return c

    lax.fori_loop(0, count >> SEG_CHUNK_LOG2, big, 0)
    for bit in range(SEG_CHUNK_LOG2 - 1, -1, -1):
        o = (count >> (bit + 1)) << (bit + 1)

        @pl.when(((count >> bit) & 1) == 1)
        def _():
            n = 1 << bit
            pltpu.make_async_copy(_rows(src, src_row + o, n), _rows(dst, dst_row + o, n), sem).start()


def _wait_tile(hbm_ref, stag_ref, slot_row, sem, to_hbm):
    vm = _rows(stag_ref, slot_row, TILE_ROWS)
    hb = _rows(hbm_ref, 0, TILE_ROWS)
    (pltpu.make_async_copy(vm, hb, sem) if to_hbm else pltpu.make_async_copy(hb, vm, sem)).wait()


def _dispatch_kernel(lsrc_ref, gdst_ref, cnt_ref, ld_ref, xn_ref, xs_ref, stag_ref, sems):
    b = pl.program_id(0)
    nb = pl.num_programs(0)
    T = MIX_TILE
    slot = b & 1
    base = slot * TILE_ROWS

    for kk_ in range(TOP_K):
        def row(t, c, kk_=kk_):
            r8 = ld_ref[0, 0, kk_ * T + t]
            stag_ref[pl.ds(pl.multiple_of(r8, ROW_SUB), ROW_SUB), :] = (
                xn_ref[pl.ds(pl.multiple_of(t * ROW_SUB, ROW_SUB), ROW_SUB), :])
            return c

        lax.fori_loop(0, T, row, 0, unroll=32)

    def seg(e, c):
        s = b * N_EXPERTS + e
        _segment_copies(stag_ref, base + lsrc_ref[s], xs_ref, gdst_ref[s], cnt_ref[s], sems.at[slot])
        return c

    lax.fori_loop(0, N_EXPERTS, seg, 0)

    @pl.when(b > 0)
    def _():
        _wait_tile(xs_ref, stag_ref, (1 - slot) * TILE_ROWS, sems.at[1 - slot], True)

    @pl.when(b == nb - 1)
    def _():
        _wait_tile(xs_ref, stag_ref, base, sems.at[slot], True)


def _dispatch_call(lsrc, gdst, cnt, ld_t, xn_rp):
    nt = ld_t.shape[0]
    T = MIX_TILE
    smem = pl.BlockSpec(memory_space=pltpu.SMEM)
    return pl.pallas_call(
        _dispatch_kernel,
        grid=(nt,),
        in_specs=[
            smem, smem, smem,
            pl.BlockSpec((1, 1, TILE_ROWS), lambda i: (i, 0, 0), memory_space=pltpu.SMEM),
            pl.BlockSpec((T * ROW_SUB, LANES), lambda i: (i, 0)),
        ],
        out_specs=pl.BlockSpec(memory_space=pl.ANY),
        out_shape=jax.ShapeDtypeStruct((nt * TILE_ROWS * ROW_SUB, LANES), jnp.float32),
        scratch_shapes=[pltpu.VMEM((2 * TILE_ROWS * ROW_SUB, LANES), jnp.float32),
                        pltpu.SemaphoreType.DMA((2,))],
        compiler_params=pltpu.CompilerParams(
            dimension_semantics=("arbitrary",), vmem_limit_bytes=VMEM_LIMIT),
        name="moe_dispatch",
    )(lsrc, gdst, cnt, ld_t, xn_rp)


def _expert_kernel(blk_ref, exp_ref, lo_ref, hi_ref, newexp_ref, newblk_ref, endblk_ref,
                   xs_hbm, wg_ref, wu_ref, wd_ref, ys_hbm,
                   xbuf_ref, ybuf_ref, wgb_ref, wub_ref, wdb_ref, in_sems, out_sems):
    i = pl.program_id(0)
    n_items = pl.num_programs(0)
    bf16 = jnp.bfloat16
    SB = EXPERT_SUB
    BS = EXPERT_BLOCK * ROW_SUB
    n_blocks = xs_hbm.shape[0] // BS
    blk = blk_ref[i]
    lo = lo_ref[i]
    hi = hi_ref[i]

    def block_rows(idx):
        start = idx * BS
        return pl.ds(start if isinstance(start, int) else pl.multiple_of(start, BS), BS)

    def in_copy(b, slot):
        return pltpu.make_async_copy(xs_hbm.at[block_rows(b)], xbuf_ref.at[block_rows(slot)], in_sems.at[slot])

    def out_copy(b, slot):
        return pltpu.make_async_copy(ybuf_ref.at[block_rows(slot)], ys_hbm.at[block_rows(b)], out_sems.at[slot])

    @pl.when(i == 0)
    def _():
        for b0 in range(EXPERT_IN_SLOTS - 1):
            in_copy(b0, b0).start()

    @pl.when(newblk_ref[i] == 1)
    def _():
        ahead = blk + (EXPERT_IN_SLOTS - 1)

        @pl.when(ahead < n_blocks)
        def _():
            in_copy(ahead, lax.rem(ahead, EXPERT_IN_SLOTS)).start()

        in_copy(blk, lax.rem(blk, EXPERT_IN_SLOTS)).wait()

        @pl.when(blk >= EXPERT_OUT_SLOTS)
        def _():
            out_copy(blk - EXPERT_OUT_SLOTS, lax.rem(blk, EXPERT_OUT_SLOTS)).wait()

    @pl.when(newexp_ref[i] == 1)
    def _():
        wgb_ref[...] = wg_ref[0].astype(bf16)
        wub_ref[...] = wu_ref[0].astype(bf16)
        wdb_ref[...] = wd_ref[0].astype(bf16)

    xbase = lax.rem(blk, EXPERT_IN_SLOTS) * BS
    ybase = lax.rem(blk, EXPERT_OUT_SLOTS) * BS

    def ffn(chunk):
        xb = jnp.concatenate([xbuf_ref[chunk(xbase, c), :] for c in range(ROW_SUB)], axis=1).astype(bf16)
        g = _dot(xb, wgb_ref[...])
        u = _dot(xb, wub_ref[...])
        h = (g * jax.nn.sigmoid(g) * u).astype(bf16)
        return _dot(h, wdb_ref[...])

    whole = jnp.logical_and(lo == 0, hi == EXPERT_BLOCK)

    @pl.when(whole)
    def _():
        def chunk(base, c):
            return pl.ds(base + c, EXPERT_BLOCK, stride=ROW_SUB)

        y = ffn(chunk)
        for c in range(ROW_SUB):
            ybuf_ref[chunk(ybase, c), :] = y[:, c * LANES:(c + 1) * LANES]

    for q in range(EXPERT_BLOCK // SB):
        r0 = q * SB

        def chunk(base, c, r0=r0):
            return pl.ds(base + r0 * ROW_SUB + c, SB, stride=ROW_SUB)

        @pl.when(jnp.logical_and(jnp.logical_not(whole), jnp.minimum(hi, r0 + SB) > jnp.maximum(lo, r0)))
        def _(r0=r0, chunk=chunk):
            y = ffn(chunk)

            @pl.when(lo <= r0)
            def _():
                for c in range(ROW_SUB):
                    ybuf_ref[chunk(ybase, c), :] = y[:, c * LANES:(c + 1) * LANES]

            @pl.when(lo > r0)
            def _():
                rows = r0 + lax.broadcasted_iota(jnp.int32, (SB, LANES), 0)
                keep = jnp.logical_and(rows >= lo, rows < hi)
                for c in range(ROW_SUB):
                    old = ybuf_ref[chunk(ybase, c), :]
                    ybuf_ref[chunk(ybase, c), :] = jnp.where(keep, y[:, c * LANES:(c + 1) * LANES], old)

    @pl.when(endblk_ref[i] == 1)
    def _():
        out_copy(blk, lax.rem(blk, EXPERT_OUT_SLOTS)).start()

    @pl.when(i == n_items - 1)
    def _():
        for b1 in range(n_blocks - EXPERT_OUT_SLOTS, n_blocks):
            out_copy(b1, b1 % EXPERT_OUT_SLOTS).wait()


def _expert_call(item_blk, item_exp, item_lo, item_hi, item_newexp, item_newblk, item_endblk,
                 xs_rp, w_eg, w_eu, w_ed):
    D = D_MODEL
    BS = EXPERT_BLOCK * ROW_SUB
    n_items = item_blk.shape[0]
    assert xs_rp.shape[0] % BS == 0 and xs_rp.shape[0] // BS >= max(EXPERT_IN_SLOTS, EXPERT_OUT_SLOTS)

    def w_map(i, blk, exp, lo, hi, newexp, newblk, endblk):
        return (exp[i], 0, 0)

    grid_spec = pltpu.PrefetchScalarGridSpec(
        num_scalar_prefetch=7,
        grid=(n_items,),
        in_specs=[
            pl.BlockSpec(memory_space=pl.ANY),
            pl.BlockSpec((1, D, D_EXPERT), w_map),
            pl.BlockSpec((1, D, D_EXPERT), w_map),
            pl.BlockSpec((1, D_EXPERT, D), w_map),
        ],
        out_specs=pl.BlockSpec(memory_space=pl.ANY),
        scratch_shapes=[
            pltpu.VMEM((EXPERT_IN_SLOTS * BS, LANES), jnp.float32),
            pltpu.VMEM((EXPERT_OUT_SLOTS * BS, LANES), jnp.float32),
            pltpu.VMEM((D, D_EXPERT), jnp.bfloat16),
            pltpu.VMEM((D, D_EXPERT), jnp.bfloat16),
            pltpu.VMEM((D_EXPERT, D), jnp.bfloat16),
            pltpu.SemaphoreType.DMA((EXPERT_IN_SLOTS,)),
            pltpu.SemaphoreType.DMA((EXPERT_OUT_SLOTS,)),
        ],
    )
    return pl.pallas_call(
        _expert_kernel,
        grid_spec=grid_spec,
        out_shape=jax.ShapeDtypeStruct(xs_rp.shape, jnp.float32),
        compiler_params=pltpu.CompilerParams(
            dimension_semantics=("arbitrary",), vmem_limit_bytes=VMEM_LIMIT),
        name="moe_experts",
    )(item_blk, item_exp, item_lo, item_hi, item_newexp, item_newblk, item_endblk, xs_rp, w_eg, w_eu, w_ed)


def _combine_kernel(lsrc_ref, gdst_ref, cnt_ref, ld_ref, w_ref, x2_ref, ys_ref, out_ref,
                    stag_ref, acc_ref, sems):
    b = pl.program_id(0)
    nb = pl.num_programs(0)
    T = MIX_TILE
    slot = b & 1
    base = slot * TILE_ROWS

    def fetch(tile, slot_):
        def seg(e, c):
            s = tile * N_EXPERTS + e
            _segment_copies(ys_ref, gdst_ref[s], stag_ref, slot_ * TILE_ROWS + lsrc_ref[s], cnt_ref[s],
                            sems.at[slot_])
            return c

        lax.fori_loop(0, N_EXPERTS, seg, 0)

    @pl.when(b == 0)
    def _():
        fetch(b, slot)

    @pl.when(b + 1 < nb)
    def _():
        fetch(b + 1, 1 - slot)

    _wait_tile(ys_ref, stag_ref, base, sems.at[slot], False)

    def tok(t, c):
        at = pl.ds(pl.multiple_of(t * ROW_SUB, ROW_SUB), ROW_SUB)
        a = x2_ref[at, :]
        for kk_ in range(TOP_K):
            r8 = ld_ref[0, 0, kk_ * T + t]
            w = w_ref[0, 0, kk_ * T + t]
            a = a + w * stag_ref[pl.ds(pl.multiple_of(r8, ROW_SUB), ROW_SUB), :]
        acc_ref[at, :] = a
        return c

    lax.fori_loop(0, T, tok, 0, unroll=4)
    for c in range(ROW_SUB):
        out_ref[:, c * LANES:(c + 1) * LANES] = acc_ref[pl.ds(c, T, stride=ROW_SUB), :]


def _combine_call(lsrc, gdst, cnt, ld_t, w_t, x2, ys_rp):
    D = D_MODEL
    N = x2.shape[0] // ROW_SUB
    nt = ld_t.shape[0]
    T = MIX_TILE
    smem = pl.BlockSpec(memory_space=pltpu.SMEM)
    smem_blk = pl.BlockSpec((1, 1, TILE_ROWS), lambda i: (i, 0, 0), memory_space=pltpu.SMEM)
    return pl.pallas_call(
        _combine_kernel,
        grid=(nt,),
        in_specs=[
            smem, smem, smem, smem_blk, smem_blk,
            pl.BlockSpec((T * ROW_SUB, LANES), lambda i: (i, 0)),
            pl.BlockSpec(memory_space=pl.ANY),
        ],
        out_specs=pl.BlockSpec((T, D), lambda i: (i, 0)),
        out_shape=jax.ShapeDtypeStruct((N, D), jnp.float32),
        scratch_shapes=[
            pltpu.VMEM((2 * TILE_ROWS * ROW_SUB, LANES), jnp.float32),
            pltpu.VMEM((T * ROW_SUB, LANES), jnp.float32),
            pltpu.SemaphoreType.DMA((2,)),
        ],
        compiler_params=pltpu.CompilerParams(
            dimension_semantics=("arbitrary",), vmem_limit_bytes=VMEM_LIMIT),
        name="moe_combine",
    )(lsrc, gdst, cnt, ld_t, w_t, x2, ys_rp)


def _work_items(gstart, totals, n_items):
    BM = EXPERT_BLOCK
    i32 = jnp.int32
    gend = gstart + totals
    first_blk = gstart // BM
    last_blk = jnp.maximum(gend - 1, gstart) // BM
    n_e = jnp.where(totals > 0, last_blk - first_blk + 1, 0)
    item_end = jnp.cumsum(n_e)
    item_start = item_end - n_e
    n_real = item_end[-1]
    j = jnp.minimum(jnp.arange(n_items, dtype=i32), n_real - 1)
    exp = jnp.sum((item_end[None, :] <= j[:, None]).astype(i32), axis=1)
    onehot = exp[:, None] == jnp.arange(N_EXPERTS, dtype=i32)[None, :]

    def of_exp(v):
        return jnp.sum(jnp.where(onehot, v[None, :], 0), axis=1)

    blk = of_exp(first_blk) + (j - of_exp(item_start))
    lo = jnp.maximum(of_exp(gstart), blk * BM) - blk * BM
    hi = jnp.minimum(of_exp(gend), (blk + 1) * BM) - blk * BM
    real = jnp.arange(n_items, dtype=i32) < n_real
    hi = jnp.where(real, hi, lo)
    prev_exp = jnp.concatenate([jnp.full((1,), -1, i32), exp[:-1]])
    prev_blk = jnp.concatenate([jnp.full((1,), -1, i32), blk[:-1]])
    next_blk = jnp.concatenate([blk[1:], jnp.full((1,), -1, i32)])
    is_last = jnp.arange(n_items, dtype=i32) == n_real - 1
    newexp = jnp.logical_and(real, exp != prev_exp).astype(i32)
    newblk = jnp.logical_and(real, blk != prev_blk).astype(i32)
    endblk = jnp.logical_and(real, jnp.logical_or(blk != next_blk, is_last)).astype(i32)
    return blk.astype(i32), exp.astype(i32), lo.astype(i32), hi.astype(i32), newexp, newblk, endblk


def _layer(x, mix_norm_w, w_in, conv_w, q_norm_w, k_norm_w, sinks, conv_out_norm_w,
           attn_out_norm_w, w_out, ffn_norm_w, w_router, router_bias, w_eg, w_eu, w_ed,
           w_sg, w_su, w_sd):
    B, S, D = x.shape
    N = B * S
    T = MIX_TILE
    nt = N // T
    bf16 = jnp.bfloat16
    f32 = jnp.float32
    i32 = jnp.int32

    wr_t = w_router.astype(f32).T
    wr_hi = wr_t.astype(bf16)
    wr_lo = (wr_t - wr_hi.astype(f32)).astype(bf16)
    sink_col = jnp.broadcast_to(
        jnp.repeat(sinks.astype(f32), BLK).reshape(N_KV_HEADS, GQA_GROUP * BLK, 1),
        (N_KV_HEADS, GQA_GROUP * BLK, LANES))
    tri = jnp.asarray(np.triu(np.ones((T, T), np.float32), k=1), dtype=bf16)
    head_ones = jnp.asarray(np.kron(np.eye(D_ATTN // HEAD_DIM, dtype=np.float32),
                                    np.ones((HEAD_DIM, HEAD_DIM), np.float32)), dtype=bf16)
    bias_tab = jnp.asarray(_attn_bias_table())

    x2, xn_rp, wts, ld, cnt = _mixer_call(
        x, mix_norm_w.reshape(1, D), w_in.astype(bf16), conv_w,
        jnp.tile(q_norm_w, N_HEADS).reshape(1, D_ATTN), jnp.tile(k_norm_w, N_KV_HEADS).reshape(1, D_KV),
        sink_col, bias_tab, conv_out_norm_w.reshape(1, D_CONV), attn_out_norm_w.reshape(1, D_ATTN),
        w_out.astype(bf16), ffn_norm_w.reshape(1, D), w_sg.astype(bf16), w_su.astype(bf16),
        w_sd.astype(bf16), wr_hi, wr_lo, router_bias.astype(f32).reshape(N_EXPERTS, 1), tri, head_ones)

    cnt_te = cnt[:, 0].reshape(nt, N_EXPERTS)
    lsrc = jnp.cumsum(cnt_te, axis=1) - cnt_te
    before = jnp.cumsum(cnt_te, axis=0) - cnt_te
    totals = jnp.sum(cnt_te, axis=0)
    gstart = jnp.cumsum(totals) - totals
    gdst = gstart[None, :] + before
    n_items = (N * TOP_K) // EXPERT_BLOCK + N_EXPERTS
    items = _work_items(gstart.astype(i32), totals.astype(i32), n_items)

    def tile_major(a):
        return a.reshape(TOP_K, nt, T).transpose(1, 0, 2).reshape(nt, 1, TILE_ROWS)

    ld_t = tile_major(ld)
    seg_tabs = (lsrc.reshape(-1).astype(i32), gdst.reshape(-1).astype(i32), cnt_te.reshape(-1).astype(i32))
    xs_rp = _dispatch_call(*seg_tabs, ld_t, xn_rp)
    ys_rp = _expert_call(*items, xs_rp, w_eg, w_eu, w_ed)
    out = _combine_call(*seg_tabs, ld_t, tile_major(wts), x2, ys_rp)
    return out.reshape(B, S, D)


def kernel(x, mix_norm_w, w_in, conv_w, q_norm_w, k_norm_w, sinks, conv_out_norm_w, attn_out_norm_w, w_out, ffn_norm_w, w_router, router_bias, w_exp_gate, w_exp_up, w_exp_down, w_sh_gate, w_sh_up, w_sh_down):
    return _layer(x, mix_norm_w[0], w_in[0], conv_w[0], q_norm_w[0], k_norm_w[0], sinks[0],
                  conv_out_norm_w[0], attn_out_norm_w[0], w_out[0], ffn_norm_w[0], w_router[0],
                  router_bias[0], w_exp_gate[0], w_exp_up[0], w_exp_down[0], w_sh_gate[0],
                  w_sh_up[0], w_sh_down[0])
```

```python
import numpy as np
import jax
import jax.numpy as jnp
from jax import lax
from jax.experimental import pallas as pl
from jax.experimental.pallas import tpu as pltpu

D_MODEL = 1024
EPS = 1e-6
D_CONV = 512
CONV_K = 3
HEAD_DIM = 64
N_HEADS = 8
N_KV_HEADS = 2
GQA_GROUP = 4
D_ATTN = 512
D_KV = 128
WINDOW = 128
BLK = 128
D_IN_PROJ = 2304
N_EXPERTS = 64
TOP_K = 8
N_GROUPS = 8
GROUP_SIZE = 8
TOPK_GROUPS = 4
D_EXPERT = 256
ROUTED_SCALE = 2.5

LANES = 128
ROW_SUB = 8
MIX_TILE = 512
TILE_ROWS = TOP_K * MIX_TILE
EXPERT_BLOCK = 1024
EXPERT_SUB = 256
EXPERT_IN_SLOTS = 4
EXPERT_OUT_SLOTS = 3
SEG_CHUNK_LOG2 = 5
NEG_BIG = -1e30
VMEM_LIMIT = 58 * 1024 * 1024


def _alibi_slopes():
    return np.array([2.0 ** (-8.0 * (h + 1) / N_HEADS) for h in range(N_HEADS)], dtype=np.float32)


def _attn_bias_table():
    qi = np.arange(BLK)[:, None]
    kj = np.arange(2 * BLK)[None, :]
    dist = qi - kj + BLK
    inwin = (dist >= 0) & (dist < WINDOW)
    slopes = _alibi_slopes()
    out = np.zeros((N_KV_HEADS, GQA_GROUP * BLK, 2 * BLK), np.float32)
    for g in range(N_KV_HEADS):
        for i in range(GQA_GROUP):
            h = g * GQA_GROUP + i
            out[g, i * BLK:(i + 1) * BLK] = np.where(inwin, -slopes[h] * dist.astype(np.float32), NEG_BIG)
    return out


def _rms(x, w):
    ms = jnp.mean(x * x, axis=-1, keepdims=True)
    return x * lax.rsqrt(ms + EPS) * w


def _head_rms(x, w_tiled, head_ones):
    C = x.shape[1]
    sq = x * x
    hi = sq.astype(jnp.bfloat16)
    lo = (sq - hi.astype(jnp.float32)).astype(jnp.bfloat16)
    ones = head_ones[0:C, 0:C]
    ssq = _dot(hi, ones) + _dot(lo, ones)
    return x * lax.rsqrt(ssq * (1.0 / HEAD_DIM) + EPS) * w_tiled


def _dot(a, b):
    return jnp.dot(a, b, preferred_element_type=jnp.float32)


def _dot_nt(a, b):
    return lax.dot_general(a, b, (((1,), (1,)), ((), ())), preferred_element_type=jnp.float32)


def _mixer_kernel(x_ref, mixw_ref, win_ref, convw_ref, qw_ref, kw_ref, sink_ref, bias_ref,
                  cnw_ref, anw_ref, wout_ref, fnw_ref, wsg_ref, wsu_ref, wsd_ref,
                  wrhl_ref, rbias_ref, tri_ref, hones_ref,
                  x2_ref, xn_ref, wts_ref, ld_ref, cnt_ref,
                  kc_ref, vc_ref, cc_ref):
    j = pl.program_id(1)
    T = MIX_TILE
    bf16 = jnp.bfloat16
    f32 = jnp.float32

    @pl.when(j == 0)
    def _():
        kc_ref[...] = jnp.zeros_like(kc_ref)
        vc_ref[...] = jnp.zeros_like(vc_ref)
        cc_ref[...] = jnp.zeros_like(cc_ref)

    x = x_ref[0]
    u = _rms(x, mixw_ref[...]).astype(bf16)
    proj = _dot(u, win_ref[...])
    b_gate = proj[:, 0:512]
    c_gate = proj[:, 512:1024]
    hh = proj[:, 1024:1536]
    q = proj[:, 1536:2048]
    k = proj[:, 2048:2176]
    v = proj[:, 2176:2304]

    ch = c_gate * hh
    prev = cc_ref[...]
    p6 = prev[6:7, :]
    p7 = prev[7:8, :]
    row = lax.broadcasted_iota(jnp.int32, (T, D_CONV), 0)
    ch_m1 = jnp.where(row == 0, p7, pltpu.roll(ch, 1, axis=0))
    ch_m2 = jnp.where(row == 0, p6, jnp.where(row == 1, p7, pltpu.roll(ch, 2, axis=0)))
    cw = convw_ref[...]
    y_conv = b_gate * (cw[0:1, :] * ch_m2 + cw[1:2, :] * ch_m1 + cw[2:3, :] * ch)
    cc_ref[...] = ch[T - 8:T, :]

    lane = lax.broadcasted_iota(jnp.int32, (1, LANES), 1)
    lane_lo = lane < HEAD_DIM
    qn = _head_rms(q, qw_ref[...], hones_ref) * (HEAD_DIM ** -0.5)
    kn = _head_rms(k, kw_ref[...], hones_ref)
    kfull = jnp.concatenate([kc_ref[...], kn], axis=0)
    vfull = jnp.concatenate([vc_ref[...], v], axis=0)
    kc_ref[...] = kn[T - BLK:T, :]
    vc_ref[...] = v[T - BLK:T, :]

    def _rep(a, g):
        r = pltpu.roll(a, HEAD_DIM, axis=1)
        two = jnp.where(lane_lo, a, r) if g == 0 else jnp.where(lane_lo, r, a)
        return jnp.concatenate([two, two], axis=1).astype(bf16)

    k_rep = [_rep(kfull, g) for g in range(N_KV_HEADS)]
    v_rep = [_rep(vfull, g) for g in range(N_KV_HEADS)]

    lane256 = lax.broadcasted_iota(jnp.int32, (1, 2 * LANES), 1)
    head_of_lane = lane256 // HEAD_DIM
    first_f = (j == 0).astype(f32)
    prev_key_mask = jnp.where(lane256 < BLK, first_f * NEG_BIG, 0.0)
    ones_cols = jnp.ones((2 * BLK, LANES), bf16)

    def lanes2(a):
        return jnp.concatenate([a, a], axis=1)

    attn_rows = []
    for i in range(T // BLK):
        grp_out = []
        for g in range(N_KV_HEADS):
            qg = qn[i * BLK:(i + 1) * BLK, g * 256:(g + 1) * 256]
            qst = jnp.concatenate(
                [jnp.where(head_of_lane == hi, qg, 0.0) for hi in range(GQA_GROUP)], axis=0).astype(bf16)
            kk = k_rep[g][i * BLK:i * BLK + 2 * BLK, :]
            vv = jnp.concatenate([v_rep[g][i * BLK:i * BLK + 2 * BLK, :], ones_cols], axis=1)
            s = _dot_nt(qst, kk) + bias_ref[g]
            if i == 0:
                s = s + prev_key_mask
            sink = sink_ref[g]
            m = jnp.maximum(jnp.max(s, axis=-1, keepdims=True), sink)
            e = jnp.exp(s - lanes2(m)).astype(bf16)
            r = _dot(e, vv)
            inv = 1.0 / (r[:, 2 * LANES:3 * LANES] + jnp.exp(sink - m))
            r = r[:, 0:2 * LANES] * lanes2(inv)
            o = jnp.where(head_of_lane == 0, r[0:BLK], 0.0)
            for hi in range(1, GQA_GROUP):
                o = jnp.where(head_of_lane == hi, r[hi * BLK:(hi + 1) * BLK], o)
            grp_out.append(o)
        attn_rows.append(jnp.concatenate(grp_out, axis=1))
    y_attn = jnp.concatenate(attn_rows, axis=0)

    y_mix = jnp.concatenate([_rms(y_conv, cnw_ref[...]), _rms(y_attn, anw_ref[...])], axis=1)
    x1 = x + _dot(y_mix.astype(bf16), wout_ref[...])

    xn = _rms(x1, fnw_ref[...])
    for c in range(ROW_SUB):
        xn_ref[pl.ds(c, T, stride=ROW_SUB), :] = xn[:, c * LANES:(c + 1) * LANES]
    xh = xn.astype(bf16)
    gs = _dot(xh, wsg_ref[...])
    us = _dot(xh, wsu_ref[...])
    hs = (gs * jax.nn.sigmoid(gs) * us).astype(bf16)
    x2 = x1 + _dot(hs, wsd_ref[...])
    for c in range(ROW_SUB):
        x2_ref[pl.ds(c, T, stride=ROW_SUB), :] = x2[:, c * LANES:(c + 1) * LANES]

    xl = (xn - xh.astype(f32)).astype(bf16)
    w_hl = wrhl_ref[...]
    a_hl = _dot_nt(w_hl, xh)
    logits = a_hl[0:N_EXPERTS] + a_hl[N_EXPERTS:2 * N_EXPERTS] + _dot_nt(w_hl[0:N_EXPERTS], xl)
    scores = jax.nn.sigmoid(logits)
    biased = scores + rbias_ref[...]

    sub8 = lax.broadcasted_iota(jnp.int32, (GROUP_SIZE, T), 0)
    gscore = []
    for g in range(N_GROUPS):
        blk = biased[g * GROUP_SIZE:(g + 1) * GROUP_SIZE, :]
        m1 = jnp.max(blk, axis=0, keepdims=True)
        first = jnp.min(jnp.where(blk == m1, sub8, GROUP_SIZE), axis=0, keepdims=True)
        m2 = jnp.max(jnp.where(sub8 == first, -jnp.inf, blk), axis=0, keepdims=True)
        gscore.append(m1 + m2)
    masked_blocks = []
    for g in range(N_GROUPS):
        rank = jnp.zeros((1, T), jnp.int32)
        for o_ in range(N_GROUPS):
            if o_ == g:
                continue
            if o_ < g:
                ahead = gscore[o_] >= gscore[g]
            else:
                ahead = gscore[o_] > gscore[g]
            rank = rank + ahead.astype(jnp.int32)
        keep = rank < TOPK_GROUPS
        blk = biased[g * GROUP_SIZE:(g + 1) * GROUP_SIZE, :]
        masked_blocks.append(jnp.where(keep, blk, -jnp.inf))
    cur = jnp.concatenate(masked_blocks, axis=0)

    eiota = lax.broadcasted_iota(jnp.int32, (N_EXPERTS, T), 0)
    row8 = lax.broadcasted_iota(jnp.int32, (TOP_K, T), 0)
    w_out = jnp.zeros((TOP_K, T), f32)
    sel_dense = jnp.zeros((N_EXPERTS, T), f32)
    onehots = []
    for kk_ in range(TOP_K):
        mx = jnp.max(cur, axis=0, keepdims=True)
        sel_idx = jnp.min(jnp.where(cur == mx, eiota, N_EXPERTS), axis=0, keepdims=True)
        onehot = eiota == sel_idx
        w_k = jnp.sum(jnp.where(onehot, scores, 0.0), axis=0, keepdims=True)
        cur = jnp.where(onehot, -jnp.inf, cur)
        sel_dense = jnp.where(onehot, 1.0, sel_dense)
        w_out = jnp.where(row8 == kk_, w_k, w_out)
        onehots.append(onehot)
    wsum = jnp.sum(w_out, axis=0, keepdims=True)
    wts_ref[...] = w_out / wsum * ROUTED_SCALE

    cum = _dot(sel_dense.astype(bf16), tri_ref[...])
    cnt = jnp.broadcast_to(jnp.sum(sel_dense, axis=1, keepdims=True), (N_EXPERTS, LANES))
    erow = lax.broadcasted_iota(jnp.int32, (N_EXPERTS, LANES), 0)
    incl = cnt
    step = 1
    while step < N_EXPERTS:
        incl = incl + jnp.where(erow >= step, pltpu.roll(incl, step, axis=0), 0.0)
        step *= 2
    lstart = incl - cnt
    ld_dense = lstart[:, 0:1] + cum
    ld_out = jnp.zeros((TOP_K, T), f32)
    for kk_ in range(TOP_K):
        p_k = jnp.sum(jnp.where(onehots[kk_], ld_dense, 0.0), axis=0, keepdims=True)
        ld_out = jnp.where(row8 == kk_, p_k, ld_out)
    tile_idx = pl.program_id(0) * pl.num_programs(1) + j
    slot_base = ((tile_idx & 1) * TILE_ROWS).astype(f32)
    ld_ref[...] = ((ld_out + slot_base) * ROW_SUB).astype(jnp.int32)
    cnt_ref[...] = cnt.astype(jnp.int32)


def _mixer_call(x, mixw, win, convw, qw, kw, sink_col, bias_tab, cnw, anw, wout, fnw,
                wsg, wsu, wsd, wrhl, rbias, tri, head_ones):
    B, S, D = x.shape
    T = MIX_TILE
    nt = S // T
    N = B * S

    def full(a):
        nd = a.ndim
        return pl.BlockSpec(a.shape, lambda b, j, _nd=nd: (0,) * _nd)

    tok_spec = pl.BlockSpec((TOP_K, T), lambda b, j: (0, b * nt + j))
    in_arrays = [mixw, win, convw, qw, kw, sink_col, bias_tab, cnw, anw, wout, fnw,
                 wsg, wsu, wsd, wrhl, rbias, tri, head_ones]
    return pl.pallas_call(
        _mixer_kernel,
        grid=(B, nt),
        in_specs=[pl.BlockSpec((1, T, D), lambda b, j: (b, j, 0))] + [full(a) for a in in_arrays],
        out_specs=[
            pl.BlockSpec((T * ROW_SUB, LANES), lambda b, j: (b * nt + j, 0)),
            pl.BlockSpec((T * ROW_SUB, LANES), lambda b, j: (b * nt + j, 0)),
            tok_spec, tok_spec,
            pl.BlockSpec((N_EXPERTS, LANES), lambda b, j: (b * nt + j, 0)),
        ],
        out_shape=[
            jax.ShapeDtypeStruct((N * ROW_SUB, LANES), jnp.float32),
            jax.ShapeDtypeStruct((N * ROW_SUB, LANES), jnp.float32),
            jax.ShapeDtypeStruct((TOP_K, N), jnp.float32),
            jax.ShapeDtypeStruct((TOP_K, N), jnp.int32),
            jax.ShapeDtypeStruct((B * nt * N_EXPERTS, LANES), jnp.int32),
        ],
        scratch_shapes=[
            pltpu.VMEM((BLK, D_KV), jnp.float32),
            pltpu.VMEM((BLK, D_KV), jnp.float32),
            pltpu.VMEM((8, D_CONV), jnp.float32),
        ],
        compiler_params=pltpu.CompilerParams(
            dimension_semantics=("arbitrary", "arbitrary"), vmem_limit_bytes=VMEM_LIMIT),
        name="mixer_router",
    )(x, *in_arrays)


def _rows(ref, row, nrows):
    return ref.at[pl.ds(pl.multiple_of(row * ROW_SUB, ROW_SUB), nrows * ROW_SUB)]


def _segment_copies(src, src_row, dst, dst_row, count, sem):
    chunk = 1 << SEG_CHUNK_LOG2

    def big(i, c):
        o = i * chunk
        pltpu.make_async_copy(_rows(src, src_row + o, chunk), _rows(dst, dst_row + o, chunk), sem).start()
        return c

    lax.fori_loop(0, count >> SEG_CHUNK_LOG2, big, 0)
    for bit in range(SEG_CHUNK_LOG2 - 1, -1, -1):
        o = (count >> (bit + 1)) << (bit + 1)

        @pl.when(((count >> bit) & 1) == 1)
        def _():
            n = 1 << bit
            pltpu.make_async_copy(_rows(src, src_row + o, n), _rows(dst, dst_row + o, n), sem).start()


def _wait_tile(hbm_ref, stag_ref, slot_row, sem, to_hbm):
    vm = _rows(stag_ref, slot_row, TILE_ROWS)
    hb = _rows(hbm_ref, 0, TILE_ROWS)
    (pltpu.make_async_copy(vm, hb, sem) if to_hbm else pltpu.make_async_copy(hb, vm, sem)).wait()


def _dispatch_kernel(lsrc_ref, gdst_ref, cnt_ref, ld_ref, xn_ref, xs_ref, stag_ref, sems):
    b = pl.program_id(0)
    nb = pl.num_programs(0)
    T = MIX_TILE
    slot = b & 1
    base = slot * TILE_ROWS

    for kk_ in range(TOP_K):
        def row(t, c, kk_=kk_):
            r8 = ld_ref[0, 0, kk_ * T + t]
            stag_ref[pl.ds(pl.multiple_of(r8, ROW_SUB), ROW_SUB), :] = (
                xn_ref[pl.ds(pl.multiple_of(t * ROW_SUB, ROW_SUB), ROW_SUB), :])
            return c

        lax.fori_loop(0, T, row, 0, unroll=32)

    def seg(e, c):
        s = b * N_EXPERTS + e
        _segment_copies(stag_ref, base + lsrc_ref[s], xs_ref, gdst_ref[s], cnt_ref[s], sems.at[slot])
        return c

    lax.fori_loop(0, N_EXPERTS, seg, 0)

    @pl.when(b > 0)
    def _():
        _wait_tile(xs_ref, stag_ref, (1 - slot) * TILE_ROWS, sems.at[1 - slot], True)

    @pl.when(b == nb - 1)
    def _():
        _wait_tile(xs_ref, stag_ref, base, sems.at[slot], True)


def _dispatch_call(lsrc, gdst, cnt, ld_t, xn_rp):
    nt = ld_t.shape[0]
    T = MIX_TILE
    smem = pl.BlockSpec(memory_space=pltpu.SMEM)
    return pl.pallas_call(
        _dispatch_kernel,
        grid=(nt,),
        in_specs=[
            smem, smem, smem,
            pl.BlockSpec((1, 1, TILE_ROWS), lambda i: (i, 0, 0), memory_space=pltpu.SMEM),
            pl.BlockSpec((T * ROW_SUB, LANES), lambda i: (i, 0)),
        ],
        out_specs=pl.BlockSpec(memory_space=pl.ANY),
        out_shape=jax.ShapeDtypeStruct((nt * TILE_ROWS * ROW_SUB, LANES), jnp.float32),
        scratch_shapes=[pltpu.VMEM((2 * TILE_ROWS * ROW_SUB, LANES), jnp.float32),
                        pltpu.SemaphoreType.DMA((2,))],
        compiler_params=pltpu.CompilerParams(
            dimension_semantics=("arbitrary",), vmem_limit_bytes=VMEM_LIMIT),
        name="moe_dispatch",
    )(lsrc, gdst, cnt, ld_t, xn_rp)


def _expert_kernel(blk_ref, exp_ref, lo_ref, hi_ref, newexp_ref, newblk_ref, endblk_ref,
                   xs_hbm, wg_ref, wu_ref, wd_ref, ys_hbm,
                   xbuf_ref, ybuf_ref, wgub_ref, wdb_ref, in_sems, out_sems):
    i = pl.program_id(0)
    n_items = pl.num_programs(0)
    bf16 = jnp.bfloat16
    SB = EXPERT_SUB
    BS = EXPERT_BLOCK * ROW_SUB
    n_blocks = xs_hbm.shape[0] // BS
    blk = blk_ref[i]
    lo = lo_ref[i]
    hi = hi_ref[i]

    def block_rows(idx):
        start = idx * BS
        return pl.ds(start if isinstance(start, int) else pl.multiple_of(start, BS), BS)

    def in_copy(b, slot):
        return pltpu.make_async_copy(xs_hbm.at[block_rows(b)], xbuf_ref.at[block_rows(slot)], in_sems.at[slot])

    def out_copy(b, slot):
        return pltpu.make_async_copy(ybuf_ref.at[block_rows(slot)], ys_hbm.at[block_rows(b)], out_sems.at[slot])

    @pl.when(i == 0)
    def _():
        for b0 in range(EXPERT_IN_SLOTS - 1):
            in_copy(b0, b0).start()

    @pl.when(newblk_ref[i] == 1)
    def _():
        ahead = blk + (EXPERT_IN_SLOTS - 1)

        @pl.when(ahead < n_blocks)
        def _():
            in_copy(ahead, lax.rem(ahead, EXPERT_IN_SLOTS)).start()

        in_copy(blk, lax.rem(blk, EXPERT_IN_SLOTS)).wait()

        @pl.when(blk >= EXPERT_OUT_SLOTS)
        def _():
            out_copy(blk - EXPERT_OUT_SLOTS, lax.rem(blk, EXPERT_OUT_SLOTS)).wait()

    @pl.when(newexp_ref[i] == 1)
    def _():
        wgub_ref[:, 0:D_EXPERT] = wg_ref[0].astype(bf16)
        wgub_ref[:, D_EXPERT:2 * D_EXPERT] = wu_ref[0].astype(bf16)
        wdb_ref[...] = wd_ref[0].astype(bf16)

    xbase = lax.rem(blk, EXPERT_IN_SLOTS) * BS
    ybase = lax.rem(blk, EXPERT_OUT_SLOTS) * BS

    def ffn(chunk):
        xb = jnp.concatenate([xbuf_ref[chunk(xbase, c), :] for c in range(ROW_SUB)], axis=1).astype(bf16)
        gu = _dot(xb, wgub_ref[...])
        g = gu[:, 0:D_EXPERT]
        u = gu[:, D_EXPERT:2 * D_EXPERT]
        h = (g * jax.nn.sigmoid(g) * u).astype(bf16)
        return _dot(h, wdb_ref[...])

    whole = jnp.logical_and(lo == 0, hi == EXPERT_BLOCK)

    @pl.when(whole)
    def _():
        def chunk(base, c):
            return pl.ds(base + c, EXPERT_BLOCK, stride=ROW_SUB)

        y = ffn(chunk)
        for c in range(ROW_SUB):
            ybuf_ref[chunk(ybase, c), :] = y[:, c * LANES:(c + 1) * LANES]

    for q in range(EXPERT_BLOCK // SB):
        r0 = q * SB

        def chunk(base, c, r0=r0):
            return pl.ds(base + r0 * ROW_SUB + c, SB, stride=ROW_SUB)

        @pl.when(jnp.logical_and(jnp.logical_not(whole), jnp.minimum(hi, r0 + SB) > jnp.maximum(lo, r0)))
        def _(r0=r0, chunk=chunk):
            y = ffn(chunk)

            @pl.when(lo <= r0)
            def _():
                for c in range(ROW_SUB):
                    ybuf_ref[chunk(ybase, c), :] = y[:, c * LANES:(c + 1) * LANES]

            @pl.when(lo > r0)
            def _():
                rows = r0 + lax.broadcasted_iota(jnp.int32, (SB, LANES), 0)
                keep = jnp.logical_and(rows >= lo, rows < hi)
                for c in range(ROW_SUB):
                    old = ybuf_ref[chunk(ybase, c), :]
                    ybuf_ref[chunk(ybase, c), :] = jnp.where(keep, y[:, c * LANES:(c + 1) * LANES], old)

    @pl.when(endblk_ref[i] == 1)
    def _():
        out_copy(blk, lax.rem(blk, EXPERT_OUT_SLOTS)).start()

    @pl.when(i == n_items - 1)
    def _():
        for b1 in range(n_blocks - EXPERT_OUT_SLOTS, n_blocks):
            out_copy(b1, b1 % EXPERT_OUT_SLOTS).wait()


def _expert_call(item_blk, item_exp, item_lo, item_hi, item_newexp, item_newblk, item_endblk,
                 xs_rp, w_eg, w_eu, w_ed):
    D = D_MODEL
    BS = EXPERT_BLOCK * ROW_SUB
    n_items = item_blk.shape[0]
    assert xs_rp.shape[0] % BS == 0 and xs_rp.shape[0] // BS >= max(EXPERT_IN_SLOTS, EXPERT_OUT_SLOTS)

    def w_map(i, blk, exp, lo, hi, newexp, newblk, endblk):
        return (exp[i], 0, 0)

    grid_spec = pltpu.PrefetchScalarGridSpec(
        num_scalar_prefetch=7,
        grid=(n_items,),
        in_specs=[
            pl.BlockSpec(memory_space=pl.ANY),
            pl.BlockSpec((1, D, D_EXPERT), w_map),
            pl.BlockSpec((1, D, D_EXPERT), w_map),
            pl.BlockSpec((1, D_EXPERT, D), w_map),
        ],
        out_specs=pl.BlockSpec(memory_space=pl.ANY),
        scratch_shapes=[
            pltpu.VMEM((EXPERT_IN_SLOTS * BS, LANES), jnp.float32),
            pltpu.VMEM((EXPERT_OUT_SLOTS * BS, LANES), jnp.float32),
            pltpu.VMEM((D, 2 * D_EXPERT), jnp.bfloat16),
            pltpu.VMEM((D_EXPERT, D), jnp.bfloat16),
            pltpu.SemaphoreType.DMA((EXPERT_IN_SLOTS,)),
            pltpu.SemaphoreType.DMA((EXPERT_OUT_SLOTS,)),
        ],
    )
    return pl.pallas_call(
        _expert_kernel,
        grid_spec=grid_spec,
        out_shape=jax.ShapeDtypeStruct(xs_rp.shape, jnp.float32),
        compiler_params=pltpu.CompilerParams(
            dimension_semantics=("arbitrary",), vmem_limit_bytes=VMEM_LIMIT),
        name="moe_experts",
    )(item_blk, item_exp, item_lo, item_hi, item_newexp, item_newblk, item_endblk, xs_rp, w_eg, w_eu, w_ed)


def _combine_kernel(lsrc_ref, gdst_ref, cnt_ref, ld_ref, w_ref, x2_ref, ys_ref, out_ref,
                    stag_ref, acc_ref, sems):
    b = pl.program_id(0)
    nb = pl.num_programs(0)
    T = MIX_TILE
    slot = b & 1
    base = slot * TILE_ROWS

    def fetch(tile, slot_):
        def seg(e, c):
            s = tile * N_EXPERTS + e
            _segment_copies(ys_ref, gdst_ref[s], stag_ref, slot_ * TILE_ROWS + lsrc_ref[s], cnt_ref[s],
                            sems.at[slot_])
            return c

        lax.fori_loop(0, N_EXPERTS, seg, 0)

    @pl.when(b == 0)
    def _():
        fetch(b, slot)

    @pl.when(b + 1 < nb)
    def _():
        fetch(b + 1, 1 - slot)

    _wait_tile(ys_ref, stag_ref, base, sems.at[slot], False)

    def tok(t, c):
        at = pl.ds(pl.multiple_of(t * ROW_SUB, ROW_SUB), ROW_SUB)
        a = x2_ref[at, :]
        for kk_ in range(TOP_K):
            r8 = ld_ref[0, 0, kk_ * T + t]
            w = w_ref[0, 0, kk_ * T + t]
            a = a + w * stag_ref[pl.ds(pl.multiple_of(r8, ROW_SUB), ROW_SUB), :]
        acc_ref[at, :] = a
        return c

    lax.fori_loop(0, T, tok, 0, unroll=4)
    for c in range(ROW_SUB):
        out_ref[:, c * LANES:(c + 1) * LANES] = acc_ref[pl.ds(c, T, stride=ROW_SUB), :]


def _combine_call(lsrc, gdst, cnt, ld_t, w_t, x2, ys_rp):
    D = D_MODEL
    N = x2.shape[0] // ROW_SUB
    nt = ld_t.shape[0]
    T = MIX_TILE
    smem = pl.BlockSpec(memory_space=pltpu.SMEM)
    smem_blk = pl.BlockSpec((1, 1, TILE_ROWS), lambda i: (i, 0, 0), memory_space=pltpu.SMEM)
    return pl.pallas_call(
        _combine_kernel,
        grid=(nt,),
        in_specs=[
            smem, smem, smem, smem_blk, smem_blk,
            pl.BlockSpec((T * ROW_SUB, LANES), lambda i: (i, 0)),
            pl.BlockSpec(memory_space=pl.ANY),
        ],
        out_specs=pl.BlockSpec((T, D), lambda i: (i, 0)),
        out_shape=jax.ShapeDtypeStruct((N, D), jnp.float32),
        scratch_shapes=[
            pltpu.VMEM((2 * TILE_ROWS * ROW_SUB, LANES), jnp.float32),
            pltpu.VMEM((T * ROW_SUB, LANES), jnp.float32),
            pltpu.SemaphoreType.DMA((2,)),
        ],
        compiler_params=pltpu.CompilerParams(
            dimension_semantics=("arbitrary",), vmem_limit_bytes=VMEM_LIMIT),
        name="moe_combine",
    )(lsrc, gdst, cnt, ld_t, w_t, x2, ys_rp)


def _work_items(gstart, totals, n_items):
    BM = EXPERT_BLOCK
    i32 = jnp.int32
    gend = gstart + totals
    first_blk = gstart // BM
    last_blk = jnp.maximum(gend - 1, gstart) // BM
    n_e = jnp.where(totals > 0, last_blk - first_blk + 1, 0)
    item_end = jnp.cumsum(n_e)
    item_start = item_end - n_e
    n_real = item_end[-1]
    j = jnp.minimum(jnp.arange(n_items, dtype=i32), n_real - 1)
    exp = jnp.sum((item_end[None, :] <= j[:, None]).astype(i32), axis=1)
    onehot = exp[:, None] == jnp.arange(N_EXPERTS, dtype=i32)[None, :]

    def of_exp(v):
        return jnp.sum(jnp.where(onehot, v[None, :], 0), axis=1)

    blk = of_exp(first_blk) + (j - of_exp(item_start))
    lo = jnp.maximum(of_exp(gstart), blk * BM) - blk * BM
    hi = jnp.minimum(of_exp(gend), (blk + 1) * BM) - blk * BM
    real = jnp.arange(n_items, dtype=i32) < n_real
    hi = jnp.where(real, hi, lo)
    prev_exp = jnp.concatenate([jnp.full((1,), -1, i32), exp[:-1]])
    prev_blk = jnp.concatenate([jnp.full((1,), -1, i32), blk[:-1]])
    next_blk = jnp.concatenate([blk[1:], jnp.full((1,), -1, i32)])
    is_last = jnp.arange(n_items, dtype=i32) == n_real - 1
    newexp = jnp.logical_and(real, exp != prev_exp).astype(i32)
    newblk = jnp.logical_and(real, blk != prev_blk).astype(i32)
    endblk = jnp.logical_and(real, jnp.logical_or(blk != next_blk, is_last)).astype(i32)
    return blk.astype(i32), exp.astype(i32), lo.astype(i32), hi.astype(i32), newexp, newblk, endblk


def _layer(x, mix_norm_w, w_in, conv_w, q_norm_w, k_norm_w, sinks, conv_out_norm_w,
           attn_out_norm_w, w_out, ffn_norm_w, w_router, router_bias, w_eg, w_eu, w_ed,
           w_sg, w_su, w_sd):
    B, S, D = x.shape
    N = B * S
    T = MIX_TILE
    nt = N // T
    bf16 = jnp.bfloat16
    f32 = jnp.float32
    i32 = jnp.int32

    wr_t = w_router.astype(f32).T
    wr_hi = wr_t.astype(bf16)
    wr_hl = jnp.concatenate([wr_hi, (wr_t - wr_hi.astype(f32)).astype(bf16)], axis=0)
    sink_col = jnp.broadcast_to(
        jnp.repeat(sinks.astype(f32), BLK).reshape(N_KV_HEADS, GQA_GROUP * BLK, 1),
        (N_KV_HEADS, GQA_GROUP * BLK, LANES))
    tri = jnp.asarray(np.triu(np.ones((T, T), np.float32), k=1), dtype=bf16)
    head_ones = jnp.asarray(np.kron(np.eye(D_ATTN // HEAD_DIM, dtype=np.float32),
                                    np.ones((HEAD_DIM, HEAD_DIM), np.float32)), dtype=bf16)
    bias_tab = jnp.asarray(_attn_bias_table())

    x2, xn_rp, wts, ld, cnt = _mixer_call(
        x, mix_norm_w.reshape(1, D), w_in.astype(bf16), conv_w,
        jnp.tile(q_norm_w, N_HEADS).reshape(1, D_ATTN), jnp.tile(k_norm_w, N_KV_HEADS).reshape(1, D_KV),
        sink_col, bias_tab, conv_out_norm_w.reshape(1, D_CONV), attn_out_norm_w.reshape(1, D_ATTN),
        w_out.astype(bf16), ffn_norm_w.reshape(1, D), w_sg.astype(bf16), w_su.astype(bf16),
        w_sd.astype(bf16), wr_hl, router_bias.astype(f32).reshape(N_EXPERTS, 1), tri, head_ones)

    cnt_te = cnt[:, 0].reshape(nt, N_EXPERTS)
    lsrc = jnp.cumsum(cnt_te, axis=1) - cnt_te
    before = jnp.cumsum(cnt_te, axis=0) - cnt_te
    totals = jnp.sum(cnt_te, axis=0)
    gstart = jnp.cumsum(totals) - totals
    gdst = gstart[None, :] + before
    n_items = (N * TOP_K) // EXPERT_BLOCK + N_EXPERTS
    items = _work_items(gstart.astype(i32), totals.astype(i32), n_items)

    def tile_major(a):
        return a.reshape(TOP_K, nt, T).transpose(1, 0, 2).reshape(nt, 1, TILE_ROWS)

    ld_t = tile_major(ld)
    seg_tabs = (lsrc.reshape(-1).astype(i32), gdst.reshape(-1).astype(i32), cnt_te.reshape(-1).astype(i32))
    xs_rp = _dispatch_call(*seg_tabs, ld_t, xn_rp)
    ys_rp = _expert_call(*items, xs_rp, w_eg, w_eu, w_ed)
    out = _combine_call(*seg_tabs, ld_t, tile_major(wts), x2, ys_rp)
    return out.reshape(B, S, D)


def kernel(x, mix_norm_w, w_in, conv_w, q_norm_w, k_norm_w, sinks, conv_out_norm_w, attn_out_norm_w, w_out, ffn_norm_w, w_router, router_bias, w_exp_gate, w_exp_up, w_exp_down, w_sh_gate, w_sh_up, w_sh_down):
    return _layer(x, mix_norm_w[0], w_in[0], conv_w[0], q_norm_w[0], k_norm_w[0], sinks[0],
                  conv_out_norm_w[0], attn_out_norm_w[0], w_out[0], ffn_norm_w[0], w_router[0],
                  router_bias[0], w_exp_gate[0], w_exp_up[0], w_exp_down[0], w_sh_gate[0],
                  w_sh_up[0], w_sh_down[0])
```

```python
import numpy as np
import jax
import jax.numpy as jnp
from jax import lax
from jax.experimental import pallas as pl
from jax.experimental.pallas import tpu as pltpu

D_MODEL = 1024
EPS = 1e-6
D_CONV = 512
CONV_K = 3
HEAD_DIM = 64
N_HEADS = 8
N_KV_HEADS = 2
GQA_GROUP = 4
D_ATTN = 512
D_KV = 128
WINDOW = 128
BLK = 128
D_IN_PROJ = 2304
N_EXPERTS = 64
TOP_K = 8
N_GROUPS = 8
GROUP_SIZE = 8
TOPK_GROUPS = 4
D_EXPERT = 256
ROUTED_SCALE = 2.5

LANES = 128
ROW_SUB = 8
MIX_TILE = 512
TILE_ROWS = TOP_K * MIX_TILE
EXPERT_BLOCK = 1024
EXPERT_SUB = 256
EXPERT_IN_SLOTS = 4
EXPERT_OUT_SLOTS = 3
SEG_CHUNK_LOG2 = 5
NEG_BIG = -1e30
VMEM_LIMIT = 58 * 1024 * 1024


def _alibi_slopes():
    return np.array([2.0 ** (-8.0 * (h + 1) / N_HEADS) for h in range(N_HEADS)], dtype=np.float32)


def _attn_bias_table():
    qi = np.arange(BLK)[:, None]
    kj = np.arange(2 * BLK)[None, :]
    dist = qi - kj + BLK
    inwin = (dist >= 0) & (dist < WINDOW)
    slopes = _alibi_slopes()
    out = np.zeros((N_KV_HEADS, GQA_GROUP * BLK, 2 * BLK), np.float32)
    for g in range(N_KV_HEADS):
        for i in range(GQA_GROUP):
            h = g * GQA_GROUP + i
            out[g, i * BLK:(i + 1) * BLK] = np.where(inwin, -slopes[h] * dist.astype(np.float32), NEG_BIG)
    return out


def _rms(x, w):
    ms = jnp.mean(x * x, axis=-1, keepdims=True)
    return x * lax.rsqrt(ms + EPS) * w


def _head_rms(x, w_tiled, head_ones):
    C = x.shape[1]
    sq = x * x
    hi = sq.astype(jnp.bfloat16)
    lo = (sq - hi.astype(jnp.float32)).astype(jnp.bfloat16)
    ones = head_ones[0:C, 0:C]
    ssq = _dot(hi, ones) + _dot(lo, ones)
    return x * lax.rsqrt(ssq * (1.0 / HEAD_DIM) + EPS) * w_tiled


def _dot(a, b):
    return jnp.dot(a, b, preferred_element_type=jnp.float32)


def _dot_nt(a, b):
    return lax.dot_general(a, b, (((1,), (1,)), ((), ())), preferred_element_type=jnp.float32)


def _mixer_kernel(x_ref, mixw_ref, win_ref, convw_ref, qw_ref, kw_ref, sink_ref, bias_ref,
                  cnw_ref, anw_ref, wout_ref, fnw_ref, wsg_ref, wsu_ref, wsd_ref,
                  wrhl_ref, rbias_ref, tri_ref, hones_ref,
                  x2_ref, xn_ref, wts_ref, ld_ref, cnt_ref,
                  kc_ref, vc_ref, cc_ref):
    j = pl.program_id(1)
    T = MIX_TILE
    bf16 = jnp.bfloat16
    f32 = jnp.float32

    @pl.when(j == 0)
    def _():
        kc_ref[...] = jnp.zeros_like(kc_ref)
        vc_ref[...] = jnp.zeros_like(vc_ref)
        cc_ref[...] = jnp.zeros_like(cc_ref)

    x = x_ref[0]
    u = _rms(x, mixw_ref[...]).astype(bf16)
    proj = _dot(u, win_ref[...])
    b_gate = proj[:, 0:512]
    c_gate = proj[:, 512:1024]
    hh = proj[:, 1024:1536]
    q = proj[:, 1536:2048]
    k = proj[:, 2048:2176]
    v = proj[:, 2176:2304]

    ch = c_gate * hh
    prev = cc_ref[...]
    p6 = prev[6:7, :]
    p7 = prev[7:8, :]
    row = lax.broadcasted_iota(jnp.int32, (T, D_CONV), 0)
    ch_m1 = jnp.where(row == 0, p7, pltpu.roll(ch, 1, axis=0))
    ch_m2 = jnp.where(row == 0, p6, jnp.where(row == 1, p7, pltpu.roll(ch, 2, axis=0)))
    cw = convw_ref[...]
    y_conv = b_gate * (cw[0:1, :] * ch_m2 + cw[1:2, :] * ch_m1 + cw[2:3, :] * ch)
    cc_ref[...] = ch[T - 8:T, :]

    lane = lax.broadcasted_iota(jnp.int32, (1, LANES), 1)
    lane_lo = lane < HEAD_DIM
    qn = _head_rms(q, qw_ref[...], hones_ref) * (HEAD_DIM ** -0.5)
    kn = _head_rms(k, kw_ref[...], hones_ref)
    kfull = jnp.concatenate([kc_ref[...], kn], axis=0)
    vfull = jnp.concatenate([vc_ref[...], v], axis=0)
    kc_ref[...] = kn[T - BLK:T, :]
    vc_ref[...] = v[T - BLK:T, :]

    def _rep(a, g):
        r = pltpu.roll(a, HEAD_DIM, axis=1)
        two = jnp.where(lane_lo, a, r) if g == 0 else jnp.where(lane_lo, r, a)
        return jnp.concatenate([two, two], axis=1).astype(bf16)

    k_rep = [_rep(kfull, g) for g in range(N_KV_HEADS)]
    v_rep = [_rep(vfull, g) for g in range(N_KV_HEADS)]

    lane256 = lax.broadcasted_iota(jnp.int32, (1, 2 * LANES), 1)
    head_of_lane = lane256 // HEAD_DIM
    first_f = (j == 0).astype(f32)
    prev_key_mask = jnp.where(lane256 < BLK, first_f * NEG_BIG, 0.0)
    ones_cols = jnp.ones((2 * BLK, LANES), bf16)

    def lanes2(a):
        return jnp.concatenate([a, a], axis=1)

    attn_rows = []
    for i in range(T // BLK):
        grp_out = []
        for g in range(N_KV_HEADS):
            qg = qn[i * BLK:(i + 1) * BLK, g * 256:(g + 1) * 256]
            qst = jnp.concatenate(
                [jnp.where(head_of_lane == hi, qg, 0.0) for hi in range(GQA_GROUP)], axis=0).astype(bf16)
            kk = k_rep[g][i * BLK:i * BLK + 2 * BLK, :]
            vv = jnp.concatenate([v_rep[g][i * BLK:i * BLK + 2 * BLK, :], ones_cols], axis=1)
            s = _dot_nt(qst, kk) + bias_ref[g]
            if i == 0:
                s = s + prev_key_mask
            sink = sink_ref[g]
            m = jnp.maximum(jnp.max(s, axis=-1, keepdims=True), sink)
            e = jnp.exp(s - lanes2(m)).astype(bf16)
            r = _dot(e, vv)
            inv = 1.0 / (r[:, 2 * LANES:3 * LANES] + jnp.exp(sink - m))
            r = r[:, 0:2 * LANES] * lanes2(inv)
            o = jnp.where(head_of_lane == 0, r[0:BLK], 0.0)
            for hi in range(1, GQA_GROUP):
                o = jnp.where(head_of_lane == hi, r[hi * BLK:(hi + 1) * BLK], o)
            grp_out.append(o)
        attn_rows.append(jnp.concatenate(grp_out, axis=1))
    y_attn = jnp.concatenate(attn_rows, axis=0)

    y_mix = jnp.concatenate([_rms(y_conv, cnw_ref[...]), _rms(y_attn, anw_ref[...])], axis=1)
    x1 = x + _dot(y_mix.astype(bf16), wout_ref[...])

    xn = _rms(x1, fnw_ref[...])
    for c in range(ROW_SUB):
        xn_ref[pl.ds(c, T, stride=ROW_SUB), :] = xn[:, c * LANES:(c + 1) * LANES]
    xh = xn.astype(bf16)
    gs = _dot(xh, wsg_ref[...])
    us = _dot(xh, wsu_ref[...])
    hs = (gs * jax.nn.sigmoid(gs) * us).astype(bf16)
    x2 = x1 + _dot(hs, wsd_ref[...])
    for c in range(ROW_SUB):
        x2_ref[pl.ds(c, T, stride=ROW_SUB), :] = x2[:, c * LANES:(c + 1) * LANES]

    xl = (xn - xh.astype(f32)).astype(bf16)
    w_hl = wrhl_ref[...]
    a_hl = _dot_nt(w_hl, xh)
    logits = a_hl[0:N_EXPERTS] + a_hl[N_EXPERTS:2 * N_EXPERTS] + _dot_nt(w_hl[0:N_EXPERTS], xl)
    scores = jax.nn.sigmoid(logits)
    biased = scores + rbias_ref[...]

    sub8 = lax.broadcasted_iota(jnp.int32, (GROUP_SIZE, T), 0)
    gscore = []
    for g in range(N_GROUPS):
        blk = biased[g * GROUP_SIZE:(g + 1) * GROUP_SIZE, :]
        m1 = jnp.max(blk, axis=0, keepdims=True)
        first = jnp.min(jnp.where(blk == m1, sub8, GROUP_SIZE), axis=0, keepdims=True)
        m2 = jnp.max(jnp.where(sub8 == first, -jnp.inf, blk), axis=0, keepdims=True)
        gscore.append(m1 + m2)
    masked_blocks = []
    for g in range(N_GROUPS):
        rank = jnp.zeros((1, T), jnp.int32)
        for o_ in range(N_GROUPS):
            if o_ == g:
                continue
            if o_ < g:
                ahead = gscore[o_] >= gscore[g]
            else:
                ahead = gscore[o_] > gscore[g]
            rank = rank + ahead.astype(jnp.int32)
        keep = rank < TOPK_GROUPS
        blk = biased[g * GROUP_SIZE:(g + 1) * GROUP_SIZE, :]
        masked_blocks.append(jnp.where(keep, blk, -jnp.inf))
    cur = jnp.concatenate(masked_blocks, axis=0)

    eiota = lax.broadcasted_iota(jnp.int32, (N_EXPERTS, T), 0)
    row8 = lax.broadcasted_iota(jnp.int32, (TOP_K, T), 0)
    w_out = jnp.zeros((TOP_K, T), f32)
    sel_dense = jnp.zeros((N_EXPERTS, T), f32)
    onehots = []
    for kk_ in range(TOP_K):
        mx = jnp.max(cur, axis=0, keepdims=True)
        sel_idx = jnp.min(jnp.where(cur == mx, eiota, N_EXPERTS), axis=0, keepdims=True)
        onehot = eiota == sel_idx
        w_k = jnp.sum(jnp.where(onehot, scores, 0.0), axis=0, keepdims=True)
        cur = jnp.where(onehot, -jnp.inf, cur)
        sel_dense = jnp.where(onehot, 1.0, sel_dense)
        w_out = jnp.where(row8 == kk_, w_k, w_out)
        onehots.append(onehot)
    wsum = jnp.sum(w_out, axis=0, keepdims=True)
    wts_ref[...] = w_out / wsum * ROUTED_SCALE

    cum = _dot(sel_dense.astype(bf16), tri_ref[...])
    cnt = jnp.broadcast_to(jnp.sum(sel_dense, axis=1, keepdims=True), (N_EXPERTS, LANES))
    erow = lax.broadcasted_iota(jnp.int32, (N_EXPERTS, LANES), 0)
    incl = cnt
    step = 1
    while step < N_EXPERTS:
        incl = incl + jnp.where(erow >= step, pltpu.roll(incl, step, axis=0), 0.0)
        step *= 2
    lstart = incl - cnt
    ld_dense = lstart[:, 0:1] + cum
    ld_out = jnp.zeros((TOP_K, T), f32)
    for kk_ in range(TOP_K):
        p_k = jnp.sum(jnp.where(onehots[kk_], ld_dense, 0.0), axis=0, keepdims=True)
        ld_out = jnp.where(row8 == kk_, p_k, ld_out)
    tile_idx = pl.program_id(0) * pl.num_programs(1) + j
    slot_base = ((tile_idx & 1) * TILE_ROWS).astype(f32)
    ld_ref[...] = ((ld_out + slot_base) * ROW_SUB).astype(jnp.int32)
    cnt_ref[...] = cnt.astype(jnp.int32)


def _mixer_call(x, mixw, win, convw, qw, kw, sink_col, bias_tab, cnw, anw, wout, fnw,
                wsg, wsu, wsd, wrhl, rbias, tri, head_ones):
    B, S, D = x.shape
    T = MIX_TILE
    nt = S // T
    N = B * S

    def full(a):
        nd = a.ndim
        return pl.BlockSpec(a.shape, lambda b, j, _nd=nd: (0,) * _nd)

    tok_spec = pl.BlockSpec((TOP_K, T), lambda b, j: (0, b * nt + j))
    in_arrays = [mixw, win, convw, qw, kw, sink_col, bias_tab, cnw, anw, wout, fnw,
                 wsg, wsu, wsd, wrhl, rbias, tri, head_ones]
    return pl.pallas_call(
        _mixer_kernel,
        grid=(B, nt),
        in_specs=[pl.BlockSpec((1, T, D), lambda b, j: (b, j, 0))] + [full(a) for a in in_arrays],
        out_specs=[
            pl.BlockSpec((T * ROW_SUB, LANES), lambda b, j: (b * nt + j, 0)),
            pl.BlockSpec((T * ROW_SUB, LANES), lambda b, j: (b * nt + j, 0)),
            tok_spec, tok_spec,
            pl.BlockSpec((N_EXPERTS, LANES), lambda b, j: (b * nt + j, 0)),
        ],
        out_shape=[
            jax.ShapeDtypeStruct((N * ROW_SUB, LANES), jnp.float32),
            jax.ShapeDtypeStruct((N * ROW_SUB, LANES), jnp.float32),
            jax.ShapeDtypeStruct((TOP_K, N), jnp.float32),
            jax.ShapeDtypeStruct((TOP_K, N), jnp.int32),
            jax.ShapeDtypeStruct((B * nt * N_EXPERTS, LANES), jnp.int32),
        ],
        scratch_shapes=[
            pltpu.VMEM((BLK, D_KV), jnp.float32),
            pltpu.VMEM((BLK, D_KV), jnp.float32),
            pltpu.VMEM((8, D_CONV), jnp.float32),
        ],
        compiler_params=pltpu.CompilerParams(
            dimension_semantics=("arbitrary", "arbitrary"), vmem_limit_bytes=VMEM_LIMIT),
        name="mixer_router",
    )(x, *in_arrays)


def _rows(ref, row, nrows):
    return ref.at[pl.ds(pl.multiple_of(row * ROW_SUB, ROW_SUB), nrows * ROW_SUB)]


def _segment_copies(src, src_row, dst, dst_row, count, sem):
    chunk = 1 << SEG_CHUNK_LOG2

    def big(i, c):
        o = i * chunk
        pltpu.make_async_copy(_rows(src, src_row + o, chunk), _rows(dst, dst_row + o, chunk), sem).start()
        return c

    lax.fori_loop(0, count >> SEG_CHUNK_LOG2, big, 0)
    for bit in range(SEG_CHUNK_LOG2 - 1, -1, -1):
        o = (count >> (bit + 1)) << (bit + 1)

        @pl.when(((count >> bit) & 1) == 1)
        def _():
            n = 1 << bit
            pltpu.make_async_copy(_rows(src, src_row + o, n), _rows(dst, dst_row + o, n), sem).start()


def _wait_tile(hbm_ref, stag_ref, slot_row, sem, to_hbm):
    vm = _rows(stag_ref, slot_row, TILE_ROWS)
    hb = _rows(hbm_ref, 0, TILE_ROWS)
    (pltpu.make_async_copy(vm, hb, sem) if to_hbm else pltpu.make_async_copy(hb, vm, sem)).wait()


def _dispatch_kernel(lsrc_ref, gdst_ref, cnt_ref, ld_ref, xn_ref, xs_ref, stag_ref, sems):
    b = pl.program_id(0)
    nb = pl.num_programs(0)
    T = MIX_TILE
    slot = b & 1
    base = slot * TILE_ROWS

    for kk_ in range(TOP_K):
        def row(t, c, kk_=kk_):
            r8 = ld_ref[0, 0, kk_ * T + t]
            stag_ref[pl.ds(pl.multiple_of(r8, ROW_SUB), ROW_SUB), :] = (
                xn_ref[pl.ds(pl.multiple_of(t * ROW_SUB, ROW_SUB), ROW_SUB), :])
            return c

        lax.fori_loop(0, T, row, 0, unroll=32)

    def seg(e, c):
        s = b * N_EXPERTS + e
        _segment_copies(stag_ref, base + lsrc_ref[s], xs_ref, gdst_ref[s], cnt_ref[s], sems.at[slot])
        return c

    lax.fori_loop(0, N_EXPERTS, seg, 0)

    @pl.when(b > 0)
    def _():
        _wait_tile(xs_ref, stag_ref, (1 - slot) * TILE_ROWS, sems.at[1 - slot], True)

    @pl.when(b == nb - 1)
    def _():
        _wait_tile(xs_ref, stag_ref, base, sems.at[slot], True)


def _dispatch_call(lsrc, gdst, cnt, ld_t, xn_rp):
    nt = ld_t.shape[0]
    T = MIX_TILE
    smem = pl.BlockSpec(memory_space=pltpu.SMEM)
    return pl.pallas_call(
        _dispatch_kernel,
        grid=(nt,),
        in_specs=[
            smem, smem, smem,
            pl.BlockSpec((1, 1, TILE_ROWS), lambda i: (i, 0, 0), memory_space=pltpu.SMEM),
            pl.BlockSpec((T * ROW_SUB, LANES), lambda i: (i, 0)),
        ],
        out_specs=pl.BlockSpec(memory_space=pl.ANY),
        out_shape=jax.ShapeDtypeStruct((nt * TILE_ROWS * ROW_SUB, LANES), jnp.float32),
        scratch_shapes=[pltpu.VMEM((2 * TILE_ROWS * ROW_SUB, LANES), jnp.float32),
                        pltpu.SemaphoreType.DMA((2,))],
        compiler_params=pltpu.CompilerParams(
            dimension_semantics=("arbitrary",), vmem_limit_bytes=VMEM_LIMIT),
        name="moe_dispatch",
    )(lsrc, gdst, cnt, ld_t, xn_rp)


def _expert_kernel(blk_ref, exp_ref, lo_ref, hi_ref, newexp_ref, newblk_ref, endblk_ref,
                   xs_hbm, wg_ref, wu_ref, wd_ref, ys_hbm,
                   xbuf_ref, ybuf_ref, wgub_ref, wdb_ref, in_sems, out_sems):
    i = pl.program_id(0)
    n_items = pl.num_programs(0)
    bf16 = jnp.bfloat16
    SB = EXPERT_SUB
    BS = EXPERT_BLOCK * ROW_SUB
    n_blocks = xs_hbm.shape[0] // BS
    blk = blk_ref[i]
    lo = lo_ref[i]
    hi = hi_ref[i]

    def block_rows(idx):
        start = idx * BS
        return pl.ds(start if isinstance(start, int) else pl.multiple_of(start, BS), BS)

    def in_copy(b, slot):
        return pltpu.make_async_copy(xs_hbm.at[block_rows(b)], xbuf_ref.at[block_rows(slot)], in_sems.at[slot])

    def out_copy(b, slot):
        return pltpu.make_async_copy(ybuf_ref.at[block_rows(slot)], ys_hbm.at[block_rows(b)], out_sems.at[slot])

    @pl.when(i == 0)
    def _():
        for b0 in range(EXPERT_IN_SLOTS - 1):
            in_copy(b0, b0).start()

    @pl.when(newblk_ref[i] == 1)
    def _():
        ahead = blk + (EXPERT_IN_SLOTS - 1)

        @pl.when(ahead < n_blocks)
        def _():
            in_copy(ahead, lax.rem(ahead, EXPERT_IN_SLOTS)).start()

        in_copy(blk, lax.rem(blk, EXPERT_IN_SLOTS)).wait()

        @pl.when(blk >= EXPERT_OUT_SLOTS)
        def _():
            out_copy(blk - EXPERT_OUT_SLOTS, lax.rem(blk, EXPERT_OUT_SLOTS)).wait()

    @pl.when(newexp_ref[i] == 1)
    def _():
        wgub_ref[:, 0:D_EXPERT] = wg_ref[0].astype(bf16)
        wgub_ref[:, D_EXPERT:2 * D_EXPERT] = wu_ref[0].astype(bf16)
        wdb_ref[...] = wd_ref[0].astype(bf16)

    xbase = lax.rem(blk, EXPERT_IN_SLOTS) * BS
    ybase = lax.rem(blk, EXPERT_OUT_SLOTS) * BS

    def chunk(base, r0, nrows, c):
        return pl.ds(base + r0 * ROW_SUB + c, nrows, stride=ROW_SUB)

    def run(r0, nrows, merge_first):
        xb = jnp.concatenate([xbuf_ref[chunk(xbase, r0, nrows, c), :] for c in range(ROW_SUB)], axis=1).astype(bf16)
        gu = _dot(xb, wgub_ref[...])
        g = gu[:, 0:D_EXPERT]
        u = gu[:, D_EXPERT:2 * D_EXPERT]
        h = (g * jax.nn.sigmoid(g) * u).astype(bf16)
        y = _dot(h, wdb_ref[...])
        first = SB if merge_first else 0
        if merge_first:
            rows = r0 + lax.broadcasted_iota(jnp.int32, (SB, LANES), 0)
            keep = jnp.logical_and(rows >= lo, rows < hi)
            for c in range(ROW_SUB):
                old = ybuf_ref[chunk(ybase, r0, SB, c), :]
                ybuf_ref[chunk(ybase, r0, SB, c), :] = jnp.where(keep, y[0:SB, c * LANES:(c + 1) * LANES], old)
        if nrows > first:
            for c in range(ROW_SUB):
                ybuf_ref[chunk(ybase, r0 + first, nrows - first, c), :] = y[first:nrows, c * LANES:(c + 1) * LANES]

    n_sub = EXPERT_BLOCK // SB
    from_top = lo == 0
    sub_aligned = (lo & (SB - 1)) == 0
    to_end = jnp.logical_and(jnp.logical_and(lo > 0, hi == EXPERT_BLOCK), jnp.logical_not(sub_aligned))
    subs_used = (hi + SB - 1) // SB
    first_sub = lo // SB
    for k in range(1, n_sub + 1):
        @pl.when(jnp.logical_and(from_top, subs_used == k))
        def _(k=k):
            run(0, k * SB, False)

    for q in range(n_sub):
        @pl.when(jnp.logical_and(to_end, first_sub == q))
        def _(q=q):
            run(q * SB, EXPERT_BLOCK - q * SB, True)

    for q in range(n_sub):
        r0 = q * SB
        inside = jnp.logical_and(lo > 0, jnp.logical_or(hi < EXPERT_BLOCK, sub_aligned))

        @pl.when(jnp.logical_and(inside, jnp.minimum(hi, r0 + SB) > jnp.maximum(lo, r0)))
        def _(r0=r0):
            @pl.when(lo <= r0)
            def _():
                run(r0, SB, False)

            @pl.when(lo > r0)
            def _():
                run(r0, SB, True)

    @pl.when(endblk_ref[i] == 1)
    def _():
        out_copy(blk, lax.rem(blk, EXPERT_OUT_SLOTS)).start()

    @pl.when(i == n_items - 1)
    def _():
        for b1 in range(n_blocks - EXPERT_OUT_SLOTS, n_blocks):
            out_copy(b1, b1 % EXPERT_OUT_SLOTS).wait()


def _expert_call(item_blk, item_exp, item_lo, item_hi, item_newexp, item_newblk, item_endblk,
                 xs_rp, w_eg, w_eu, w_ed):
    D = D_MODEL
    BS = EXPERT_BLOCK * ROW_SUB
    n_items = item_blk.shape[0]
    assert xs_rp.shape[0] % BS == 0 and xs_rp.shape[0] // BS >= max(EXPERT_IN_SLOTS, EXPERT_OUT_SLOTS)

    def w_map(i, blk, exp, lo, hi, newexp, newblk, endblk):
        return (exp[i], 0, 0)

    grid_spec = pltpu.PrefetchScalarGridSpec(
        num_scalar_prefetch=7,
        grid=(n_items,),
        in_specs=[
            pl.BlockSpec(memory_space=pl.ANY),
            pl.BlockSpec((1, D, D_EXPERT), w_map),
            pl.BlockSpec((1, D, D_EXPERT), w_map),
            pl.BlockSpec((1, D_EXPERT, D), w_map),
        ],
        out_specs=pl.BlockSpec(memory_space=pl.ANY),
        scratch_shapes=[
            pltpu.VMEM((EXPERT_IN_SLOTS * BS, LANES), jnp.float32),
            pltpu.VMEM((EXPERT_OUT_SLOTS * BS, LANES), jnp.float32),
            pltpu.VMEM((D, 2 * D_EXPERT), jnp.bfloat16),
            pltpu.VMEM((D_EXPERT, D), jnp.bfloat16),
            pltpu.SemaphoreType.DMA((EXPERT_IN_SLOTS,)),
            pltpu.SemaphoreType.DMA((EXPERT_OUT_SLOTS,)),
        ],
    )
    return pl.pallas_call(
        _expert_kernel,
        grid_spec=grid_spec,
        out_shape=jax.ShapeDtypeStruct(xs_rp.shape, jnp.float32),
        compiler_params=pltpu.CompilerParams(
            dimension_semantics=("arbitrary",), vmem_limit_bytes=VMEM_LIMIT),
        name="moe_experts",
    )(item_blk, item_exp, item_lo, item_hi, item_newexp, item_newblk, item_endblk, xs_rp, w_eg, w_eu, w_ed)


def _combine_kernel(lsrc_ref, gdst_ref, cnt_ref, ld_ref, w_ref, x2_ref, ys_ref, out_ref,
                    stag_ref, acc_ref, sems):
    b = pl.program_id(0)
    nb = pl.num_programs(0)
    T = MIX_TILE
    slot = b & 1
    base = slot * TILE_ROWS

    def fetch(tile, slot_):
        def seg(e, c):
            s = tile * N_EXPERTS + e
            _segment_copies(ys_ref, gdst_ref[s], stag_ref, slot_ * TILE_ROWS + lsrc_ref[s], cnt_ref[s],
                            sems.at[slot_])
            return c

        lax.fori_loop(0, N_EXPERTS, seg, 0)

    @pl.when(b == 0)
    def _():
        fetch(b, slot)

    @pl.when(b + 1 < nb)
    def _():
        fetch(b + 1, 1 - slot)

    _wait_tile(ys_ref, stag_ref, base, sems.at[slot], False)

    def tok(t, c):
        at = pl.ds(pl.multiple_of(t * ROW_SUB, ROW_SUB), ROW_SUB)
        a = x2_ref[at, :]
        for kk_ in range(TOP_K):
            r8 = ld_ref[0, 0, kk_ * T + t]
            w = w_ref[0, 0, kk_ * T + t]
            a = a + w * stag_ref[pl.ds(pl.multiple_of(r8, ROW_SUB), ROW_SUB), :]
        acc_ref[at, :] = a
        return c

    lax.fori_loop(0, T, tok, 0, unroll=4)
    for c in range(ROW_SUB):
        out_ref[:, c * LANES:(c + 1) * LANES] = acc_ref[pl.ds(c, T, stride=ROW_SUB), :]


def _combine_call(lsrc, gdst, cnt, ld_t, w_t, x2, ys_rp):
    D = D_MODEL
    N = x2.shape[0] // ROW_SUB
    nt = ld_t.shape[0]
    T = MIX_TILE
    smem = pl.BlockSpec(memory_space=pltpu.SMEM)
    smem_blk = pl.BlockSpec((1, 1, TILE_ROWS), lambda i: (i, 0, 0), memory_space=pltpu.SMEM)
    return pl.pallas_call(
        _combine_kernel,
        grid=(nt,),
        in_specs=[
            smem, smem, smem, smem_blk, smem_blk,
            pl.BlockSpec((T * ROW_SUB, LANES), lambda i: (i, 0)),
            pl.BlockSpec(memory_space=pl.ANY),
        ],
        out_specs=pl.BlockSpec((T, D), lambda i: (i, 0)),
        out_shape=jax.ShapeDtypeStruct((N, D), jnp.float32),
        scratch_shapes=[
            pltpu.VMEM((2 * TILE_ROWS * ROW_SUB, LANES), jnp.float32),
            pltpu.VMEM((T * ROW_SUB, LANES), jnp.float32),
            pltpu.SemaphoreType.DMA((2,)),
        ],
        compiler_params=pltpu.CompilerParams(
            dimension_semantics=("arbitrary",), vmem_limit_bytes=VMEM_LIMIT),
        name="moe_combine",
    )(lsrc, gdst, cnt, ld_t, w_t, x2, ys_rp)


def _work_items(gstart, totals, n_items):
    BM = EXPERT_BLOCK
    i32 = jnp.int32
    gend = gstart + totals
    first_blk = gstart // BM
    last_blk = jnp.maximum(gend - 1, gstart) // BM
    n_e = jnp.where(totals > 0, last_blk - first_blk + 1, 0)
    item_end = jnp.cumsum(n_e)
    item_start = item_end - n_e
    n_real = item_end[-1]
    j = jnp.minimum(jnp.arange(n_items, dtype=i32), n_real - 1)
    exp = jnp.sum((item_end[None, :] <= j[:, None]).astype(i32), axis=1)
    onehot = exp[:, None] == jnp.arange(N_EXPERTS, dtype=i32)[None, :]

    def of_exp(v):
        return jnp.sum(jnp.where(onehot, v[None, :], 0), axis=1)

    blk = of_exp(first_blk) + (j - of_exp(item_start))
    lo = jnp.maximum(of_exp(gstart), blk * BM) - blk * BM
    hi = jnp.minimum(of_exp(gend), (blk + 1) * BM) - blk * BM
    real = jnp.arange(n_items, dtype=i32) < n_real
    hi = jnp.where(real, hi, lo)
    prev_exp = jnp.concatenate([jnp.full((1,), -1, i32), exp[:-1]])
    prev_blk = jnp.concatenate([jnp.full((1,), -1, i32), blk[:-1]])
    next_blk = jnp.concatenate([blk[1:], jnp.full((1,), -1, i32)])
    is_last = jnp.arange(n_items, dtype=i32) == n_real - 1
    newexp = jnp.logical_and(real, exp != prev_exp).astype(i32)
    newblk = jnp.logical_and(real, blk != prev_blk).astype(i32)
    endblk = jnp.logical_and(real, jnp.logical_or(blk != next_blk, is_last)).astype(i32)
    return blk.astype(i32), exp.astype(i32), lo.astype(i32), hi.astype(i32), newexp, newblk, endblk


def _layer(x, mix_norm_w, w_in, conv_w, q_norm_w, k_norm_w, sinks, conv_out_norm_w,
           attn_out_norm_w, w_out, ffn_norm_w, w_router, router_bias, w_eg, w_eu, w_ed,
           w_sg, w_su, w_sd):
    B, S, D = x.shape
    N = B * S
    T = MIX_TILE
    nt = N // T
    bf16 = jnp.bfloat16
    f32 = jnp.float32
    i32 = jnp.int32

    wr_t = w_router.astype(f32).T
    wr_hi = wr_t.astype(bf16)
    wr_hl = jnp.concatenate([wr_hi, (wr_t - wr_hi.astype(f32)).astype(bf16)], axis=0)
    sink_col = jnp.broadcast_to(
        jnp.repeat(sinks.astype(f32), BLK).reshape(N_KV_HEADS, GQA_GROUP * BLK, 1),
        (N_KV_HEADS, GQA_GROUP * BLK, LANES))
    tri = jnp.asarray(np.triu(np.ones((T, T), np.float32), k=1), dtype=bf16)
    head_ones = jnp.asarray(np.kron(np.eye(D_ATTN // HEAD_DIM, dtype=np.float32),
                                    np.ones((HEAD_DIM, HEAD_DIM), np.float32)), dtype=bf16)
    bias_tab = jnp.asarray(_attn_bias_table())

    x2, xn_rp, wts, ld, cnt = _mixer_call(
        x, mix_norm_w.reshape(1, D), w_in.astype(bf16), conv_w,
        jnp.tile(q_norm_w, N_HEADS).reshape(1, D_ATTN), jnp.tile(k_norm_w, N_KV_HEADS).reshape(1, D_KV),
        sink_col, bias_tab, conv_out_norm_w.reshape(1, D_CONV), attn_out_norm_w.reshape(1, D_ATTN),
        w_out.astype(bf16), ffn_norm_w.reshape(1, D), w_sg.astype(bf16), w_su.astype(bf16),
        w_sd.astype(bf16), wr_hl, router_bias.astype(f32).reshape(N_EXPERTS, 1), tri, head_ones)

    cnt_te = cnt[:, 0].reshape(nt, N_EXPERTS)
    lsrc = jnp.cumsum(cnt_te, axis=1) - cnt_te
    before = jnp.cumsum(cnt_te, axis=0) - cnt_te
    totals = jnp.sum(cnt_te, axis=0)
    gstart = jnp.cumsum(totals) - totals
    gdst = gstart[None, :] + before
    n_items = (N * TOP_K) // EXPERT_BLOCK + N_EXPERTS
    items = _work_items(gstart.astype(i32), totals.astype(i32), n_items)

    def tile_major(a):
        return a.reshape(TOP_K, nt, T).transpose(1, 0, 2).reshape(nt, 1, TILE_ROWS)

    ld_t = tile_major(ld)
    seg_tabs = (lsrc.reshape(-1).astype(i32), gdst.reshape(-1).astype(i32), cnt_te.reshape(-1).astype(i32))
    xs_rp = _dispatch_call(*seg_tabs, ld_t, xn_rp)
    ys_rp = _expert_call(*items, xs_rp, w_eg, w_eu, w_ed)
    out = _combine_call(*seg_tabs, ld_t, tile_major(wts), x2, ys_rp)
    return out.reshape(B, S, D)


def kernel(x, mix_norm_w, w_in, conv_w, q_norm_w, k_norm_w, sinks, conv_out_norm_w, attn_out_norm_w, w_out, ffn_norm_w, w_router, router_bias, w_exp_gate, w_exp_up, w_exp_down, w_sh_gate, w_sh_up, w_sh_down):
    return _layer(x, mix_norm_w[0], w_in[0], conv_w[0], q_norm_w[0], k_norm_w[0], sinks[0],
                  conv_out_norm_w[0], attn_out_norm_w[0], w_out[0], ffn_norm_w[0], w_router[0],
                  router_bias[0], w_exp_gate[0], w_exp_up[0], w_exp_down[0], w_sh_gate[0],
                  w_sh_up[0], w_sh_down[0])
```

```python
import numpy as np
import jax
import jax.numpy as jnp
from jax import lax
from jax.experimental import pallas as pl
from jax.experimental.pallas import tpu as pltpu

D_MODEL = 1024
EPS = 1e-6
D_CONV = 512
CONV_K = 3
HEAD_DIM = 64
N_HEADS = 8
N_KV_HEADS = 2
GQA_GROUP = 4
D_ATTN = 512
D_KV = 128
WINDOW = 128
BLK = 128
D_IN_PROJ = 2304
N_EXPERTS = 64
TOP_K = 8
N_GROUPS = 8
GROUP_SIZE = 8
TOPK_GROUPS = 4
D_EXPERT = 256
ROUTED_SCALE = 2.5

LANES = 128
ROW_SUB = 8
MIX_TILE = 512
TILE_ROWS = TOP_K * MIX_TILE
EXPERT_BLOCK = 1024
EXPERT_SUB = 256
EXPERT_IN_SLOTS = 4
EXPERT_OUT_SLOTS = 3
SEG_CHUNK_LOG2 = 5
NEG_BIG = -1e30
VMEM_LIMIT = 58 * 1024 * 1024


def _alibi_slopes():
    return np.array([2.0 ** (-8.0 * (h + 1) / N_HEADS) for h in range(N_HEADS)], dtype=np.float32)


def _attn_bias_table():
    qi = np.arange(BLK)[:, None]
    kj = np.arange(2 * BLK)[None, :]
    dist = qi - kj + BLK
    inwin = (dist >= 0) & (dist < WINDOW)
    slopes = _alibi_slopes()
    out = np.zeros((N_KV_HEADS, GQA_GROUP * BLK, 2 * BLK), np.float32)
    for g in range(N_KV_HEADS):
        for i in range(GQA_GROUP):
            h = g * GQA_GROUP + i
            out[g, i * BLK:(i + 1) * BLK] = np.where(inwin, -slopes[h] * dist.astype(np.float32), NEG_BIG)
    return out


def _rms(x, w):
    ms = jnp.mean(x * x, axis=-1, keepdims=True)
    return x * lax.rsqrt(ms + EPS) * w


def _head_rms(x, w_tiled, head_ones):
    C = x.shape[1]
    sq = x * x
    hi = sq.astype(jnp.bfloat16)
    lo = (sq - hi.astype(jnp.float32)).astype(jnp.bfloat16)
    ones = head_ones[0:C, 0:C]
    ssq = _dot(hi, ones) + _dot(lo, ones)
    return x * lax.rsqrt(ssq * (1.0 / HEAD_DIM) + EPS) * w_tiled


def _dot(a, b):
    return jnp.dot(a, b, preferred_element_type=jnp.float32)


def _dot_nt(a, b):
    return lax.dot_general(a, b, (((1,), (1,)), ((), ())), preferred_element_type=jnp.float32)


def _mixer_kernel(x_ref, mixw_ref, win_ref, convw_ref, qw_ref, kw_ref, sink_ref, bias_ref,
                  cnw_ref, anw_ref, wout_ref, fnw_ref, wsg_ref, wsu_ref, wsd_ref,
                  wrhl_ref, rbias_ref, tri_ref, hones_ref,
                  x2_ref, xn_ref, wts_ref, ld_ref, cnt_ref,
                  kc_ref, vc_ref, cc_ref):
    j = pl.program_id(1)
    T = MIX_TILE
    bf16 = jnp.bfloat16
    f32 = jnp.float32

    @pl.when(j == 0)
    def _():
        kc_ref[...] = jnp.zeros_like(kc_ref)
        vc_ref[...] = jnp.zeros_like(vc_ref)
        cc_ref[...] = jnp.zeros_like(cc_ref)

    x = x_ref[0]
    u = _rms(x, mixw_ref[...]).astype(bf16)
    qkv = _dot(u, win_ref[:, 3 * D_CONV:D_IN_PROJ])
    q = qkv[:, 0:D_ATTN]
    k = qkv[:, D_ATTN:D_ATTN + D_KV]
    v = qkv[:, D_ATTN + D_KV:D_ATTN + 2 * D_KV]
    conv_piece = 2 * LANES
    conv_pieces = []

    lane = lax.broadcasted_iota(jnp.int32, (1, LANES), 1)
    lane_lo = lane < HEAD_DIM
    qn = _head_rms(q, qw_ref[...], hones_ref) * (HEAD_DIM ** -0.5)
    kn = _head_rms(k, kw_ref[...], hones_ref)
    kfull = jnp.concatenate([kc_ref[...], kn], axis=0)
    vfull = jnp.concatenate([vc_ref[...], v], axis=0)
    kc_ref[...] = kn[T - BLK:T, :]
    vc_ref[...] = v[T - BLK:T, :]

    def _rep(a, g):
        r = pltpu.roll(a, HEAD_DIM, axis=1)
        two = jnp.where(lane_lo, a, r) if g == 0 else jnp.where(lane_lo, r, a)
        return jnp.concatenate([two, two], axis=1).astype(bf16)

    k_rep = [_rep(kfull, g) for g in range(N_KV_HEADS)]
    v_rep = [_rep(vfull, g) for g in range(N_KV_HEADS)]

    lane256 = lax.broadcasted_iota(jnp.int32, (1, 2 * LANES), 1)
    head_of_lane = lane256 // HEAD_DIM
    first_f = (j == 0).astype(f32)
    prev_key_mask = jnp.where(lane256 < BLK, first_f * NEG_BIG, 0.0)
    ones_cols = jnp.ones((2 * BLK, LANES), bf16)

    def lanes2(a):
        return jnp.concatenate([a, a], axis=1)

    attn_rows = []
    for i in range(T // BLK):
        grp_out = []
        for g in range(N_KV_HEADS):
            if len(conv_pieces) * conv_piece < 3 * D_CONV:
                c0 = len(conv_pieces) * conv_piece
                conv_pieces.append(_dot(u, win_ref[:, c0:c0 + conv_piece]))
            qg = qn[i * BLK:(i + 1) * BLK, g * 256:(g + 1) * 256]
            qst = jnp.concatenate(
                [jnp.where(head_of_lane == hi, qg, 0.0) for hi in range(GQA_GROUP)], axis=0).astype(bf16)
            kk = k_rep[g][i * BLK:i * BLK + 2 * BLK, :]
            vv = jnp.concatenate([v_rep[g][i * BLK:i * BLK + 2 * BLK, :], ones_cols], axis=1)
            s = _dot_nt(qst, kk) + bias_ref[g]
            if i == 0:
                s = s + prev_key_mask
            sink = sink_ref[g]
            m = jnp.maximum(jnp.max(s, axis=-1, keepdims=True), sink)
            e = jnp.exp(s - lanes2(m)).astype(bf16)
            r = _dot(e, vv)
            inv = 1.0 / (r[:, 2 * LANES:3 * LANES] + jnp.exp(sink - m))
            r = r[:, 0:2 * LANES] * lanes2(inv)
            o = jnp.where(head_of_lane == 0, r[0:BLK], 0.0)
            for hi in range(1, GQA_GROUP):
                o = jnp.where(head_of_lane == hi, r[hi * BLK:(hi + 1) * BLK], o)
            grp_out.append(o)
        attn_rows.append(jnp.concatenate(grp_out, axis=1))
    y_attn = jnp.concatenate(attn_rows, axis=0)

    assert len(conv_pieces) * conv_piece == 3 * D_CONV
    conv_proj = jnp.concatenate(conv_pieces, axis=1)
    b_gate = conv_proj[:, 0:D_CONV]
    c_gate = conv_proj[:, D_CONV:2 * D_CONV]
    hh = conv_proj[:, 2 * D_CONV:3 * D_CONV]
    ch = c_gate * hh
    prev = cc_ref[...]
    p6 = prev[6:7, :]
    p7 = prev[7:8, :]
    row = lax.broadcasted_iota(jnp.int32, (T, D_CONV), 0)
    ch_m1 = jnp.where(row == 0, p7, pltpu.roll(ch, 1, axis=0))
    ch_m2 = jnp.where(row == 0, p6, jnp.where(row == 1, p7, pltpu.roll(ch, 2, axis=0)))
    cw = convw_ref[...]
    y_conv = b_gate * (cw[0:1, :] * ch_m2 + cw[1:2, :] * ch_m1 + cw[2:3, :] * ch)
    cc_ref[...] = ch[T - 8:T, :]

    y_mix = jnp.concatenate([_rms(y_conv, cnw_ref[...]), _rms(y_attn, anw_ref[...])], axis=1)
    x1 = x + _dot(y_mix.astype(bf16), wout_ref[...])

    xn = _rms(x1, fnw_ref[...])
    for c in range(ROW_SUB):
        xn_ref[pl.ds(c, T, stride=ROW_SUB), :] = xn[:, c * LANES:(c + 1) * LANES]
    xh = xn.astype(bf16)

    xl = (xn - xh.astype(f32)).astype(bf16)
    w_hl = wrhl_ref[...]
    a_hl = _dot_nt(w_hl, xh)
    logits = a_hl[0:N_EXPERTS] + a_hl[N_EXPERTS:2 * N_EXPERTS] + _dot_nt(w_hl[0:N_EXPERTS], xl)

    gs = _dot(xh, wsg_ref[...])
    us = _dot(xh, wsu_ref[...])
    hs = (gs * jax.nn.sigmoid(gs) * us).astype(bf16)
    x2 = x1 + _dot(hs, wsd_ref[...])
    for c in range(ROW_SUB):
        x2_ref[pl.ds(c, T, stride=ROW_SUB), :] = x2[:, c * LANES:(c + 1) * LANES]

    scores = jax.nn.sigmoid(logits)
    biased = scores + rbias_ref[...]

    sub8 = lax.broadcasted_iota(jnp.int32, (GROUP_SIZE, T), 0)
    gscore = []
    for g in range(N_GROUPS):
        blk = biased[g * GROUP_SIZE:(g + 1) * GROUP_SIZE, :]
        m1 = jnp.max(blk, axis=0, keepdims=True)
        first = jnp.min(jnp.where(blk == m1, sub8, GROUP_SIZE), axis=0, keepdims=True)
        m2 = jnp.max(jnp.where(sub8 == first, -jnp.inf, blk), axis=0, keepdims=True)
        gscore.append(m1 + m2)
    masked_blocks = []
    for g in range(N_GROUPS):
        rank = jnp.zeros((1, T), jnp.int32)
        for o_ in range(N_GROUPS):
            if o_ == g:
                continue
            if o_ < g:
                ahead = gscore[o_] >= gscore[g]
            else:
                ahead = gscore[o_] > gscore[g]
            rank = rank + ahead.astype(jnp.int32)
        keep = rank < TOPK_GROUPS
        blk = biased[g * GROUP_SIZE:(g + 1) * GROUP_SIZE, :]
        masked_blocks.append(jnp.where(keep, blk, -jnp.inf))
    cur = jnp.concatenate(masked_blocks, axis=0)

    eiota = lax.broadcasted_iota(jnp.int32, (N_EXPERTS, T), 0)
    row8 = lax.broadcasted_iota(jnp.int32, (TOP_K, T), 0)
    w_out = jnp.zeros((TOP_K, T), f32)
    sel_dense = jnp.zeros((N_EXPERTS, T), f32)
    onehots = []
    for kk_ in range(TOP_K):
        mx = jnp.max(cur, axis=0, keepdims=True)
        sel_idx = jnp.min(jnp.where(cur == mx, eiota, N_EXPERTS), axis=0, keepdims=True)
        onehot = eiota == sel_idx
        w_k = jnp.sum(jnp.where(onehot, scores, 0.0), axis=0, keepdims=True)
        cur = jnp.where(onehot, -jnp.inf, cur)
        sel_dense = jnp.where(onehot, 1.0, sel_dense)
        w_out = jnp.where(row8 == kk_, w_k, w_out)
        onehots.append(onehot)
    wsum = jnp.sum(w_out, axis=0, keepdims=True)
    wts_ref[...] = w_out / wsum * ROUTED_SCALE

    cum = _dot(sel_dense.astype(bf16), tri_ref[...])
    cnt = jnp.broadcast_to(jnp.sum(sel_dense, axis=1, keepdims=True), (N_EXPERTS, LANES))
    erow = lax.broadcasted_iota(jnp.int32, (N_EXPERTS, LANES), 0)
    incl = cnt
    step = 1
    while step < N_EXPERTS:
        incl = incl + jnp.where(erow >= step, pltpu.roll(incl, step, axis=0), 0.0)
        step *= 2
    lstart = incl - cnt
    ld_dense = lstart[:, 0:1] + cum
    ld_out = jnp.zeros((TOP_K, T), f32)
    for kk_ in range(TOP_K):
        p_k = jnp.sum(jnp.where(onehots[kk_], ld_dense, 0.0), axis=0, keepdims=True)
        ld_out = jnp.where(row8 == kk_, p_k, ld_out)
    tile_idx = pl.program_id(0) * pl.num_programs(1) + j
    slot_base = ((tile_idx & 1) * TILE_ROWS).astype(f32)
    ld_ref[...] = ((ld_out + slot_base) * ROW_SUB).astype(jnp.int32)
    cnt_ref[...] = cnt.astype(jnp.int32)


def _mixer_call(x, mixw, win, convw, qw, kw, sink_col, bias_tab, cnw, anw, wout, fnw,
                wsg, wsu, wsd, wrhl, rbias, tri, head_ones):
    B, S, D = x.shape
    T = MIX_TILE
    nt = S // T
    N = B * S

    def full(a):
        nd = a.ndim
        return pl.BlockSpec(a.shape, lambda b, j, _nd=nd: (0,) * _nd)

    tok_spec = pl.BlockSpec((TOP_K, T), lambda b, j: (0, b * nt + j))
    in_arrays = [mixw, win, convw, qw, kw, sink_col, bias_tab, cnw, anw, wout, fnw,
                 wsg, wsu, wsd, wrhl, rbias, tri, head_ones]
    return pl.pallas_call(
        _mixer_kernel,
        grid=(B, nt),
        in_specs=[pl.BlockSpec((1, T, D), lambda b, j: (b, j, 0))] + [full(a) for a in in_arrays],
        out_specs=[
            pl.BlockSpec((T * ROW_SUB, LANES), lambda b, j: (b * nt + j, 0)),
            pl.BlockSpec((T * ROW_SUB, LANES), lambda b, j: (b * nt + j, 0)),
            tok_spec, tok_spec,
            pl.BlockSpec((N_EXPERTS, LANES), lambda b, j: (b * nt + j, 0)),
        ],
        out_shape=[
            jax.ShapeDtypeStruct((N * ROW_SUB, LANES), jnp.float32),
            jax.ShapeDtypeStruct((N * ROW_SUB, LANES), jnp.float32),
            jax.ShapeDtypeStruct((TOP_K, N), jnp.float32),
            jax.ShapeDtypeStruct((TOP_K, N), jnp.int32),
            jax.ShapeDtypeStruct((B * nt * N_EXPERTS, LANES), jnp.int32),
        ],
        scratch_shapes=[
            pltpu.VMEM((BLK, D_KV), jnp.float32),
            pltpu.VMEM((BLK, D_KV), jnp.float32),
            pltpu.VMEM((8, D_CONV), jnp.float32),
        ],
        compiler_params=pltpu.CompilerParams(
            dimension_semantics=("arbitrary", "arbitrary"), vmem_limit_bytes=VMEM_LIMIT),
        name="mixer_router",
    )(x, *in_arrays)


def _rows(ref, row, nrows):
    return ref.at[pl.ds(pl.multiple_of(row * ROW_SUB, ROW_SUB), nrows * ROW_SUB)]


def _segment_copies(src, src_row, dst, dst_row, count, sem):
    chunk = 1 << SEG_CHUNK_LOG2

    def big(i, c):
        o = i * chunk
        pltpu.make_async_copy(_rows(src, src_row + o, chunk), _rows(dst, dst_row + o, chunk), sem).start()
        return c

    lax.fori_loop(0, count >> SEG_CHUNK_LOG2, big, 0)
    for bit in range(SEG_CHUNK_LOG2 - 1, -1, -1):
        o = (count >> (bit + 1)) << (bit + 1)

        @pl.when(((count >> bit) & 1) == 1)
        def _():
            n = 1 << bit
            pltpu.make_async_copy(_rows(src, src_row + o, n), _rows(dst, dst_row + o, n), sem).start()


def _wait_tile(hbm_ref, stag_ref, slot_row, sem, to_hbm):
    vm = _rows(stag_ref, slot_row, TILE_ROWS)
    hb = _rows(hbm_ref, 0, TILE_ROWS)
    (pltpu.make_async_copy(vm, hb, sem) if to_hbm else pltpu.make_async_copy(hb, vm, sem)).wait()


def _dispatch_kernel(lsrc_ref, gdst_ref, cnt_ref, ld_ref, xn_ref, xs_ref, stag_ref, sems):
    b = pl.program_id(0)
    nb = pl.num_programs(0)
    T = MIX_TILE
    slot = b & 1
    base = slot * TILE_ROWS

    for kk_ in range(TOP_K):
        def row(t, c, kk_=kk_):
            r8 = ld_ref[0, 0, kk_ * T + t]
            stag_ref[pl.ds(pl.multiple_of(r8, ROW_SUB), ROW_SUB), :] = (
                xn_ref[pl.ds(pl.multiple_of(t * ROW_SUB, ROW_SUB), ROW_SUB), :])
            return c

        lax.fori_loop(0, T, row, 0, unroll=32)

    def seg(e, c):
        s = b * N_EXPERTS + e
        _segment_copies(stag_ref, base + lsrc_ref[s], xs_ref, gdst_ref[s], cnt_ref[s], sems.at[slot])
        return c

    lax.fori_loop(0, N_EXPERTS, seg, 0)

    @pl.when(b > 0)
    def _():
        _wait_tile(xs_ref, stag_ref, (1 - slot) * TILE_ROWS, sems.at[1 - slot], True)

    @pl.when(b == nb - 1)
    def _():
        _wait_tile(xs_ref, stag_ref, base, sems.at[slot], True)


def _dispatch_call(lsrc, gdst, cnt, ld_t, xn_rp):
    nt = ld_t.shape[0]
    T = MIX_TILE
    smem = pl.BlockSpec(memory_space=pltpu.SMEM)
    return pl.pallas_call(
        _dispatch_kernel,
        grid=(nt,),
        in_specs=[
            smem, smem, smem,
            pl.BlockSpec((1, 1, TILE_ROWS), lambda i: (i, 0, 0), memory_space=pltpu.SMEM),
            pl.BlockSpec((T * ROW_SUB, LANES), lambda i: (i, 0)),
        ],
        out_specs=pl.BlockSpec(memory_space=pl.ANY),
        out_shape=jax.ShapeDtypeStruct((nt * TILE_ROWS * ROW_SUB, LANES), jnp.float32),
        scratch_shapes=[pltpu.VMEM((2 * TILE_ROWS * ROW_SUB, LANES), jnp.float32),
                        pltpu.SemaphoreType.DMA((2,))],
        compiler_params=pltpu.CompilerParams(
            dimension_semantics=("arbitrary",), vmem_limit_bytes=VMEM_LIMIT),
        name="moe_dispatch",
    )(lsrc, gdst, cnt, ld_t, xn_rp)


def _expert_kernel(blk_ref, exp_ref, lo_ref, hi_ref, newexp_ref, newblk_ref, endblk_ref,
                   xs_hbm, wg_ref, wu_ref, wd_ref, ys_hbm,
                   xbuf_ref, ybuf_ref, wgub_ref, wdb_ref, in_sems, out_sems):
    i = pl.program_id(0)
    n_items = pl.num_programs(0)
    bf16 = jnp.bfloat16
    SB = EXPERT_SUB
    BS = EXPERT_BLOCK * ROW_SUB
    n_blocks = xs_hbm.shape[0] // BS
    blk = blk_ref[i]
    lo = lo_ref[i]
    hi = hi_ref[i]

    def block_rows(idx):
        start = idx * BS
        return pl.ds(start if isinstance(start, int) else pl.multiple_of(start, BS), BS)

    def in_copy(b, slot):
        return pltpu.make_async_copy(xs_hbm.at[block_rows(b)], xbuf_ref.at[block_rows(slot)], in_sems.at[slot])

    def out_copy(b, slot):
        return pltpu.make_async_copy(ybuf_ref.at[block_rows(slot)], ys_hbm.at[block_rows(b)], out_sems.at[slot])

    @pl.when(i == 0)
    def _():
        for b0 in range(EXPERT_IN_SLOTS - 1):
            in_copy(b0, b0).start()

    @pl.when(newblk_ref[i] == 1)
    def _():
        ahead = blk + (EXPERT_IN_SLOTS - 1)

        @pl.when(ahead < n_blocks)
        def _():
            in_copy(ahead, lax.rem(ahead, EXPERT_IN_SLOTS)).start()

        in_copy(blk, lax.rem(blk, EXPERT_IN_SLOTS)).wait()

        @pl.when(blk >= EXPERT_OUT_SLOTS)
        def _():
            out_copy(blk - EXPERT_OUT_SLOTS, lax.rem(blk, EXPERT_OUT_SLOTS)).wait()

    @pl.when(newexp_ref[i] == 1)
    def _():
        wgub_ref[:, 0:D_EXPERT] = wg_ref[0].astype(bf16)
        wgub_ref[:, D_EXPERT:2 * D_EXPERT] = wu_ref[0].astype(bf16)
        wdb_ref[...] = wd_ref[0].astype(bf16)

    xbase = lax.rem(blk, EXPERT_IN_SLOTS) * BS
    ybase = lax.rem(blk, EXPERT_OUT_SLOTS) * BS

    def chunk(base, r0, nrows, c):
        return pl.ds(base + r0 * ROW_SUB + c, nrows, stride=ROW_SUB)

    def run(r0, nrows, merge_first):
        xb = jnp.concatenate([xbuf_ref[chunk(xbase, r0, nrows, c), :] for c in range(ROW_SUB)], axis=1).astype(bf16)
        gu = _dot(xb, wgub_ref[...])
        g = gu[:, 0:D_EXPERT]
        u = gu[:, D_EXPERT:2 * D_EXPERT]
        h = (g * jax.nn.sigmoid(g) * u).astype(bf16)
        y = _dot(h, wdb_ref[...])
        first = SB if merge_first else 0
        if merge_first:
            rows = r0 + lax.broadcasted_iota(jnp.int32, (SB, LANES), 0)
            keep = jnp.logical_and(rows >= lo, rows < hi)
            for c in range(ROW_SUB):
                old = ybuf_ref[chunk(ybase, r0, SB, c), :]
                ybuf_ref[chunk(ybase, r0, SB, c), :] = jnp.where(keep, y[0:SB, c * LANES:(c + 1) * LANES], old)
        if nrows > first:
            for c in range(ROW_SUB):
                ybuf_ref[chunk(ybase, r0 + first, nrows - first, c), :] = y[first:nrows, c * LANES:(c + 1) * LANES]

    n_sub = EXPERT_BLOCK // SB
    from_top = lo == 0
    sub_aligned = (lo & (SB - 1)) == 0
    to_end = jnp.logical_and(jnp.logical_and(lo > 0, hi == EXPERT_BLOCK), jnp.logical_not(sub_aligned))
    subs_used = (hi + SB - 1) // SB
    first_sub = lo // SB
    for k in range(1, n_sub + 1):
        @pl.when(jnp.logical_and(from_top, subs_used == k))
        def _(k=k):
            run(0, k * SB, False)

    for q in range(n_sub):
        @pl.when(jnp.logical_and(to_end, first_sub == q))
        def _(q=q):
            run(q * SB, EXPERT_BLOCK - q * SB, True)

    for q in range(n_sub):
        r0 = q * SB
        inside = jnp.logical_and(lo > 0, jnp.logical_or(hi < EXPERT_BLOCK, sub_aligned))

        @pl.when(jnp.logical_and(inside, jnp.minimum(hi, r0 + SB) > jnp.maximum(lo, r0)))
        def _(r0=r0):
            @pl.when(lo <= r0)
            def _():
                run(r0, SB, False)

            @pl.when(lo > r0)
            def _():
                run(r0, SB, True)

    @pl.when(endblk_ref[i] == 1)
    def _():
        out_copy(blk, lax.rem(blk, EXPERT_OUT_SLOTS)).start()

    @pl.when(i == n_items - 1)
    def _():
        for b1 in range(n_blocks - EXPERT_OUT_SLOTS, n_blocks):
            out_copy(b1, b1 % EXPERT_OUT_SLOTS).wait()


def _expert_call(item_blk, item_exp, item_lo, item_hi, item_newexp, item_newblk, item_endblk,
                 xs_rp, w_eg, w_eu, w_ed):
    D = D_MODEL
    BS = EXPERT_BLOCK * ROW_SUB
    n_items = item_blk.shape[0]
    assert xs_rp.shape[0] % BS == 0 and xs_rp.shape[0] // BS >= max(EXPERT_IN_SLOTS, EXPERT_OUT_SLOTS)

    def w_map(i, blk, exp, lo, hi, newexp, newblk, endblk):
        return (exp[i], 0, 0)

    grid_spec = pltpu.PrefetchScalarGridSpec(
        num_scalar_prefetch=7,
        grid=(n_items,),
        in_specs=[
            pl.BlockSpec(memory_space=pl.ANY),
            pl.BlockSpec((1, D, D_EXPERT), w_map),
            pl.BlockSpec((1, D, D_EXPERT), w_map),
            pl.BlockSpec((1, D_EXPERT, D), w_map),
        ],
        out_specs=pl.BlockSpec(memory_space=pl.ANY),
        scratch_shapes=[
            pltpu.VMEM((EXPERT_IN_SLOTS * BS, LANES), jnp.float32),
            pltpu.VMEM((EXPERT_OUT_SLOTS * BS, LANES), jnp.float32),
            pltpu.VMEM((D, 2 * D_EXPERT), jnp.bfloat16),
            pltpu.VMEM((D_EXPERT, D), jnp.bfloat16),
            pltpu.SemaphoreType.DMA((EXPERT_IN_SLOTS,)),
            pltpu.SemaphoreType.DMA((EXPERT_OUT_SLOTS,)),
        ],
    )
    return pl.pallas_call(
        _expert_kernel,
        grid_spec=grid_spec,
        out_shape=jax.ShapeDtypeStruct(xs_rp.shape, jnp.float32),
        compiler_params=pltpu.CompilerParams(
            dimension_semantics=("arbitrary",), vmem_limit_bytes=VMEM_LIMIT),
        name="moe_experts",
    )(item_blk, item_exp, item_lo, item_hi, item_newexp, item_newblk, item_endblk, xs_rp, w_eg, w_eu, w_ed)


def _combine_kernel(lsrc_ref, gdst_ref, cnt_ref, ld_ref, w_ref, x2_ref, ys_ref, out_ref,
                    stag_ref, acc_ref, sems):
    b = pl.program_id(0)
    nb = pl.num_programs(0)
    T = MIX_TILE
    slot = b & 1
    base = slot * TILE_ROWS

    def fetch(tile, slot_):
        def seg(e, c):
            s = tile * N_EXPERTS + e
            _segment_copies(ys_ref, gdst_ref[s], stag_ref, slot_ * TILE_ROWS + lsrc_ref[s], cnt_ref[s],
                            sems.at[slot_])
            return c

        lax.fori_loop(0, N_EXPERTS, seg, 0)

    @pl.when(b == 0)
    def _():
        fetch(b, slot)

    @pl.when(b + 1 < nb)
    def _():
        fetch(b + 1, 1 - slot)

    _wait_tile(ys_ref, stag_ref, base, sems.at[slot], False)

    def tok(t, c):
        at = pl.ds(pl.multiple_of(t * ROW_SUB, ROW_SUB), ROW_SUB)
        a = x2_ref[at, :]
        for kk_ in range(TOP_K):
            r8 = ld_ref[0, 0, kk_ * T + t]
            w = w_ref[0, 0, kk_ * T + t]
            a = a + w * stag_ref[pl.ds(pl.multiple_of(r8, ROW_SUB), ROW_SUB), :]
        acc_ref[at, :] = a
        return c

    lax.fori_loop(0, T, tok, 0, unroll=4)
    for c in range(ROW_SUB):
        out_ref[:, c * LANES:(c + 1) * LANES] = acc_ref[pl.ds(c, T, stride=ROW_SUB), :]


def _combine_call(lsrc, gdst, cnt, ld_t, w_t, x2, ys_rp):
    D = D_MODEL
    N = x2.shape[0] // ROW_SUB
    nt = ld_t.shape[0]
    T = MIX_TILE
    smem = pl.BlockSpec(memory_space=pltpu.SMEM)
    smem_blk = pl.BlockSpec((1, 1, TILE_ROWS), lambda i: (i, 0, 0), memory_space=pltpu.SMEM)
    return pl.pallas_call(
        _combine_kernel,
        grid=(nt,),
        in_specs=[
            smem, smem, smem, smem_blk, smem_blk,
            pl.BlockSpec((T * ROW_SUB, LANES), lambda i: (i, 0)),
            pl.BlockSpec(memory_space=pl.ANY),
        ],
        out_specs=pl.BlockSpec((T, D), lambda i: (i, 0)),
        out_shape=jax.ShapeDtypeStruct((N, D), jnp.float32),
        scratch_shapes=[
            pltpu.VMEM((2 * TILE_ROWS * ROW_SUB, LANES), jnp.float32),
            pltpu.VMEM((T * ROW_SUB, LANES), jnp.float32),
            pltpu.SemaphoreType.DMA((2,)),
        ],
        compiler_params=pltpu.CompilerParams(
            dimension_semantics=("arbitrary",), vmem_limit_bytes=VMEM_LIMIT),
        name="moe_combine",
    )(lsrc, gdst, cnt, ld_t, w_t, x2, ys_rp)


def _work_items(gstart, totals, n_items):
    BM = EXPERT_BLOCK
    i32 = jnp.int32
    gend = gstart + totals
    first_blk = gstart // BM
    last_blk = jnp.maximum(gend - 1, gstart) // BM
    n_e = jnp.where(totals > 0, last_blk - first_blk + 1, 0)
    item_end = jnp.cumsum(n_e)
    item_start = item_end - n_e
    n_real = item_end[-1]
    j = jnp.minimum(jnp.arange(n_items, dtype=i32), n_real - 1)
    exp = jnp.sum((item_end[None, :] <= j[:, None]).astype(i32), axis=1)
    onehot = exp[:, None] == jnp.arange(N_EXPERTS, dtype=i32)[None, :]

    def of_exp(v):
        return jnp.sum(jnp.where(onehot, v[None, :], 0), axis=1)

    blk = of_exp(first_blk) + (j - of_exp(item_start))
    lo = jnp.maximum(of_exp(gstart), blk * BM) - blk * BM
    hi = jnp.minimum(of_exp(gend), (blk + 1) * BM) - blk * BM
    real = jnp.arange(n_items, dtype=i32) < n_real
    hi = jnp.where(real, hi, lo)
    prev_exp = jnp.concatenate([jnp.full((1,), -1, i32), exp[:-1]])
    prev_blk = jnp.concatenate([jnp.full((1,), -1, i32), blk[:-1]])
    next_blk = jnp.concatenate([blk[1:], jnp.full((1,), -1, i32)])
    is_last = jnp.arange(n_items, dtype=i32) == n_real - 1
    newexp = jnp.logical_and(real, exp != prev_exp).astype(i32)
    newblk = jnp.logical_and(real, blk != prev_blk).astype(i32)
    endblk = jnp.logical_and(real, jnp.logical_or(blk != next_blk, is_last)).astype(i32)
    return blk.astype(i32), exp.astype(i32), lo.astype(i32), hi.astype(i32), newexp, newblk, endblk


def _layer(x, mix_norm_w, w_in, conv_w, q_norm_w, k_norm_w, sinks, conv_out_norm_w,
           attn_out_norm_w, w_out, ffn_norm_w, w_router, router_bias, w_eg, w_eu, w_ed,
           w_sg, w_su, w_sd):
    B, S, D = x.shape
    N = B * S
    T = MIX_TILE
    nt = N // T
    bf16 = jnp.bfloat16
    f32 = jnp.float32
    i32 = jnp.int32

    wr_t = w_router.astype(f32).T
    wr_hi = wr_t.astype(bf16)
    wr_hl = jnp.concatenate([wr_hi, (wr_t - wr_hi.astype(f32)).astype(bf16)], axis=0)
    sink_col = jnp.broadcast_to(
        jnp.repeat(sinks.astype(f32), BLK).reshape(N_KV_HEADS, GQA_GROUP * BLK, 1),
        (N_KV_HEADS, GQA_GROUP * BLK, LANES))
    tri = jnp.asarray(np.triu(np.ones((T, T), np.float32), k=1), dtype=bf16)
    head_ones = jnp.asarray(np.kron(np.eye(D_ATTN // HEAD_DIM, dtype=np.float32),
                                    np.ones((HEAD_DIM, HEAD_DIM), np.float32)), dtype=bf16)
    bias_tab = jnp.asarray(_attn_bias_table())

    x2, xn_rp, wts, ld, cnt = _mixer_call(
        x, mix_norm_w.reshape(1, D), w_in.astype(bf16), conv_w,
        jnp.tile(q_norm_w, N_HEADS).reshape(1, D_ATTN), jnp.tile(k_norm_w, N_KV_HEADS).reshape(1, D_KV),
        sink_col, bias_tab, conv_out_norm_w.reshape(1, D_CONV), attn_out_norm_w.reshape(1, D_ATTN),
        w_out.astype(bf16), ffn_norm_w.reshape(1, D), w_sg.astype(bf16), w_su.astype(bf16),
        w_sd.astype(bf16), wr_hl, router_bias.astype(f32).reshape(N_EXPERTS, 1), tri, head_ones)

    cnt_te = cnt[:, 0].reshape(nt, N_EXPERTS)
    lsrc = jnp.cumsum(cnt_te, axis=1) - cnt_te
    before = jnp.cumsum(cnt_te, axis=0) - cnt_te
    totals = jnp.sum(cnt_te, axis=0)
    gstart = jnp.cumsum(totals) - totals
    gdst = gstart[None, :] + before
    n_items = (N * TOP_K) // EXPERT_BLOCK + N_EXPERTS
    items = _work_items(gstart.astype(i32), totals.astype(i32), n_items)

    def tile_major(a):
        return a.reshape(TOP_K, nt, T).transpose(1, 0, 2).reshape(nt, 1, TILE_ROWS)

    ld_t = tile_major(ld)
    seg_tabs = (lsrc.reshape(-1).astype(i32), gdst.reshape(-1).astype(i32), cnt_te.reshape(-1).astype(i32))
    xs_rp = _dispatch_call(*seg_tabs, ld_t, xn_rp)
    ys_rp = _expert_call(*items, xs_rp, w_eg, w_eu, w_ed)
    out = _combine_call(*seg_tabs, ld_t, tile_major(wts), x2, ys_rp)
    return out.reshape(B, S, D)


def kernel(x, mix_norm_w, w_in, conv_w, q_norm_w, k_norm_w, sinks, conv_out_norm_w, attn_out_norm_w, w_out, ffn_norm_w, w_router, router_bias, w_exp_gate, w_exp_up, w_exp_down, w_sh_gate, w_sh_up, w_sh_down):
    return _layer(x, mix_norm_w[0], w_in[0], conv_w[0], q_norm_w[0], k_norm_w[0], sinks[0],
                  conv_out_norm_w[0], attn_out_norm_w[0], w_out[0], ffn_norm_w[0], w_router[0],
                  router_bias[0], w_exp_gate[0], w_exp_up[0], w_exp_down[0], w_sh_gate[0],
                  w_sh_up[0], w_sh_down[0])
```

```python
import numpy as np
import jax
import jax.numpy as jnp
from jax import lax
from jax.experimental import pallas as pl
from jax.experimental.pallas import tpu as pltpu

D_MODEL = 1024
EPS = 1e-6
D_CONV = 512
CONV_K = 3
HEAD_DIM = 64
N_HEADS = 8
N_KV_HEADS = 2
GQA_GROUP = 4
D_ATTN = 512
D_KV = 128
WINDOW = 128
BLK = 128
D_IN_PROJ = 2304
N_EXPERTS = 64
TOP_K = 8
N_GROUPS = 8
GROUP_SIZE = 8
TOPK_GROUPS = 4
D_EXPERT = 256
ROUTED_SCALE = 2.5

LANES = 128
ROW_SUB = 8
MIX_TILE = 512
TILE_ROWS = TOP_K * MIX_TILE
EXPERT_BLOCK = 1024
EXPERT_SUB = 256
EXPERT_IN_SLOTS = 4
EXPERT_OUT_SLOTS = 3
SEG_CHUNK_LOG2 = 5
NEG_BIG = -1e30
VMEM_LIMIT = 58 * 1024 * 1024


def _alibi_slopes():
    return np.array([2.0 ** (-8.0 * (h + 1) / N_HEADS) for h in range(N_HEADS)], dtype=np.float32)


def _attn_bias_table():
    qi = np.arange(BLK)[:, None]
    kj = np.arange(2 * BLK)[None, :]
    dist = qi - kj + BLK
    inwin = (dist >= 0) & (dist < WINDOW)
    slopes = _alibi_slopes()
    out = np.zeros((N_KV_HEADS, GQA_GROUP * BLK, 2 * BLK), np.float32)
    for g in range(N_KV_HEADS):
        for i in range(GQA_GROUP):
            h = g * GQA_GROUP + i
            out[g, i * BLK:(i + 1) * BLK] = np.where(inwin, -slopes[h] * dist.astype(np.float32), NEG_BIG)
    return out


def _rms(x, w):
    ms = jnp.mean(x * x, axis=-1, keepdims=True)
    return x * lax.rsqrt(ms + EPS) * w


def _head_rms(x, w_tiled, head_ones):
    C = x.shape[1]
    W = min(C, 2 * LANES)
    sq = x * x
    hi = sq.astype(jnp.bfloat16)
    lo = (sq - hi.astype(jnp.float32)).astype(jnp.bfloat16)
    ones = head_ones[0:W, 0:W]
    ssq = jnp.concatenate(
        [_dot(hi[:, c:c + W], ones) + _dot(lo[:, c:c + W], ones) for c in range(0, C, W)], axis=1)
    return x * lax.rsqrt(ssq * (1.0 / HEAD_DIM) + EPS) * w_tiled


def _dot(a, b):
    return jnp.dot(a, b, preferred_element_type=jnp.float32)


def _dot_nt(a, b):
    return lax.dot_general(a, b, (((1,), (1,)), ((), ())), preferred_element_type=jnp.float32)


def _mixer_kernel(x_ref, mixw_ref, win_ref, convw_ref, qw_ref, kw_ref, sink_ref, bias_ref,
                  cnw_ref, anw_ref, wout_ref, fnw_ref, wsg_ref, wsu_ref, wsd_ref,
                  wrhl_ref, rbias_ref, tri_ref, hones_ref,
                  x2_ref, xn_ref, wts_ref, ld_ref, cnt_ref,
                  kc_ref, vc_ref, cc_ref):
    j = pl.program_id(1)
    T = MIX_TILE
    bf16 = jnp.bfloat16
    f32 = jnp.float32

    @pl.when(j == 0)
    def _():
        kc_ref[...] = jnp.zeros_like(kc_ref)
        vc_ref[...] = jnp.zeros_like(vc_ref)
        cc_ref[...] = jnp.zeros_like(cc_ref)

    x = x_ref[0]
    u = _rms(x, mixw_ref[...]).astype(bf16)
    qkv = _dot(u, win_ref[:, 3 * D_CONV:D_IN_PROJ])
    q = qkv[:, 0:D_ATTN]
    k = qkv[:, D_ATTN:D_ATTN + D_KV]
    v = qkv[:, D_ATTN + D_KV:D_ATTN + 2 * D_KV]
    conv_piece = 2 * LANES
    conv_pieces = []

    lane = lax.broadcasted_iota(jnp.int32, (1, LANES), 1)
    lane_lo = lane < HEAD_DIM
    qn = _head_rms(q, qw_ref[...], hones_ref) * (HEAD_DIM ** -0.5)
    kn = _head_rms(k, kw_ref[...], hones_ref)
    kfull = jnp.concatenate([kc_ref[...], kn], axis=0)
    vfull = jnp.concatenate([vc_ref[...], v], axis=0)
    kc_ref[...] = kn[T - BLK:T, :]
    vc_ref[...] = v[T - BLK:T, :]

    def _rep(a, g):
        r = pltpu.roll(a, HEAD_DIM, axis=1)
        two = jnp.where(lane_lo, a, r) if g == 0 else jnp.where(lane_lo, r, a)
        return jnp.concatenate([two, two], axis=1).astype(bf16)

    k_rep = [_rep(kfull, g) for g in range(N_KV_HEADS)]
    v_rep = [_rep(vfull, g) for g in range(N_KV_HEADS)]

    lane256 = lax.broadcasted_iota(jnp.int32, (1, 2 * LANES), 1)
    head_of_lane = lane256 // HEAD_DIM
    first_f = (j == 0).astype(f32)
    prev_key_mask = jnp.where(lane256 < BLK, first_f * NEG_BIG, 0.0)
    ones_cols = jnp.ones((2 * BLK, LANES), bf16)

    def lanes2(a):
        return jnp.concatenate([a, a], axis=1)

    attn_rows = []
    for i in range(T // BLK):
        grp_out = []
        for g in range(N_KV_HEADS):
            if len(conv_pieces) * conv_piece < 3 * D_CONV:
                c0 = len(conv_pieces) * conv_piece
                conv_pieces.append(_dot(u, win_ref[:, c0:c0 + conv_piece]))
            qg = qn[i * BLK:(i + 1) * BLK, g * 256:(g + 1) * 256]
            qst = jnp.concatenate(
                [jnp.where(head_of_lane == hi, qg, 0.0) for hi in range(GQA_GROUP)], axis=0).astype(bf16)
            kk = k_rep[g][i * BLK:i * BLK + 2 * BLK, :]
            vv = jnp.concatenate([v_rep[g][i * BLK:i * BLK + 2 * BLK, :], ones_cols], axis=1)
            s = _dot_nt(qst, kk) + bias_ref[g]
            if i == 0:
                s = s + prev_key_mask
            sink = sink_ref[g]
            m = jnp.maximum(jnp.max(s, axis=-1, keepdims=True), sink)
            e = jnp.exp(s - lanes2(m)).astype(bf16)
            r = _dot(e, vv)
            inv = 1.0 / (r[:, 2 * LANES:3 * LANES] + jnp.exp(sink - m))
            r = r[:, 0:2 * LANES] * lanes2(inv)
            o = jnp.where(head_of_lane == 0, r[0:BLK], 0.0)
            for hi in range(1, GQA_GROUP):
                o = jnp.where(head_of_lane == hi, r[hi * BLK:(hi + 1) * BLK], o)
            grp_out.append(o)
        attn_rows.append(jnp.concatenate(grp_out, axis=1))
    y_attn = jnp.concatenate(attn_rows, axis=0)

    assert len(conv_pieces) * conv_piece == 3 * D_CONV
    conv_proj = jnp.concatenate(conv_pieces, axis=1)
    b_gate = conv_proj[:, 0:D_CONV]
    c_gate = conv_proj[:, D_CONV:2 * D_CONV]
    hh = conv_proj[:, 2 * D_CONV:3 * D_CONV]
    ch = c_gate * hh
    prev = cc_ref[...]
    p6 = prev[6:7, :]
    p7 = prev[7:8, :]
    row = lax.broadcasted_iota(jnp.int32, (T, D_CONV), 0)
    ch_m1 = jnp.where(row == 0, p7, pltpu.roll(ch, 1, axis=0))
    ch_m2 = jnp.where(row == 0, p6, jnp.where(row == 1, p7, pltpu.roll(ch, 2, axis=0)))
    cw = convw_ref[...]
    y_conv = b_gate * (cw[0:1, :] * ch_m2 + cw[1:2, :] * ch_m1 + cw[2:3, :] * ch)
    cc_ref[...] = ch[T - 8:T, :]

    y_mix = jnp.concatenate([_rms(y_conv, cnw_ref[...]), _rms(y_attn, anw_ref[...])], axis=1)
    x1 = x + _dot(y_mix.astype(bf16), wout_ref[...])

    xn = _rms(x1, fnw_ref[...])
    for c in range(ROW_SUB):
        xn_ref[pl.ds(c, T, stride=ROW_SUB), :] = xn[:, c * LANES:(c + 1) * LANES]
    xh = xn.astype(bf16)

    xl = (xn - xh.astype(f32)).astype(bf16)
    w_hl = wrhl_ref[...]
    a_hl = _dot_nt(w_hl, xh)
    logits = a_hl[0:N_EXPERTS] + a_hl[N_EXPERTS:2 * N_EXPERTS] + _dot_nt(w_hl[0:N_EXPERTS], xl)

    gs = _dot(xh, wsg_ref[...])
    us = _dot(xh, wsu_ref[...])
    hs = (gs * jax.nn.sigmoid(gs) * us).astype(bf16)
    x2 = x1 + _dot(hs, wsd_ref[...])
    for c in range(ROW_SUB):
        x2_ref[pl.ds(c, T, stride=ROW_SUB), :] = x2[:, c * LANES:(c + 1) * LANES]

    scores = jax.nn.sigmoid(logits)
    biased = scores + rbias_ref[...]

    sub8 = lax.broadcasted_iota(jnp.int32, (GROUP_SIZE, T), 0)
    gscore = []
    for g in range(N_GROUPS):
        blk = biased[g * GROUP_SIZE:(g + 1) * GROUP_SIZE, :]
        m1 = jnp.max(blk, axis=0, keepdims=True)
        first = jnp.min(jnp.where(blk == m1, sub8, GROUP_SIZE), axis=0, keepdims=True)
        m2 = jnp.max(jnp.where(sub8 == first, -jnp.inf, blk), axis=0, keepdims=True)
        gscore.append(m1 + m2)
    masked_blocks = []
    for g in range(N_GROUPS):
        rank = jnp.zeros((1, T), jnp.int32)
        for o_ in range(N_GROUPS):
            if o_ == g:
                continue
            if o_ < g:
                ahead = gscore[o_] >= gscore[g]
            else:
                ahead = gscore[o_] > gscore[g]
            rank = rank + ahead.astype(jnp.int32)
        keep = rank < TOPK_GROUPS
        blk = biased[g * GROUP_SIZE:(g + 1) * GROUP_SIZE, :]
        masked_blocks.append(jnp.where(keep, blk, -jnp.inf))
    cur = jnp.concatenate(masked_blocks, axis=0)

    eiota = lax.broadcasted_iota(jnp.int32, (N_EXPERTS, T), 0)
    row8 = lax.broadcasted_iota(jnp.int32, (TOP_K, T), 0)
    w_out = jnp.zeros((TOP_K, T), f32)
    sel_dense = jnp.zeros((N_EXPERTS, T), f32)
    onehots = []
    for kk_ in range(TOP_K):
        mx = jnp.max(cur, axis=0, keepdims=True)
        sel_idx = jnp.min(jnp.where(cur == mx, eiota, N_EXPERTS), axis=0, keepdims=True)
        onehot = eiota == sel_idx
        w_k = jnp.sum(jnp.where(onehot, scores, 0.0), axis=0, keepdims=True)
        cur = jnp.where(onehot, -jnp.inf, cur)
        sel_dense = jnp.where(onehot, 1.0, sel_dense)
        w_out = jnp.where(row8 == kk_, w_k, w_out)
        onehots.append(onehot)
    wsum = jnp.sum(w_out, axis=0, keepdims=True)
    wts_ref[...] = w_out / wsum * ROUTED_SCALE

    cum = _dot(sel_dense.astype(bf16), tri_ref[...])
    cnt = jnp.broadcast_to(jnp.sum(sel_dense, axis=1, keepdims=True), (N_EXPERTS, LANES))
    erow = lax.broadcasted_iota(jnp.int32, (N_EXPERTS, LANES), 0)
    incl = cnt
    step = 1
    while step < N_EXPERTS:
        incl = incl + jnp.where(erow >= step, pltpu.roll(incl, step, axis=0), 0.0)
        step *= 2
    lstart = incl - cnt
    ld_dense = lstart[:, 0:1] + cum
    ld_out = jnp.zeros((TOP_K, T), f32)
    for kk_ in range(TOP_K):
        p_k = jnp.sum(jnp.where(onehots[kk_], ld_dense, 0.0), axis=0, keepdims=True)
        ld_out = jnp.where(row8 == kk_, p_k, ld_out)
    tile_idx = pl.program_id(0) * pl.num_programs(1) + j
    slot_base = ((tile_idx & 1) * TILE_ROWS).astype(f32)
    ld_ref[...] = ((ld_out + slot_base) * ROW_SUB).astype(jnp.int32)
    cnt_ref[...] = cnt.astype(jnp.int32)


def _mixer_call(x, mixw, win, convw, qw, kw, sink_col, bias_tab, cnw, anw, wout, fnw,
                wsg, wsu, wsd, wrhl, rbias, tri, head_ones):
    B, S, D = x.shape
    T = MIX_TILE
    nt = S // T
    N = B * S

    def full(a):
        nd = a.ndim
        return pl.BlockSpec(a.shape, lambda b, j, _nd=nd: (0,) * _nd)

    tok_spec = pl.BlockSpec((TOP_K, T), lambda b, j: (0, b * nt + j))
    in_arrays = [mixw, win, convw, qw, kw, sink_col, bias_tab, cnw, anw, wout, fnw,
                 wsg, wsu, wsd, wrhl, rbias, tri, head_ones]
    return pl.pallas_call(
        _mixer_kernel,
        grid=(B, nt),
        in_specs=[pl.BlockSpec((1, T, D), lambda b, j: (b, j, 0))] + [full(a) for a in in_arrays],
        out_specs=[
            pl.BlockSpec((T * ROW_SUB, LANES), lambda b, j: (b * nt + j, 0)),
            pl.BlockSpec((T * ROW_SUB, LANES), lambda b, j: (b * nt + j, 0)),
            tok_spec, tok_spec,
            pl.BlockSpec((N_EXPERTS, LANES), lambda b, j: (b * nt + j, 0)),
        ],
        out_shape=[
            jax.ShapeDtypeStruct((N * ROW_SUB, LANES), jnp.float32),
            jax.ShapeDtypeStruct((N * ROW_SUB, LANES), jnp.float32),
            jax.ShapeDtypeStruct((TOP_K, N), jnp.float32),
            jax.ShapeDtypeStruct((TOP_K, N), jnp.int32),
            jax.ShapeDtypeStruct((B * nt * N_EXPERTS, LANES), jnp.int32),
        ],
        scratch_shapes=[
            pltpu.VMEM((BLK, D_KV), jnp.float32),
            pltpu.VMEM((BLK, D_KV), jnp.float32),
            pltpu.VMEM((8, D_CONV), jnp.float32),
        ],
        compiler_params=pltpu.CompilerParams(
            dimension_semantics=("arbitrary", "arbitrary"), vmem_limit_bytes=VMEM_LIMIT),
        name="mixer_router",
    )(x, *in_arrays)


def _rows(ref, row, nrows):
    return ref.at[pl.ds(pl.multiple_of(row * ROW_SUB, ROW_SUB), nrows * ROW_SUB)]


def _segment_copies(src, src_row, dst, dst_row, count, sem):
    chunk = 1 << SEG_CHUNK_LOG2

    def big(i, c):
        o = i * chunk
        pltpu.make_async_copy(_rows(src, src_row + o, chunk), _rows(dst, dst_row + o, chunk), sem).start()
        return c

    lax.fori_loop(0, count >> SEG_CHUNK_LOG2, big, 0)
    for bit in range(SEG_CHUNK_LOG2 - 1, -1, -1):
        o = (count >> (bit + 1)) << (bit + 1)

        @pl.when(((count >> bit) & 1) == 1)
        def _():
            n = 1 << bit
            pltpu.make_async_copy(_rows(src, src_row + o, n), _rows(dst, dst_row + o, n), sem).start()


def _wait_tile(hbm_ref, stag_ref, slot_row, sem, to_hbm):
    vm = _rows(stag_ref, slot_row, TILE_ROWS)
    hb = _rows(hbm_ref, 0, TILE_ROWS)
    (pltpu.make_async_copy(vm, hb, sem) if to_hbm else pltpu.make_async_copy(hb, vm, sem)).wait()


def _dispatch_kernel(lsrc_ref, gdst_ref, cnt_ref, ld_ref, xn_ref, xs_ref, stag_ref, sems):
    b = pl.program_id(0)
    nb = pl.num_programs(0)
    T = MIX_TILE
    slot = b & 1
    base = slot * TILE_ROWS

    for kk_ in range(TOP_K):
        def row(t, c, kk_=kk_):
            r8 = ld_ref[0, 0, kk_ * T + t]
            stag_ref[pl.ds(pl.multiple_of(r8, ROW_SUB), ROW_SUB), :] = (
                xn_ref[pl.ds(pl.multiple_of(t * ROW_SUB, ROW_SUB), ROW_SUB), :])
            return c

        lax.fori_loop(0, T, row, 0, unroll=32)

    def seg(e, c):
        s = b * N_EXPERTS + e
        _segment_copies(stag_ref, base + lsrc_ref[s], xs_ref, gdst_ref[s], cnt_ref[s], sems.at[slot])
        return c

    lax.fori_loop(0, N_EXPERTS, seg, 0)

    @pl.when(b > 0)
    def _():
        _wait_tile(xs_ref, stag_ref, (1 - slot) * TILE_ROWS, sems.at[1 - slot], True)

    @pl.when(b == nb - 1)
    def _():
        _wait_tile(xs_ref, stag_ref, base, sems.at[slot], True)


def _dispatch_call(lsrc, gdst, cnt, ld_t, xn_rp):
    nt = ld_t.shape[0]
    T = MIX_TILE
    smem = pl.BlockSpec(memory_space=pltpu.SMEM)
    return pl.pallas_call(
        _dispatch_kernel,
        grid=(nt,),
        in_specs=[
            smem, smem, smem,
            pl.BlockSpec((1, 1, TILE_ROWS), lambda i: (i, 0, 0), memory_space=pltpu.SMEM),
            pl.BlockSpec((T * ROW_SUB, LANES), lambda i: (i, 0)),
        ],
        out_specs=pl.BlockSpec(memory_space=pl.ANY),
        out_shape=jax.ShapeDtypeStruct((nt * TILE_ROWS * ROW_SUB, LANES), jnp.float32),
        scratch_shapes=[pltpu.VMEM((2 * TILE_ROWS * ROW_SUB, LANES), jnp.float32),
                        pltpu.SemaphoreType.DMA((2,))],
        compiler_params=pltpu.CompilerParams(
            dimension_semantics=("arbitrary",), vmem_limit_bytes=VMEM_LIMIT),
        name="moe_dispatch",
    )(lsrc, gdst, cnt, ld_t, xn_rp)


def _expert_kernel(blk_ref, exp_ref, lo_ref, hi_ref, newexp_ref, newblk_ref, endblk_ref,
                   xs_hbm, wg_ref, wu_ref, wd_ref, ys_hbm,
                   xbuf_ref, ybuf_ref, wgub_ref, wdb_ref, in_sems, out_sems):
    i = pl.program_id(0)
    n_items = pl.num_programs(0)
    bf16 = jnp.bfloat16
    SB = EXPERT_SUB
    BS = EXPERT_BLOCK * ROW_SUB
    n_blocks = xs_hbm.shape[0] // BS
    blk = blk_ref[i]
    lo = lo_ref[i]
    hi = hi_ref[i]

    def block_rows(idx):
        start = idx * BS
        return pl.ds(start if isinstance(start, int) else pl.multiple_of(start, BS), BS)

    def in_copy(b, slot):
        return pltpu.make_async_copy(xs_hbm.at[block_rows(b)], xbuf_ref.at[block_rows(slot)], in_sems.at[slot])

    def out_copy(b, slot):
        return pltpu.make_async_copy(ybuf_ref.at[block_rows(slot)], ys_hbm.at[block_rows(b)], out_sems.at[slot])

    @pl.when(i == 0)
    def _():
        for b0 in range(EXPERT_IN_SLOTS - 1):
            in_copy(b0, b0).start()

    @pl.when(newblk_ref[i] == 1)
    def _():
        ahead = blk + (EXPERT_IN_SLOTS - 1)

        @pl.when(ahead < n_blocks)
        def _():
            in_copy(ahead, lax.rem(ahead, EXPERT_IN_SLOTS)).start()

        in_copy(blk, lax.rem(blk, EXPERT_IN_SLOTS)).wait()

        @pl.when(blk >= EXPERT_OUT_SLOTS)
        def _():
            out_copy(blk - EXPERT_OUT_SLOTS, lax.rem(blk, EXPERT_OUT_SLOTS)).wait()

    @pl.when(newexp_ref[i] == 1)
    def _():
        wgub_ref[:, 0:D_EXPERT] = wg_ref[0].astype(bf16)
        wgub_ref[:, D_EXPERT:2 * D_EXPERT] = wu_ref[0].astype(bf16)
        wdb_ref[...] = wd_ref[0].astype(bf16)

    xbase = lax.rem(blk, EXPERT_IN_SLOTS) * BS
    ybase = lax.rem(blk, EXPERT_OUT_SLOTS) * BS

    def chunk(base, r0, nrows, c):
        return pl.ds(base + r0 * ROW_SUB + c, nrows, stride=ROW_SUB)

    def run(r0, nrows, merge_first):
        xb = jnp.concatenate([xbuf_ref[chunk(xbase, r0, nrows, c), :] for c in range(ROW_SUB)], axis=1).astype(bf16)
        gu = _dot(xb, wgub_ref[...])
        g = gu[:, 0:D_EXPERT]
        u = gu[:, D_EXPERT:2 * D_EXPERT]
        h = (g * jax.nn.sigmoid(g) * u).astype(bf16)
        y = _dot(h, wdb_ref[...])
        first = SB if merge_first else 0
        if merge_first:
            rows = r0 + lax.broadcasted_iota(jnp.int32, (SB, LANES), 0)
            keep = jnp.logical_and(rows >= lo, rows < hi)
            for c in range(ROW_SUB):
                old = ybuf_ref[chunk(ybase, r0, SB, c), :]
                ybuf_ref[chunk(ybase, r0, SB, c), :] = jnp.where(keep, y[0:SB, c * LANES:(c + 1) * LANES], old)
        if nrows > first:
            for c in range(ROW_SUB):
                ybuf_ref[chunk(ybase, r0 + first, nrows - first, c), :] = y[first:nrows, c * LANES:(c + 1) * LANES]

    n_sub = EXPERT_BLOCK // SB
    from_top = lo == 0
    sub_aligned = (lo & (SB - 1)) == 0
    to_end = jnp.logical_and(jnp.logical_and(lo > 0, hi == EXPERT_BLOCK), jnp.logical_not(sub_aligned))
    subs_used = (hi + SB - 1) // SB
    first_sub = lo // SB
    for k in range(1, n_sub + 1):
        @pl.when(jnp.logical_and(from_top, subs_used == k))
        def _(k=k):
            run(0, k * SB, False)

    for q in range(n_sub):
        @pl.when(jnp.logical_and(to_end, first_sub == q))
        def _(q=q):
            run(q * SB, EXPERT_BLOCK - q * SB, True)

    for q in range(n_sub):
        r0 = q * SB
        inside = jnp.logical_and(lo > 0, jnp.logical_or(hi < EXPERT_BLOCK, sub_aligned))

        @pl.when(jnp.logical_and(inside, jnp.minimum(hi, r0 + SB) > jnp.maximum(lo, r0)))
        def _(r0=r0):
            @pl.when(lo <= r0)
            def _():
                run(r0, SB, False)

            @pl.when(lo > r0)
            def _():
                run(r0, SB, True)

    @pl.when(endblk_ref[i] == 1)
    def _():
        out_copy(blk, lax.rem(blk, EXPERT_OUT_SLOTS)).start()

    @pl.when(i == n_items - 1)
    def _():
        for b1 in range(n_blocks - EXPERT_OUT_SLOTS, n_blocks):
            out_copy(b1, b1 % EXPERT_OUT_SLOTS).wait()


def _expert_call(item_blk, item_exp, item_lo, item_hi, item_newexp, item_newblk, item_endblk,
                 xs_rp, w_eg, w_eu, w_ed):
    D = D_MODEL
    BS = EXPERT_BLOCK * ROW_SUB
    n_items = item_blk.shape[0]
    assert xs_rp.shape[0] % BS == 0 and xs_rp.shape[0] // BS >= max(EXPERT_IN_SLOTS, EXPERT_OUT_SLOTS)

    def w_map(i, blk, exp, lo, hi, newexp, newblk, endblk):
        return (exp[i], 0, 0)

    grid_spec = pltpu.PrefetchScalarGridSpec(
        num_scalar_prefetch=7,
        grid=(n_items,),
        in_specs=[
            pl.BlockSpec(memory_space=pl.ANY),
            pl.BlockSpec((1, D, D_EXPERT), w_map),
            pl.BlockSpec((1, D, D_EXPERT), w_map),
            pl.BlockSpec((1, D_EXPERT, D), w_map),
        ],
        out_specs=pl.BlockSpec(memory_space=pl.ANY),
        scratch_shapes=[
            pltpu.VMEM((EXPERT_IN_SLOTS * BS, LANES), jnp.float32),
            pltpu.VMEM((EXPERT_OUT_SLOTS * BS, LANES), jnp.float32),
            pltpu.VMEM((D, 2 * D_EXPERT), jnp.bfloat16),
            pltpu.VMEM((D_EXPERT, D), jnp.bfloat16),
            pltpu.SemaphoreType.DMA((EXPERT_IN_SLOTS,)),
            pltpu.SemaphoreType.DMA((EXPERT_OUT_SLOTS,)),
        ],
    )
    return pl.pallas_call(
        _expert_kernel,
        grid_spec=grid_spec,
        out_shape=jax.ShapeDtypeStruct(xs_rp.shape, jnp.float32),
        compiler_params=pltpu.CompilerParams(
            dimension_semantics=("arbitrary",), vmem_limit_bytes=VMEM_LIMIT),
        name="moe_experts",
    )(item_blk, item_exp, item_lo, item_hi, item_newexp, item_newblk, item_endblk, xs_rp, w_eg, w_eu, w_ed)


def _combine_kernel(lsrc_ref, gdst_ref, cnt_ref, ld_ref, w_ref, x2_ref, ys_ref, out_ref,
                    stag_ref, acc_ref, sems):
    b = pl.program_id(0)
    nb = pl.num_programs(0)
    T = MIX_TILE
    slot = b & 1
    base = slot * TILE_ROWS

    def fetch(tile, slot_):
        def seg(e, c):
            s = tile * N_EXPERTS + e
            _segment_copies(ys_ref, gdst_ref[s], stag_ref, slot_ * TILE_ROWS + lsrc_ref[s], cnt_ref[s],
                            sems.at[slot_])
            return c

        lax.fori_loop(0, N_EXPERTS, seg, 0)

    @pl.when(b == 0)
    def _():
        fetch(b, slot)

    @pl.when(b + 1 < nb)
    def _():
        fetch(b + 1, 1 - slot)

    _wait_tile(ys_ref, stag_ref, base, sems.at[slot], False)

    def tok(t, c):
        at = pl.ds(pl.multiple_of(t * ROW_SUB, ROW_SUB), ROW_SUB)
        a = x2_ref[at, :]
        for kk_ in range(TOP_K):
            r8 = ld_ref[0, 0, kk_ * T + t]
            w = w_ref[0, 0, kk_ * T + t]
            a = a + w * stag_ref[pl.ds(pl.multiple_of(r8, ROW_SUB), ROW_SUB), :]
        acc_ref[at, :] = a
        return c

    lax.fori_loop(0, T, tok, 0, unroll=4)
    for c in range(ROW_SUB):
        out_ref[:, c * LANES:(c + 1) * LANES] = acc_ref[pl.ds(c, T, stride=ROW_SUB), :]


def _combine_call(lsrc, gdst, cnt, ld_t, w_t, x2, ys_rp):
    D = D_MODEL
    N = x2.shape[0] // ROW_SUB
    nt = ld_t.shape[0]
    T = MIX_TILE
    smem = pl.BlockSpec(memory_space=pltpu.SMEM)
    smem_blk = pl.BlockSpec((1, 1, TILE_ROWS), lambda i: (i, 0, 0), memory_space=pltpu.SMEM)
    return pl.pallas_call(
        _combine_kernel,
        grid=(nt,),
        in_specs=[
            smem, smem, smem, smem_blk, smem_blk,
            pl.BlockSpec((T * ROW_SUB, LANES), lambda i: (i, 0)),
            pl.BlockSpec(memory_space=pl.ANY),
        ],
        out_specs=pl.BlockSpec((T, D), lambda i: (i, 0)),
        out_shape=jax.ShapeDtypeStruct((N, D), jnp.float32),
        scratch_shapes=[
            pltpu.VMEM((2 * TILE_ROWS * ROW_SUB, LANES), jnp.float32),
            pltpu.VMEM((T * ROW_SUB, LANES), jnp.float32),
            pltpu.SemaphoreType.DMA((2,)),
        ],
        compiler_params=pltpu.CompilerParams(
            dimension_semantics=("arbitrary",), vmem_limit_bytes=VMEM_LIMIT),
        name="moe_combine",
    )(lsrc, gdst, cnt, ld_t, w_t, x2, ys_rp)


def _work_items(gstart, totals, n_items):
    BM = EXPERT_BLOCK
    i32 = jnp.int32
    gend = gstart + totals
    first_blk = gstart // BM
    last_blk = jnp.maximum(gend - 1, gstart) // BM
    n_e = jnp.where(totals > 0, last_blk - first_blk + 1, 0)
    item_end = jnp.cumsum(n_e)
    item_start = item_end - n_e
    n_real = item_end[-1]
    j = jnp.minimum(jnp.arange(n_items, dtype=i32), n_real - 1)
    exp = jnp.sum((item_end[None, :] <= j[:, None]).astype(i32), axis=1)
    onehot = exp[:, None] == jnp.arange(N_EXPERTS, dtype=i32)[None, :]

    def of_exp(v):
        return jnp.sum(jnp.where(onehot, v[None, :], 0), axis=1)

    blk = of_exp(first_blk) + (j - of_exp(item_start))
    lo = jnp.maximum(of_exp(gstart), blk * BM) - blk * BM
    hi = jnp.minimum(of_exp(gend), (blk + 1) * BM) - blk * BM
    real = jnp.arange(n_items, dtype=i32) < n_real
    hi = jnp.where(real, hi, lo)
    prev_exp = jnp.concatenate([jnp.full((1,), -1, i32), exp[:-1]])
    prev_blk = jnp.concatenate([jnp.full((1,), -1, i32), blk[:-1]])
    next_blk = jnp.concatenate([blk[1:], jnp.full((1,), -1, i32)])
    is_last = jnp.arange(n_items, dtype=i32) == n_real - 1
    newexp = jnp.logical_and(real, exp != prev_exp).astype(i32)
    newblk = jnp.logical_and(real, blk != prev_blk).astype(i32)
    endblk = jnp.logical_and(real, jnp.logical_or(blk != next_blk, is_last)).astype(i32)
    return blk.astype(i32), exp.astype(i32), lo.astype(i32), hi.astype(i32), newexp, newblk, endblk


def _layer(x, mix_norm_w, w_in, conv_w, q_norm_w, k_norm_w, sinks, conv_out_norm_w,
           attn_out_norm_w, w_out, ffn_norm_w, w_router, router_bias, w_eg, w_eu, w_ed,
           w_sg, w_su, w_sd):
    B, S, D = x.shape
    N = B * S
    T = MIX_TILE
    nt = N // T
    bf16 = jnp.bfloat16
    f32 = jnp.float32
    i32 = jnp.int32

    wr_t = w_router.astype(f32).T
    wr_hi = wr_t.astype(bf16)
    wr_hl = jnp.concatenate([wr_hi, (wr_t - wr_hi.astype(f32)).astype(bf16)], axis=0)
    sink_col = jnp.broadcast_to(
        jnp.repeat(sinks.astype(f32), BLK).reshape(N_KV_HEADS, GQA_GROUP * BLK, 1),
        (N_KV_HEADS, GQA_GROUP * BLK, LANES))
    tri = jnp.asarray(np.triu(np.ones((T, T), np.float32), k=1), dtype=bf16)
    head_ones = jnp.asarray(np.kron(np.eye(2 * LANES // HEAD_DIM, dtype=np.float32),
                                    np.ones((HEAD_DIM, HEAD_DIM), np.float32)), dtype=bf16)
    bias_tab = jnp.asarray(_attn_bias_table())

    x2, xn_rp, wts, ld, cnt = _mixer_call(
        x, mix_norm_w.reshape(1, D), w_in.astype(bf16), conv_w,
        jnp.tile(q_norm_w, N_HEADS).reshape(1, D_ATTN), jnp.tile(k_norm_w, N_KV_HEADS).reshape(1, D_KV),
        sink_col, bias_tab, conv_out_norm_w.reshape(1, D_CONV), attn_out_norm_w.reshape(1, D_ATTN),
        w_out.astype(bf16), ffn_norm_w.reshape(1, D), w_sg.astype(bf16), w_su.astype(bf16),
        w_sd.astype(bf16), wr_hl, router_bias.astype(f32).reshape(N_EXPERTS, 1), tri, head_ones)

    cnt_te = cnt[:, 0].reshape(nt, N_EXPERTS)
    lsrc = jnp.cumsum(cnt_te, axis=1) - cnt_te
    before = jnp.cumsum(cnt_te, axis=0) - cnt_te
    totals = jnp.sum(cnt_te, axis=0)
    gstart = jnp.cumsum(totals) - totals
    gdst = gstart[None, :] + before
    n_items = (N * TOP_K) // EXPERT_BLOCK + N_EXPERTS
    items = _work_items(gstart.astype(i32), totals.astype(i32), n_items)

    def tile_major(a):
        return a.reshape(TOP_K, nt, T).transpose(1, 0, 2).reshape(nt, 1, TILE_ROWS)

    ld_t = tile_major(ld)
    seg_tabs = (lsrc.reshape(-1).astype(i32), gdst.reshape(-1).astype(i32), cnt_te.reshape(-1).astype(i32))
    xs_rp = _dispatch_call(*seg_tabs, ld_t, xn_rp)
    ys_rp = _expert_call(*items, xs_rp, w_eg, w_eu, w_ed)
    out = _combine_call(*seg_tabs, ld_t, tile_major(wts), x2, ys_rp)
    return out.reshape(B, S, D)


def kernel(x, mix_norm_w, w_in, conv_w, q_norm_w, k_norm_w, sinks, conv_out_norm_w, attn_out_norm_w, w_out, ffn_norm_w, w_router, router_bias, w_exp_gate, w_exp_up, w_exp_down, w_sh_gate, w_sh_up, w_sh_down):
    return _layer(x, mix_norm_w[0], w_in[0], conv_w[0], q_norm_w[0], k_norm_w[0], sinks[0],
                  conv_out_norm_w[0], attn_out_norm_w[0], w_out[0], ffn_norm_w[0], w_router[0],
                  router_bias[0], w_exp_gate[0], w_exp_up[0], w_exp_down[0], w_sh_gate[0],
                  w_sh_up[0], w_sh_down[0])
```

```python
import numpy as np
import jax
import jax.numpy as jnp
from jax import lax
from jax.experimental import pallas as pl
from jax.experimental.pallas import tpu as pltpu

D_MODEL = 1024
EPS = 1e-6
D_CONV = 512
CONV_K = 3
HEAD_DIM = 64
N_HEADS = 8
N_KV_HEADS = 2
GQA_GROUP = 4
D_ATTN = 512
D_KV = 128
WINDOW = 128
BLK = 128
D_IN_PROJ = 2304
N_EXPERTS = 64
TOP_K = 8
N_GROUPS = 8
GROUP_SIZE = 8
TOPK_GROUPS = 4
D_EXPERT = 256
ROUTED_SCALE = 2.5

LANES = 128
ROW_SUB = 8
MIX_TILE = 512
TILE_ROWS = TOP_K * MIX_TILE
EXPERT_BLOCK = 1024
EXPERT_SUB = 256
EXPERT_IN_SLOTS = 4
EXPERT_OUT_SLOTS = 3
SEG_CHUNK_LOG2 = 5
NEG_BIG = -1e30
VMEM_LIMIT = 58 * 1024 * 1024


def _alibi_slopes():
    return np.array([2.0 ** (-8.0 * (h + 1) / N_HEADS) for h in range(N_HEADS)], dtype=np.float32)


def _attn_bias_table():
    qi = np.arange(BLK)[:, None]
    kj = np.arange(2 * BLK)[None, :]
    dist = qi - kj + BLK
    inwin = (dist >= 0) & (dist < WINDOW)
    slopes = _alibi_slopes()
    out = np.zeros((N_KV_HEADS, GQA_GROUP * BLK, 2 * BLK), np.float32)
    for g in range(N_KV_HEADS):
        for i in range(GQA_GROUP):
            h = g * GQA_GROUP + i
            out[g, i * BLK:(i + 1) * BLK] = np.where(inwin, -slopes[h] * dist.astype(np.float32), NEG_BIG)
    return out


def _rms(x, w):
    ms = jnp.mean(x * x, axis=-1, keepdims=True)
    return x * lax.rsqrt(ms + EPS) * w


def _head_rms(x, w_tiled, head_ones):
    C = x.shape[1]
    W = min(C, 2 * LANES)
    sq = x * x
    hi = sq.astype(jnp.bfloat16)
    lo = (sq - hi.astype(jnp.float32)).astype(jnp.bfloat16)
    ones = head_ones[0:W, 0:W]
    ssq = jnp.concatenate(
        [_dot(hi[:, c:c + W], ones) + _dot(lo[:, c:c + W], ones) for c in range(0, C, W)], axis=1)
    return x * lax.rsqrt(ssq * (1.0 / HEAD_DIM) + EPS) * w_tiled


def _dot(a, b):
    return jnp.dot(a, b, preferred_element_type=jnp.float32)


def _dot_nt(a, b):
    return lax.dot_general(a, b, (((1,), (1,)), ((), ())), preferred_element_type=jnp.float32)


def _mixer_kernel(x_ref, mixw_ref, win_ref, convw_ref, qw_ref, kw_ref, sink_ref, bias_ref,
                  cnw_ref, anw_ref, wout_ref, fnw_ref, wsg_ref, wsu_ref, wsd_ref,
                  wrhl_ref, rbias_ref, tri_ref, hones_ref,
                  x2_ref, xn_ref, wts_ref, ld_ref, cnt_ref,
                  kc_ref, vc_ref, cc_ref):
    j = pl.program_id(1)
    T = MIX_TILE
    bf16 = jnp.bfloat16
    f32 = jnp.float32

    @pl.when(j == 0)
    def _():
        kc_ref[...] = jnp.zeros_like(kc_ref)
        vc_ref[...] = jnp.zeros_like(vc_ref)
        cc_ref[...] = jnp.zeros_like(cc_ref)

    x = x_ref[0]
    u = _rms(x, mixw_ref[...]).astype(bf16)
    qkv = _dot(u, win_ref[:, 3 * D_CONV:D_IN_PROJ])
    q = qkv[:, 0:D_ATTN]
    k = qkv[:, D_ATTN:D_ATTN + D_KV]
    v = qkv[:, D_ATTN + D_KV:D_ATTN + 2 * D_KV]
    conv_piece = 2 * LANES
    conv_pieces = []

    lane = lax.broadcasted_iota(jnp.int32, (1, LANES), 1)
    lane_lo = lane < HEAD_DIM
    qn = _head_rms(q, qw_ref[...], hones_ref) * (HEAD_DIM ** -0.5)
    kn = _head_rms(k, kw_ref[...], hones_ref)
    kfull = jnp.concatenate([kc_ref[...], kn], axis=0)
    vfull = jnp.concatenate([vc_ref[...], v], axis=0)
    kc_ref[...] = kn[T - BLK:T, :]
    vc_ref[...] = v[T - BLK:T, :]

    def _rep(a, g):
        r = pltpu.roll(a, HEAD_DIM, axis=1)
        two = jnp.where(lane_lo, a, r) if g == 0 else jnp.where(lane_lo, r, a)
        return jnp.concatenate([two, two], axis=1).astype(bf16)

    k_rep = [_rep(kfull, g) for g in range(N_KV_HEADS)]
    v_rep = [_rep(vfull, g) for g in range(N_KV_HEADS)]

    lane256 = lax.broadcasted_iota(jnp.int32, (1, 2 * LANES), 1)
    head_of_lane = lane256 // HEAD_DIM
    first_f = (j == 0).astype(f32)
    prev_key_mask = jnp.where(lane256 < BLK, first_f * NEG_BIG, 0.0)
    ones_cols = jnp.ones((2 * BLK, LANES), bf16)

    def lanes2(a):
        return jnp.concatenate([a, a], axis=1)

    attn_rows = []
    for i in range(T // BLK):
        grp_out = []
        for g in range(N_KV_HEADS):
            if len(conv_pieces) * conv_piece < 3 * D_CONV:
                c0 = len(conv_pieces) * conv_piece
                conv_pieces.append(_dot(u, win_ref[:, c0:c0 + conv_piece]))
            qg = qn[i * BLK:(i + 1) * BLK, g * 256:(g + 1) * 256]
            qst = jnp.concatenate(
                [jnp.where(head_of_lane == hi, qg, 0.0) for hi in range(GQA_GROUP)], axis=0).astype(bf16)
            kk = k_rep[g][i * BLK:i * BLK + 2 * BLK, :]
            vv = jnp.concatenate([v_rep[g][i * BLK:i * BLK + 2 * BLK, :], ones_cols], axis=1)
            s = _dot_nt(qst, kk) + bias_ref[g]
            if i == 0:
                s = s + prev_key_mask
            sink = sink_ref[g]
            m = jnp.maximum(jnp.max(s, axis=-1, keepdims=True), sink)
            e = jnp.exp(s - lanes2(m)).astype(bf16)
            r = _dot(e, vv)
            inv = 1.0 / (r[:, 2 * LANES:3 * LANES] + jnp.exp(sink - m))
            r = r[:, 0:2 * LANES] * lanes2(inv)
            o = jnp.where(head_of_lane == 0, r[0:BLK], 0.0)
            for hi in range(1, GQA_GROUP):
                o = jnp.where(head_of_lane == hi, r[hi * BLK:(hi + 1) * BLK], o)
            grp_out.append(o)
        attn_rows.append(jnp.concatenate(grp_out, axis=1))
    y_attn = jnp.concatenate(attn_rows, axis=0)

    assert len(conv_pieces) * conv_piece == 3 * D_CONV
    conv_proj = jnp.concatenate(conv_pieces, axis=1)
    b_gate = conv_proj[:, 0:D_CONV]
    c_gate = conv_proj[:, D_CONV:2 * D_CONV]
    hh = conv_proj[:, 2 * D_CONV:3 * D_CONV]
    ch = c_gate * hh
    prev = cc_ref[...]
    p6 = prev[6:7, :]
    p7 = prev[7:8, :]
    row = lax.broadcasted_iota(jnp.int32, (T, D_CONV), 0)
    ch_m1 = jnp.where(row == 0, p7, pltpu.roll(ch, 1, axis=0))
    ch_m2 = jnp.where(row == 0, p6, jnp.where(row == 1, p7, pltpu.roll(ch, 2, axis=0)))
    cw = convw_ref[...]
    y_conv = b_gate * (cw[0:1, :] * ch_m2 + cw[1:2, :] * ch_m1 + cw[2:3, :] * ch)
    cc_ref[...] = ch[T - 8:T, :]

    y_mix = jnp.concatenate([_rms(y_conv, cnw_ref[...]), _rms(y_attn, anw_ref[...])], axis=1)
    x1 = x + _dot(y_mix.astype(bf16), wout_ref[...])

    xn = _rms(x1, fnw_ref[...])
    for c in range(ROW_SUB):
        xn_ref[pl.ds(c, T, stride=ROW_SUB), :] = xn[:, c * LANES:(c + 1) * LANES]
    xh = xn.astype(bf16)

    xl = (xn - xh.astype(f32)).astype(bf16)
    w_hl = wrhl_ref[...]
    a_hl = _dot_nt(w_hl, xh)
    logits = a_hl[0:N_EXPERTS] + a_hl[N_EXPERTS:2 * N_EXPERTS] + _dot_nt(w_hl[0:N_EXPERTS], xl)

    gs = _dot(xh, wsg_ref[...])
    us = _dot(xh, wsu_ref[...])
    hs = (gs * jax.nn.sigmoid(gs) * us).astype(bf16)
    x2 = x1 + _dot(hs, wsd_ref[...])
    for c in range(ROW_SUB):
        x2_ref[pl.ds(c, T, stride=ROW_SUB), :] = x2[:, c * LANES:(c + 1) * LANES]

    scores = jax.nn.sigmoid(logits)
    biased = scores + rbias_ref[...]

    sub8 = lax.broadcasted_iota(jnp.int32, (GROUP_SIZE, T), 0)
    gscore = []
    for g in range(N_GROUPS):
        blk = biased[g * GROUP_SIZE:(g + 1) * GROUP_SIZE, :]
        m1 = jnp.max(blk, axis=0, keepdims=True)
        first = jnp.min(jnp.where(blk == m1, sub8, GROUP_SIZE), axis=0, keepdims=True)
        m2 = jnp.max(jnp.where(sub8 == first, -jnp.inf, blk), axis=0, keepdims=True)
        gscore.append(m1 + m2)
    masked_blocks = []
    for g in range(N_GROUPS):
        rank = jnp.zeros((1, T), jnp.int32)
        for o_ in range(N_GROUPS):
            if o_ == g:
                continue
            if o_ < g:
                ahead = gscore[o_] >= gscore[g]
            else:
                ahead = gscore[o_] > gscore[g]
            rank = rank + ahead.astype(jnp.int32)
        keep = rank < TOPK_GROUPS
        blk = biased[g * GROUP_SIZE:(g + 1) * GROUP_SIZE, :]
        masked_blocks.append(jnp.where(keep, blk, -jnp.inf))
    cur = jnp.concatenate(masked_blocks, axis=0)

    eiota = lax.broadcasted_iota(jnp.int32, (N_EXPERTS, T), 0)
    row8 = lax.broadcasted_iota(jnp.int32, (TOP_K, T), 0)
    w_out = jnp.zeros((TOP_K, T), f32)
    sel_dense = jnp.zeros((N_EXPERTS, T), f32)
    onehots = []
    for kk_ in range(TOP_K):
        mx = jnp.max(cur, axis=0, keepdims=True)
        sel_idx = jnp.min(jnp.where(cur == mx, eiota, N_EXPERTS), axis=0, keepdims=True)
        onehot = eiota == sel_idx
        w_k = jnp.sum(jnp.where(onehot, scores, 0.0), axis=0, keepdims=True)
        cur = jnp.where(onehot, -jnp.inf, cur)
        sel_dense = jnp.where(onehot, 1.0, sel_dense)
        w_out = jnp.where(row8 == kk_, w_k, w_out)
        onehots.append(onehot)
    wsum = jnp.sum(w_out, axis=0, keepdims=True)
    wts_ref[...] = w_out / wsum * ROUTED_SCALE

    cum = _dot(sel_dense.astype(bf16), tri_ref[...])
    cnt = jnp.broadcast_to(jnp.sum(sel_dense, axis=1, keepdims=True), (N_EXPERTS, LANES))
    erow = lax.broadcasted_iota(jnp.int32, (N_EXPERTS, LANES), 0)
    incl = cnt
    step = 1
    while step < N_EXPERTS:
        incl = incl + jnp.where(erow >= step, pltpu.roll(incl, step, axis=0), 0.0)
        step *= 2
    lstart = incl - cnt
    ld_dense = lstart[:, 0:1] + cum
    ld_out = jnp.zeros((TOP_K, T), f32)
    for kk_ in range(TOP_K):
        p_k = jnp.sum(jnp.where(onehots[kk_], ld_dense, 0.0), axis=0, keepdims=True)
        ld_out = jnp.where(row8 == kk_, p_k, ld_out)
    tile_idx = pl.program_id(0) * pl.num_programs(1) + j
    slot_base = ((tile_idx & 1) * TILE_ROWS).astype(f32)
    ld_ref[...] = ((ld_out + slot_base) * ROW_SUB).astype(jnp.int32)
    cnt_ref[...] = cnt.astype(jnp.int32)


def _mixer_call(x, mixw, win, convw, qw, kw, sink_col, bias_tab, cnw, anw, wout, fnw,
                wsg, wsu, wsd, wrhl, rbias, tri, head_ones):
    B, S, D = x.shape
    T = MIX_TILE
    nt = S // T
    N = B * S

    def full(a):
        nd = a.ndim
        return pl.BlockSpec(a.shape, lambda b, j, _nd=nd: (0,) * _nd)

    tok_spec = pl.BlockSpec((TOP_K, T), lambda b, j: (0, b * nt + j))
    in_arrays = [mixw, win, convw, qw, kw, sink_col, bias_tab, cnw, anw, wout, fnw,
                 wsg, wsu, wsd, wrhl, rbias, tri, head_ones]
    return pl.pallas_call(
        _mixer_kernel,
        grid=(B, nt),
        in_specs=[pl.BlockSpec((1, T, D), lambda b, j: (b, j, 0))] + [full(a) for a in in_arrays],
        out_specs=[
            pl.BlockSpec((T * ROW_SUB, LANES), lambda b, j: (b * nt + j, 0)),
            pl.BlockSpec((T * ROW_SUB, LANES), lambda b, j: (b * nt + j, 0)),
            tok_spec, tok_spec,
            pl.BlockSpec((N_EXPERTS, LANES), lambda b, j: (b * nt + j, 0)),
        ],
        out_shape=[
            jax.ShapeDtypeStruct((N * ROW_SUB, LANES), jnp.float32),
            jax.ShapeDtypeStruct((N * ROW_SUB, LANES), jnp.float32),
            jax.ShapeDtypeStruct((TOP_K, N), jnp.float32),
            jax.ShapeDtypeStruct((TOP_K, N), jnp.int32),
            jax.ShapeDtypeStruct((B * nt * N_EXPERTS, LANES), jnp.int32),
        ],
        scratch_shapes=[
            pltpu.VMEM((BLK, D_KV), jnp.float32),
            pltpu.VMEM((BLK, D_KV), jnp.float32),
            pltpu.VMEM((8, D_CONV), jnp.float32),
        ],
        compiler_params=pltpu.CompilerParams(
            dimension_semantics=("arbitrary", "arbitrary"), vmem_limit_bytes=VMEM_LIMIT),
        name="mixer_router",
    )(x, *in_arrays)


def _rows(ref, row, nrows):
    return ref.at[pl.ds(pl.multiple_of(row * ROW_SUB, ROW_SUB), nrows * ROW_SUB)]


def _segment_copies(src, src_row, dst, dst_row, count, sem):
    chunk = 1 << SEG_CHUNK_LOG2

    def big(i, c):
        o = i * chunk
        pltpu.make_async_copy(_rows(src, src_row + o, chunk), _rows(dst, dst_row + o, chunk), sem).start()
        return c

    lax.fori_loop(0, count >> SEG_CHUNK_LOG2, big, 0)
    for bit in range(SEG_CHUNK_LOG2 - 1, -1, -1):
        o = (count >> (bit + 1)) << (bit + 1)

        @pl.when(((count >> bit) & 1) == 1)
        def _():
            n = 1 << bit
            pltpu.make_async_copy(_rows(src, src_row + o, n), _rows(dst, dst_row + o, n), sem).start()


def _wait_tile(hbm_ref, stag_ref, slot_row, sem, to_hbm):
    vm = _rows(stag_ref, slot_row, TILE_ROWS)
    hb = _rows(hbm_ref, 0, TILE_ROWS)
    (pltpu.make_async_copy(vm, hb, sem) if to_hbm else pltpu.make_async_copy(hb, vm, sem)).wait()


def _dispatch_kernel(lsrc_ref, gdst_ref, cnt_ref, ld_ref, xn_ref, xs_ref, stag_ref, sems):
    b = pl.program_id(0)
    nb = pl.num_programs(0)
    T = MIX_TILE
    slot = b & 1
    base = slot * TILE_ROWS

    for kk_ in range(TOP_K):
        def row(t, c, kk_=kk_):
            r8 = ld_ref[0, 0, kk_ * T + t]
            stag_ref[pl.ds(pl.multiple_of(r8, ROW_SUB), ROW_SUB), :] = (
                xn_ref[pl.ds(pl.multiple_of(t * ROW_SUB, ROW_SUB), ROW_SUB), :])
            return c

        lax.fori_loop(0, T, row, 0, unroll=32)

    def seg(e, c):
        s = b * N_EXPERTS + e
        _segment_copies(stag_ref, base + lsrc_ref[s], xs_ref, gdst_ref[s], cnt_ref[s], sems.at[slot])
        return c

    lax.fori_loop(0, N_EXPERTS, seg, 0)

    @pl.when(b > 0)
    def _():
        _wait_tile(xs_ref, stag_ref, (1 - slot) * TILE_ROWS, sems.at[1 - slot], True)

    @pl.when(b == nb - 1)
    def _():
        _wait_tile(xs_ref, stag_ref, base, sems.at[slot], True)


def _dispatch_call(lsrc, gdst, cnt, ld_t, xn_rp):
    nt = ld_t.shape[0]
    T = MIX_TILE
    smem = pl.BlockSpec(memory_space=pltpu.SMEM)
    return pl.pallas_call(
        _dispatch_kernel,
        grid=(nt,),
        in_specs=[
            smem, smem, smem,
            pl.BlockSpec((1, 1, TILE_ROWS), lambda i: (i, 0, 0), memory_space=pltpu.SMEM),
            pl.BlockSpec((T * ROW_SUB, LANES), lambda i: (i, 0)),
        ],
        out_specs=pl.BlockSpec(memory_space=pl.ANY),
        out_shape=jax.ShapeDtypeStruct((nt * TILE_ROWS * ROW_SUB, LANES), jnp.float32),
        scratch_shapes=[pltpu.VMEM((2 * TILE_ROWS * ROW_SUB, LANES), jnp.float32),
                        pltpu.SemaphoreType.DMA((2,))],
        compiler_params=pltpu.CompilerParams(
            dimension_semantics=("arbitrary",), vmem_limit_bytes=VMEM_LIMIT),
        name="moe_dispatch",
    )(lsrc, gdst, cnt, ld_t, xn_rp)


def _expert_kernel(blk_ref, exp_ref, lo_ref, hi_ref, newexp_ref, newblk_ref, endblk_ref,
                   xs_hbm, wg_ref, wu_ref, wd_ref, ys_hbm,
                   xbuf_ref, ybuf_ref, wgub_ref, wdb_ref, in_sems, out_sems):
    i = pl.program_id(0)
    n_items = pl.num_programs(0)
    bf16 = jnp.bfloat16
    SB = EXPERT_SUB
    BS = EXPERT_BLOCK * ROW_SUB
    n_blocks = xs_hbm.shape[0] // EXPERT_BLOCK
    blk = blk_ref[i]
    lo = lo_ref[i]
    hi = hi_ref[i]

    class _Group:
        def __init__(self, copies):
            self.copies = copies

        def start(self):
            for cp in self.copies:
                cp.start()

        def wait(self):
            for cp in self.copies:
                cp.wait()

    def hbm_chunk(ref, b, c):
        start = b * EXPERT_BLOCK
        start = start if isinstance(start, int) else pl.multiple_of(start, EXPERT_BLOCK)
        return ref.at[pl.ds(start, EXPERT_BLOCK), c, :]

    def vmem_chunk(ref, slot, c):
        start = slot * EXPERT_BLOCK
        start = start if isinstance(start, int) else pl.multiple_of(start, EXPERT_BLOCK)
        return ref.at[pl.ds(start, EXPERT_BLOCK), pl.ds(c * LANES, LANES)]

    def in_copy(b, slot):
        return _Group([pltpu.make_async_copy(hbm_chunk(xs_hbm, b, c), vmem_chunk(xbuf_ref, slot, c), in_sems.at[slot])
                       for c in range(ROW_SUB)])

    def out_copy(b, slot):
        return _Group([pltpu.make_async_copy(vmem_chunk(ybuf_ref, slot, c), hbm_chunk(ys_hbm, b, c), out_sems.at[slot])
                       for c in range(ROW_SUB)])

    @pl.when(i == 0)
    def _():
        for b0 in range(EXPERT_IN_SLOTS - 1):
            in_copy(b0, b0).start()

    @pl.when(newblk_ref[i] == 1)
    def _():
        ahead = blk + (EXPERT_IN_SLOTS - 1)

        @pl.when(ahead < n_blocks)
        def _():
            in_copy(ahead, lax.rem(ahead, EXPERT_IN_SLOTS)).start()

        in_copy(blk, lax.rem(blk, EXPERT_IN_SLOTS)).wait()

        @pl.when(blk >= EXPERT_OUT_SLOTS)
        def _():
            out_copy(blk - EXPERT_OUT_SLOTS, lax.rem(blk, EXPERT_OUT_SLOTS)).wait()

    @pl.when(newexp_ref[i] == 1)
    def _():
        wgub_ref[:, 0:D_EXPERT] = wg_ref[0].astype(bf16)
        wgub_ref[:, D_EXPERT:2 * D_EXPERT] = wu_ref[0].astype(bf16)
        wdb_ref[...] = wd_ref[0].astype(bf16)

    xbase = pl.multiple_of(lax.rem(blk, EXPERT_IN_SLOTS) * EXPERT_BLOCK, EXPERT_BLOCK)
    ybase = pl.multiple_of(lax.rem(blk, EXPERT_OUT_SLOTS) * EXPERT_BLOCK, EXPERT_BLOCK)

    def run(r0, nrows, merge_first):
        xb = xbuf_ref[pl.ds(xbase + r0, nrows), :].astype(bf16)
        gu = _dot(xb, wgub_ref[...])
        g = gu[:, 0:D_EXPERT]
        u = gu[:, D_EXPERT:2 * D_EXPERT]
        h = (g * jax.nn.sigmoid(g) * u).astype(bf16)
        y = _dot(h, wdb_ref[...])
        first = SB if merge_first else 0
        if merge_first:
            rows = r0 + lax.broadcasted_iota(jnp.int32, (SB, D_MODEL), 0)
            keep = jnp.logical_and(rows >= lo, rows < hi)
            old = ybuf_ref[pl.ds(ybase + r0, SB), :]
            ybuf_ref[pl.ds(ybase + r0, SB), :] = jnp.where(keep, y[0:SB], old)
        if nrows > first:
            ybuf_ref[pl.ds(ybase + r0 + first, nrows - first), :] = y[first:nrows]

    n_sub = EXPERT_BLOCK // SB
    from_top = lo == 0
    sub_aligned = (lo & (SB - 1)) == 0
    to_end = jnp.logical_and(jnp.logical_and(lo > 0, hi == EXPERT_BLOCK), jnp.logical_not(sub_aligned))
    subs_used = (hi + SB - 1) // SB
    first_sub = lo // SB
    for k in range(1, n_sub + 1):
        @pl.when(jnp.logical_and(from_top, subs_used == k))
        def _(k=k):
            run(0, k * SB, False)

    for q in range(n_sub):
        @pl.when(jnp.logical_and(to_end, first_sub == q))
        def _(q=q):
            run(q * SB, EXPERT_BLOCK - q * SB, True)

    for q in range(n_sub):
        r0 = q * SB
        inside = jnp.logical_and(lo > 0, jnp.logical_or(hi < EXPERT_BLOCK, sub_aligned))

        @pl.when(jnp.logical_and(inside, jnp.minimum(hi, r0 + SB) > jnp.maximum(lo, r0)))
        def _(r0=r0):
            @pl.when(lo <= r0)
            def _():
                run(r0, SB, False)

            @pl.when(lo > r0)
            def _():
                run(r0, SB, True)

    @pl.when(endblk_ref[i] == 1)
    def _():
        out_copy(blk, lax.rem(blk, EXPERT_OUT_SLOTS)).start()

    @pl.when(i == n_items - 1)
    def _():
        for b1 in range(n_blocks - EXPERT_OUT_SLOTS, n_blocks):
            out_copy(b1, b1 % EXPERT_OUT_SLOTS).wait()


def _expert_call(item_blk, item_exp, item_lo, item_hi, item_newexp, item_newblk, item_endblk,
                 xs_rp, w_eg, w_eu, w_ed):
    D = D_MODEL
    BS = EXPERT_BLOCK * ROW_SUB
    n_items = item_blk.shape[0]
    assert xs_rp.shape[0] % BS == 0 and xs_rp.shape[0] // BS >= max(EXPERT_IN_SLOTS, EXPERT_OUT_SLOTS)
    xs3 = xs_rp.reshape(xs_rp.shape[0] // ROW_SUB, ROW_SUB, LANES)

    def w_map(i, blk, exp, lo, hi, newexp, newblk, endblk):
        return (exp[i], 0, 0)

    grid_spec = pltpu.PrefetchScalarGridSpec(
        num_scalar_prefetch=7,
        grid=(n_items,),
        in_specs=[
            pl.BlockSpec(memory_space=pl.ANY),
            pl.BlockSpec((1, D, D_EXPERT), w_map),
            pl.BlockSpec((1, D, D_EXPERT), w_map),
            pl.BlockSpec((1, D_EXPERT, D), w_map),
        ],
        out_specs=pl.BlockSpec(memory_space=pl.ANY),
        scratch_shapes=[
            pltpu.VMEM((EXPERT_IN_SLOTS * EXPERT_BLOCK, D), jnp.float32),
            pltpu.VMEM((EXPERT_OUT_SLOTS * EXPERT_BLOCK, D), jnp.float32),
            pltpu.VMEM((D, 2 * D_EXPERT), jnp.bfloat16),
            pltpu.VMEM((D_EXPERT, D), jnp.bfloat16),
            pltpu.SemaphoreType.DMA((EXPERT_IN_SLOTS,)),
            pltpu.SemaphoreType.DMA((EXPERT_OUT_SLOTS,)),
        ],
    )
    return pl.pallas_call(
        _expert_kernel,
        grid_spec=grid_spec,
        out_shape=jax.ShapeDtypeStruct(xs3.shape, jnp.float32),
        compiler_params=pltpu.CompilerParams(
            dimension_semantics=("arbitrary",), vmem_limit_bytes=VMEM_LIMIT),
        name="moe_experts",
    )(item_blk, item_exp, item_lo, item_hi, item_newexp, item_newblk, item_endblk, xs3, w_eg, w_eu, w_ed
      ).reshape(xs_rp.shape)


def _combine_kernel(lsrc_ref, gdst_ref, cnt_ref, ld_ref, w_ref, x2_ref, ys_ref, out_ref,
                    stag_ref, acc_ref, sems):
    b = pl.program_id(0)
    nb = pl.num_programs(0)
    T = MIX_TILE
    slot = b & 1
    base = slot * TILE_ROWS

    def fetch(tile, slot_):
        def seg(e, c):
            s = tile * N_EXPERTS + e
            _segment_copies(ys_ref, gdst_ref[s], stag_ref, slot_ * TILE_ROWS + lsrc_ref[s], cnt_ref[s],
                            sems.at[slot_])
            return c

        lax.fori_loop(0, N_EXPERTS, seg, 0)

    @pl.when(b == 0)
    def _():
        fetch(b, slot)

    @pl.when(b + 1 < nb)
    def _():
        fetch(b + 1, 1 - slot)

    _wait_tile(ys_ref, stag_ref, base, sems.at[slot], False)

    def tok(t, c):
        at = pl.ds(pl.multiple_of(t * ROW_SUB, ROW_SUB), ROW_SUB)
        a = x2_ref[at, :]
        for kk_ in range(TOP_K):
            r8 = ld_ref[0, 0, kk_ * T + t]
            w = w_ref[0, 0, kk_ * T + t]
            a = a + w * stag_ref[pl.ds(pl.multiple_of(r8, ROW_SUB), ROW_SUB), :]
        acc_ref[at, :] = a
        return c

    lax.fori_loop(0, T, tok, 0, unroll=4)
    for c in range(ROW_SUB):
        out_ref[:, c * LANES:(c + 1) * LANES] = acc_ref[pl.ds(c, T, stride=ROW_SUB), :]


def _combine_call(lsrc, gdst, cnt, ld_t, w_t, x2, ys_rp):
    D = D_MODEL
    N = x2.shape[0] // ROW_SUB
    nt = ld_t.shape[0]
    T = MIX_TILE
    smem = pl.BlockSpec(memory_space=pltpu.SMEM)
    smem_blk = pl.BlockSpec((1, 1, TILE_ROWS), lambda i: (i, 0, 0), memory_space=pltpu.SMEM)
    return pl.pallas_call(
        _combine_kernel,
        grid=(nt,),
        in_specs=[
            smem, smem, smem, smem_blk, smem_blk,
            pl.BlockSpec((T * ROW_SUB, LANES), lambda i: (i, 0)),
            pl.BlockSpec(memory_space=pl.ANY),
        ],
        out_specs=pl.BlockSpec((T, D), lambda i: (i, 0)),
        out_shape=jax.ShapeDtypeStruct((N, D), jnp.float32),
        scratch_shapes=[
            pltpu.VMEM((2 * TILE_ROWS * ROW_SUB, LANES), jnp.float32),
            pltpu.VMEM((T * ROW_SUB, LANES), jnp.float32),
            pltpu.SemaphoreType.DMA((2,)),
        ],
        compiler_params=pltpu.CompilerParams(
            dimension_semantics=("arbitrary",), vmem_limit_bytes=VMEM_LIMIT),
        name="moe_combine",
    )(lsrc, gdst, cnt, ld_t, w_t, x2, ys_rp)


def _work_items(gstart, totals, n_items):
    BM = EXPERT_BLOCK
    i32 = jnp.int32
    gend = gstart + totals
    first_blk = gstart // BM
    last_blk = jnp.maximum(gend - 1, gstart) // BM
    n_e = jnp.where(totals > 0, last_blk - first_blk + 1, 0)
    item_end = jnp.cumsum(n_e)
    item_start = item_end - n_e
    n_real = item_end[-1]
    j = jnp.minimum(jnp.arange(n_items, dtype=i32), n_real - 1)
    exp = jnp.sum((item_end[None, :] <= j[:, None]).astype(i32), axis=1)
    onehot = exp[:, None] == jnp.arange(N_EXPERTS, dtype=i32)[None, :]

    def of_exp(v):
        return jnp.sum(jnp.where(onehot, v[None, :], 0), axis=1)

    blk = of_exp(first_blk) + (j - of_exp(item_start))
    lo = jnp.maximum(of_exp(gstart), blk * BM) - blk * BM
    hi = jnp.minimum(of_exp(gend), (blk + 1) * BM) - blk * BM
    real = jnp.arange(n_items, dtype=i32) < n_real
    hi = jnp.where(real, hi, lo)
    prev_exp = jnp.concatenate([jnp.full((1,), -1, i32), exp[:-1]])
    prev_blk = jnp.concatenate([jnp.full((1,), -1, i32), blk[:-1]])
    next_blk = jnp.concatenate([blk[1:], jnp.full((1,), -1, i32)])
    is_last = jnp.arange(n_items, dtype=i32) == n_real - 1
    newexp = jnp.logical_and(real, exp != prev_exp).astype(i32)
    newblk = jnp.logical_and(real, blk != prev_blk).astype(i32)
    endblk = jnp.logical_and(real, jnp.logical_or(blk != next_blk, is_last)).astype(i32)
    return blk.astype(i32), exp.astype(i32), lo.astype(i32), hi.astype(i32), newexp, newblk, endblk


def _layer(x, mix_norm_w, w_in, conv_w, q_norm_w, k_norm_w, sinks, conv_out_norm_w,
           attn_out_norm_w, w_out, ffn_norm_w, w_router, router_bias, w_eg, w_eu, w_ed,
           w_sg, w_su, w_sd):
    B, S, D = x.shape
    N = B * S
    T = MIX_TILE
    nt = N // T
    bf16 = jnp.bfloat16
    f32 = jnp.float32
    i32 = jnp.int32

    wr_t = w_router.astype(f32).T
    wr_hi = wr_t.astype(bf16)
    wr_hl = jnp.concatenate([wr_hi, (wr_t - wr_hi.astype(f32)).astype(bf16)], axis=0)
    sink_col = jnp.broadcast_to(
        jnp.repeat(sinks.astype(f32), BLK).reshape(N_KV_HEADS, GQA_GROUP * BLK, 1),
        (N_KV_HEADS, GQA_GROUP * BLK, LANES))
    tri = jnp.asarray(np.triu(np.ones((T, T), np.float32), k=1), dtype=bf16)
    head_ones = jnp.asarray(np.kron(np.eye(2 * LANES // HEAD_DIM, dtype=np.float32),
                                    np.ones((HEAD_DIM, HEAD_DIM), np.float32)), dtype=bf16)
    bias_tab = jnp.asarray(_attn_bias_table())

    x2, xn_rp, wts, ld, cnt = _mixer_call(
        x, mix_norm_w.reshape(1, D), w_in.astype(bf16), conv_w,
        jnp.tile(q_norm_w, N_HEADS).reshape(1, D_ATTN), jnp.tile(k_norm_w, N_KV_HEADS).reshape(1, D_KV),
        sink_col, bias_tab, conv_out_norm_w.reshape(1, D_CONV), attn_out_norm_w.reshape(1, D_ATTN),
        w_out.astype(bf16), ffn_norm_w.reshape(1, D), w_sg.astype(bf16), w_su.astype(bf16),
        w_sd.astype(bf16), wr_hl, router_bias.astype(f32).reshape(N_EXPERTS, 1), tri, head_ones)

    cnt_te = cnt[:, 0].reshape(nt, N_EXPERTS)
    lsrc = jnp.cumsum(cnt_te, axis=1) - cnt_te
    before = jnp.cumsum(cnt_te, axis=0) - cnt_te
    totals = jnp.sum(cnt_te, axis=0)
    gstart = jnp.cumsum(totals) - totals
    gdst = gstart[None, :] + before
    n_items = (N * TOP_K) // EXPERT_BLOCK + N_EXPERTS
    items = _work_items(gstart.astype(i32), totals.astype(i32), n_items)

    def tile_major(a):
        return a.reshape(TOP_K, nt, T).transpose(1, 0, 2).reshape(nt, 1, TILE_ROWS)

    ld_t = tile_major(ld)
    seg_tabs = (lsrc.reshape(-1).astype(i32), gdst.reshape(-1).astype(i32), cnt_te.reshape(-1).astype(i32))
    xs_rp = _dispatch_call(*seg_tabs, ld_t, xn_rp)
    ys_rp = _expert_call(*items, xs_rp, w_eg, w_eu, w_ed)
    out = _combine_call(*seg_tabs, ld_t, tile_major(wts), x2, ys_rp)
    return out.reshape(B, S, D)


def kernel(x, mix_norm_w, w_in, conv_w, q_norm_w, k_norm_w, sinks, conv_out_norm_w, attn_out_norm_w, w_out, ffn_norm_w, w_router, router_bias, w_exp_gate, w_exp_up, w_exp_down, w_sh_gate, w_sh_up, w_sh_down):
    return _layer(x, mix_norm_w[0], w_in[0], conv_w[0], q_norm_w[0], k_norm_w[0], sinks[0],
                  conv_out_norm_w[0], attn_out_norm_w[0], w_out[0], ffn_norm_w[0], w_router[0],
                  router_bias[0], w_exp_gate[0], w_exp_up[0], w_exp_down[0], w_sh_gate[0],
                  w_sh_up[0], w_sh_down[0])
```

```python
import numpy as np
import jax
import jax.numpy as jnp
from jax import lax
from jax.experimental import pallas as pl
from jax.experimental.pallas import tpu as pltpu

D_MODEL = 1024
EPS = 1e-6
D_CONV = 512
CONV_K = 3
HEAD_DIM = 64
N_HEADS = 8
N_KV_HEADS = 2
GQA_GROUP = 4
D_ATTN = 512
D_KV = 128
WINDOW = 128
BLK = 128
D_IN_PROJ = 2304
N_EXPERTS = 64
TOP_K = 8
N_GROUPS = 8
GROUP_SIZE = 8
TOPK_GROUPS = 4
D_EXPERT = 256
ROUTED_SCALE = 2.5

LANES = 128
ROW_SUB = 8
MIX_TILE = 512
TILE_ROWS = TOP_K * MIX_TILE
EXPERT_BLOCK = 1024
EXPERT_SUB = 256
EXPERT_IN_SLOTS = 4
EXPERT_OUT_SLOTS = 3
SEG_CHUNK_LOG2 = 5
NEG_BIG = -1e30
VMEM_LIMIT = 58 * 1024 * 1024


def _alibi_slopes():
    return np.array([2.0 ** (-8.0 * (h + 1) / N_HEADS) for h in range(N_HEADS)], dtype=np.float32)


def _attn_bias_table():
    qi = np.arange(BLK)[:, None]
    kj = np.arange(2 * BLK)[None, :]
    dist = qi - kj + BLK
    inwin = (dist >= 0) & (dist < WINDOW)
    slopes = _alibi_slopes()
    out = np.zeros((N_KV_HEADS, GQA_GROUP * BLK, 2 * BLK), np.float32)
    for g in range(N_KV_HEADS):
        for i in range(GQA_GROUP):
            h = g * GQA_GROUP + i
            out[g, i * BLK:(i + 1) * BLK] = np.where(inwin, -slopes[h] * dist.astype(np.float32), NEG_BIG)
    return out


def _rms(x, w):
    ms = jnp.mean(x * x, axis=-1, keepdims=True)
    return x * lax.rsqrt(ms + EPS) * w


def _head_rms(x, w_tiled, head_ones):
    C = x.shape[1]
    W = min(C, 2 * LANES)
    sq = x * x
    hi = sq.astype(jnp.bfloat16)
    lo = (sq - hi.astype(jnp.float32)).astype(jnp.bfloat16)
    ones = head_ones[0:W, 0:W]
    ssq = jnp.concatenate(
        [_dot(hi[:, c:c + W], ones) + _dot(lo[:, c:c + W], ones) for c in range(0, C, W)], axis=1)
    return x * lax.rsqrt(ssq * (1.0 / HEAD_DIM) + EPS) * w_tiled


def _dot(a, b):
    return jnp.dot(a, b, preferred_element_type=jnp.float32)


def _dot_nt(a, b):
    return lax.dot_general(a, b, (((1,), (1,)), ((), ())), preferred_element_type=jnp.float32)


def _mixer_kernel(x_ref, mixw_ref, win_ref, convw_ref, qw_ref, kw_ref, sink_ref, bias_ref,
                  cnw_ref, anw_ref, wout_ref, fnw_ref, wsg_ref, wsu_ref, wsd_ref,
                  wrhl_ref, rbias_ref, tri_ref, hones_ref,
                  x2_ref, xn_ref, wts_ref, ld_ref, cnt_ref,
                  kc_ref, vc_ref, cc_ref):
    j = pl.program_id(1)
    T = MIX_TILE
    bf16 = jnp.bfloat16
    f32 = jnp.float32

    @pl.when(j == 0)
    def _():
        kc_ref[...] = jnp.zeros_like(kc_ref)
        vc_ref[...] = jnp.zeros_like(vc_ref)
        cc_ref[...] = jnp.zeros_like(cc_ref)

    x = x_ref[0]
    u = _rms(x, mixw_ref[...]).astype(bf16)
    qkv = _dot(u, win_ref[:, 3 * D_CONV:D_IN_PROJ])
    q = qkv[:, 0:D_ATTN]
    k = qkv[:, D_ATTN:D_ATTN + D_KV]
    v = qkv[:, D_ATTN + D_KV:D_ATTN + 2 * D_KV]
    conv_piece = 2 * LANES
    conv_pieces = []

    lane = lax.broadcasted_iota(jnp.int32, (1, LANES), 1)
    lane_lo = lane < HEAD_DIM
    qn = _head_rms(q, qw_ref[...], hones_ref) * (HEAD_DIM ** -0.5)
    kn = _head_rms(k, kw_ref[...], hones_ref)
    kfull = jnp.concatenate([kc_ref[...], kn], axis=0)
    vfull = jnp.concatenate([vc_ref[...], v], axis=0)
    kc_ref[...] = kn[T - BLK:T, :]
    vc_ref[...] = v[T - BLK:T, :]

    def _rep(a, g):
        r = pltpu.roll(a, HEAD_DIM, axis=1)
        two = jnp.where(lane_lo, a, r) if g == 0 else jnp.where(lane_lo, r, a)
        return jnp.concatenate([two, two], axis=1).astype(bf16)

    k_rep = [_rep(kfull, g) for g in range(N_KV_HEADS)]
    v_rep = [_rep(vfull, g) for g in range(N_KV_HEADS)]

    lane256 = lax.broadcasted_iota(jnp.int32, (1, 2 * LANES), 1)
    head_of_lane = lane256 // HEAD_DIM
    first_f = (j == 0).astype(f32)
    prev_key_mask = jnp.where(lane256 < BLK, first_f * NEG_BIG, 0.0)
    ones_cols = jnp.ones((2 * BLK, LANES), bf16)

    def lanes2(a):
        return jnp.concatenate([a, a], axis=1)

    attn_rows = []
    for i in range(T // BLK):
        grp_out = []
        for g in range(N_KV_HEADS):
            if len(conv_pieces) * conv_piece < 3 * D_CONV:
                c0 = len(conv_pieces) * conv_piece
                conv_pieces.append(_dot(u, win_ref[:, c0:c0 + conv_piece]))
            qg = qn[i * BLK:(i + 1) * BLK, g * 256:(g + 1) * 256]
            qst = jnp.concatenate(
                [jnp.where(head_of_lane == hi, qg, 0.0) for hi in range(GQA_GROUP)], axis=0).astype(bf16)
            kk = k_rep[g][i * BLK:i * BLK + 2 * BLK, :]
            vv = jnp.concatenate([v_rep[g][i * BLK:i * BLK + 2 * BLK, :], ones_cols], axis=1)
            s = _dot_nt(qst, kk) + bias_ref[g]
            if i == 0:
                s = s + prev_key_mask
            sink = sink_ref[g]
            m = jnp.maximum(jnp.max(s, axis=-1, keepdims=True), sink)
            e = jnp.exp(s - lanes2(m)).astype(bf16)
            r = _dot(e, vv)
            inv = 1.0 / (r[:, 2 * LANES:3 * LANES] + jnp.exp(sink - m))
            r = r[:, 0:2 * LANES] * lanes2(inv)
            o = jnp.where(head_of_lane == 0, r[0:BLK], 0.0)
            for hi in range(1, GQA_GROUP):
                o = jnp.where(head_of_lane == hi, r[hi * BLK:(hi + 1) * BLK], o)
            grp_out.append(o)
        attn_rows.append(jnp.concatenate(grp_out, axis=1))
    y_attn = jnp.concatenate(attn_rows, axis=0)

    assert len(conv_pieces) * conv_piece == 3 * D_CONV
    conv_proj = jnp.concatenate(conv_pieces, axis=1)
    b_gate = conv_proj[:, 0:D_CONV]
    c_gate = conv_proj[:, D_CONV:2 * D_CONV]
    hh = conv_proj[:, 2 * D_CONV:3 * D_CONV]
    ch = c_gate * hh
    prev = cc_ref[...]
    p6 = prev[6:7, :]
    p7 = prev[7:8, :]
    row = lax.broadcasted_iota(jnp.int32, (T, D_CONV), 0)
    ch_m1 = jnp.where(row == 0, p7, pltpu.roll(ch, 1, axis=0))
    ch_m2 = jnp.where(row == 0, p6, jnp.where(row == 1, p7, pltpu.roll(ch, 2, axis=0)))
    cw = convw_ref[...]
    y_conv = b_gate * (cw[0:1, :] * ch_m2 + cw[1:2, :] * ch_m1 + cw[2:3, :] * ch)
    cc_ref[...] = ch[T - 8:T, :]

    y_mix = jnp.concatenate([_rms(y_conv, cnw_ref[...]), _rms(y_attn, anw_ref[...])], axis=1)
    x1 = x + _dot(y_mix.astype(bf16), wout_ref[...])

    xn = _rms(x1, fnw_ref[...])
    for c in range(ROW_SUB):
        xn_ref[pl.ds(c, T, stride=ROW_SUB), :] = xn[:, c * LANES:(c + 1) * LANES]
    xh = xn.astype(bf16)

    xl = (xn - xh.astype(f32)).astype(bf16)
    w_hl = wrhl_ref[...]
    a_hl = _dot_nt(w_hl, xh)
    logits = a_hl[0:N_EXPERTS] + a_hl[N_EXPERTS:2 * N_EXPERTS] + _dot_nt(w_hl[0:N_EXPERTS], xl)

    gs = _dot(xh, wsg_ref[...])
    us = _dot(xh, wsu_ref[...])
    hs = (gs * jax.nn.sigmoid(gs) * us).astype(bf16)
    x2 = x1 + _dot(hs, wsd_ref[...])
    for c in range(ROW_SUB):
        x2_ref[pl.ds(c, T, stride=ROW_SUB), :] = x2[:, c * LANES:(c + 1) * LANES]

    scores = jax.nn.sigmoid(logits)
    biased = scores + rbias_ref[...]

    sub8 = lax.broadcasted_iota(jnp.int32, (GROUP_SIZE, T), 0)
    gscore = []
    for g in range(N_GROUPS):
        blk = biased[g * GROUP_SIZE:(g + 1) * GROUP_SIZE, :]
        m1 = jnp.max(blk, axis=0, keepdims=True)
        first = jnp.min(jnp.where(blk == m1, sub8, GROUP_SIZE), axis=0, keepdims=True)
        m2 = jnp.max(jnp.where(sub8 == first, -jnp.inf, blk), axis=0, keepdims=True)
        gscore.append(m1 + m2)
    masked_blocks = []
    for g in range(N_GROUPS):
        rank = jnp.zeros((1, T), jnp.int32)
        for o_ in range(N_GROUPS):
            if o_ == g:
                continue
            if o_ < g:
                ahead = gscore[o_] >= gscore[g]
            else:
                ahead = gscore[o_] > gscore[g]
            rank = rank + ahead.astype(jnp.int32)
        keep = rank < TOPK_GROUPS
        blk = biased[g * GROUP_SIZE:(g + 1) * GROUP_SIZE, :]
        masked_blocks.append(jnp.where(keep, blk, -jnp.inf))
    cur = jnp.concatenate(masked_blocks, axis=0)

    eiota = lax.broadcasted_iota(jnp.int32, (N_EXPERTS, T), 0)
    row8 = lax.broadcasted_iota(jnp.int32, (TOP_K, T), 0)
    w_out = jnp.zeros((TOP_K, T), f32)
    sel_dense = jnp.zeros((N_EXPERTS, T), f32)
    onehots = []
    for kk_ in range(TOP_K):
        mx = jnp.max(cur, axis=0, keepdims=True)
        sel_idx = jnp.min(jnp.where(cur == mx, eiota, N_EXPERTS), axis=0, keepdims=True)
        onehot = eiota == sel_idx
        w_k = jnp.sum(jnp.where(onehot, scores, 0.0), axis=0, keepdims=True)
        cur = jnp.where(onehot, -jnp.inf, cur)
        sel_dense = jnp.where(onehot, 1.0, sel_dense)
        w_out = jnp.where(row8 == kk_, w_k, w_out)
        onehots.append(onehot)
    wsum = jnp.sum(w_out, axis=0, keepdims=True)
    wts_ref[...] = w_out / wsum * ROUTED_SCALE

    cum = _dot(sel_dense.astype(bf16), tri_ref[...])
    cnt = jnp.broadcast_to(jnp.sum(sel_dense, axis=1, keepdims=True), (N_EXPERTS, LANES))
    erow = lax.broadcasted_iota(jnp.int32, (N_EXPERTS, LANES), 0)
    incl = cnt
    step = 1
    while step < N_EXPERTS:
        incl = incl + jnp.where(erow >= step, pltpu.roll(incl, step, axis=0), 0.0)
        step *= 2
    lstart = incl - cnt
    ld_dense = lstart[:, 0:1] + cum
    ld_out = jnp.zeros((TOP_K, T), f32)
    for kk_ in range(TOP_K):
        p_k = jnp.sum(jnp.where(onehots[kk_], ld_dense, 0.0), axis=0, keepdims=True)
        ld_out = jnp.where(row8 == kk_, p_k, ld_out)
    tile_idx = pl.program_id(0) * pl.num_programs(1) + j
    slot_base = ((tile_idx & 1) * TILE_ROWS).astype(f32)
    ld_ref[...] = ((ld_out + slot_base) * ROW_SUB).astype(jnp.int32)
    cnt_ref[...] = cnt.astype(jnp.int32)


def _mixer_call(x, mixw, win, convw, qw, kw, sink_col, bias_tab, cnw, anw, wout, fnw,
                wsg, wsu, wsd, wrhl, rbias, tri, head_ones):
    B, S, D = x.shape
    T = MIX_TILE
    nt = S // T
    N = B * S

    def full(a):
        nd = a.ndim
        return pl.BlockSpec(a.shape, lambda b, j, _nd=nd: (0,) * _nd)

    tok_spec = pl.BlockSpec((TOP_K, T), lambda b, j: (0, b * nt + j))
    in_arrays = [mixw, win, convw, qw, kw, sink_col, bias_tab, cnw, anw, wout, fnw,
                 wsg, wsu, wsd, wrhl, rbias, tri, head_ones]
    return pl.pallas_call(
        _mixer_kernel,
        grid=(B, nt),
        in_specs=[pl.BlockSpec((1, T, D), lambda b, j: (b, j, 0))] + [full(a) for a in in_arrays],
        out_specs=[
            pl.BlockSpec((T * ROW_SUB, LANES), lambda b, j: (b * nt + j, 0)),
            pl.BlockSpec((T * ROW_SUB, LANES), lambda b, j: (b * nt + j, 0)),
            tok_spec, tok_spec,
            pl.BlockSpec((N_EXPERTS, LANES), lambda b, j: (b * nt + j, 0)),
        ],
        out_shape=[
            jax.ShapeDtypeStruct((N * ROW_SUB, LANES), jnp.float32),
            jax.ShapeDtypeStruct((N * ROW_SUB, LANES), jnp.float32),
            jax.ShapeDtypeStruct((TOP_K, N), jnp.float32),
            jax.ShapeDtypeStruct((TOP_K, N), jnp.int32),
            jax.ShapeDtypeStruct((B * nt * N_EXPERTS, LANES), jnp.int32),
        ],
        scratch_shapes=[
            pltpu.VMEM((BLK, D_KV), jnp.float32),
            pltpu.VMEM((BLK, D_KV), jnp.float32),
            pltpu.VMEM((8, D_CONV), jnp.float32),
        ],
        compiler_params=pltpu.CompilerParams(
            dimension_semantics=("arbitrary", "arbitrary"), vmem_limit_bytes=VMEM_LIMIT),
        name="mixer_router",
    )(x, *in_arrays)


def _rows(ref, row, nrows):
    return ref.at[pl.ds(pl.multiple_of(row * ROW_SUB, ROW_SUB), nrows * ROW_SUB)]


def _segment_copies(src, src_row, dst, dst_row, count, sem):
    chunk = 1 << SEG_CHUNK_LOG2

    def big(i, c):
        o = i * chunk
        pltpu.make_async_copy(_rows(src, src_row + o, chunk), _rows(dst, dst_row + o, chunk), sem).start()
        return c

    lax.fori_loop(0, count >> SEG_CHUNK_LOG2, big, 0)
    for bit in range(SEG_CHUNK_LOG2 - 1, -1, -1):
        o = (count >> (bit + 1)) << (bit + 1)

        @pl.when(((count >> bit) & 1) == 1)
        def _():
            n = 1 << bit
            pltpu.make_async_copy(_rows(src, src_row + o, n), _rows(dst, dst_row + o, n), sem).start()


def _wait_tile(hbm_ref, stag_ref, slot_row, sem, to_hbm):
    vm = _rows(stag_ref, slot_row, TILE_ROWS)
    hb = _rows(hbm_ref, 0, TILE_ROWS)
    (pltpu.make_async_copy(vm, hb, sem) if to_hbm else pltpu.make_async_copy(hb, vm, sem)).wait()


def _dispatch_kernel(lsrc_ref, gdst_ref, cnt_ref, ld_ref, xn_ref, xs_ref, stag_ref, sems):
    b = pl.program_id(0)
    nb = pl.num_programs(0)
    T = MIX_TILE
    slot = b & 1
    base = slot * TILE_ROWS

    for kk_ in range(TOP_K):
        def row(t, c, kk_=kk_):
            r8 = ld_ref[0, 0, kk_ * T + t]
            stag_ref[pl.ds(pl.multiple_of(r8, ROW_SUB), ROW_SUB), :] = (
                xn_ref[pl.ds(pl.multiple_of(t * ROW_SUB, ROW_SUB), ROW_SUB), :])
            return c

        lax.fori_loop(0, T, row, 0, unroll=32)

    def seg(e, c):
        s = b * N_EXPERTS + e
        _segment_copies(stag_ref, base + lsrc_ref[s], xs_ref, gdst_ref[s], cnt_ref[s], sems.at[slot])
        return c

    lax.fori_loop(0, N_EXPERTS, seg, 0)

    @pl.when(b > 0)
    def _():
        _wait_tile(xs_ref, stag_ref, (1 - slot) * TILE_ROWS, sems.at[1 - slot], True)

    @pl.when(b == nb - 1)
    def _():
        _wait_tile(xs_ref, stag_ref, base, sems.at[slot], True)


def _dispatch_call(lsrc, gdst, cnt, ld_t, xn_rp):
    nt = ld_t.shape[0]
    T = MIX_TILE
    smem = pl.BlockSpec(memory_space=pltpu.SMEM)
    return pl.pallas_call(
        _dispatch_kernel,
        grid=(nt,),
        in_specs=[
            smem, smem, smem,
            pl.BlockSpec((1, 1, TILE_ROWS), lambda i: (i, 0, 0), memory_space=pltpu.SMEM),
            pl.BlockSpec((T * ROW_SUB, LANES), lambda i: (i, 0)),
        ],
        out_specs=pl.BlockSpec(memory_space=pl.ANY),
        out_shape=jax.ShapeDtypeStruct((nt * TILE_ROWS * ROW_SUB, LANES), jnp.float32),
        scratch_shapes=[pltpu.VMEM((2 * TILE_ROWS * ROW_SUB, LANES), jnp.float32),
                        pltpu.SemaphoreType.DMA((2,))],
        compiler_params=pltpu.CompilerParams(
            dimension_semantics=("arbitrary",), vmem_limit_bytes=VMEM_LIMIT),
        name="moe_dispatch",
    )(lsrc, gdst, cnt, ld_t, xn_rp)


def _expert_kernel(blk_ref, exp_ref, lo_ref, hi_ref, newexp_ref, newblk_ref, endblk_ref,
                   xs_hbm, wg_ref, wu_ref, wd_ref, ys_hbm,
                   xbuf_ref, ybuf_ref, wgub_ref, wdb_ref, in_sems, out_sems):
    i = pl.program_id(0)
    n_items = pl.num_programs(0)
    bf16 = jnp.bfloat16
    SB = EXPERT_SUB
    BS = EXPERT_BLOCK * ROW_SUB
    n_blocks = xs_hbm.shape[0] // EXPERT_BLOCK
    blk = blk_ref[i]
    lo = lo_ref[i]
    hi = hi_ref[i]

    class _Group:
        def __init__(self, copies):
            self.copies = copies

        def start(self):
            for cp in self.copies:
                cp.start()

        def wait(self):
            for cp in self.copies:
                cp.wait()

    def hbm_chunk(ref, b, c):
        start = b * EXPERT_BLOCK
        start = start if isinstance(start, int) else pl.multiple_of(start, EXPERT_BLOCK)
        return ref.at[pl.ds(start, EXPERT_BLOCK), c, :]

    def vmem_chunk(ref, slot, c):
        start = slot * EXPERT_BLOCK
        start = start if isinstance(start, int) else pl.multiple_of(start, EXPERT_BLOCK)
        return ref.at[pl.ds(start, EXPERT_BLOCK), pl.ds(c * LANES, LANES)]

    def in_copy(b, slot):
        return _Group([pltpu.make_async_copy(hbm_chunk(xs_hbm, b, c), vmem_chunk(xbuf_ref, slot, c), in_sems.at[slot])
                       for c in range(ROW_SUB)])

    def out_copy(b, slot):
        return _Group([pltpu.make_async_copy(vmem_chunk(ybuf_ref, slot, c), hbm_chunk(ys_hbm, b, c), out_sems.at[slot])
                       for c in range(ROW_SUB)])

    @pl.when(i == 0)
    def _():
        for b0 in range(EXPERT_IN_SLOTS - 1):
            in_copy(b0, b0).start()

    @pl.when(newblk_ref[i] == 1)
    def _():
        ahead = blk + (EXPERT_IN_SLOTS - 1)

        @pl.when(ahead < n_blocks)
        def _():
            in_copy(ahead, lax.rem(ahead, EXPERT_IN_SLOTS)).start()

        in_copy(blk, lax.rem(blk, EXPERT_IN_SLOTS)).wait()

        @pl.when(blk >= EXPERT_OUT_SLOTS)
        def _():
            out_copy(blk - EXPERT_OUT_SLOTS, lax.rem(blk, EXPERT_OUT_SLOTS)).wait()

    @pl.when(newexp_ref[i] == 1)
    def _():
        wgub_ref[:, 0:D_EXPERT] = wg_ref[0].astype(bf16)
        wgub_ref[:, D_EXPERT:2 * D_EXPERT] = wu_ref[0].astype(bf16)
        wdb_ref[...] = wd_ref[0].astype(bf16)

    xbase = pl.multiple_of(lax.rem(blk, EXPERT_IN_SLOTS) * EXPERT_BLOCK, EXPERT_BLOCK)
    ybase = pl.multiple_of(lax.rem(blk, EXPERT_OUT_SLOTS) * EXPERT_BLOCK, EXPERT_BLOCK)

    def run(r0, nrows, merge_first):
        xb = xbuf_ref[pl.ds(xbase + r0, nrows), :].astype(bf16)
        gu = _dot(xb, wgub_ref[...])
        g = gu[:, 0:D_EXPERT]
        u = gu[:, D_EXPERT:2 * D_EXPERT]
        h = (g * jax.nn.sigmoid(g) * u).astype(bf16)
        y = _dot(h, wdb_ref[...])
        first = SB if merge_first else 0
        if merge_first:
            rows = r0 + lax.broadcasted_iota(jnp.int32, (SB, D_MODEL), 0)
            keep = jnp.logical_and(rows >= lo, rows < hi)
            old = ybuf_ref[pl.ds(ybase + r0, SB), :]
            ybuf_ref[pl.ds(ybase + r0, SB), :] = jnp.where(keep, y[0:SB], old)
        if nrows > first:
            ybuf_ref[pl.ds(ybase + r0 + first, nrows - first), :] = y[first:nrows]

    n_sub = EXPERT_BLOCK // SB
    from_top = lo == 0
    sub_aligned = (lo & (SB - 1)) == 0
    to_end = jnp.logical_and(jnp.logical_and(lo > 0, hi == EXPERT_BLOCK), jnp.logical_not(sub_aligned))
    subs_used = (hi + SB - 1) // SB
    first_sub = lo // SB
    for k in range(1, n_sub + 1):
        @pl.when(jnp.logical_and(from_top, subs_used == k))
        def _(k=k):
            run(0, k * SB, False)

    for q in range(n_sub):
        @pl.when(jnp.logical_and(to_end, first_sub == q))
        def _(q=q):
            run(q * SB, EXPERT_BLOCK - q * SB, True)

    for q in range(n_sub):
        r0 = q * SB
        inside = jnp.logical_and(lo > 0, jnp.logical_or(hi < EXPERT_BLOCK, sub_aligned))

        @pl.when(jnp.logical_and(inside, jnp.minimum(hi, r0 + SB) > jnp.maximum(lo, r0)))
        def _(r0=r0):
            @pl.when(lo <= r0)
            def _():
                run(r0, SB, False)

            @pl.when(lo > r0)
            def _():
                run(r0, SB, True)

    @pl.when(endblk_ref[i] == 1)
    def _():
        out_copy(blk, lax.rem(blk, EXPERT_OUT_SLOTS)).start()

    @pl.when(i == n_items - 1)
    def _():
        for b1 in range(n_blocks - EXPERT_OUT_SLOTS, n_blocks):
            out_copy(b1, b1 % EXPERT_OUT_SLOTS).wait()


def _expert_call(item_blk, item_exp, item_lo, item_hi, item_newexp, item_newblk, item_endblk,
                 xs_rp, w_eg, w_eu, w_ed):
    D = D_MODEL
    BS = EXPERT_BLOCK * ROW_SUB
    n_items = item_blk.shape[0]
    assert xs_rp.shape[0] % BS == 0 and xs_rp.shape[0] // BS >= max(EXPERT_IN_SLOTS, EXPERT_OUT_SLOTS)
    xs3 = xs_rp.reshape(xs_rp.shape[0] // ROW_SUB, ROW_SUB, LANES)

    def w_map(i, blk, exp, lo, hi, newexp, newblk, endblk):
        return (exp[i], 0, 0)

    grid_spec = pltpu.PrefetchScalarGridSpec(
        num_scalar_prefetch=7,
        grid=(n_items,),
        in_specs=[
            pl.BlockSpec(memory_space=pl.ANY),
            pl.BlockSpec((1, D, D_EXPERT), w_map),
            pl.BlockSpec((1, D, D_EXPERT), w_map),
            pl.BlockSpec((1, D_EXPERT, D), w_map),
        ],
        out_specs=pl.BlockSpec(memory_space=pl.ANY),
        scratch_shapes=[
            pltpu.VMEM((EXPERT_IN_SLOTS * EXPERT_BLOCK, D), jnp.float32),
            pltpu.VMEM((EXPERT_OUT_SLOTS * EXPERT_BLOCK, D), jnp.float32),
            pltpu.VMEM((D, 2 * D_EXPERT), jnp.bfloat16),
            pltpu.VMEM((D_EXPERT, D), jnp.bfloat16),
            pltpu.SemaphoreType.DMA((EXPERT_IN_SLOTS,)),
            pltpu.SemaphoreType.DMA((EXPERT_OUT_SLOTS,)),
        ],
    )
    return pl.pallas_call(
        _expert_kernel,
        grid_spec=grid_spec,
        out_shape=jax.ShapeDtypeStruct(xs3.shape, jnp.float32),
        compiler_params=pltpu.CompilerParams(
            dimension_semantics=("arbitrary",), vmem_limit_bytes=VMEM_LIMIT),
        name="moe_experts",
    )(item_blk, item_exp, item_lo, item_hi, item_newexp, item_newblk, item_endblk, xs3, w_eg, w_eu, w_ed
      ).reshape(xs_rp.shape)


def _combine_kernel(lsrc_ref, gdst_ref, cnt_ref, ld_ref, w_ref, x2_ref, ys_ref, out_hbm,
                    stag_ref, acc_ref, sems, out_sems):
    b = pl.program_id(0)
    nb = pl.num_programs(0)
    T = MIX_TILE
    slot = b & 1
    base = slot * TILE_ROWS

    def out_copies(tile, slot_):
        src0 = slot_ * T
        src0 = src0 if isinstance(src0, int) else pl.multiple_of(src0, T)
        return [pltpu.make_async_copy(acc_ref.at[pl.ds(src0, T), c, :],
                                      out_hbm.at[pl.ds(pl.multiple_of(tile * T, T), T), pl.ds(c * LANES, LANES)],
                                      out_sems.at[slot_]) for c in range(ROW_SUB)]

    @pl.when(b >= 2)
    def _():
        for cp in out_copies(b - 2, slot):
            cp.wait()

    def fetch(tile, slot_):
        def seg(e, c):
            s = tile * N_EXPERTS + e
            _segment_copies(ys_ref, gdst_ref[s], stag_ref, slot_ * TILE_ROWS + lsrc_ref[s], cnt_ref[s],
                            sems.at[slot_])
            return c

        lax.fori_loop(0, N_EXPERTS, seg, 0)

    @pl.when(b == 0)
    def _():
        fetch(b, slot)

    @pl.when(b + 1 < nb)
    def _():
        fetch(b + 1, 1 - slot)

    _wait_tile(ys_ref, stag_ref, base, sems.at[slot], False)

    def tok(t, c):
        at = pl.ds(pl.multiple_of(t * ROW_SUB, ROW_SUB), ROW_SUB)
        a = x2_ref[at, :]
        for kk_ in range(TOP_K):
            r8 = ld_ref[0, 0, kk_ * T + t]
            w = w_ref[0, 0, kk_ * T + t]
            a = a + w * stag_ref[pl.ds(pl.multiple_of(r8, ROW_SUB), ROW_SUB), :]
        acc_ref[slot * T + t] = a
        return c

    lax.fori_loop(0, T, tok, 0, unroll=4)
    for cp in out_copies(b, slot):
        cp.start()

    @pl.when(b == nb - 1)
    def _():
        @pl.when(nb >= 2)
        def _():
            for cp in out_copies(b - 1, 1 - slot):
                cp.wait()

        for cp in out_copies(b, slot):
            cp.wait()


def _combine_call(lsrc, gdst, cnt, ld_t, w_t, x2, ys_rp):
    D = D_MODEL
    N = x2.shape[0] // ROW_SUB
    nt = ld_t.shape[0]
    T = MIX_TILE
    smem = pl.BlockSpec(memory_space=pltpu.SMEM)
    smem_blk = pl.BlockSpec((1, 1, TILE_ROWS), lambda i: (i, 0, 0), memory_space=pltpu.SMEM)
    return pl.pallas_call(
        _combine_kernel,
        grid=(nt,),
        in_specs=[
            smem, smem, smem, smem_blk, smem_blk,
            pl.BlockSpec((T * ROW_SUB, LANES), lambda i: (i, 0)),
            pl.BlockSpec(memory_space=pl.ANY),
        ],
        out_specs=pl.BlockSpec(memory_space=pl.ANY),
        out_shape=jax.ShapeDtypeStruct((N, D), jnp.float32),
        scratch_shapes=[
            pltpu.VMEM((2 * TILE_ROWS * ROW_SUB, LANES), jnp.float32),
            pltpu.VMEM((2 * T, ROW_SUB, LANES), jnp.float32),
            pltpu.SemaphoreType.DMA((2,)),
            pltpu.SemaphoreType.DMA((2,)),
        ],
        compiler_params=pltpu.CompilerParams(
            dimension_semantics=("arbitrary",), vmem_limit_bytes=VMEM_LIMIT),
        name="moe_combine",
    )(lsrc, gdst, cnt, ld_t, w_t, x2, ys_rp)


def _work_items(gstart, totals, n_items):
    BM = EXPERT_BLOCK
    i32 = jnp.int32
    gend = gstart + totals
    first_blk = gstart // BM
    last_blk = jnp.maximum(gend - 1, gstart) // BM
    n_e = jnp.where(totals > 0, last_blk - first_blk + 1, 0)
    item_end = jnp.cumsum(n_e)
    item_start = item_end - n_e
    n_real = item_end[-1]
    j = jnp.minimum(jnp.arange(n_items, dtype=i32), n_real - 1)
    exp = jnp.sum((item_end[None, :] <= j[:, None]).astype(i32), axis=1)
    onehot = exp[:, None] == jnp.arange(N_EXPERTS, dtype=i32)[None, :]

    def of_exp(v):
        return jnp.sum(jnp.where(onehot, v[None, :], 0), axis=1)

    blk = of_exp(first_blk) + (j - of_exp(item_start))
    lo = jnp.maximum(of_exp(gstart), blk * BM) - blk * BM
    hi = jnp.minimum(of_exp(gend), (blk + 1) * BM) - blk * BM
    real = jnp.arange(n_items, dtype=i32) < n_real
    hi = jnp.where(real, hi, lo)
    prev_exp = jnp.concatenate([jnp.full((1,), -1, i32), exp[:-1]])
    prev_blk = jnp.concatenate([jnp.full((1,), -1, i32), blk[:-1]])
    next_blk = jnp.concatenate([blk[1:], jnp.full((1,), -1, i32)])
    is_last = jnp.arange(n_items, dtype=i32) == n_real - 1
    newexp = jnp.logical_and(real, exp != prev_exp).astype(i32)
    newblk = jnp.logical_and(real, blk != prev_blk).astype(i32)
    endblk = jnp.logical_and(real, jnp.logical_or(blk != next_blk, is_last)).astype(i32)
    return blk.astype(i32), exp.astype(i32), lo.astype(i32), hi.astype(i32), newexp, newblk, endblk


def _layer(x, mix_norm_w, w_in, conv_w, q_norm_w, k_norm_w, sinks, conv_out_norm_w,
           attn_out_norm_w, w_out, ffn_norm_w, w_router, router_bias, w_eg, w_eu, w_ed,
           w_sg, w_su, w_sd):
    B, S, D = x.shape
    N = B * S
    T = MIX_TILE
    nt = N // T
    bf16 = jnp.bfloat16
    f32 = jnp.float32
    i32 = jnp.int32

    wr_t = w_router.astype(f32).T
    wr_hi = wr_t.astype(bf16)
    wr_hl = jnp.concatenate([wr_hi, (wr_t - wr_hi.astype(f32)).astype(bf16)], axis=0)
    sink_col = jnp.broadcast_to(
        jnp.repeat(sinks.astype(f32), BLK).reshape(N_KV_HEADS, GQA_GROUP * BLK, 1),
        (N_KV_HEADS, GQA_GROUP * BLK, LANES))
    tri = jnp.asarray(np.triu(np.ones((T, T), np.float32), k=1), dtype=bf16)
    head_ones = jnp.asarray(np.kron(np.eye(2 * LANES // HEAD_DIM, dtype=np.float32),
                                    np.ones((HEAD_DIM, HEAD_DIM), np.float32)), dtype=bf16)
    bias_tab = jnp.asarray(_attn_bias_table())

    x2, xn_rp, wts, ld, cnt = _mixer_call(
        x, mix_norm_w.reshape(1, D), w_in.astype(bf16), conv_w,
        jnp.tile(q_norm_w, N_HEADS).reshape(1, D_ATTN), jnp.tile(k_norm_w, N_KV_HEADS).reshape(1, D_KV),
        sink_col, bias_tab, conv_out_norm_w.reshape(1, D_CONV), attn_out_norm_w.reshape(1, D_ATTN),
        w_out.astype(bf16), ffn_norm_w.reshape(1, D), w_sg.astype(bf16), w_su.astype(bf16),
        w_sd.astype(bf16), wr_hl, router_bias.astype(f32).reshape(N_EXPERTS, 1), tri, head_ones)

    cnt_te = cnt[:, 0].reshape(nt, N_EXPERTS)
    lsrc = jnp.cumsum(cnt_te, axis=1) - cnt_te
    before = jnp.cumsum(cnt_te, axis=0) - cnt_te
    totals = jnp.sum(cnt_te, axis=0)
    gstart = jnp.cumsum(totals) - totals
    gdst = gstart[None, :] + before
    n_items = (N * TOP_K) // EXPERT_BLOCK + N_EXPERTS
    items = _work_items(gstart.astype(i32), totals.astype(i32), n_items)

    def tile_major(a):
        return a.reshape(TOP_K, nt, T).transpose(1, 0, 2).reshape(nt, 1, TILE_ROWS)

    ld_t = tile_major(ld)
    seg_tabs = (lsrc.reshape(-1).astype(i32), gdst.reshape(-1).astype(i32), cnt_te.reshape(-1).astype(i32))
    xs_rp = _dispatch_call(*seg_tabs, ld_t, xn_rp)
    ys_rp = _expert_call(*items, xs_rp, w_eg, w_eu, w_ed)
    out = _combine_call(*seg_tabs, ld_t, tile_major(wts), x2, ys_rp)
    return out.reshape(B, S, D)


def kernel(x, mix_norm_w, w_in, conv_w, q_norm_w, k_norm_w, sinks, conv_out_norm_w, attn_out_norm_w, w_out, ffn_norm_w, w_router, router_bias, w_exp_gate, w_exp_up, w_exp_down, w_sh_gate, w_sh_up, w_sh_down):
    return _layer(x, mix_norm_w[0], w_in[0], conv_w[0], q_norm_w[0], k_norm_w[0], sinks[0],
                  conv_out_norm_w[0], attn_out_norm_w[0], w_out[0], ffn_norm_w[0], w_router[0],
                  router_bias[0], w_exp_gate[0], w_exp_up[0], w_exp_down[0], w_sh_gate[0],
                  w_sh_up[0], w_sh_down[0])
```

```python
import numpy as np
import jax
import jax.numpy as jnp
from jax import lax
from jax.experimental import pallas as pl
from jax.experimental.pallas import tpu as pltpu

D_MODEL = 1024
EPS = 1e-6
D_CONV = 512
CONV_K = 3
HEAD_DIM = 64
N_HEADS = 8
N_KV_HEADS = 2
GQA_GROUP = 4
D_ATTN = 512
D_KV = 128
WINDOW = 128
BLK = 128
D_IN_PROJ = 2304
N_EXPERTS = 64
TOP_K = 8
N_GROUPS = 8
GROUP_SIZE = 8
TOPK_GROUPS = 4
D_EXPERT = 256
ROUTED_SCALE = 2.5

LANES = 128
ROW_SUB = 8
MIX_TILE = 512
TILE_ROWS = TOP_K * MIX_TILE
EXPERT_BLOCK = 1024
EXPERT_SUB = 256
EXPERT_IN_SLOTS = 4
EXPERT_OUT_SLOTS = 3
COMBINE_ACC_SLOTS = 3
SEG_CHUNK_LOG2 = 5
NEG_BIG = -1e30
VMEM_LIMIT = 58 * 1024 * 1024


def _alibi_slopes():
    return np.array([2.0 ** (-8.0 * (h + 1) / N_HEADS) for h in range(N_HEADS)], dtype=np.float32)


def _attn_bias_table():
    qi = np.arange(BLK)[:, None]
    kj = np.arange(2 * BLK)[None, :]
    dist = qi - kj + BLK
    inwin = (dist >= 0) & (dist < WINDOW)
    slopes = _alibi_slopes()
    out = np.zeros((N_KV_HEADS, GQA_GROUP * BLK, 2 * BLK), np.float32)
    for g in range(N_KV_HEADS):
        for i in range(GQA_GROUP):
            h = g * GQA_GROUP + i
            out[g, i * BLK:(i + 1) * BLK] = np.where(inwin, -slopes[h] * dist.astype(np.float32), NEG_BIG)
    return out


def _rms(x, w):
    ms = jnp.mean(x * x, axis=-1, keepdims=True)
    return x * lax.rsqrt(ms + EPS) * w


def _head_rms(x, w_tiled, head_ones):
    C = x.shape[1]
    W = min(C, 2 * LANES)
    sq = x * x
    hi = sq.astype(jnp.bfloat16)
    lo = (sq - hi.astype(jnp.float32)).astype(jnp.bfloat16)
    ones = head_ones[0:W, 0:W]
    ssq = jnp.concatenate(
        [_dot(hi[:, c:c + W], ones) + _dot(lo[:, c:c + W], ones) for c in range(0, C, W)], axis=1)
    return x * lax.rsqrt(ssq * (1.0 / HEAD_DIM) + EPS) * w_tiled


def _dot(a, b):
    return jnp.dot(a, b, preferred_element_type=jnp.float32)


def _dot_nt(a, b):
    return lax.dot_general(a, b, (((1,), (1,)), ((), ())), preferred_element_type=jnp.float32)


def _mixer_kernel(x_ref, mixw_ref, win_ref, convw_ref, qw_ref, kw_ref, sink_ref, bias_ref,
                  cnw_ref, anw_ref, wout_ref, fnw_ref, wsg_ref, wsu_ref, wsd_ref,
                  wrhl_ref, rbias_ref, tri_ref, hones_ref,
                  x2_ref, xn_ref, wts_ref, ld_ref, cnt_ref,
                  kc_ref, vc_ref, cc_ref):
    j = pl.program_id(1)
    T = MIX_TILE
    bf16 = jnp.bfloat16
    f32 = jnp.float32

    @pl.when(j == 0)
    def _():
        kc_ref[...] = jnp.zeros_like(kc_ref)
        vc_ref[...] = jnp.zeros_like(vc_ref)
        cc_ref[...] = jnp.zeros_like(cc_ref)

    x = x_ref[0]
    u = _rms(x, mixw_ref[...]).astype(bf16)
    qkv = _dot(u, win_ref[:, 3 * D_CONV:D_IN_PROJ])
    q = qkv[:, 0:D_ATTN]
    k = qkv[:, D_ATTN:D_ATTN + D_KV]
    v = qkv[:, D_ATTN + D_KV:D_ATTN + 2 * D_KV]
    conv_piece = 2 * LANES
    conv_pieces = []

    lane = lax.broadcasted_iota(jnp.int32, (1, LANES), 1)
    lane_lo = lane < HEAD_DIM
    qn = _head_rms(q, qw_ref[...], hones_ref) * (HEAD_DIM ** -0.5)
    kn = _head_rms(k, kw_ref[...], hones_ref)
    kfull = jnp.concatenate([kc_ref[...], kn], axis=0)
    vfull = jnp.concatenate([vc_ref[...], v], axis=0)
    kc_ref[...] = kn[T - BLK:T, :]
    vc_ref[...] = v[T - BLK:T, :]

    def _rep(a, g):
        r = pltpu.roll(a, HEAD_DIM, axis=1)
        two = jnp.where(lane_lo, a, r) if g == 0 else jnp.where(lane_lo, r, a)
        return jnp.concatenate([two, two], axis=1).astype(bf16)

    k_rep = [_rep(kfull, g) for g in range(N_KV_HEADS)]
    v_rep = [_rep(vfull, g) for g in range(N_KV_HEADS)]

    lane256 = lax.broadcasted_iota(jnp.int32, (1, 2 * LANES), 1)
    head_of_lane = lane256 // HEAD_DIM
    first_f = (j == 0).astype(f32)
    prev_key_mask = jnp.where(lane256 < BLK, first_f * NEG_BIG, 0.0)
    ones_cols = jnp.ones((2 * BLK, LANES), bf16)

    def lanes2(a):
        return jnp.concatenate([a, a], axis=1)

    attn_rows = []
    for i in range(T // BLK):
        grp_out = []
        for g in range(N_KV_HEADS):
            if len(conv_pieces) * conv_piece < 3 * D_CONV:
                c0 = len(conv_pieces) * conv_piece
                conv_pieces.append(_dot(u, win_ref[:, c0:c0 + conv_piece]))
            qg = qn[i * BLK:(i + 1) * BLK, g * 256:(g + 1) * 256]
            qst = jnp.concatenate(
                [jnp.where(head_of_lane == hi, qg, 0.0) for hi in range(GQA_GROUP)], axis=0).astype(bf16)
            kk = k_rep[g][i * BLK:i * BLK + 2 * BLK, :]
            vv = jnp.concatenate([v_rep[g][i * BLK:i * BLK + 2 * BLK, :], ones_cols], axis=1)
            s = _dot_nt(qst, kk) + bias_ref[g]
            if i == 0:
                s = s + prev_key_mask
            sink = sink_ref[g]
            m = jnp.maximum(jnp.max(s, axis=-1, keepdims=True), sink)
            e = jnp.exp(s - lanes2(m)).astype(bf16)
            r = _dot(e, vv)
            inv = 1.0 / (r[:, 2 * LANES:3 * LANES] + jnp.exp(sink - m))
            r = r[:, 0:2 * LANES] * lanes2(inv)
            o = jnp.where(head_of_lane == 0, r[0:BLK], 0.0)
            for hi in range(1, GQA_GROUP):
                o = jnp.where(head_of_lane == hi, r[hi * BLK:(hi + 1) * BLK], o)
            grp_out.append(o)
        attn_rows.append(jnp.concatenate(grp_out, axis=1))
    y_attn = jnp.concatenate(attn_rows, axis=0)

    assert len(conv_pieces) * conv_piece == 3 * D_CONV
    conv_proj = jnp.concatenate(conv_pieces, axis=1)
    b_gate = conv_proj[:, 0:D_CONV]
    c_gate = conv_proj[:, D_CONV:2 * D_CONV]
    hh = conv_proj[:, 2 * D_CONV:3 * D_CONV]
    ch = c_gate * hh
    prev = cc_ref[...]
    p6 = prev[6:7, :]
    p7 = prev[7:8, :]
    row = lax.broadcasted_iota(jnp.int32, (T, D_CONV), 0)
    ch_m1 = jnp.where(row == 0, p7, pltpu.roll(ch, 1, axis=0))
    ch_m2 = jnp.where(row == 0, p6, jnp.where(row == 1, p7, pltpu.roll(ch, 2, axis=0)))
    cw = convw_ref[...]
    y_conv = b_gate * (cw[0:1, :] * ch_m2 + cw[1:2, :] * ch_m1 + cw[2:3, :] * ch)
    cc_ref[...] = ch[T - 8:T, :]

    y_mix = jnp.concatenate([_rms(y_conv, cnw_ref[...]), _rms(y_attn, anw_ref[...])], axis=1)
    x1 = x + _dot(y_mix.astype(bf16), wout_ref[...])

    xn = _rms(x1, fnw_ref[...])
    xn_ref[...] = xn
    xh = xn.astype(bf16)

    xl = (xn - xh.astype(f32)).astype(bf16)
    w_hl = wrhl_ref[...]
    a_hl = _dot_nt(w_hl, xh)
    logits = a_hl[0:N_EXPERTS] + a_hl[N_EXPERTS:2 * N_EXPERTS] + _dot_nt(w_hl[0:N_EXPERTS], xl)

    gs = _dot(xh, wsg_ref[...])
    us = _dot(xh, wsu_ref[...])
    hs = (gs * jax.nn.sigmoid(gs) * us).astype(bf16)
    x2 = x1 + _dot(hs, wsd_ref[...])
    x2_ref[...] = x2

    scores = jax.nn.sigmoid(logits)
    biased = scores + rbias_ref[...]

    sub8 = lax.broadcasted_iota(jnp.int32, (GROUP_SIZE, T), 0)
    gscore = []
    for g in range(N_GROUPS):
        blk = biased[g * GROUP_SIZE:(g + 1) * GROUP_SIZE, :]
        m1 = jnp.max(blk, axis=0, keepdims=True)
        first = jnp.min(jnp.where(blk == m1, sub8, GROUP_SIZE), axis=0, keepdims=True)
        m2 = jnp.max(jnp.where(sub8 == first, -jnp.inf, blk), axis=0, keepdims=True)
        gscore.append(m1 + m2)
    masked_blocks = []
    for g in range(N_GROUPS):
        rank = jnp.zeros((1, T), jnp.int32)
        for o_ in range(N_GROUPS):
            if o_ == g:
                continue
            if o_ < g:
                ahead = gscore[o_] >= gscore[g]
            else:
                ahead = gscore[o_] > gscore[g]
            rank = rank + ahead.astype(jnp.int32)
        keep = rank < TOPK_GROUPS
        blk = biased[g * GROUP_SIZE:(g + 1) * GROUP_SIZE, :]
        masked_blocks.append(jnp.where(keep, blk, -jnp.inf))
    cur = jnp.concatenate(masked_blocks, axis=0)

    eiota = lax.broadcasted_iota(jnp.int32, (N_EXPERTS, T), 0)
    row8 = lax.broadcasted_iota(jnp.int32, (TOP_K, T), 0)
    w_out = jnp.zeros((TOP_K, T), f32)
    sel_dense = jnp.zeros((N_EXPERTS, T), f32)
    onehots = []
    for kk_ in range(TOP_K):
        mx = jnp.max(cur, axis=0, keepdims=True)
        sel_idx = jnp.min(jnp.where(cur == mx, eiota, N_EXPERTS), axis=0, keepdims=True)
        onehot = eiota == sel_idx
        w_k = jnp.sum(jnp.where(onehot, scores, 0.0), axis=0, keepdims=True)
        cur = jnp.where(onehot, -jnp.inf, cur)
        sel_dense = jnp.where(onehot, 1.0, sel_dense)
        w_out = jnp.where(row8 == kk_, w_k, w_out)
        onehots.append(onehot)
    wsum = jnp.sum(w_out, axis=0, keepdims=True)
    wts_ref[...] = w_out / wsum * ROUTED_SCALE

    cum = _dot(sel_dense.astype(bf16), tri_ref[...])
    cnt = jnp.broadcast_to(jnp.sum(sel_dense, axis=1, keepdims=True), (N_EXPERTS, LANES))
    erow = lax.broadcasted_iota(jnp.int32, (N_EXPERTS, LANES), 0)
    incl = cnt
    step = 1
    while step < N_EXPERTS:
        incl = incl + jnp.where(erow >= step, pltpu.roll(incl, step, axis=0), 0.0)
        step *= 2
    lstart = incl - cnt
    ld_dense = lstart[:, 0:1] + cum
    ld_out = jnp.zeros((TOP_K, T), f32)
    for kk_ in range(TOP_K):
        p_k = jnp.sum(jnp.where(onehots[kk_], ld_dense, 0.0), axis=0, keepdims=True)
        ld_out = jnp.where(row8 == kk_, p_k, ld_out)
    tile_idx = pl.program_id(0) * pl.num_programs(1) + j
    slot_base = ((tile_idx & 1) * TILE_ROWS).astype(f32)
    ld_ref[...] = ((ld_out + slot_base) * ROW_SUB).astype(jnp.int32)
    cnt_ref[...] = cnt.astype(jnp.int32)


def _mixer_call(x, mixw, win, convw, qw, kw, sink_col, bias_tab, cnw, anw, wout, fnw,
                wsg, wsu, wsd, wrhl, rbias, tri, head_ones):
    B, S, D = x.shape
    T = MIX_TILE
    nt = S // T
    N = B * S

    def full(a):
        nd = a.ndim
        return pl.BlockSpec(a.shape, lambda b, j, _nd=nd: (0,) * _nd)

    tok_spec = pl.BlockSpec((TOP_K, T), lambda b, j: (0, b * nt + j))
    in_arrays = [mixw, win, convw, qw, kw, sink_col, bias_tab, cnw, anw, wout, fnw,
                 wsg, wsu, wsd, wrhl, rbias, tri, head_ones]
    return pl.pallas_call(
        _mixer_kernel,
        grid=(B, nt),
        in_specs=[pl.BlockSpec((1, T, D), lambda b, j: (b, j, 0))] + [full(a) for a in in_arrays],
        out_specs=[
            pl.BlockSpec((T, D), lambda b, j: (b * nt + j, 0)),
            pl.BlockSpec((T, D), lambda b, j: (b * nt + j, 0)),
            tok_spec, tok_spec,
            pl.BlockSpec((N_EXPERTS, LANES), lambda b, j: (b * nt + j, 0)),
        ],
        out_shape=[
            jax.ShapeDtypeStruct((N, D), jnp.float32),
            jax.ShapeDtypeStruct((N, D), jnp.float32),
            jax.ShapeDtypeStruct((TOP_K, N), jnp.float32),
            jax.ShapeDtypeStruct((TOP_K, N), jnp.int32),
            jax.ShapeDtypeStruct((B * nt * N_EXPERTS, LANES), jnp.int32),
        ],
        scratch_shapes=[
            pltpu.VMEM((BLK, D_KV), jnp.float32),
            pltpu.VMEM((BLK, D_KV), jnp.float32),
            pltpu.VMEM((8, D_CONV), jnp.float32),
        ],
        compiler_params=pltpu.CompilerParams(
            dimension_semantics=("arbitrary", "arbitrary"), vmem_limit_bytes=VMEM_LIMIT),
        name="mixer_router",
    )(x, *in_arrays)


def _rows(ref, row, nrows):
    return ref.at[pl.ds(pl.multiple_of(row * ROW_SUB, ROW_SUB), nrows * ROW_SUB)]


def _segment_copies(src, src_row, dst, dst_row, count, sem):
    chunk = 1 << SEG_CHUNK_LOG2

    def big(i, c):
        o = i * chunk
        pltpu.make_async_copy(_rows(src, src_row + o, chunk), _rows(dst, dst_row + o, chunk), sem).start()
        return c

    lax.fori_loop(0, count >> SEG_CHUNK_LOG2, big, 0)
    for bit in range(SEG_CHUNK_LOG2 - 1, -1, -1):
        o = (count >> (bit + 1)) << (bit + 1)

        @pl.when(((count >> bit) & 1) == 1)
        def _():
            n = 1 << bit
            pltpu.make_async_copy(_rows(src, src_row + o, n), _rows(dst, dst_row + o, n), sem).start()


def _wait_tile(hbm_ref, stag_ref, slot_row, sem, to_hbm):
    vm = _rows(stag_ref, slot_row, TILE_ROWS)
    hb = _rows(hbm_ref, 0, TILE_ROWS)
    (pltpu.make_async_copy(vm, hb, sem) if to_hbm else pltpu.make_async_copy(hb, vm, sem)).wait()


def _tile_relayout_copies(std_hbm, tile, rpt_ref, slot, sem, to_vmem):
    T = MIX_TILE
    row0 = tile * T
    row0 = row0 if isinstance(row0, int) else pl.multiple_of(row0, T)
    slot0 = slot * T
    slot0 = slot0 if isinstance(slot0, int) else pl.multiple_of(slot0, T)
    copies = []
    for c in range(ROW_SUB):
        hb = std_hbm.at[pl.ds(row0, T), pl.ds(c * LANES, LANES)]
        vm = rpt_ref.at[pl.ds(slot0, T), c, :]
        copies.append(pltpu.make_async_copy(hb, vm, sem) if to_vmem else pltpu.make_async_copy(vm, hb, sem))
    return copies


def _dispatch_kernel(lsrc_ref, gdst_ref, cnt_ref, ld_ref, xn_hbm, xs_ref, stag_ref, xin_ref, sems, in_sems):
    b = pl.program_id(0)
    nb = pl.num_programs(0)
    T = MIX_TILE
    slot = b & 1
    base = slot * TILE_ROWS

    def fetch_rows(tile, slot_):
        return _tile_relayout_copies(xn_hbm, tile, xin_ref, slot_, in_sems.at[slot_], True)

    @pl.when(b == 0)
    def _():
        for cp in fetch_rows(b, slot):
            cp.start()

    @pl.when(b + 1 < nb)
    def _():
        for cp in fetch_rows(b + 1, 1 - slot):
            cp.start()

    for cp in fetch_rows(b, slot):
        cp.wait()

    for kk_ in range(TOP_K):
        def row(t, c, kk_=kk_):
            r8 = ld_ref[0, 0, kk_ * T + t]
            stag_ref[pl.ds(pl.multiple_of(r8, ROW_SUB), ROW_SUB), :] = xin_ref[slot * T + t]
            return c

        lax.fori_loop(0, T, row, 0, unroll=32)

    def seg(e, c):
        s = b * N_EXPERTS + e
        _segment_copies(stag_ref, base + lsrc_ref[s], xs_ref, gdst_ref[s], cnt_ref[s], sems.at[slot])
        return c

    lax.fori_loop(0, N_EXPERTS, seg, 0)

    @pl.when(b > 0)
    def _():
        _wait_tile(xs_ref, stag_ref, (1 - slot) * TILE_ROWS, sems.at[1 - slot], True)

    @pl.when(b == nb - 1)
    def _():
        _wait_tile(xs_ref, stag_ref, base, sems.at[slot], True)


def _dispatch_call(lsrc, gdst, cnt, ld_t, xn):
    nt = ld_t.shape[0]
    T = MIX_TILE
    smem = pl.BlockSpec(memory_space=pltpu.SMEM)
    return pl.pallas_call(
        _dispatch_kernel,
        grid=(nt,),
        in_specs=[
            smem, smem, smem,
            pl.BlockSpec((1, 1, TILE_ROWS), lambda i: (i, 0, 0), memory_space=pltpu.SMEM),
            pl.BlockSpec(memory_space=pl.ANY),
        ],
        out_specs=pl.BlockSpec(memory_space=pl.ANY),
        out_shape=jax.ShapeDtypeStruct((nt * TILE_ROWS * ROW_SUB, LANES), jnp.float32),
        scratch_shapes=[pltpu.VMEM((2 * TILE_ROWS * ROW_SUB, LANES), jnp.float32),
                        pltpu.VMEM((2 * T, ROW_SUB, LANES), jnp.float32),
                        pltpu.SemaphoreType.DMA((2,)),
                        pltpu.SemaphoreType.DMA((2,))],
        compiler_params=pltpu.CompilerParams(
            dimension_semantics=("arbitrary",), vmem_limit_bytes=VMEM_LIMIT),
        name="moe_dispatch",
    )(lsrc, gdst, cnt, ld_t, xn)


def _expert_kernel(blk_ref, exp_ref, lo_ref, hi_ref, newexp_ref, newblk_ref, endblk_ref,
                   xs_hbm, wg_ref, wu_ref, wd_ref, ys_hbm,
                   xbuf_ref, ybuf_ref, wgub_ref, wdb_ref, in_sems, out_sems):
    i = pl.program_id(0)
    n_items = pl.num_programs(0)
    bf16 = jnp.bfloat16
    SB = EXPERT_SUB
    BS = EXPERT_BLOCK * ROW_SUB
    n_blocks = xs_hbm.shape[0] // EXPERT_BLOCK
    blk = blk_ref[i]
    lo = lo_ref[i]
    hi = hi_ref[i]

    class _Group:
        def __init__(self, copies):
            self.copies = copies

        def start(self):
            for cp in self.copies:
                cp.start()

        def wait(self):
            for cp in self.copies:
                cp.wait()

    def hbm_chunk(ref, b, c):
        start = b * EXPERT_BLOCK
        start = start if isinstance(start, int) else pl.multiple_of(start, EXPERT_BLOCK)
        return ref.at[pl.ds(start, EXPERT_BLOCK), c, :]

    def vmem_chunk(ref, slot, c):
        start = slot * EXPERT_BLOCK
        start = start if isinstance(start, int) else pl.multiple_of(start, EXPERT_BLOCK)
        return ref.at[pl.ds(start, EXPERT_BLOCK), pl.ds(c * LANES, LANES)]

    def in_copy(b, slot):
        return _Group([pltpu.make_async_copy(hbm_chunk(xs_hbm, b, c), vmem_chunk(xbuf_ref, slot, c), in_sems.at[slot])
                       for c in range(ROW_SUB)])

    def out_copy(b, slot):
        return _Group([pltpu.make_async_copy(vmem_chunk(ybuf_ref, slot, c), hbm_chunk(ys_hbm, b, c), out_sems.at[slot])
                       for c in range(ROW_SUB)])

    @pl.when(i == 0)
    def _():
        for b0 in range(EXPERT_IN_SLOTS - 1):
            in_copy(b0, b0).start()

    @pl.when(newblk_ref[i] == 1)
    def _():
        ahead = blk + (EXPERT_IN_SLOTS - 1)

        @pl.when(ahead < n_blocks)
        def _():
            in_copy(ahead, lax.rem(ahead, EXPERT_IN_SLOTS)).start()

        in_copy(blk, lax.rem(blk, EXPERT_IN_SLOTS)).wait()

        @pl.when(blk >= EXPERT_OUT_SLOTS)
        def _():
            out_copy(blk - EXPERT_OUT_SLOTS, lax.rem(blk, EXPERT_OUT_SLOTS)).wait()

    @pl.when(newexp_ref[i] == 1)
    def _():
        wgub_ref[:, 0:D_EXPERT] = wg_ref[0].astype(bf16)
        wgub_ref[:, D_EXPERT:2 * D_EXPERT] = wu_ref[0].astype(bf16)
        wdb_ref[...] = wd_ref[0].astype(bf16)

    xbase = pl.multiple_of(lax.rem(blk, EXPERT_IN_SLOTS) * EXPERT_BLOCK, EXPERT_BLOCK)
    ybase = pl.multiple_of(lax.rem(blk, EXPERT_OUT_SLOTS) * EXPERT_BLOCK, EXPERT_BLOCK)

    def run(r0, nrows, merge_first):
        xb = xbuf_ref[pl.ds(xbase + r0, nrows), :].astype(bf16)
        gu = _dot(xb, wgub_ref[...])
        g = gu[:, 0:D_EXPERT]
        u = gu[:, D_EXPERT:2 * D_EXPERT]
        h = (g * jax.nn.sigmoid(g) * u).astype(bf16)
        y = _dot(h, wdb_ref[...])
        first = SB if merge_first else 0
        if merge_first:
            rows = r0 + lax.broadcasted_iota(jnp.int32, (SB, D_MODEL), 0)
            keep = jnp.logical_and(rows >= lo, rows < hi)
            old = ybuf_ref[pl.ds(ybase + r0, SB), :]
            ybuf_ref[pl.ds(ybase + r0, SB), :] = jnp.where(keep, y[0:SB], old)
        if nrows > first:
            ybuf_ref[pl.ds(ybase + r0 + first, nrows - first), :] = y[first:nrows]

    n_sub = EXPERT_BLOCK // SB
    from_top = lo == 0
    sub_aligned = (lo & (SB - 1)) == 0
    to_end = jnp.logical_and(jnp.logical_and(lo > 0, hi == EXPERT_BLOCK), jnp.logical_not(sub_aligned))
    subs_used = (hi + SB - 1) // SB
    first_sub = lo // SB
    for k in range(1, n_sub + 1):
        @pl.when(jnp.logical_and(from_top, subs_used == k))
        def _(k=k):
            run(0, k * SB, False)

    for q in range(n_sub):
        @pl.when(jnp.logical_and(to_end, first_sub == q))
        def _(q=q):
            run(q * SB, EXPERT_BLOCK - q * SB, True)

    for q in range(n_sub):
        r0 = q * SB
        inside = jnp.logical_and(lo > 0, jnp.logical_or(hi < EXPERT_BLOCK, sub_aligned))

        @pl.when(jnp.logical_and(inside, jnp.minimum(hi, r0 + SB) > jnp.maximum(lo, r0)))
        def _(r0=r0):
            @pl.when(lo <= r0)
            def _():
                run(r0, SB, False)

            @pl.when(lo > r0)
            def _():
                run(r0, SB, True)

    @pl.when(endblk_ref[i] == 1)
    def _():
        out_copy(blk, lax.rem(blk, EXPERT_OUT_SLOTS)).start()

    @pl.when(i == n_items - 1)
    def _():
        for b1 in range(n_blocks - EXPERT_OUT_SLOTS, n_blocks):
            out_copy(b1, b1 % EXPERT_OUT_SLOTS).wait()


def _expert_call(item_blk, item_exp, item_lo, item_hi, item_newexp, item_newblk, item_endblk,
                 xs_rp, w_eg, w_eu, w_ed):
    D = D_MODEL
    BS = EXPERT_BLOCK * ROW_SUB
    n_items = item_blk.shape[0]
    assert xs_rp.shape[0] % BS == 0 and xs_rp.shape[0] // BS >= max(EXPERT_IN_SLOTS, EXPERT_OUT_SLOTS)
    xs3 = xs_rp.reshape(xs_rp.shape[0] // ROW_SUB, ROW_SUB, LANES)

    def w_map(i, blk, exp, lo, hi, newexp, newblk, endblk):
        return (exp[i], 0, 0)

    grid_spec = pltpu.PrefetchScalarGridSpec(
        num_scalar_prefetch=7,
        grid=(n_items,),
        in_specs=[
            pl.BlockSpec(memory_space=pl.ANY),
            pl.BlockSpec((1, D, D_EXPERT), w_map),
            pl.BlockSpec((1, D, D_EXPERT), w_map),
            pl.BlockSpec((1, D_EXPERT, D), w_map),
        ],
        out_specs=pl.BlockSpec(memory_space=pl.ANY),
        scratch_shapes=[
            pltpu.VMEM((EXPERT_IN_SLOTS * EXPERT_BLOCK, D), jnp.float32),
            pltpu.VMEM((EXPERT_OUT_SLOTS * EXPERT_BLOCK, D), jnp.float32),
            pltpu.VMEM((D, 2 * D_EXPERT), jnp.bfloat16),
            pltpu.VMEM((D_EXPERT, D), jnp.bfloat16),
            pltpu.SemaphoreType.DMA((EXPERT_IN_SLOTS,)),
            pltpu.SemaphoreType.DMA((EXPERT_OUT_SLOTS,)),
        ],
    )
    return pl.pallas_call(
        _expert_kernel,
        grid_spec=grid_spec,
        out_shape=jax.ShapeDtypeStruct(xs3.shape, jnp.float32),
        compiler_params=pltpu.CompilerParams(
            dimension_semantics=("arbitrary",), vmem_limit_bytes=VMEM_LIMIT),
        name="moe_experts",
    )(item_blk, item_exp, item_lo, item_hi, item_newexp, item_newblk, item_endblk, xs3, w_eg, w_eu, w_ed
      ).reshape(xs_rp.shape)


def _combine_kernel(lsrc_ref, gdst_ref, cnt_ref, ld_ref, w_ref, x2_hbm, ys_ref, out_hbm,
                    stag_ref, acc_ref, sems, in_sems, out_sems):
    b = pl.program_id(0)
    nb = pl.num_programs(0)
    T = MIX_TILE
    slot = b & 1
    base = slot * TILE_ROWS
    aslot = lax.rem(b, COMBINE_ACC_SLOTS)

    def in_copies(tile):
        s_ = lax.rem(tile, COMBINE_ACC_SLOTS)
        return _tile_relayout_copies(x2_hbm, tile, acc_ref, s_, in_sems.at[s_], True)

    def out_copies(tile):
        s_ = lax.rem(tile, COMBINE_ACC_SLOTS)
        return _tile_relayout_copies(out_hbm, tile, acc_ref, s_, out_sems.at[s_], False)

    @pl.when(b >= COMBINE_ACC_SLOTS - 1)
    def _():
        for cp in out_copies(b - (COMBINE_ACC_SLOTS - 1)):
            cp.wait()

    @pl.when(b == 0)
    def _():
        for cp in in_copies(b):
            cp.start()

    @pl.when(b + 1 < nb)
    def _():
        for cp in in_copies(b + 1):
            cp.start()

    def fetch(tile, slot_):
        def seg(e, c):
            s = tile * N_EXPERTS + e
            _segment_copies(ys_ref, gdst_ref[s], stag_ref, slot_ * TILE_ROWS + lsrc_ref[s], cnt_ref[s],
                            sems.at[slot_])
            return c

        lax.fori_loop(0, N_EXPERTS, seg, 0)

    @pl.when(b == 0)
    def _():
        fetch(b, slot)

    @pl.when(b + 1 < nb)
    def _():
        fetch(b + 1, 1 - slot)

    _wait_tile(ys_ref, stag_ref, base, sems.at[slot], False)
    for cp in in_copies(b):
        cp.wait()

    def tok(t, c):
        a = acc_ref[aslot * T + t]
        for kk_ in range(TOP_K):
            r8 = ld_ref[0, 0, kk_ * T + t]
            w = w_ref[0, 0, kk_ * T + t]
            a = a + w * stag_ref[pl.ds(pl.multiple_of(r8, ROW_SUB), ROW_SUB), :]
        acc_ref[aslot * T + t] = a
        return c

    lax.fori_loop(0, T, tok, 0, unroll=4)
    for cp in out_copies(b):
        cp.start()

    @pl.when(b == nb - 1)
    def _():
        for back in range(COMBINE_ACC_SLOTS - 2, -1, -1):
            @pl.when(b >= back)
            def _(back=back):
                for cp in out_copies(b - back):
                    cp.wait()


def _combine_call(lsrc, gdst, cnt, ld_t, w_t, x2, ys_rp):
    N, D = x2.shape
    nt = ld_t.shape[0]
    T = MIX_TILE
    smem = pl.BlockSpec(memory_space=pltpu.SMEM)
    smem_blk = pl.BlockSpec((1, 1, TILE_ROWS), lambda i: (i, 0, 0), memory_space=pltpu.SMEM)
    return pl.pallas_call(
        _combine_kernel,
        grid=(nt,),
        in_specs=[
            smem, smem, smem, smem_blk, smem_blk,
            pl.BlockSpec(memory_space=pl.ANY),
            pl.BlockSpec(memory_space=pl.ANY),
        ],
        out_specs=pl.BlockSpec(memory_space=pl.ANY),
        out_shape=jax.ShapeDtypeStruct((N, D), jnp.float32),
        scratch_shapes=[
            pltpu.VMEM((2 * TILE_ROWS * ROW_SUB, LANES), jnp.float32),
            pltpu.VMEM((COMBINE_ACC_SLOTS * T, ROW_SUB, LANES), jnp.float32),
            pltpu.SemaphoreType.DMA((2,)),
            pltpu.SemaphoreType.DMA((COMBINE_ACC_SLOTS,)),
            pltpu.SemaphoreType.DMA((COMBINE_ACC_SLOTS,)),
        ],
        compiler_params=pltpu.CompilerParams(
            dimension_semantics=("arbitrary",), vmem_limit_bytes=VMEM_LIMIT),
        name="moe_combine",
    )(lsrc, gdst, cnt, ld_t, w_t, x2, ys_rp)


def _work_items(gstart, totals, n_items):
    BM = EXPERT_BLOCK
    i32 = jnp.int32
    gend = gstart + totals
    first_blk = gstart // BM
    last_blk = jnp.maximum(gend - 1, gstart) // BM
    n_e = jnp.where(totals > 0, last_blk - first_blk + 1, 0)
    item_end = jnp.cumsum(n_e)
    item_start = item_end - n_e
    n_real = item_end[-1]
    j = jnp.minimum(jnp.arange(n_items, dtype=i32), n_real - 1)
    exp = jnp.sum((item_end[None, :] <= j[:, None]).astype(i32), axis=1)
    onehot = exp[:, None] == jnp.arange(N_EXPERTS, dtype=i32)[None, :]

    def of_exp(v):
        return jnp.sum(jnp.where(onehot, v[None, :], 0), axis=1)

    blk = of_exp(first_blk) + (j - of_exp(item_start))
    lo = jnp.maximum(of_exp(gstart), blk * BM) - blk * BM
    hi = jnp.minimum(of_exp(gend), (blk + 1) * BM) - blk * BM
    real = jnp.arange(n_items, dtype=i32) < n_real
    hi = jnp.where(real, hi, lo)
    prev_exp = jnp.concatenate([jnp.full((1,), -1, i32), exp[:-1]])
    prev_blk = jnp.concatenate([jnp.full((1,), -1, i32), blk[:-1]])
    next_blk = jnp.concatenate([blk[1:], jnp.full((1,), -1, i32)])
    is_last = jnp.arange(n_items, dtype=i32) == n_real - 1
    newexp = jnp.logical_and(real, exp != prev_exp).astype(i32)
    newblk = jnp.logical_and(real, blk != prev_blk).astype(i32)
    endblk = jnp.logical_and(real, jnp.logical_or(blk != next_blk, is_last)).astype(i32)
    return blk.astype(i32), exp.astype(i32), lo.astype(i32), hi.astype(i32), newexp, newblk, endblk


def _layer(x, mix_norm_w, w_in, conv_w, q_norm_w, k_norm_w, sinks, conv_out_norm_w,
           attn_out_norm_w, w_out, ffn_norm_w, w_router, router_bias, w_eg, w_eu, w_ed,
           w_sg, w_su, w_sd):
    B, S, D = x.shape
    N = B * S
    T = MIX_TILE
    nt = N // T
    bf16 = jnp.bfloat16
    f32 = jnp.float32
    i32 = jnp.int32

    wr_t = w_router.astype(f32).T
    wr_hi = wr_t.astype(bf16)
    wr_hl = jnp.concatenate([wr_hi, (wr_t - wr_hi.astype(f32)).astype(bf16)], axis=0)
    sink_col = jnp.broadcast_to(
        jnp.repeat(sinks.astype(f32), BLK).reshape(N_KV_HEADS, GQA_GROUP * BLK, 1),
        (N_KV_HEADS, GQA_GROUP * BLK, LANES))
    tri = jnp.asarray(np.triu(np.ones((T, T), np.float32), k=1), dtype=bf16)
    head_ones = jnp.asarray(np.kron(np.eye(2 * LANES // HEAD_DIM, dtype=np.float32),
                                    np.ones((HEAD_DIM, HEAD_DIM), np.float32)), dtype=bf16)
    bias_tab = jnp.asarray(_attn_bias_table())

    x2, xn, wts, ld, cnt = _mixer_call(
        x, mix_norm_w.reshape(1, D), w_in.astype(bf16), conv_w,
        jnp.tile(q_norm_w, N_HEADS).reshape(1, D_ATTN), jnp.tile(k_norm_w, N_KV_HEADS).reshape(1, D_KV),
        sink_col, bias_tab, conv_out_norm_w.reshape(1, D_CONV), attn_out_norm_w.reshape(1, D_ATTN),
        w_out.astype(bf16), ffn_norm_w.reshape(1, D), w_sg.astype(bf16), w_su.astype(bf16),
        w_sd.astype(bf16), wr_hl, router_bias.astype(f32).reshape(N_EXPERTS, 1), tri, head_ones)

    cnt_te = cnt[:, 0].reshape(nt, N_EXPERTS)
    lsrc = jnp.cumsum(cnt_te, axis=1) - cnt_te
    before = jnp.cumsum(cnt_te, axis=0) - cnt_te
    totals = jnp.sum(cnt_te, axis=0)
    gstart = jnp.cumsum(totals) - totals
    gdst = gstart[None, :] + before
    n_items = (N * TOP_K) // EXPERT_BLOCK + N_EXPERTS
    items = _work_items(gstart.astype(i32), totals.astype(i32), n_items)

    def tile_major(a):
        return a.reshape(TOP_K, nt, T).transpose(1, 0, 2).reshape(nt, 1, TILE_ROWS)

    ld_t = tile_major(ld)
    seg_tabs = (lsrc.reshape(-1).astype(i32), gdst.reshape(-1).astype(i32), cnt_te.reshape(-1).astype(i32))
    xs_rp = _dispatch_call(*seg_tabs, ld_t, xn)
    ys_rp = _expert_call(*items, xs_rp, w_eg, w_eu, w_ed)
    out = _combine_call(*seg_tabs, ld_t, tile_major(wts), x2, ys_rp)
    return out.reshape(B, S, D)


def kernel(x, mix_norm_w, w_in, conv_w, q_norm_w, k_norm_w, sinks, conv_out_norm_w, attn_out_norm_w, w_out, ffn_norm_w, w_router, router_bias, w_exp_gate, w_exp_up, w_exp_down, w_sh_gate, w_sh_up, w_sh_down):
    return _layer(x, mix_norm_w[0], w_in[0], conv_w[0], q_norm_w[0], k_norm_w[0], sinks[0],
                  conv_out_norm_w[0], attn_out_norm_w[0], w_out[0], ffn_norm_w[0], w_router[0],
                  router_bias[0], w_exp_gate[0], w_exp_up[0], w_exp_down[0], w_sh_gate[0],
                  w_sh_up[0], w_sh_down[0])
```

```python
import numpy as np
import jax
import jax.numpy as jnp
from jax import lax
from jax.experimental import pallas as pl
from jax.experimental.pallas import tpu as pltpu

D_MODEL = 1024
EPS = 1e-6
D_CONV = 512
HEAD_DIM = 64
N_HEADS = 8
N_KV_HEADS = 2
GQA_GROUP = 4
D_ATTN = 512
D_KV = 128
WINDOW = 128
BLK = 128
D_IN_PROJ = 2304
N_EXPERTS = 64
TOP_K = 8
N_GROUPS = 8
GROUP_SIZE = 8
TOPK_GROUPS = 4
D_EXPERT = 256
ROUTED_SCALE = 2.5

LANES = 128
ROW_SUB = 8
MIX_TILE = 512
TILE_ROWS = TOP_K * MIX_TILE
EXPERT_BLOCK = 1024
EXPERT_SUB = 256
EXPERT_IN_SLOTS = 4
EXPERT_OUT_SLOTS = 3
COMBINE_ACC_SLOTS = 3
SEG_CHUNK_LOG2 = 5
NEG_BIG = -1e30
VMEM_LIMIT = 58 * 1024 * 1024


def _alibi_slopes():
    return np.array([2.0 ** (-8.0 * (h + 1) / N_HEADS) for h in range(N_HEADS)], dtype=np.float32)


def _attn_bias_table():
    qi = np.arange(BLK)[:, None]
    kj = np.arange(2 * BLK)[None, :]
    dist = qi - kj + BLK
    inwin = (dist >= 0) & (dist < WINDOW)
    slopes = _alibi_slopes()
    out = np.zeros((N_KV_HEADS, GQA_GROUP * BLK, 2 * BLK), np.float32)
    for g in range(N_KV_HEADS):
        for i in range(GQA_GROUP):
            h = g * GQA_GROUP + i
            out[g, i * BLK:(i + 1) * BLK] = np.where(inwin, -slopes[h] * dist.astype(np.float32), NEG_BIG)
    return out


def _rms(x, w):
    ms = jnp.mean(x * x, axis=-1, keepdims=True)
    return x * lax.rsqrt(ms + EPS) * w


def _head_rms(x, w_tiled, head_ones):
    C = x.shape[1]
    W = min(C, 2 * LANES)
    sq = x * x
    hi = sq.astype(jnp.bfloat16)
    lo = (sq - hi.astype(jnp.float32)).astype(jnp.bfloat16)
    ones = head_ones[0:W, 0:W]
    ssq = jnp.concatenate(
        [_dot(hi[:, c:c + W], ones) + _dot(lo[:, c:c + W], ones) for c in range(0, C, W)], axis=1)
    return x * lax.rsqrt(ssq * (1.0 / HEAD_DIM) + EPS) * w_tiled


def _dot(a, b):
    return jnp.dot(a, b, preferred_element_type=jnp.float32)


def _dot_nt(a, b):
    return lax.dot_general(a, b, (((1,), (1,)), ((), ())), preferred_element_type=jnp.float32)


def _mixer_kernel(x_ref, mixw_ref, win_ref, convw_ref, qw_ref, kw_ref, sink_ref, bias_ref,
                  cnw_ref, anw_ref, wout_ref, fnw_ref, wsg_ref, wsu_ref, wsd_ref,
                  wrhl_ref, rbias_ref, tri_ref, hones_ref,
                  x2_ref, xn_ref, wts_ref, ld_ref, cnt_ref,
                  kc_ref, vc_ref, cc_ref):
    j = pl.program_id(1)
    T = MIX_TILE
    bf16 = jnp.bfloat16
    f32 = jnp.float32

    @pl.when(j == 0)
    def _():
        kc_ref[...] = jnp.zeros_like(kc_ref)
        vc_ref[...] = jnp.zeros_like(vc_ref)
        cc_ref[...] = jnp.zeros_like(cc_ref)

    x = x_ref[0]
    u = _rms(x, mixw_ref[...]).astype(bf16)
    qkv = _dot(u, win_ref[:, 3 * D_CONV:D_IN_PROJ])
    q = qkv[:, 0:D_ATTN]
    k = qkv[:, D_ATTN:D_ATTN + D_KV]
    v = qkv[:, D_ATTN + D_KV:D_ATTN + 2 * D_KV]
    conv_piece = 2 * LANES
    conv_pieces = []

    lane = lax.broadcasted_iota(jnp.int32, (1, LANES), 1)
    lane_lo = lane < HEAD_DIM
    qn = _head_rms(q, qw_ref[...], hones_ref) * (HEAD_DIM ** -0.5)
    kn = _head_rms(k, kw_ref[...], hones_ref)
    kfull = jnp.concatenate([kc_ref[...], kn], axis=0)
    vfull = jnp.concatenate([vc_ref[...], v], axis=0)
    kc_ref[...] = kn[T - BLK:T, :]
    vc_ref[...] = v[T - BLK:T, :]

    def _rep(a, g):
        r = pltpu.roll(a, HEAD_DIM, axis=1)
        two = jnp.where(lane_lo, a, r) if g == 0 else jnp.where(lane_lo, r, a)
        return jnp.concatenate([two, two], axis=1).astype(bf16)

    k_rep = [_rep(kfull, g) for g in range(N_KV_HEADS)]
    v_rep = [_rep(vfull, g) for g in range(N_KV_HEADS)]

    lane256 = lax.broadcasted_iota(jnp.int32, (1, 2 * LANES), 1)
    head_of_lane = lane256 // HEAD_DIM
    first_f = (j == 0).astype(f32)
    prev_key_mask = jnp.where(lane256 < BLK, first_f * NEG_BIG, 0.0)
    ones_cols = jnp.ones((2 * BLK, LANES), bf16)

    def lanes2(a):
        return jnp.concatenate([a, a], axis=1)

    attn_rows = []
    for i in range(T // BLK):
        grp_out = []
        for g in range(N_KV_HEADS):
            if len(conv_pieces) * conv_piece < 3 * D_CONV:
                c0 = len(conv_pieces) * conv_piece
                conv_pieces.append(_dot(u, win_ref[:, c0:c0 + conv_piece]))
            qg = qn[i * BLK:(i + 1) * BLK, g * 256:(g + 1) * 256]
            qst = jnp.concatenate(
                [jnp.where(head_of_lane == hi, qg, 0.0) for hi in range(GQA_GROUP)], axis=0).astype(bf16)
            kk = k_rep[g][i * BLK:i * BLK + 2 * BLK, :]
            vv = jnp.concatenate([v_rep[g][i * BLK:i * BLK + 2 * BLK, :], ones_cols], axis=1)
            s = _dot_nt(qst, kk) + bias_ref[g]
            if i == 0:
                s = s + prev_key_mask
            sink = sink_ref[g]
            m = jnp.maximum(jnp.max(s, axis=-1, keepdims=True), sink)
            e = jnp.exp(s - lanes2(m)).astype(bf16)
            r = _dot(e, vv)
            inv = 1.0 / (r[:, 2 * LANES:3 * LANES] + jnp.exp(sink - m))
            r = r[:, 0:2 * LANES] * lanes2(inv)
            o = jnp.where(head_of_lane == 0, r[0:BLK], 0.0)
            for hi in range(1, GQA_GROUP):
                o = jnp.where(head_of_lane == hi, r[hi * BLK:(hi + 1) * BLK], o)
            grp_out.append(o)
        attn_rows.append(jnp.concatenate(grp_out, axis=1))
    y_attn = jnp.concatenate(attn_rows, axis=0)

    assert len(conv_pieces) * conv_piece == 3 * D_CONV
    conv_proj = jnp.concatenate(conv_pieces, axis=1)
    b_gate = conv_proj[:, 0:D_CONV]
    c_gate = conv_proj[:, D_CONV:2 * D_CONV]
    hh = conv_proj[:, 2 * D_CONV:3 * D_CONV]
    ch = c_gate * hh
    prev = cc_ref[...]
    p6 = prev[6:7, :]
    p7 = prev[7:8, :]
    row = lax.broadcasted_iota(jnp.int32, (T, D_CONV), 0)
    ch_m1 = jnp.where(row == 0, p7, pltpu.roll(ch, 1, axis=0))
    ch_m2 = jnp.where(row == 0, p6, jnp.where(row == 1, p7, pltpu.roll(ch, 2, axis=0)))
    cw = convw_ref[...]
    y_conv = b_gate * (cw[0:1, :] * ch_m2 + cw[1:2, :] * ch_m1 + cw[2:3, :] * ch)
    cc_ref[...] = ch[T - 8:T, :]

    y_mix = jnp.concatenate([_rms(y_conv, cnw_ref[...]), _rms(y_attn, anw_ref[...])], axis=1)
    x1 = x + _dot(y_mix.astype(bf16), wout_ref[...])

    xn = _rms(x1, fnw_ref[...])
    xn_ref[...] = xn
    xh = xn.astype(bf16)

    xl = (xn - xh.astype(f32)).astype(bf16)
    w_hl = wrhl_ref[...]
    a_hl = _dot_nt(w_hl, xh)
    logits = a_hl[0:N_EXPERTS] + a_hl[N_EXPERTS:2 * N_EXPERTS] + _dot_nt(w_hl[0:N_EXPERTS], xl)

    gs = _dot(xh, wsg_ref[...])
    us = _dot(xh, wsu_ref[...])
    hs = (gs * jax.nn.sigmoid(gs) * us).astype(bf16)
    x2 = x1 + _dot(hs, wsd_ref[...])
    x2_ref[...] = x2

    scores = jax.nn.sigmoid(logits)
    biased = scores + rbias_ref[...]

    sub8 = lax.broadcasted_iota(jnp.int32, (GROUP_SIZE, T), 0)
    gscore = []
    for g in range(N_GROUPS):
        blk = biased[g * GROUP_SIZE:(g + 1) * GROUP_SIZE, :]
        m1 = jnp.max(blk, axis=0, keepdims=True)
        first = jnp.min(jnp.where(blk == m1, sub8, GROUP_SIZE), axis=0, keepdims=True)
        m2 = jnp.max(jnp.where(sub8 == first, -jnp.inf, blk), axis=0, keepdims=True)
        gscore.append(m1 + m2)
    masked_blocks = []
    for g in range(N_GROUPS):
        rank = jnp.zeros((1, T), jnp.int32)
        for o_ in range(N_GROUPS):
            if o_ == g:
                continue
            if o_ < g:
                ahead = gscore[o_] >= gscore[g]
            else:
                ahead = gscore[o_] > gscore[g]
            rank = rank + ahead.astype(jnp.int32)
        keep = rank < TOPK_GROUPS
        blk = biased[g * GROUP_SIZE:(g + 1) * GROUP_SIZE, :]
        masked_blocks.append(jnp.where(keep, blk, -jnp.inf))
    cur = jnp.concatenate(masked_blocks, axis=0)

    eiota = lax.broadcasted_iota(jnp.int32, (N_EXPERTS, T), 0)
    row8 = lax.broadcasted_iota(jnp.int32, (TOP_K, T), 0)
    w_out = jnp.zeros((TOP_K, T), f32)
    sel_dense = jnp.zeros((N_EXPERTS, T), f32)
    onehots = []
    for kk_ in range(TOP_K):
        mx = jnp.max(cur, axis=0, keepdims=True)
        sel_idx = jnp.min(jnp.where(cur == mx, eiota, N_EXPERTS), axis=0, keepdims=True)
        onehot = eiota == sel_idx
        w_k = jnp.sum(jnp.where(onehot, scores, 0.0), axis=0, keepdims=True)
        cur = jnp.where(onehot, -jnp.inf, cur)
        sel_dense = jnp.where(onehot, 1.0, sel_dense)
        w_out = jnp.where(row8 == kk_, w_k, w_out)
        onehots.append(onehot)
    wsum = jnp.sum(w_out, axis=0, keepdims=True)
    wts_ref[...] = w_out / wsum * ROUTED_SCALE

    cum = _dot(sel_dense.astype(bf16), tri_ref[...])
    cnt = jnp.broadcast_to(jnp.sum(sel_dense, axis=1, keepdims=True), (N_EXPERTS, LANES))
    erow = lax.broadcasted_iota(jnp.int32, (N_EXPERTS, LANES), 0)
    incl = cnt
    step = 1
    while step < N_EXPERTS:
        incl = incl + jnp.where(erow >= step, pltpu.roll(incl, step, axis=0), 0.0)
        step *= 2
    lstart = incl - cnt
    ld_dense = lstart[:, 0:1] + cum
    ld_out = jnp.zeros((TOP_K, T), f32)
    for kk_ in range(TOP_K):
        p_k = jnp.sum(jnp.where(onehots[kk_], ld_dense, 0.0), axis=0, keepdims=True)
        ld_out = jnp.where(row8 == kk_, p_k, ld_out)
    tile_idx = pl.program_id(0) * pl.num_programs(1) + j
    slot_base = ((tile_idx & 1) * TILE_ROWS).astype(f32)
    ld_ref[...] = ((ld_out + slot_base) * ROW_SUB).astype(jnp.int32)
    cnt_ref[...] = cnt.astype(jnp.int32)


def _mixer_call(x, mixw, win, convw, qw, kw, sink_col, bias_tab, cnw, anw, wout, fnw,
                wsg, wsu, wsd, wrhl, rbias, tri, head_ones):
    B, S, D = x.shape
    T = MIX_TILE
    nt = S // T
    N = B * S

    def full(a):
        nd = a.ndim
        return pl.BlockSpec(a.shape, lambda b, j, _nd=nd: (0,) * _nd)

    tok_spec = pl.BlockSpec((TOP_K, T), lambda b, j: (0, b * nt + j))
    in_arrays = [mixw, win, convw, qw, kw, sink_col, bias_tab, cnw, anw, wout, fnw,
                 wsg, wsu, wsd, wrhl, rbias, tri, head_ones]
    return pl.pallas_call(
        _mixer_kernel,
        grid=(B, nt),
        in_specs=[pl.BlockSpec((1, T, D), lambda b, j: (b, j, 0))] + [full(a) for a in in_arrays],
        out_specs=[
            pl.BlockSpec((T, D), lambda b, j: (b * nt + j, 0)),
            pl.BlockSpec((T, D), lambda b, j: (b * nt + j, 0)),
            tok_spec, tok_spec,
            pl.BlockSpec((N_EXPERTS, LANES), lambda b, j: (b * nt + j, 0)),
        ],
        out_shape=[
            jax.ShapeDtypeStruct((N, D), jnp.float32),
            jax.ShapeDtypeStruct((N, D), jnp.float32),
            jax.ShapeDtypeStruct((TOP_K, N), jnp.float32),
            jax.ShapeDtypeStruct((TOP_K, N), jnp.int32),
            jax.ShapeDtypeStruct((B * nt * N_EXPERTS, LANES), jnp.int32),
        ],
        scratch_shapes=[
            pltpu.VMEM((BLK, D_KV), jnp.float32),
            pltpu.VMEM((BLK, D_KV), jnp.float32),
            pltpu.VMEM((8, D_CONV), jnp.float32),
        ],
        compiler_params=pltpu.CompilerParams(
            dimension_semantics=("arbitrary", "arbitrary"), vmem_limit_bytes=VMEM_LIMIT),
        name="mixer_router",
    )(x, *in_arrays)


def _rows(ref, row, nrows):
    return ref.at[pl.ds(pl.multiple_of(row * ROW_SUB, ROW_SUB), nrows * ROW_SUB)]


def _segment_copies(src, src_row, dst, dst_row, count, sem):
    chunk = 1 << SEG_CHUNK_LOG2

    def big(i, c):
        o = i * chunk
        pltpu.make_async_copy(_rows(src, src_row + o, chunk), _rows(dst, dst_row + o, chunk), sem).start()
        return c

    lax.fori_loop(0, count >> SEG_CHUNK_LOG2, big, 0)
    for bit in range(SEG_CHUNK_LOG2 - 1, -1, -1):
        o = (count >> (bit + 1)) << (bit + 1)

        @pl.when(((count >> bit) & 1) == 1)
        def _():
            n = 1 << bit
            pltpu.make_async_copy(_rows(src, src_row + o, n), _rows(dst, dst_row + o, n), sem).start()


def _wait_tile(hbm_ref, stag_ref, slot_row, sem, to_hbm):
    vm = _rows(stag_ref, slot_row, TILE_ROWS)
    hb = _rows(hbm_ref, 0, TILE_ROWS)
    (pltpu.make_async_copy(vm, hb, sem) if to_hbm else pltpu.make_async_copy(hb, vm, sem)).wait()


def _tile_relayout_copies(std_hbm, tile, rpt_ref, slot, sem, to_vmem):
    T = MIX_TILE
    row0 = tile * T
    row0 = row0 if isinstance(row0, int) else pl.multiple_of(row0, T)
    slot0 = slot * T
    slot0 = slot0 if isinstance(slot0, int) else pl.multiple_of(slot0, T)
    copies = []
    for c in range(ROW_SUB):
        hb = std_hbm.at[pl.ds(row0, T), pl.ds(c * LANES, LANES)]
        vm = rpt_ref.at[pl.ds(slot0, T), c, :]
        copies.append(pltpu.make_async_copy(hb, vm, sem) if to_vmem else pltpu.make_async_copy(vm, hb, sem))
    return copies


def _dispatch_kernel(lsrc_ref, gdst_ref, cnt_ref, ld_ref, xn_hbm, xs_ref, stag_ref, xin_ref, sems, in_sems):
    b = pl.program_id(0)
    nb = pl.num_programs(0)
    T = MIX_TILE
    slot = b & 1
    base = slot * TILE_ROWS

    def fetch_rows(tile, slot_):
        return _tile_relayout_copies(xn_hbm, tile, xin_ref, slot_, in_sems.at[slot_], True)

    @pl.when(b == 0)
    def _():
        for cp in fetch_rows(b, slot):
            cp.start()

    @pl.when(b + 1 < nb)
    def _():
        for cp in fetch_rows(b + 1, 1 - slot):
            cp.start()

    for cp in fetch_rows(b, slot):
        cp.wait()

    for kk_ in range(TOP_K):
        def row(t, c, kk_=kk_):
            r8 = ld_ref[0, 0, kk_ * T + t]
            stag_ref[pl.ds(pl.multiple_of(r8, ROW_SUB), ROW_SUB), :] = xin_ref[slot * T + t]
            return c

        lax.fori_loop(0, T, row, 0, unroll=32)

    def seg(e, c):
        s = b * N_EXPERTS + e
        _segment_copies(stag_ref, base + lsrc_ref[s], xs_ref, gdst_ref[s], cnt_ref[s], sems.at[slot])
        return c

    lax.fori_loop(0, N_EXPERTS, seg, 0)

    @pl.when(b > 0)
    def _():
        _wait_tile(xs_ref, stag_ref, (1 - slot) * TILE_ROWS, sems.at[1 - slot], True)

    @pl.when(b == nb - 1)
    def _():
        _wait_tile(xs_ref, stag_ref, base, sems.at[slot], True)


def _dispatch_call(lsrc, gdst, cnt, ld_t, xn):
    nt = ld_t.shape[0]
    T = MIX_TILE
    smem = pl.BlockSpec(memory_space=pltpu.SMEM)
    return pl.pallas_call(
        _dispatch_kernel,
        grid=(nt,),
        in_specs=[
            smem, smem, smem,
            pl.BlockSpec((1, 1, TILE_ROWS), lambda i: (i, 0, 0), memory_space=pltpu.SMEM),
            pl.BlockSpec(memory_space=pl.ANY),
        ],
        out_specs=pl.BlockSpec(memory_space=pl.ANY),
        out_shape=jax.ShapeDtypeStruct((nt * TILE_ROWS * ROW_SUB, LANES), jnp.float32),
        scratch_shapes=[pltpu.VMEM((2 * TILE_ROWS * ROW_SUB, LANES), jnp.float32),
                        pltpu.VMEM((2 * T, ROW_SUB, LANES), jnp.float32),
                        pltpu.SemaphoreType.DMA((2,)),
                        pltpu.SemaphoreType.DMA((2,))],
        compiler_params=pltpu.CompilerParams(
            dimension_semantics=("arbitrary",), vmem_limit_bytes=VMEM_LIMIT),
        name="moe_dispatch",
    )(lsrc, gdst, cnt, ld_t, xn)


def _expert_kernel(blk_ref, exp_ref, lo_ref, hi_ref, newexp_ref, newblk_ref, endblk_ref,
                   xs_hbm, wg_ref, wu_ref, wd_ref, ys_hbm,
                   xbuf_ref, ybuf_ref, wgub_ref, wdb_ref, in_sems, out_sems):
    i = pl.program_id(0)
    n_items = pl.num_programs(0)
    bf16 = jnp.bfloat16
    SB = EXPERT_SUB
    n_blocks = xs_hbm.shape[0] // EXPERT_BLOCK
    blk = blk_ref[i]
    lo = lo_ref[i]
    hi = hi_ref[i]

    class _Group:
        def __init__(self, copies):
            self.copies = copies

        def start(self):
            for cp in self.copies:
                cp.start()

        def wait(self):
            for cp in self.copies:
                cp.wait()

    def hbm_chunk(ref, b, c):
        start = b * EXPERT_BLOCK
        start = start if isinstance(start, int) else pl.multiple_of(start, EXPERT_BLOCK)
        return ref.at[pl.ds(start, EXPERT_BLOCK), c, :]

    def vmem_chunk(ref, slot, c):
        start = slot * EXPERT_BLOCK
        start = start if isinstance(start, int) else pl.multiple_of(start, EXPERT_BLOCK)
        return ref.at[pl.ds(start, EXPERT_BLOCK), pl.ds(c * LANES, LANES)]

    def in_copy(b, slot):
        return _Group([pltpu.make_async_copy(hbm_chunk(xs_hbm, b, c), vmem_chunk(xbuf_ref, slot, c), in_sems.at[slot])
                       for c in range(ROW_SUB)])

    def out_copy(b, slot):
        return _Group([pltpu.make_async_copy(vmem_chunk(ybuf_ref, slot, c), hbm_chunk(ys_hbm, b, c), out_sems.at[slot])
                       for c in range(ROW_SUB)])

    @pl.when(i == 0)
    def _():
        for b0 in range(EXPERT_IN_SLOTS - 1):
            in_copy(b0, b0).start()

    @pl.when(newblk_ref[i] == 1)
    def _():
        ahead = blk + (EXPERT_IN_SLOTS - 1)

        @pl.when(ahead < n_blocks)
        def _():
            in_copy(ahead, lax.rem(ahead, EXPERT_IN_SLOTS)).start()

        in_copy(blk, lax.rem(blk, EXPERT_IN_SLOTS)).wait()

        @pl.when(blk >= EXPERT_OUT_SLOTS)
        def _():
            out_copy(blk - EXPERT_OUT_SLOTS, lax.rem(blk, EXPERT_OUT_SLOTS)).wait()

    @pl.when(newexp_ref[i] == 1)
    def _():
        wgub_ref[:, 0:D_EXPERT] = wg_ref[0].astype(bf16)
        wgub_ref[:, D_EXPERT:2 * D_EXPERT] = wu_ref[0].astype(bf16)
        wdb_ref[...] = wd_ref[0].astype(bf16)

    xbase = pl.multiple_of(lax.rem(blk, EXPERT_IN_SLOTS) * EXPERT_BLOCK, EXPERT_BLOCK)
    ybase = pl.multiple_of(lax.rem(blk, EXPERT_OUT_SLOTS) * EXPERT_BLOCK, EXPERT_BLOCK)

    def run(r0, nrows, merge_first):
        xb = xbuf_ref[pl.ds(xbase + r0, nrows), :].astype(bf16)
        gu = _dot(xb, wgub_ref[...])
        g = gu[:, 0:D_EXPERT]
        u = gu[:, D_EXPERT:2 * D_EXPERT]
        h = (g * jax.nn.sigmoid(g) * u).astype(bf16)
        y = _dot(h, wdb_ref[...])
        first = SB if merge_first else 0
        if merge_first:
            rows = r0 + lax.broadcasted_iota(jnp.int32, (SB, D_MODEL), 0)
            keep = jnp.logical_and(rows >= lo, rows < hi)
            old = ybuf_ref[pl.ds(ybase + r0, SB), :]
            ybuf_ref[pl.ds(ybase + r0, SB), :] = jnp.where(keep, y[0:SB], old)
        if nrows > first:
            ybuf_ref[pl.ds(ybase + r0 + first, nrows - first), :] = y[first:nrows]

    n_sub = EXPERT_BLOCK // SB
    from_top = lo == 0
    sub_aligned = (lo & (SB - 1)) == 0
    to_end = jnp.logical_and(jnp.logical_and(lo > 0, hi == EXPERT_BLOCK), jnp.logical_not(sub_aligned))
    subs_used = (hi + SB - 1) // SB
    first_sub = lo // SB
    for k in range(1, n_sub + 1):
        @pl.when(jnp.logical_and(from_top, subs_used == k))
        def _(k=k):
            run(0, k * SB, False)

    for q in range(n_sub):
        @pl.when(jnp.logical_and(to_end, first_sub == q))
        def _(q=q):
            run(q * SB, EXPERT_BLOCK - q * SB, True)

    for q in range(n_sub):
        r0 = q * SB
        inside = jnp.logical_and(lo > 0, jnp.logical_or(hi < EXPERT_BLOCK, sub_aligned))

        @pl.when(jnp.logical_and(inside, jnp.minimum(hi, r0 + SB) > jnp.maximum(lo, r0)))
        def _(r0=r0):
            @pl.when(lo <= r0)
            def _():
                run(r0, SB, False)

            @pl.when(lo > r0)
            def _():
                run(r0, SB, True)

    @pl.when(endblk_ref[i] == 1)
    def _():
        out_copy(blk, lax.rem(blk, EXPERT_OUT_SLOTS)).start()

    @pl.when(i == n_items - 1)
    def _():
        for b1 in range(n_blocks - EXPERT_OUT_SLOTS, n_blocks):
            out_copy(b1, b1 % EXPERT_OUT_SLOTS).wait()


def _expert_call(item_blk, item_exp, item_lo, item_hi, item_newexp, item_newblk, item_endblk,
                 xs_rp, w_eg, w_eu, w_ed):
    D = D_MODEL
    BS = EXPERT_BLOCK * ROW_SUB
    n_items = item_blk.shape[0]
    assert xs_rp.shape[0] % BS == 0 and xs_rp.shape[0] // BS >= max(EXPERT_IN_SLOTS, EXPERT_OUT_SLOTS)
    xs3 = xs_rp.reshape(xs_rp.shape[0] // ROW_SUB, ROW_SUB, LANES)

    def w_map(i, blk, exp, lo, hi, newexp, newblk, endblk):
        return (exp[i], 0, 0)

    grid_spec = pltpu.PrefetchScalarGridSpec(
        num_scalar_prefetch=7,
        grid=(n_items,),
        in_specs=[
            pl.BlockSpec(memory_space=pl.ANY),
            pl.BlockSpec((1, D, D_EXPERT), w_map),
            pl.BlockSpec((1, D, D_EXPERT), w_map),
            pl.BlockSpec((1, D_EXPERT, D), w_map),
        ],
        out_specs=pl.BlockSpec(memory_space=pl.ANY),
        scratch_shapes=[
            pltpu.VMEM((EXPERT_IN_SLOTS * EXPERT_BLOCK, D), jnp.float32),
            pltpu.VMEM((EXPERT_OUT_SLOTS * EXPERT_BLOCK, D), jnp.float32),
            pltpu.VMEM((D, 2 * D_EXPERT), jnp.bfloat16),
            pltpu.VMEM((D_EXPERT, D), jnp.bfloat16),
            pltpu.SemaphoreType.DMA((EXPERT_IN_SLOTS,)),
            pltpu.SemaphoreType.DMA((EXPERT_OUT_SLOTS,)),
        ],
    )
    return pl.pallas_call(
        _expert_kernel,
        grid_spec=grid_spec,
        out_shape=jax.ShapeDtypeStruct(xs3.shape, jnp.float32),
        compiler_params=pltpu.CompilerParams(
            dimension_semantics=("arbitrary",), vmem_limit_bytes=VMEM_LIMIT),
        name="moe_experts",
    )(item_blk, item_exp, item_lo, item_hi, item_newexp, item_newblk, item_endblk, xs3, w_eg, w_eu, w_ed
      ).reshape(xs_rp.shape)


def _combine_kernel(lsrc_ref, gdst_ref, cnt_ref, ld_ref, w_ref, x2_hbm, ys_ref, out_hbm,
                    stag_ref, acc_ref, sems, in_sems, out_sems):
    b = pl.program_id(0)
    nb = pl.num_programs(0)
    T = MIX_TILE
    slot = b & 1
    base = slot * TILE_ROWS
    aslot = lax.rem(b, COMBINE_ACC_SLOTS)

    def in_copies(tile):
        s_ = lax.rem(tile, COMBINE_ACC_SLOTS)
        return _tile_relayout_copies(x2_hbm, tile, acc_ref, s_, in_sems.at[s_], True)

    def out_copies(tile):
        s_ = lax.rem(tile, COMBINE_ACC_SLOTS)
        return _tile_relayout_copies(out_hbm, tile, acc_ref, s_, out_sems.at[s_], False)

    @pl.when(b >= COMBINE_ACC_SLOTS - 1)
    def _():
        for cp in out_copies(b - (COMBINE_ACC_SLOTS - 1)):
            cp.wait()

    @pl.when(b == 0)
    def _():
        for cp in in_copies(b):
            cp.start()

    @pl.when(b + 1 < nb)
    def _():
        for cp in in_copies(b + 1):
            cp.start()

    def fetch(tile, slot_):
        def seg(e, c):
            s = tile * N_EXPERTS + e
            _segment_copies(ys_ref, gdst_ref[s], stag_ref, slot_ * TILE_ROWS + lsrc_ref[s], cnt_ref[s],
                            sems.at[slot_])
            return c

        lax.fori_loop(0, N_EXPERTS, seg, 0)

    @pl.when(b == 0)
    def _():
        fetch(b, slot)

    @pl.when(b + 1 < nb)
    def _():
        fetch(b + 1, 1 - slot)

    _wait_tile(ys_ref, stag_ref, base, sems.at[slot], False)
    for cp in in_copies(b):
        cp.wait()

    def tok(t, c):
        a = acc_ref[aslot * T + t]
        for kk_ in range(TOP_K):
            r8 = ld_ref[0, 0, kk_ * T + t]
            w = w_ref[0, 0, kk_ * T + t]
            a = a + w * stag_ref[pl.ds(pl.multiple_of(r8, ROW_SUB), ROW_SUB), :]
        acc_ref[aslot * T + t] = a
        return c

    lax.fori_loop(0, T, tok, 0, unroll=4)
    for cp in out_copies(b):
        cp.start()

    @pl.when(b == nb - 1)
    def _():
        for back in range(COMBINE_ACC_SLOTS - 2, -1, -1):
            @pl.when(b >= back)
            def _(back=back):
                for cp in out_copies(b - back):
                    cp.wait()


def _combine_call(lsrc, gdst, cnt, ld_t, w_t, x2, ys_rp):
    N, D = x2.shape
    nt = ld_t.shape[0]
    T = MIX_TILE
    smem = pl.BlockSpec(memory_space=pltpu.SMEM)
    smem_blk = pl.BlockSpec((1, 1, TILE_ROWS), lambda i: (i, 0, 0), memory_space=pltpu.SMEM)
    return pl.pallas_call(
        _combine_kernel,
        grid=(nt,),
        in_specs=[
            smem, smem, smem, smem_blk, smem_blk,
            pl.BlockSpec(memory_space=pl.ANY),
            pl.BlockSpec(memory_space=pl.ANY),
        ],
        out_specs=pl.BlockSpec(memory_space=pl.ANY),
        out_shape=jax.ShapeDtypeStruct((N, D), jnp.float32),
        scratch_shapes=[
            pltpu.VMEM((2 * TILE_ROWS * ROW_SUB, LANES), jnp.float32),
            pltpu.VMEM((COMBINE_ACC_SLOTS * T, ROW_SUB, LANES), jnp.float32),
            pltpu.SemaphoreType.DMA((2,)),
            pltpu.SemaphoreType.DMA((COMBINE_ACC_SLOTS,)),
            pltpu.SemaphoreType.DMA((COMBINE_ACC_SLOTS,)),
        ],
        compiler_params=pltpu.CompilerParams(
            dimension_semantics=("arbitrary",), vmem_limit_bytes=VMEM_LIMIT),
        name="moe_combine",
    )(lsrc, gdst, cnt, ld_t, w_t, x2, ys_rp)


def _work_items(gstart, totals, n_items):
    BM = EXPERT_BLOCK
    i32 = jnp.int32
    gend = gstart + totals
    first_blk = gstart // BM
    last_blk = jnp.maximum(gend - 1, gstart) // BM
    n_e = jnp.where(totals > 0, last_blk - first_blk + 1, 0)
    item_end = jnp.cumsum(n_e)
    item_start = item_end - n_e
    n_real = item_end[-1]
    j = jnp.minimum(jnp.arange(n_items, dtype=i32), n_real - 1)
    exp = jnp.sum((item_end[None, :] <= j[:, None]).astype(i32), axis=1)
    onehot = exp[:, None] == jnp.arange(N_EXPERTS, dtype=i32)[None, :]

    def of_exp(v):
        return jnp.sum(jnp.where(onehot, v[None, :], 0), axis=1)

    blk = of_exp(first_blk) + (j - of_exp(item_start))
    lo = jnp.maximum(of_exp(gstart), blk * BM) - blk * BM
    hi = jnp.minimum(of_exp(gend), (blk + 1) * BM) - blk * BM
    real = jnp.arange(n_items, dtype=i32) < n_real
    hi = jnp.where(real, hi, lo)
    prev_exp = jnp.concatenate([jnp.full((1,), -1, i32), exp[:-1]])
    prev_blk = jnp.concatenate([jnp.full((1,), -1, i32), blk[:-1]])
    next_blk = jnp.concatenate([blk[1:], jnp.full((1,), -1, i32)])
    is_last = jnp.arange(n_items, dtype=i32) == n_real - 1
    newexp = jnp.logical_and(real, exp != prev_exp).astype(i32)
    newblk = jnp.logical_and(real, blk != prev_blk).astype(i32)
    endblk = jnp.logical_and(real, jnp.logical_or(blk != next_blk, is_last)).astype(i32)
    return blk.astype(i32), exp.astype(i32), lo.astype(i32), hi.astype(i32), newexp, newblk, endblk


def _layer(x, mix_norm_w, w_in, conv_w, q_norm_w, k_norm_w, sinks, conv_out_norm_w,
           attn_out_norm_w, w_out, ffn_norm_w, w_router, router_bias, w_eg, w_eu, w_ed,
           w_sg, w_su, w_sd):
    B, S, D = x.shape
    N = B * S
    T = MIX_TILE
    nt = N // T
    bf16 = jnp.bfloat16
    f32 = jnp.float32
    i32 = jnp.int32

    wr_t = w_router.astype(f32).T
    wr_hi = wr_t.astype(bf16)
    wr_hl = jnp.concatenate([wr_hi, (wr_t - wr_hi.astype(f32)).astype(bf16)], axis=0)
    sink_col = jnp.broadcast_to(
        jnp.repeat(sinks.astype(f32), BLK).reshape(N_KV_HEADS, GQA_GROUP * BLK, 1),
        (N_KV_HEADS, GQA_GROUP * BLK, LANES))
    tri = jnp.asarray(np.triu(np.ones((T, T), np.float32), k=1), dtype=bf16)
    head_ones = jnp.asarray(np.kron(np.eye(2 * LANES // HEAD_DIM, dtype=np.float32),
                                    np.ones((HEAD_DIM, HEAD_DIM), np.float32)), dtype=bf16)
    bias_tab = jnp.asarray(_attn_bias_table())

    x2, xn, wts, ld, cnt = _mixer_call(
        x, mix_norm_w.reshape(1, D), w_in.astype(bf16), conv_w,
        jnp.tile(q_norm_w, N_HEADS).reshape(1, D_ATTN), jnp.tile(k_norm_w, N_KV_HEADS).reshape(1, D_KV),
        sink_col, bias_tab, conv_out_norm_w.reshape(1, D_CONV), attn_out_norm_w.reshape(1, D_ATTN),
        w_out.astype(bf16), ffn_norm_w.reshape(1, D), w_sg.astype(bf16), w_su.astype(bf16),
        w_sd.astype(bf16), wr_hl, router_bias.astype(f32).reshape(N_EXPERTS, 1), tri, head_ones)

    cnt_te = cnt[:, 0].reshape(nt, N_EXPERTS)
    lsrc = jnp.cumsum(cnt_te, axis=1) - cnt_te
    before = jnp.cumsum(cnt_te, axis=0) - cnt_te
    totals = jnp.sum(cnt_te, axis=0)
    gstart = jnp.cumsum(totals) - totals
    gdst = gstart[None, :] + before
    n_items = (N * TOP_K) // EXPERT_BLOCK + N_EXPERTS
    items = _work_items(gstart.astype(i32), totals.astype(i32), n_items)

    def tile_major(a):
        return a.reshape(TOP_K, nt, T).transpose(1, 0, 2).reshape(nt, 1, TILE_ROWS)

    ld_t = tile_major(ld)
    seg_tabs = (lsrc.reshape(-1).astype(i32), gdst.reshape(-1).astype(i32), cnt_te.reshape(-1).astype(i32))
    xs_rp = _dispatch_call(*seg_tabs, ld_t, xn)
    ys_rp = _expert_call(*items, xs_rp, w_eg, w_eu, w_ed)
    out = _combine_call(*seg_tabs, ld_t, tile_major(wts), x2, ys_rp)
    return out.reshape(B, S, D)


def kernel(x, mix_norm_w, w_in, conv_w, q_norm_w, k_norm_w, sinks, conv_out_norm_w, attn_out_norm_w, w_out, ffn_norm_w, w_router, router_bias, w_exp_gate, w_exp_up, w_exp_down, w_sh_gate, w_sh_up, w_sh_down):
    B, S, D = x.shape
    assert D == D_MODEL and w_in.shape == (1, D_MODEL, D_IN_PROJ) and w_exp_gate.shape == (1, N_EXPERTS, D_MODEL, D_EXPERT)
    assert S % MIX_TILE == 0 and (B * S * TOP_K) % EXPERT_BLOCK == 0
    return _layer(x, mix_norm_w[0], w_in[0], conv_w[0], q_norm_w[0], k_norm_w[0], sinks[0],
                  conv_out_norm_w[0], attn_out_norm_w[0], w_out[0], ffn_norm_w[0], w_router[0],
                  router_bias[0], w_exp_gate[0], w_exp_up[0], w_exp_down[0], w_sh_gate[0],
                  w_sh_up[0], w_sh_down[0])
```

```python
import numpy as np
import jax
import jax.numpy as jnp
from jax import lax
from jax.experimental import pallas as pl
from jax.experimental.pallas import tpu as pltpu

D_MODEL = 1024
EPS = 1e-6
D_CONV = 512
HEAD_DIM = 64
N_HEADS = 8
N_KV_HEADS = 2
GQA_GROUP = 4
D_ATTN = 512
D_KV = 128
WINDOW = 128
BLK = 128
D_IN_PROJ = 2304
N_EXPERTS = 64
TOP_K = 8
N_GROUPS = 8
GROUP_SIZE = 8
TOPK_GROUPS = 4
D_EXPERT = 256
ROUTED_SCALE = 2.5

LANES = 128
ROW_SUB = 8
MIX_TILE = 512
TILE_ROWS = TOP_K * MIX_TILE
EXPERT_BLOCK = 1024
EXPERT_SUB = 256
EXPERT_IN_SLOTS = 4
EXPERT_OUT_SLOTS = 3
COMBINE_ACC_SLOTS = 3
SEG_CHUNK_LOG2 = 5
NEG_BIG = -1e30
VMEM_LIMIT = 58 * 1024 * 1024


def _alibi_slopes():
    return np.array([2.0 ** (-8.0 * (h + 1) / N_HEADS) for h in range(N_HEADS)], dtype=np.float32)


def _attn_bias_table():
    qi = np.arange(BLK)[:, None]
    kj = np.arange(2 * BLK)[None, :]
    dist = qi - kj + BLK
    inwin = (dist >= 0) & (dist < WINDOW)
    slopes = _alibi_slopes()
    out = np.zeros((N_KV_HEADS, GQA_GROUP * BLK, 2 * BLK), np.float32)
    for g in range(N_KV_HEADS):
        for i in range(GQA_GROUP):
            h = g * GQA_GROUP + i
            out[g, i * BLK:(i + 1) * BLK] = np.where(inwin, -slopes[h] * dist.astype(np.float32), NEG_BIG)
    return out


def _rms(x, w):
    ms = jnp.mean(x * x, axis=-1, keepdims=True)
    return x * lax.rsqrt(ms + EPS) * w


def _head_rms(x, w_tiled, head_ones):
    C = x.shape[1]
    W = min(C, 2 * LANES)
    sq = x * x
    hi = sq.astype(jnp.bfloat16)
    lo = (sq - hi.astype(jnp.float32)).astype(jnp.bfloat16)
    ones = head_ones[0:W, 0:W]
    ssq = jnp.concatenate(
        [_dot(hi[:, c:c + W], ones) + _dot(lo[:, c:c + W], ones) for c in range(0, C, W)], axis=1)
    return x * lax.rsqrt(ssq * (1.0 / HEAD_DIM) + EPS) * w_tiled


def _dot(a, b):
    return jnp.dot(a, b, preferred_element_type=jnp.float32)


def _dot_nt(a, b):
    return lax.dot_general(a, b, (((1,), (1,)), ((), ())), preferred_element_type=jnp.float32)


def _mixer_kernel(x_ref, mixw_ref, win_ref, convw_ref, qw_ref, kw_ref, sink_ref, bias_ref,
                  cnw_ref, anw_ref, wout_ref, fnw_ref, wsg_ref, wsu_ref, wsd_ref,
                  wrhl_ref, rbias_ref, tri_ref, hones_ref,
                  x2_ref, xn_ref, wts_ref, ld_ref, cnt_ref,
                  kc_ref, vc_ref, cc_ref):
    j = pl.program_id(1)
    T = MIX_TILE
    bf16 = jnp.bfloat16
    f32 = jnp.float32

    @pl.when(j == 0)
    def _():
        kc_ref[...] = jnp.zeros_like(kc_ref)
        vc_ref[...] = jnp.zeros_like(vc_ref)
        cc_ref[...] = jnp.zeros_like(cc_ref)

    x = x_ref[0]
    u = _rms(x, mixw_ref[...]).astype(bf16)
    qkv = _dot(u, win_ref[:, 3 * D_CONV:D_IN_PROJ])
    q = qkv[:, 0:D_ATTN]
    k = qkv[:, D_ATTN:D_ATTN + D_KV]
    v = qkv[:, D_ATTN + D_KV:D_ATTN + 2 * D_KV]
    conv_piece = 2 * LANES
    conv_pieces = []

    lane = lax.broadcasted_iota(jnp.int32, (1, LANES), 1)
    lane_lo = lane < HEAD_DIM
    qn = _head_rms(q, qw_ref[...], hones_ref) * (HEAD_DIM ** -0.5)
    kn = _head_rms(k, kw_ref[...], hones_ref)
    kfull = jnp.concatenate([kc_ref[...], kn], axis=0)
    vfull = jnp.concatenate([vc_ref[...], v], axis=0)
    kc_ref[...] = kn[T - BLK:T, :]
    vc_ref[...] = v[T - BLK:T, :]

    def _rep(a, g):
        r = pltpu.roll(a, HEAD_DIM, axis=1)
        two = jnp.where(lane_lo, a, r) if g == 0 else jnp.where(lane_lo, r, a)
        return jnp.concatenate([two, two], axis=1).astype(bf16)

    k_rep = [_rep(kfull, g) for g in range(N_KV_HEADS)]
    v_rep = [_rep(vfull, g) for g in range(N_KV_HEADS)]

    lane256 = lax.broadcasted_iota(jnp.int32, (1, 2 * LANES), 1)
    head_of_lane = lane256 // HEAD_DIM
    first_f = (j == 0).astype(f32)
    prev_key_mask = jnp.where(lane256 < BLK, first_f * NEG_BIG, 0.0)
    ones_cols = jnp.ones((2 * BLK, LANES), bf16)

    def lanes2(a):
        return jnp.concatenate([a, a], axis=1)

    attn_rows = []
    for i in range(T // BLK):
        grp_out = []
        for g in range(N_KV_HEADS):
            if len(conv_pieces) * conv_piece < 3 * D_CONV:
                c0 = len(conv_pieces) * conv_piece
                conv_pieces.append(_dot(u, win_ref[:, c0:c0 + conv_piece]))
            qg = qn[i * BLK:(i + 1) * BLK, g * 256:(g + 1) * 256]
            qst = jnp.concatenate(
                [jnp.where(head_of_lane == hi, qg, 0.0) for hi in range(GQA_GROUP)], axis=0).astype(bf16)
            kk = k_rep[g][i * BLK:i * BLK + 2 * BLK, :]
            vv = jnp.concatenate([v_rep[g][i * BLK:i * BLK + 2 * BLK, :], ones_cols], axis=1)
            s = _dot_nt(qst, kk) + bias_ref[g]
            if i == 0:
                s = s + prev_key_mask
            sink = sink_ref[g]
            m = jnp.maximum(jnp.max(s, axis=-1, keepdims=True), sink)
            e = jnp.exp(s - lanes2(m)).astype(bf16)
            r = _dot(e, vv)
            inv = 1.0 / (r[:, 2 * LANES:3 * LANES] + jnp.exp(sink - m))
            r = r[:, 0:2 * LANES] * lanes2(inv)
            o = jnp.where(head_of_lane == 0, r[0:BLK], 0.0)
            for hi in range(1, GQA_GROUP):
                o = jnp.where(head_of_lane == hi, r[hi * BLK:(hi + 1) * BLK], o)
            grp_out.append(o)
        attn_rows.append(jnp.concatenate(grp_out, axis=1))
    y_attn = jnp.concatenate(attn_rows, axis=0)

    assert len(conv_pieces) * conv_piece == 3 * D_CONV
    conv_proj = jnp.concatenate(conv_pieces, axis=1)
    b_gate = conv_proj[:, 0:D_CONV]
    c_gate = conv_proj[:, D_CONV:2 * D_CONV]
    hh = conv_proj[:, 2 * D_CONV:3 * D_CONV]
    ch = c_gate * hh
    prev = cc_ref[...]
    p6 = prev[6:7, :]
    p7 = prev[7:8, :]
    row = lax.broadcasted_iota(jnp.int32, (T, D_CONV), 0)
    ch_m1 = jnp.where(row == 0, p7, pltpu.roll(ch, 1, axis=0))
    ch_m2 = jnp.where(row == 0, p6, jnp.where(row == 1, p7, pltpu.roll(ch, 2, axis=0)))
    cw = convw_ref[...]
    y_conv = b_gate * (cw[0:1, :] * ch_m2 + cw[1:2, :] * ch_m1 + cw[2:3, :] * ch)
    cc_ref[...] = ch[T - 8:T, :]

    y_mix = jnp.concatenate([_rms(y_conv, cnw_ref[...]), _rms(y_attn, anw_ref[...])], axis=1)
    x1 = x + _dot(y_mix.astype(bf16), wout_ref[...])

    xn = _rms(x1, fnw_ref[...])
    xn_ref[...] = xn
    xh = xn.astype(bf16)

    xl = (xn - xh.astype(f32)).astype(bf16)
    w_hl = wrhl_ref[...]
    a_hl = _dot_nt(w_hl, xh)
    logits = a_hl[0:N_EXPERTS] + a_hl[N_EXPERTS:2 * N_EXPERTS] + _dot_nt(w_hl[0:N_EXPERTS], xl)

    gs = _dot(xh, wsg_ref[...])
    us = _dot(xh, wsu_ref[...])
    hs = (gs * jax.nn.sigmoid(gs) * us).astype(bf16)
    x2 = x1 + _dot(hs, wsd_ref[...])
    x2_ref[...] = x2

    scores = jax.nn.sigmoid(logits)
    biased = scores + rbias_ref[...]

    sub8 = lax.broadcasted_iota(jnp.int32, (GROUP_SIZE, T), 0)
    gscore = []
    for g in range(N_GROUPS):
        blk = biased[g * GROUP_SIZE:(g + 1) * GROUP_SIZE, :]
        m1 = jnp.max(blk, axis=0, keepdims=True)
        first = jnp.min(jnp.where(blk == m1, sub8, GROUP_SIZE), axis=0, keepdims=True)
        m2 = jnp.max(jnp.where(sub8 == first, -jnp.inf, blk), axis=0, keepdims=True)
        gscore.append(m1 + m2)
    masked_blocks = []
    for g in range(N_GROUPS):
        rank = jnp.zeros((1, T), jnp.int32)
        for o_ in range(N_GROUPS):
            if o_ == g:
                continue
            if o_ < g:
                ahead = gscore[o_] >= gscore[g]
            else:
                ahead = gscore[o_] > gscore[g]
            rank = rank + ahead.astype(jnp.int32)
        keep = rank < TOPK_GROUPS
        blk = biased[g * GROUP_SIZE:(g + 1) * GROUP_SIZE, :]
        masked_blocks.append(jnp.where(keep, blk, -jnp.inf))
    cur = jnp.concatenate(masked_blocks, axis=0)

    eiota = lax.broadcasted_iota(jnp.int32, (N_EXPERTS, T), 0)
    row8 = lax.broadcasted_iota(jnp.int32, (TOP_K, T), 0)
    w_out = jnp.zeros((TOP_K, T), f32)
    sel_dense = jnp.zeros((N_EXPERTS, T), f32)
    onehots = []
    for kk_ in range(TOP_K):
        mx = jnp.max(cur, axis=0, keepdims=True)
        sel_idx = jnp.min(jnp.where(cur == mx, eiota, N_EXPERTS), axis=0, keepdims=True)
        onehot = eiota == sel_idx
        w_k = jnp.sum(jnp.where(onehot, scores, 0.0), axis=0, keepdims=True)
        cur = jnp.where(onehot, -jnp.inf, cur)
        sel_dense = jnp.where(onehot, 1.0, sel_dense)
        w_out = jnp.where(row8 == kk_, w_k, w_out)
        onehots.append(onehot)
    wsum = jnp.sum(w_out, axis=0, keepdims=True)
    wts_ref[...] = w_out / wsum * ROUTED_SCALE

    cum = _dot(sel_dense.astype(bf16), tri_ref[...])
    cnt = jnp.broadcast_to(jnp.sum(sel_dense, axis=1, keepdims=True), (N_EXPERTS, LANES))
    erow = lax.broadcasted_iota(jnp.int32, (N_EXPERTS, LANES), 0)
    incl = cnt
    step = 1
    while step < N_EXPERTS:
        incl = incl + jnp.where(erow >= step, pltpu.roll(incl, step, axis=0), 0.0)
        step *= 2
    lstart = incl - cnt
    ld_dense = lstart[:, 0:1] + cum
    ld_out = jnp.zeros((TOP_K, T), f32)
    for kk_ in range(TOP_K):
        p_k = jnp.sum(jnp.where(onehots[kk_], ld_dense, 0.0), axis=0, keepdims=True)
        ld_out = jnp.where(row8 == kk_, p_k, ld_out)
    tile_idx = pl.program_id(0) * pl.num_programs(1) + j
    slot_base = ((tile_idx & 1) * TILE_ROWS).astype(f32)
    ld_ref[...] = ((ld_out + slot_base) * ROW_SUB).astype(jnp.int32)
    cnt_ref[...] = cnt.astype(jnp.int32)


def _mixer_call(x, mixw, win, convw, qw, kw, sink_col, bias_tab, cnw, anw, wout, fnw,
                wsg, wsu, wsd, wrhl, rbias, tri, head_ones):
    B, S, D = x.shape
    T = MIX_TILE
    nt = S // T
    N = B * S

    def full(a):
        nd = a.ndim
        return pl.BlockSpec(a.shape, lambda b, j, _nd=nd: (0,) * _nd)

    tok_spec = pl.BlockSpec((TOP_K, T), lambda b, j: (0, b * nt + j))
    in_arrays = [mixw, win, convw, qw, kw, sink_col, bias_tab, cnw, anw, wout, fnw,
                 wsg, wsu, wsd, wrhl, rbias, tri, head_ones]
    return pl.pallas_call(
        _mixer_kernel,
        grid=(B, nt),
        in_specs=[pl.BlockSpec((1, T, D), lambda b, j: (b, j, 0))] + [full(a) for a in in_arrays],
        out_specs=[
            pl.BlockSpec((T, D), lambda b, j: (b * nt + j, 0)),
            pl.BlockSpec((T, D), lambda b, j: (b * nt + j, 0)),
            tok_spec, tok_spec,
            pl.BlockSpec((N_EXPERTS, LANES), lambda b, j: (b * nt + j, 0)),
        ],
        out_shape=[
            jax.ShapeDtypeStruct((N, D), jnp.float32),
            jax.ShapeDtypeStruct((N, D), jnp.float32),
            jax.ShapeDtypeStruct((TOP_K, N), jnp.float32),
            jax.ShapeDtypeStruct((TOP_K, N), jnp.int32),
            jax.ShapeDtypeStruct((B * nt * N_EXPERTS, LANES), jnp.int32),
        ],
        scratch_shapes=[
            pltpu.VMEM((BLK, D_KV), jnp.float32),
            pltpu.VMEM((BLK, D_KV), jnp.float32),
            pltpu.VMEM((8, D_CONV), jnp.float32),
        ],
        compiler_params=pltpu.CompilerParams(
            dimension_semantics=("arbitrary", "arbitrary"), vmem_limit_bytes=VMEM_LIMIT),
        name="mixer_router",
    )(x, *in_arrays)


def _rows(ref, row, nrows):
    return ref.at[pl.ds(pl.multiple_of(row * ROW_SUB, ROW_SUB), nrows * ROW_SUB)]


def _segment_copies(src, src_row, dst, dst_row, count, sem):
    chunk = 1 << SEG_CHUNK_LOG2

    def big(i, c):
        o = i * chunk
        pltpu.make_async_copy(_rows(src, src_row + o, chunk), _rows(dst, dst_row + o, chunk), sem).start()
        return c

    lax.fori_loop(0, count >> SEG_CHUNK_LOG2, big, 0)
    for bit in range(SEG_CHUNK_LOG2 - 1, -1, -1):
        o = (count >> (bit + 1)) << (bit + 1)

        @pl.when(((count >> bit) & 1) == 1)
        def _():
            n = 1 << bit
            pltpu.make_async_copy(_rows(src, src_row + o, n), _rows(dst, dst_row + o, n), sem).start()


def _wait_tile(hbm_ref, stag_ref, slot_row, sem, to_hbm):
    vm = _rows(stag_ref, slot_row, TILE_ROWS)
    hb = _rows(hbm_ref, 0, TILE_ROWS)
    (pltpu.make_async_copy(vm, hb, sem) if to_hbm else pltpu.make_async_copy(hb, vm, sem)).wait()


def _tile_relayout_copies(std_hbm, tile, rpt_ref, slot, sem, to_vmem):
    T = MIX_TILE
    row0 = tile * T
    row0 = row0 if isinstance(row0, int) else pl.multiple_of(row0, T)
    slot0 = slot * T
    slot0 = slot0 if isinstance(slot0, int) else pl.multiple_of(slot0, T)
    copies = []
    for c in range(ROW_SUB):
        hb = std_hbm.at[pl.ds(row0, T), pl.ds(c * LANES, LANES)]
        vm = rpt_ref.at[pl.ds(slot0, T), c, :]
        copies.append(pltpu.make_async_copy(hb, vm, sem) if to_vmem else pltpu.make_async_copy(vm, hb, sem))
    return copies


def _dispatch_kernel(lsrc_ref, gdst_ref, cnt_ref, ld_ref, xn_hbm, xs_ref, stag_ref, xin_ref, sems, in_sems):
    b = pl.program_id(0)
    nb = pl.num_programs(0)
    T = MIX_TILE
    slot = b & 1
    base = slot * TILE_ROWS

    def fetch_rows(tile, slot_):
        return _tile_relayout_copies(xn_hbm, tile, xin_ref, slot_, in_sems.at[slot_], True)

    @pl.when(b == 0)
    def _():
        for cp in fetch_rows(b, slot):
            cp.start()

    @pl.when(b + 1 < nb)
    def _():
        for cp in fetch_rows(b + 1, 1 - slot):
            cp.start()

    for cp in fetch_rows(b, slot):
        cp.wait()

    for kk_ in range(TOP_K):
        def row(t, c, kk_=kk_):
            r8 = ld_ref[0, 0, kk_ * T + t]
            stag_ref[pl.ds(pl.multiple_of(r8, ROW_SUB), ROW_SUB), :] = xin_ref[slot * T + t]
            return c

        lax.fori_loop(0, T, row, 0, unroll=32)

    def seg(e, c):
        s = b * N_EXPERTS + e
        _segment_copies(stag_ref, base + lsrc_ref[s], xs_ref, gdst_ref[s], cnt_ref[s], sems.at[slot])
        return c

    lax.fori_loop(0, N_EXPERTS, seg, 0)

    @pl.when(b > 0)
    def _():
        _wait_tile(xs_ref, stag_ref, (1 - slot) * TILE_ROWS, sems.at[1 - slot], True)

    @pl.when(b == nb - 1)
    def _():
        _wait_tile(xs_ref, stag_ref, base, sems.at[slot], True)


def _dispatch_call(lsrc, gdst, cnt, ld_t, xn):
    nt = ld_t.shape[0]
    T = MIX_TILE
    smem = pl.BlockSpec(memory_space=pltpu.SMEM)
    return pl.pallas_call(
        _dispatch_kernel,
        grid=(nt,),
        in_specs=[
            smem, smem, smem,
            pl.BlockSpec((1, 1, TILE_ROWS), lambda i: (i, 0, 0), memory_space=pltpu.SMEM),
            pl.BlockSpec(memory_space=pl.ANY),
        ],
        out_specs=pl.BlockSpec(memory_space=pl.ANY),
        out_shape=jax.ShapeDtypeStruct((nt * TILE_ROWS * ROW_SUB, LANES), jnp.float32),
        scratch_shapes=[pltpu.VMEM((2 * TILE_ROWS * ROW_SUB, LANES), jnp.float32),
                        pltpu.VMEM((2 * T, ROW_SUB, LANES), jnp.float32),
                        pltpu.SemaphoreType.DMA((2,)),
                        pltpu.SemaphoreType.DMA((2,))],
        compiler_params=pltpu.CompilerParams(
            dimension_semantics=("arbitrary",), vmem_limit_bytes=VMEM_LIMIT),
        name="moe_dispatch",
    )(lsrc, gdst, cnt, ld_t, xn)


def _expert_kernel(blk_ref, exp_ref, lo_ref, hi_ref, newexp_ref, newblk_ref, endblk_ref,
                   xs_hbm, wg_ref, wu_ref, wd_ref, ys_hbm,
                   xbuf_ref, ybuf_ref, wgub_ref, wdb_ref, in_sems, out_sems):
    i = pl.program_id(0)
    n_items = pl.num_programs(0)
    bf16 = jnp.bfloat16
    SB = EXPERT_SUB
    n_blocks = xs_hbm.shape[0] // EXPERT_BLOCK
    blk = blk_ref[i]
    lo = lo_ref[i]
    hi = hi_ref[i]

    class _Group:
        def __init__(self, copies):
            self.copies = copies

        def start(self):
            for cp in self.copies:
                cp.start()

        def wait(self):
            for cp in self.copies:
                cp.wait()

    def hbm_chunk(ref, b, c):
        start = b * EXPERT_BLOCK
        start = start if isinstance(start, int) else pl.multiple_of(start, EXPERT_BLOCK)
        return ref.at[pl.ds(start, EXPERT_BLOCK), c, :]

    def vmem_chunk(ref, slot, c):
        start = slot * EXPERT_BLOCK
        start = start if isinstance(start, int) else pl.multiple_of(start, EXPERT_BLOCK)
        return ref.at[pl.ds(start, EXPERT_BLOCK), pl.ds(c * LANES, LANES)]

    def in_copy(b, slot):
        return _Group([pltpu.make_async_copy(hbm_chunk(xs_hbm, b, c), vmem_chunk(xbuf_ref, slot, c), in_sems.at[slot])
                       for c in range(ROW_SUB)])

    def out_copy(b, slot):
        return _Group([pltpu.make_async_copy(vmem_chunk(ybuf_ref, slot, c), hbm_chunk(ys_hbm, b, c), out_sems.at[slot])
                       for c in range(ROW_SUB)])

    @pl.when(i == 0)
    def _():
        for b0 in range(EXPERT_IN_SLOTS - 1):
            in_copy(b0, b0).start()

    @pl.when(newblk_ref[i] == 1)
    def _():
        ahead = blk + (EXPERT_IN_SLOTS - 1)

        @pl.when(ahead < n_blocks)
        def _():
            in_copy(ahead, lax.rem(ahead, EXPERT_IN_SLOTS)).start()

        in_copy(blk, lax.rem(blk, EXPERT_IN_SLOTS)).wait()

        @pl.when(blk >= EXPERT_OUT_SLOTS)
        def _():
            out_copy(blk - EXPERT_OUT_SLOTS, lax.rem(blk, EXPERT_OUT_SLOTS)).wait()

    @pl.when(newexp_ref[i] == 1)
    def _():
        wgub_ref[:, 0:D_EXPERT] = wg_ref[0].astype(bf16)
        wgub_ref[:, D_EXPERT:2 * D_EXPERT] = wu_ref[0].astype(bf16)
        wdb_ref[...] = wd_ref[0].astype(bf16)

    xbase = pl.multiple_of(lax.rem(blk, EXPERT_IN_SLOTS) * EXPERT_BLOCK, EXPERT_BLOCK)
    ybase = pl.multiple_of(lax.rem(blk, EXPERT_OUT_SLOTS) * EXPERT_BLOCK, EXPERT_BLOCK)

    def run(r0, nrows, merge_first):
        xb = xbuf_ref[pl.ds(xbase + r0, nrows), :].astype(bf16)
        gu = _dot(xb, wgub_ref[...])
        g = gu[:, 0:D_EXPERT]
        u = gu[:, D_EXPERT:2 * D_EXPERT]
        h = (g * jax.nn.sigmoid(g) * u).astype(bf16)
        y = _dot(h, wdb_ref[...])
        first = SB if merge_first else 0
        if merge_first:
            rows = r0 + lax.broadcasted_iota(jnp.int32, (SB, D_MODEL), 0)
            keep = jnp.logical_and(rows >= lo, rows < hi)
            old = ybuf_ref[pl.ds(ybase + r0, SB), :]
            ybuf_ref[pl.ds(ybase + r0, SB), :] = jnp.where(keep, y[0:SB], old)
        if nrows > first:
            ybuf_ref[pl.ds(ybase + r0 + first, nrows - first), :] = y[first:nrows]

    n_sub = EXPERT_BLOCK // SB
    from_top = lo == 0
    sub_aligned = (lo & (SB - 1)) == 0
    to_end = jnp.logical_and(jnp.logical_and(lo > 0, hi == EXPERT_BLOCK), jnp.logical_not(sub_aligned))
    subs_used = (hi + SB - 1) // SB
    first_sub = lo // SB
    for k in range(1, n_sub + 1):
        @pl.when(jnp.logical_and(from_top, subs_used == k))
        def _(k=k):
            run(0, k * SB, False)

    for q in range(n_sub):
        @pl.when(jnp.logical_and(to_end, first_sub == q))
        def _(q=q):
            run(q * SB, EXPERT_BLOCK - q * SB, True)

    for q in range(n_sub):
        r0 = q * SB
        inside = jnp.logical_and(lo > 0, jnp.logical_or(hi < EXPERT_BLOCK, sub_aligned))

        @pl.when(jnp.logical_and(inside, jnp.minimum(hi, r0 + SB) > jnp.maximum(lo, r0)))
        def _(r0=r0):
            @pl.when(lo <= r0)
            def _():
                run(r0, SB, False)

            @pl.when(lo > r0)
            def _():
                run(r0, SB, True)

    @pl.when(endblk_ref[i] == 1)
    def _():
        out_copy(blk, lax.rem(blk, EXPERT_OUT_SLOTS)).start()

    @pl.when(i == n_items - 1)
    def _():
        for b1 in range(n_blocks - EXPERT_OUT_SLOTS, n_blocks):
            out_copy(b1, b1 % EXPERT_OUT_SLOTS).wait()


def _expert_call(item_blk, item_exp, item_lo, item_hi, item_newexp, item_newblk, item_endblk,
                 xs_rp, w_eg, w_eu, w_ed):
    D = D_MODEL
    BS = EXPERT_BLOCK * ROW_SUB
    n_items = item_blk.shape[0]
    assert xs_rp.shape[0] % BS == 0 and xs_rp.shape[0] // BS >= max(EXPERT_IN_SLOTS, EXPERT_OUT_SLOTS)
    xs3 = xs_rp.reshape(xs_rp.shape[0] // ROW_SUB, ROW_SUB, LANES)

    def w_map(i, blk, exp, lo, hi, newexp, newblk, endblk):
        return (exp[i], 0, 0)

    grid_spec = pltpu.PrefetchScalarGridSpec(
        num_scalar_prefetch=7,
        grid=(n_items,),
        in_specs=[
            pl.BlockSpec(memory_space=pl.ANY),
            pl.BlockSpec((1, D, D_EXPERT), w_map),
            pl.BlockSpec((1, D, D_EXPERT), w_map),
            pl.BlockSpec((1, D_EXPERT, D), w_map),
        ],
        out_specs=pl.BlockSpec(memory_space=pl.ANY),
        scratch_shapes=[
            pltpu.VMEM((EXPERT_IN_SLOTS * EXPERT_BLOCK, D), jnp.float32),
            pltpu.VMEM((EXPERT_OUT_SLOTS * EXPERT_BLOCK, D), jnp.float32),
            pltpu.VMEM((D, 2 * D_EXPERT), jnp.bfloat16),
            pltpu.VMEM((D_EXPERT, D), jnp.bfloat16),
            pltpu.SemaphoreType.DMA((EXPERT_IN_SLOTS,)),
            pltpu.SemaphoreType.DMA((EXPERT_OUT_SLOTS,)),
        ],
    )
    return pl.pallas_call(
        _expert_kernel,
        grid_spec=grid_spec,
        out_shape=jax.ShapeDtypeStruct(xs3.shape, jnp.float32),
        compiler_params=pltpu.CompilerParams(
            dimension_semantics=("arbitrary",), vmem_limit_bytes=VMEM_LIMIT),
        name="moe_experts",
    )(item_blk, item_exp, item_lo, item_hi, item_newexp, item_newblk, item_endblk, xs3, w_eg, w_eu, w_ed
      ).reshape(xs_rp.shape)


def _combine_kernel(lsrc_ref, gdst_ref, cnt_ref, ld_ref, w_ref, x2_hbm, ys_ref, out_hbm,
                    stag_ref, acc_ref, sems, in_sems, out_sems):
    b = pl.program_id(0)
    nb = pl.num_programs(0)
    T = MIX_TILE
    slot = b & 1
    base = slot * TILE_ROWS
    aslot = lax.rem(b, COMBINE_ACC_SLOTS)

    def in_copies(tile):
        s_ = lax.rem(tile, COMBINE_ACC_SLOTS)
        return _tile_relayout_copies(x2_hbm, tile, acc_ref, s_, in_sems.at[s_], True)

    def out_copies(tile):
        s_ = lax.rem(tile, COMBINE_ACC_SLOTS)
        return _tile_relayout_copies(out_hbm, tile, acc_ref, s_, out_sems.at[s_], False)

    @pl.when(b >= COMBINE_ACC_SLOTS - 1)
    def _():
        for cp in out_copies(b - (COMBINE_ACC_SLOTS - 1)):
            cp.wait()

    @pl.when(b == 0)
    def _():
        for cp in in_copies(b):
            cp.start()

    @pl.when(b + 1 < nb)
    def _():
        for cp in in_copies(b + 1):
            cp.start()

    def fetch(tile, slot_):
        def seg(e, c):
            s = tile * N_EXPERTS + e
            _segment_copies(ys_ref, gdst_ref[s], stag_ref, slot_ * TILE_ROWS + lsrc_ref[s], cnt_ref[s],
                            sems.at[slot_])
            return c

        lax.fori_loop(0, N_EXPERTS, seg, 0)

    @pl.when(b == 0)
    def _():
        fetch(b, slot)

    @pl.when(b + 1 < nb)
    def _():
        fetch(b + 1, 1 - slot)

    _wait_tile(ys_ref, stag_ref, base, sems.at[slot], False)
    for cp in in_copies(b):
        cp.wait()

    def tok(t, c):
        a = acc_ref[aslot * T + t]
        for kk_ in range(TOP_K):
            r8 = ld_ref[0, 0, kk_ * T + t]
            w = w_ref[0, 0, kk_ * T + t]
            a = a + w * stag_ref[pl.ds(pl.multiple_of(r8, ROW_SUB), ROW_SUB), :]
        acc_ref[aslot * T + t] = a
        return c

    lax.fori_loop(0, T, tok, 0, unroll=8)
    for cp in out_copies(b):
        cp.start()

    @pl.when(b == nb - 1)
    def _():
        for back in range(COMBINE_ACC_SLOTS - 2, -1, -1):
            @pl.when(b >= back)
            def _(back=back):
                for cp in out_copies(b - back):
                    cp.wait()


def _combine_call(lsrc, gdst, cnt, ld_t, w_t, x2, ys_rp):
    N, D = x2.shape
    nt = ld_t.shape[0]
    T = MIX_TILE
    smem = pl.BlockSpec(memory_space=pltpu.SMEM)
    smem_blk = pl.BlockSpec((1, 1, TILE_ROWS), lambda i: (i, 0, 0), memory_space=pltpu.SMEM)
    return pl.pallas_call(
        _combine_kernel,
        grid=(nt,),
        in_specs=[
            smem, smem, smem, smem_blk, smem_blk,
            pl.BlockSpec(memory_space=pl.ANY),
            pl.BlockSpec(memory_space=pl.ANY),
        ],
        out_specs=pl.BlockSpec(memory_space=pl.ANY),
        out_shape=jax.ShapeDtypeStruct((N, D), jnp.float32),
        scratch_shapes=[
            pltpu.VMEM((2 * TILE_ROWS * ROW_SUB, LANES), jnp.float32),
            pltpu.VMEM((COMBINE_ACC_SLOTS * T, ROW_SUB, LANES), jnp.float32),
            pltpu.SemaphoreType.DMA((2,)),
            pltpu.SemaphoreType.DMA((COMBINE_ACC_SLOTS,)),
            pltpu.SemaphoreType.DMA((COMBINE_ACC_SLOTS,)),
        ],
        compiler_params=pltpu.CompilerParams(
            dimension_semantics=("arbitrary",), vmem_limit_bytes=VMEM_LIMIT),
        name="moe_combine",
    )(lsrc, gdst, cnt, ld_t, w_t, x2, ys_rp)


def _work_items(gstart, totals, n_items):
    BM = EXPERT_BLOCK
    i32 = jnp.int32
    gend = gstart + totals
    first_blk = gstart // BM
    last_blk = jnp.maximum(gend - 1, gstart) // BM
    n_e = jnp.where(totals > 0, last_blk - first_blk + 1, 0)
    item_end = jnp.cumsum(n_e)
    item_start = item_end - n_e
    n_real = item_end[-1]
    j = jnp.minimum(jnp.arange(n_items, dtype=i32), n_real - 1)
    exp = jnp.sum((item_end[None, :] <= j[:, None]).astype(i32), axis=1)
    onehot = exp[:, None] == jnp.arange(N_EXPERTS, dtype=i32)[None, :]

    def of_exp(v):
        return jnp.sum(jnp.where(onehot, v[None, :], 0), axis=1)

    blk = of_exp(first_blk) + (j - of_exp(item_start))
    lo = jnp.maximum(of_exp(gstart), blk * BM) - blk * BM
    hi = jnp.minimum(of_exp(gend), (blk + 1) * BM) - blk * BM
    real = jnp.arange(n_items, dtype=i32) < n_real
    hi = jnp.where(real, hi, lo)
    prev_exp = jnp.concatenate([jnp.full((1,), -1, i32), exp[:-1]])
    prev_blk = jnp.concatenate([jnp.full((1,), -1, i32), blk[:-1]])
    next_blk = jnp.concatenate([blk[1:], jnp.full((1,), -1, i32)])
    is_last = jnp.arange(n_items, dtype=i32) == n_real - 1
    newexp = jnp.logical_and(real, exp != prev_exp).astype(i32)
    newblk = jnp.logical_and(real, blk != prev_blk).astype(i32)
    endblk = jnp.logical_and(real, jnp.logical_or(blk != next_blk, is_last)).astype(i32)
    return blk.astype(i32), exp.astype(i32), lo.astype(i32), hi.astype(i32), newexp, newblk, endblk


def _layer(x, mix_norm_w, w_in, conv_w, q_norm_w, k_norm_w, sinks, conv_out_norm_w,
           attn_out_norm_w, w_out, ffn_norm_w, w_router, router_bias, w_eg, w_eu, w_ed,
           w_sg, w_su, w_sd):
    B, S, D = x.shape
    N = B * S
    T = MIX_TILE
    nt = N // T
    bf16 = jnp.bfloat16
    f32 = jnp.float32
    i32 = jnp.int32

    wr_t = w_router.astype(f32).T
    wr_hi = wr_t.astype(bf16)
    wr_hl = jnp.concatenate([wr_hi, (wr_t - wr_hi.astype(f32)).astype(bf16)], axis=0)
    sink_col = jnp.broadcast_to(
        jnp.repeat(sinks.astype(f32), BLK).reshape(N_KV_HEADS, GQA_GROUP * BLK, 1),
        (N_KV_HEADS, GQA_GROUP * BLK, LANES))
    tri = jnp.asarray(np.triu(np.ones((T, T), np.float32), k=1), dtype=bf16)
    head_ones = jnp.asarray(np.kron(np.eye(2 * LANES // HEAD_DIM, dtype=np.float32),
                                    np.ones((HEAD_DIM, HEAD_DIM), np.float32)), dtype=bf16)
    bias_tab = jnp.asarray(_attn_bias_table())

    x2, xn, wts, ld, cnt = _mixer_call(
        x, mix_norm_w.reshape(1, D), w_in.astype(bf16), conv_w,
        jnp.tile(q_norm_w, N_HEADS).reshape(1, D_ATTN), jnp.tile(k_norm_w, N_KV_HEADS).reshape(1, D_KV),
        sink_col, bias_tab, conv_out_norm_w.reshape(1, D_CONV), attn_out_norm_w.reshape(1, D_ATTN),
        w_out.astype(bf16), ffn_norm_w.reshape(1, D), w_sg.astype(bf16), w_su.astype(bf16),
        w_sd.astype(bf16), wr_hl, router_bias.astype(f32).reshape(N_EXPERTS, 1), tri, head_ones)

    cnt_te = cnt[:, 0].reshape(nt, N_EXPERTS)
    lsrc = jnp.cumsum(cnt_te, axis=1) - cnt_te
    before = jnp.cumsum(cnt_te, axis=0) - cnt_te
    totals = jnp.sum(cnt_te, axis=0)
    gstart = jnp.cumsum(totals) - totals
    gdst = gstart[None, :] + before
    n_items = (N * TOP_K) // EXPERT_BLOCK + N_EXPERTS
    items = _work_items(gstart.astype(i32), totals.astype(i32), n_items)

    def tile_major(a):
        return a.reshape(TOP_K, nt, T).transpose(1, 0, 2).reshape(nt, 1, TILE_ROWS)

    ld_t = tile_major(ld)
    seg_tabs = (lsrc.reshape(-1).astype(i32), gdst.reshape(-1).astype(i32), cnt_te.reshape(-1).astype(i32))
    xs_rp = _dispatch_call(*seg_tabs, ld_t, xn)
    ys_rp = _expert_call(*items, xs_rp, w_eg, w_eu, w_ed)
    out = _combine_call(*seg_tabs, ld_t, tile_major(wts), x2, ys_rp)
    return out.reshape(B, S, D)


def kernel(x, mix_norm_w, w_in, conv_w, q_norm_w, k_norm_w, sinks, conv_out_norm_w, attn_out_norm_w, w_out, ffn_norm_w, w_router, router_bias, w_exp_gate, w_exp_up, w_exp_down, w_sh_gate, w_sh_up, w_sh_down):
    B, S, D = x.shape
    assert D == D_MODEL and w_in.shape == (1, D_MODEL, D_IN_PROJ) and w_exp_gate.shape == (1, N_EXPERTS, D_MODEL, D_EXPERT)
    assert S % MIX_TILE == 0 and (B * S * TOP_K) % EXPERT_BLOCK == 0
    return _layer(x, mix_norm_w[0], w_in[0], conv_w[0], q_norm_w[0], k_norm_w[0], sinks[0],
                  conv_out_norm_w[0], attn_out_norm_w[0], w_out[0], ffn_norm_w[0], w_router[0],
                  router_bias[0], w_exp_gate[0], w_exp_up[0], w_exp_down[0], w_sh_gate[0],
                  w_sh_up[0], w_sh_down[0])
```

```python
import numpy as np
import jax
import jax.numpy as jnp
from jax import lax
from jax.experimental import pallas as pl
from jax.experimental.pallas import tpu as pltpu

D_MODEL = 1024
EPS = 1e-6
D_CONV = 512
HEAD_DIM = 64
N_HEADS = 8
N_KV_HEADS = 2
GQA_GROUP = 4
D_ATTN = 512
D_KV = 128
WINDOW = 128
BLK = 128
D_IN_PROJ = 2304
N_EXPERTS = 64
TOP_K = 8
N_GROUPS = 8
GROUP_SIZE = 8
TOPK_GROUPS = 4
D_EXPERT = 256
ROUTED_SCALE = 2.5

LANES = 128
ROW_SUB = 8
MIX_TILE = 512
TILE_ROWS = TOP_K * MIX_TILE
EXPERT_BLOCK = 1024
EXPERT_SUB = 256
EXPERT_IN_SLOTS = 4
EXPERT_OUT_SLOTS = 3
COMBINE_ACC_SLOTS = 3
SEG_CHUNK_LOG2 = 5
NEG_BIG = -1e30
VMEM_LIMIT = 58 * 1024 * 1024


def _alibi_slopes():
    return np.array([2.0 ** (-8.0 * (h + 1) / N_HEADS) for h in range(N_HEADS)], dtype=np.float32)


def _attn_bias_table():
    qi = np.arange(BLK)[:, None]
    kj = np.arange(2 * BLK)[None, :]
    dist = qi - kj + BLK
    inwin = (dist >= 0) & (dist < WINDOW)
    slopes = _alibi_slopes()
    out = np.zeros((N_KV_HEADS, GQA_GROUP * BLK, 2 * BLK), np.float32)
    for g in range(N_KV_HEADS):
        for i in range(GQA_GROUP):
            h = g * GQA_GROUP + i
            out[g, i * BLK:(i + 1) * BLK] = np.where(inwin, -slopes[h] * dist.astype(np.float32), NEG_BIG)
    return out


def _rms(x, w):
    ms = jnp.mean(x * x, axis=-1, keepdims=True)
    return x * lax.rsqrt(ms + EPS) * w


def _head_rms(x, w_tiled, head_ones):
    C = x.shape[1]
    W = min(C, 2 * LANES)
    sq = x * x
    hi = sq.astype(jnp.bfloat16)
    lo = (sq - hi.astype(jnp.float32)).astype(jnp.bfloat16)
    ones = head_ones[0:W, 0:W]
    ssq = jnp.concatenate(
        [_dot(hi[:, c:c + W], ones) + _dot(lo[:, c:c + W], ones) for c in range(0, C, W)], axis=1)
    return x * lax.rsqrt(ssq * (1.0 / HEAD_DIM) + EPS) * w_tiled


def _dot(a, b):
    return jnp.dot(a, b, preferred_element_type=jnp.float32)


def _dot_nt(a, b):
    return lax.dot_general(a, b, (((1,), (1,)), ((), ())), preferred_element_type=jnp.float32)


def _mixer_kernel(x_ref, mixw_ref, win_ref, convw_ref, qw_ref, kw_ref, sink_ref, bias_ref,
                  cnw_ref, anw_ref, wout_ref, fnw_ref, wsg_ref, wsu_ref, wsd_ref,
                  wrhl_ref, rbias_ref, tri_ref, hones_ref,
                  x2_ref, xn_ref, wts_ref, ld_ref, cnt_ref,
                  kc_ref, vc_ref, cc_ref):
    j = pl.program_id(1)
    T = MIX_TILE
    bf16 = jnp.bfloat16
    f32 = jnp.float32

    @pl.when(j == 0)
    def _():
        kc_ref[...] = jnp.zeros_like(kc_ref)
        vc_ref[...] = jnp.zeros_like(vc_ref)
        cc_ref[...] = jnp.zeros_like(cc_ref)

    x = x_ref[0]
    u = _rms(x, mixw_ref[...]).astype(bf16)
    qkv = _dot(u, win_ref[:, 3 * D_CONV:D_IN_PROJ])
    q = qkv[:, 0:D_ATTN]
    k = qkv[:, D_ATTN:D_ATTN + D_KV]
    v = qkv[:, D_ATTN + D_KV:D_ATTN + 2 * D_KV]
    conv_piece = 2 * LANES
    conv_pieces = []

    lane = lax.broadcasted_iota(jnp.int32, (1, LANES), 1)
    lane_lo = lane < HEAD_DIM
    qn = _head_rms(q, qw_ref[...], hones_ref) * (HEAD_DIM ** -0.5)
    kn = _head_rms(k, kw_ref[...], hones_ref)
    kfull = jnp.concatenate([kc_ref[...], kn], axis=0)
    vfull = jnp.concatenate([vc_ref[...], v], axis=0)
    kc_ref[...] = kn[T - BLK:T, :]
    vc_ref[...] = v[T - BLK:T, :]

    def _rep(a, g):
        r = pltpu.roll(a, HEAD_DIM, axis=1)
        two = jnp.where(lane_lo, a, r) if g == 0 else jnp.where(lane_lo, r, a)
        return jnp.concatenate([two, two], axis=1).astype(bf16)

    k_rep = [_rep(kfull, g) for g in range(N_KV_HEADS)]
    v_rep = [_rep(vfull, g) for g in range(N_KV_HEADS)]

    lane256 = lax.broadcasted_iota(jnp.int32, (1, 2 * LANES), 1)
    head_of_lane = lane256 // HEAD_DIM
    first_f = (j == 0).astype(f32)
    prev_key_mask = jnp.where(lane256 < BLK, first_f * NEG_BIG, 0.0)
    ones_cols = jnp.ones((2 * BLK, LANES), bf16)

    def lanes2(a):
        return jnp.concatenate([a, a], axis=1)

    attn_rows = []
    for i in range(T // BLK):
        grp_out = []
        for g in range(N_KV_HEADS):
            if len(conv_pieces) * conv_piece < 3 * D_CONV:
                c0 = len(conv_pieces) * conv_piece
                conv_pieces.append(_dot(u, win_ref[:, c0:c0 + conv_piece]))
            qg = qn[i * BLK:(i + 1) * BLK, g * 256:(g + 1) * 256]
            qst = jnp.concatenate(
                [jnp.where(head_of_lane == hi, qg, 0.0) for hi in range(GQA_GROUP)], axis=0).astype(bf16)
            kk = k_rep[g][i * BLK:i * BLK + 2 * BLK, :]
            vv = jnp.concatenate([v_rep[g][i * BLK:i * BLK + 2 * BLK, :], ones_cols], axis=1)
            s = _dot_nt(qst, kk) + bias_ref[g]
            if i == 0:
                s = s + prev_key_mask
            sink = sink_ref[g]
            m = jnp.maximum(jnp.max(s, axis=-1, keepdims=True), sink)
            e = jnp.exp(s - lanes2(m)).astype(bf16)
            r = _dot(e, vv)
            inv = 1.0 / (r[:, 2 * LANES:3 * LANES] + jnp.exp(sink - m))
            r = r[:, 0:2 * LANES] * lanes2(inv)
            o = jnp.where(head_of_lane == 0, r[0:BLK], 0.0)
            for hi in range(1, GQA_GROUP):
                o = jnp.where(head_of_lane == hi, r[hi * BLK:(hi + 1) * BLK], o)
            grp_out.append(o)
        attn_rows.append(jnp.concatenate(grp_out, axis=1))
    y_attn = jnp.concatenate(attn_rows, axis=0)

    assert len(conv_pieces) * conv_piece == 3 * D_CONV
    conv_proj = jnp.concatenate(conv_pieces, axis=1)
    b_gate = conv_proj[:, 0:D_CONV]
    c_gate = conv_proj[:, D_CONV:2 * D_CONV]
    hh = conv_proj[:, 2 * D_CONV:3 * D_CONV]
    ch = c_gate * hh
    prev = cc_ref[...]
    p6 = prev[6:7, :]
    p7 = prev[7:8, :]
    row = lax.broadcasted_iota(jnp.int32, (T, D_CONV), 0)
    ch_m1 = jnp.where(row == 0, p7, pltpu.roll(ch, 1, axis=0))
    ch_m2 = jnp.where(row == 0, p6, jnp.where(row == 1, p7, pltpu.roll(ch, 2, axis=0)))
    cw = convw_ref[...]
    y_conv = b_gate * (cw[0:1, :] * ch_m2 + cw[1:2, :] * ch_m1 + cw[2:3, :] * ch)
    cc_ref[...] = ch[T - 8:T, :]

    y_mix = jnp.concatenate([_rms(y_conv, cnw_ref[...]), _rms(y_attn, anw_ref[...])], axis=1)
    x1 = x + _dot(y_mix.astype(bf16), wout_ref[...])

    xn = _rms(x1, fnw_ref[...])
    xn_ref[...] = xn
    xh = xn.astype(bf16)

    xl = (xn - xh.astype(f32)).astype(bf16)
    w_hl = wrhl_ref[...]
    a_hl = _dot_nt(w_hl, xh)
    logits = a_hl[0:N_EXPERTS] + a_hl[N_EXPERTS:2 * N_EXPERTS] + _dot_nt(w_hl[0:N_EXPERTS], xl)

    gs = _dot(xh, wsg_ref[...])
    us = _dot(xh, wsu_ref[...])
    hs = (gs * jax.nn.sigmoid(gs) * us).astype(bf16)
    x2 = x1 + _dot(hs, wsd_ref[...])
    x2_ref[...] = x2

    scores = jax.nn.sigmoid(logits)
    biased = scores + rbias_ref[...]

    sub8 = lax.broadcasted_iota(jnp.int32, (GROUP_SIZE, T), 0)
    gscore = []
    for g in range(N_GROUPS):
        blk = biased[g * GROUP_SIZE:(g + 1) * GROUP_SIZE, :]
        m1 = jnp.max(blk, axis=0, keepdims=True)
        first = jnp.min(jnp.where(blk == m1, sub8, GROUP_SIZE), axis=0, keepdims=True)
        m2 = jnp.max(jnp.where(sub8 == first, -jnp.inf, blk), axis=0, keepdims=True)
        gscore.append(m1 + m2)
    masked_blocks = []
    for g in range(N_GROUPS):
        rank = jnp.zeros((1, T), jnp.int32)
        for o_ in range(N_GROUPS):
            if o_ == g:
                continue
            if o_ < g:
                ahead = gscore[o_] >= gscore[g]
            else:
                ahead = gscore[o_] > gscore[g]
            rank = rank + ahead.astype(jnp.int32)
        keep = rank < TOPK_GROUPS
        blk = biased[g * GROUP_SIZE:(g + 1) * GROUP_SIZE, :]
        masked_blocks.append(jnp.where(keep, blk, -jnp.inf))
    cur = jnp.concatenate(masked_blocks, axis=0)

    eiota = lax.broadcasted_iota(jnp.int32, (N_EXPERTS, T), 0)
    row8 = lax.broadcasted_iota(jnp.int32, (TOP_K, T), 0)
    w_out = jnp.zeros((TOP_K, T), f32)
    sel_dense = jnp.zeros((N_EXPERTS, T), f32)
    onehots = []
    for kk_ in range(TOP_K):
        mx = jnp.max(cur, axis=0, keepdims=True)
        sel_idx = jnp.min(jnp.where(cur == mx, eiota, N_EXPERTS), axis=0, keepdims=True)
        onehot = eiota == sel_idx
        w_k = jnp.sum(jnp.where(onehot, scores, 0.0), axis=0, keepdims=True)
        cur = jnp.where(onehot, -jnp.inf, cur)
        sel_dense = jnp.where(onehot, 1.0, sel_dense)
        w_out = jnp.where(row8 == kk_, w_k, w_out)
        onehots.append(onehot)
    wsum = jnp.sum(w_out, axis=0, keepdims=True)
    wts_ref[...] = w_out / wsum * ROUTED_SCALE

    cum = _dot(sel_dense.astype(bf16), tri_ref[...])
    cnt = jnp.broadcast_to(jnp.sum(sel_dense, axis=1, keepdims=True), (N_EXPERTS, LANES))
    erow = lax.broadcasted_iota(jnp.int32, (N_EXPERTS, LANES), 0)
    incl = cnt
    step = 1
    while step < N_EXPERTS:
        incl = incl + jnp.where(erow >= step, pltpu.roll(incl, step, axis=0), 0.0)
        step *= 2
    lstart = incl - cnt
    ld_dense = lstart[:, 0:1] + cum
    ld_out = jnp.zeros((TOP_K, T), f32)
    for kk_ in range(TOP_K):
        p_k = jnp.sum(jnp.where(onehots[kk_], ld_dense, 0.0), axis=0, keepdims=True)
        ld_out = jnp.where(row8 == kk_, p_k, ld_out)
    tile_idx = pl.program_id(0) * pl.num_programs(1) + j
    slot_base = ((tile_idx & 1) * TILE_ROWS).astype(f32)
    ld_ref[...] = ((ld_out + slot_base) * ROW_SUB).astype(jnp.int32)
    cnt_ref[...] = cnt.astype(jnp.int32)


def _mixer_call(x, mixw, win, convw, qw, kw, sink_col, bias_tab, cnw, anw, wout, fnw,
                wsg, wsu, wsd, wrhl, rbias, tri, head_ones):
    B, S, D = x.shape
    T = MIX_TILE
    nt = S // T
    N = B * S

    def full(a):
        nd = a.ndim
        return pl.BlockSpec(a.shape, lambda b, j, _nd=nd: (0,) * _nd)

    tok_spec = pl.BlockSpec((TOP_K, T), lambda b, j: (0, b * nt + j))
    in_arrays = [mixw, win, convw, qw, kw, sink_col, bias_tab, cnw, anw, wout, fnw,
                 wsg, wsu, wsd, wrhl, rbias, tri, head_ones]
    return pl.pallas_call(
        _mixer_kernel,
        grid=(B, nt),
        in_specs=[pl.BlockSpec((1, T, D), lambda b, j: (b, j, 0))] + [full(a) for a in in_arrays],
        out_specs=[
            pl.BlockSpec((T, D), lambda b, j: (b * nt + j, 0)),
            pl.BlockSpec((T, D), lambda b, j: (b * nt + j, 0)),
            tok_spec, tok_spec,
            pl.BlockSpec((N_EXPERTS, LANES), lambda b, j: (b * nt + j, 0)),
        ],
        out_shape=[
            jax.ShapeDtypeStruct((N, D), jnp.float32),
            jax.ShapeDtypeStruct((N, D), jnp.float32),
            jax.ShapeDtypeStruct((TOP_K, N), jnp.float32),
            jax.ShapeDtypeStruct((TOP_K, N), jnp.int32),
            jax.ShapeDtypeStruct((B * nt * N_EXPERTS, LANES), jnp.int32),
        ],
        scratch_shapes=[
            pltpu.VMEM((BLK, D_KV), jnp.float32),
            pltpu.VMEM((BLK, D_KV), jnp.float32),
            pltpu.VMEM((8, D_CONV), jnp.float32),
        ],
        compiler_params=pltpu.CompilerParams(
            dimension_semantics=("arbitrary", "arbitrary"), vmem_limit_bytes=VMEM_LIMIT),
        name="mixer_router",
    )(x, *in_arrays)


def _rows(ref, row, nrows):
    return ref.at[pl.ds(pl.multiple_of(row * ROW_SUB, ROW_SUB), nrows * ROW_SUB)]


def _segment_copies(src, src_row, dst, dst_row, count, sem):
    chunk = 1 << SEG_CHUNK_LOG2

    def big(i, c):
        o = i * chunk
        pltpu.make_async_copy(_rows(src, src_row + o, chunk), _rows(dst, dst_row + o, chunk), sem).start()
        return c

    lax.fori_loop(0, count >> SEG_CHUNK_LOG2, big, 0)
    for bit in range(SEG_CHUNK_LOG2 - 1, -1, -1):
        o = (count >> (bit + 1)) << (bit + 1)

        @pl.when(((count >> bit) & 1) == 1)
        def _():
            n = 1 << bit
            pltpu.make_async_copy(_rows(src, src_row + o, n), _rows(dst, dst_row + o, n), sem).start()


def _wait_tile(hbm_ref, stag_ref, slot_row, sem, to_hbm):
    vm = _rows(stag_ref, slot_row, TILE_ROWS)
    hb = _rows(hbm_ref, 0, TILE_ROWS)
    (pltpu.make_async_copy(vm, hb, sem) if to_hbm else pltpu.make_async_copy(hb, vm, sem)).wait()


def _tile_relayout_copies(std_hbm, tile, rpt_ref, slot, sem, to_vmem):
    T = MIX_TILE
    row0 = tile * T
    row0 = row0 if isinstance(row0, int) else pl.multiple_of(row0, T)
    slot0 = slot * T
    slot0 = slot0 if isinstance(slot0, int) else pl.multiple_of(slot0, T)
    copies = []
    for c in range(ROW_SUB):
        hb = std_hbm.at[pl.ds(row0, T), pl.ds(c * LANES, LANES)]
        vm = rpt_ref.at[pl.ds(slot0, T), c, :]
        copies.append(pltpu.make_async_copy(hb, vm, sem) if to_vmem else pltpu.make_async_copy(vm, hb, sem))
    return copies


def _dispatch_kernel(lsrc_ref, gdst_ref, cnt_ref, ld_ref, xn_hbm, xs_ref, stag_ref, xin_ref, sems, in_sems):
    b = pl.program_id(0)
    nb = pl.num_programs(0)
    T = MIX_TILE
    slot = b & 1
    base = slot * TILE_ROWS

    def fetch_rows(tile, slot_):
        return _tile_relayout_copies(xn_hbm, tile, xin_ref, slot_, in_sems.at[slot_], True)

    @pl.when(b == 0)
    def _():
        for cp in fetch_rows(b, slot):
            cp.start()

    @pl.when(b + 1 < nb)
    def _():
        for cp in fetch_rows(b + 1, 1 - slot):
            cp.start()

    for cp in fetch_rows(b, slot):
        cp.wait()

    for kk_ in range(TOP_K):
        def row(t, c, kk_=kk_):
            r8 = ld_ref[0, 0, kk_ * T + t]
            stag_ref[pl.ds(pl.multiple_of(r8, ROW_SUB), ROW_SUB), :] = xin_ref[slot * T + t]
            return c

        lax.fori_loop(0, T, row, 0, unroll=64)

    def seg(e, c):
        s = b * N_EXPERTS + e
        _segment_copies(stag_ref, base + lsrc_ref[s], xs_ref, gdst_ref[s], cnt_ref[s], sems.at[slot])
        return c

    lax.fori_loop(0, N_EXPERTS, seg, 0)

    @pl.when(b > 0)
    def _():
        _wait_tile(xs_ref, stag_ref, (1 - slot) * TILE_ROWS, sems.at[1 - slot], True)

    @pl.when(b == nb - 1)
    def _():
        _wait_tile(xs_ref, stag_ref, base, sems.at[slot], True)


def _dispatch_call(lsrc, gdst, cnt, ld_t, xn):
    nt = ld_t.shape[0]
    T = MIX_TILE
    smem = pl.BlockSpec(memory_space=pltpu.SMEM)
    return pl.pallas_call(
        _dispatch_kernel,
        grid=(nt,),
        in_specs=[
            smem, smem, smem,
            pl.BlockSpec((1, 1, TILE_ROWS), lambda i: (i, 0, 0), memory_space=pltpu.SMEM),
            pl.BlockSpec(memory_space=pl.ANY),
        ],
        out_specs=pl.BlockSpec(memory_space=pl.ANY),
        out_shape=jax.ShapeDtypeStruct((nt * TILE_ROWS * ROW_SUB, LANES), jnp.float32),
        scratch_shapes=[pltpu.VMEM((2 * TILE_ROWS * ROW_SUB, LANES), jnp.float32),
                        pltpu.VMEM((2 * T, ROW_SUB, LANES), jnp.float32),
                        pltpu.SemaphoreType.DMA((2,)),
                        pltpu.SemaphoreType.DMA((2,))],
        compiler_params=pltpu.CompilerParams(
            dimension_semantics=("arbitrary",), vmem_limit_bytes=VMEM_LIMIT),
        name="moe_dispatch",
    )(lsrc, gdst, cnt, ld_t, xn)


def _expert_kernel(blk_ref, exp_ref, lo_ref, hi_ref, newexp_ref, newblk_ref, endblk_ref,
                   xs_hbm, wg_ref, wu_ref, wd_ref, ys_hbm,
                   xbuf_ref, ybuf_ref, wgub_ref, wdb_ref, in_sems, out_sems):
    i = pl.program_id(0)
    n_items = pl.num_programs(0)
    bf16 = jnp.bfloat16
    SB = EXPERT_SUB
    n_blocks = xs_hbm.shape[0] // EXPERT_BLOCK
    blk = blk_ref[i]
    lo = lo_ref[i]
    hi = hi_ref[i]

    class _Group:
        def __init__(self, copies):
            self.copies = copies

        def start(self):
            for cp in self.copies:
                cp.start()

        def wait(self):
            for cp in self.copies:
                cp.wait()

    def hbm_chunk(ref, b, c):
        start = b * EXPERT_BLOCK
        start = start if isinstance(start, int) else pl.multiple_of(start, EXPERT_BLOCK)
        return ref.at[pl.ds(start, EXPERT_BLOCK), c, :]

    def vmem_chunk(ref, slot, c):
        start = slot * EXPERT_BLOCK
        start = start if isinstance(start, int) else pl.multiple_of(start, EXPERT_BLOCK)
        return ref.at[pl.ds(start, EXPERT_BLOCK), pl.ds(c * LANES, LANES)]

    def in_copy(b, slot):
        return _Group([pltpu.make_async_copy(hbm_chunk(xs_hbm, b, c), vmem_chunk(xbuf_ref, slot, c), in_sems.at[slot])
                       for c in range(ROW_SUB)])

    def out_copy(b, slot):
        return _Group([pltpu.make_async_copy(vmem_chunk(ybuf_ref, slot, c), hbm_chunk(ys_hbm, b, c), out_sems.at[slot])
                       for c in range(ROW_SUB)])

    @pl.when(i == 0)
    def _():
        for b0 in range(EXPERT_IN_SLOTS - 1):
            in_copy(b0, b0).start()

    @pl.when(newblk_ref[i] == 1)
    def _():
        ahead = blk + (EXPERT_IN_SLOTS - 1)

        @pl.when(ahead < n_blocks)
        def _():
            in_copy(ahead, lax.rem(ahead, EXPERT_IN_SLOTS)).start()

        in_copy(blk, lax.rem(blk, EXPERT_IN_SLOTS)).wait()

        @pl.when(blk >= EXPERT_OUT_SLOTS)
        def _():
            out_copy(blk - EXPERT_OUT_SLOTS, lax.rem(blk, EXPERT_OUT_SLOTS)).wait()

    @pl.when(newexp_ref[i] == 1)
    def _():
        wgub_ref[:, 0:D_EXPERT] = wg_ref[0].astype(bf16)
        wgub_ref[:, D_EXPERT:2 * D_EXPERT] = wu_ref[0].astype(bf16)
        wdb_ref[...] = wd_ref[0].astype(bf16)

    xbase = pl.multiple_of(lax.rem(blk, EXPERT_IN_SLOTS) * EXPERT_BLOCK, EXPERT_BLOCK)
    ybase = pl.multiple_of(lax.rem(blk, EXPERT_OUT_SLOTS) * EXPERT_BLOCK, EXPERT_BLOCK)

    def run(r0, nrows, merge_first):
        xb = xbuf_ref[pl.ds(xbase + r0, nrows), :].astype(bf16)
        gu = _dot(xb, wgub_ref[...])
        g = gu[:, 0:D_EXPERT]
        u = gu[:, D_EXPERT:2 * D_EXPERT]
        h = (g * jax.nn.sigmoid(g) * u).astype(bf16)
        y = _dot(h, wdb_ref[...])
        first = SB if merge_first else 0
        if merge_first:
            rows = r0 + lax.broadcasted_iota(jnp.int32, (SB, D_MODEL), 0)
            keep = jnp.logical_and(rows >= lo, rows < hi)
            old = ybuf_ref[pl.ds(ybase + r0, SB), :]
            ybuf_ref[pl.ds(ybase + r0, SB), :] = jnp.where(keep, y[0:SB], old)
        if nrows > first:
            ybuf_ref[pl.ds(ybase + r0 + first, nrows - first), :] = y[first:nrows]

    n_sub = EXPERT_BLOCK // SB
    from_top = lo == 0
    sub_aligned = (lo & (SB - 1)) == 0
    to_end = jnp.logical_and(jnp.logical_and(lo > 0, hi == EXPERT_BLOCK), jnp.logical_not(sub_aligned))
    subs_used = (hi + SB - 1) // SB
    first_sub = lo // SB
    for k in range(1, n_sub + 1):
        @pl.when(jnp.logical_and(from_top, subs_used == k))
        def _(k=k):
            run(0, k * SB, False)

    for q in range(n_sub):
        @pl.when(jnp.logical_and(to_end, first_sub == q))
        def _(q=q):
            run(q * SB, EXPERT_BLOCK - q * SB, True)

    for q in range(n_sub):
        r0 = q * SB
        inside = jnp.logical_and(lo > 0, jnp.logical_or(hi < EXPERT_BLOCK, sub_aligned))

        @pl.when(jnp.logical_and(inside, jnp.minimum(hi, r0 + SB) > jnp.maximum(lo, r0)))
        def _(r0=r0):
            @pl.when(lo <= r0)
            def _():
                run(r0, SB, False)

            @pl.when(lo > r0)
            def _():
                run(r0, SB, True)

    @pl.when(endblk_ref[i] == 1)
    def _():
        out_copy(blk, lax.rem(blk, EXPERT_OUT_SLOTS)).start()

    @pl.when(i == n_items - 1)
    def _():
        for b1 in range(n_blocks - EXPERT_OUT_SLOTS, n_blocks):
            out_copy(b1, b1 % EXPERT_OUT_SLOTS).wait()


def _expert_call(item_blk, item_exp, item_lo, item_hi, item_newexp, item_newblk, item_endblk,
                 xs_rp, w_eg, w_eu, w_ed):
    D = D_MODEL
    BS = EXPERT_BLOCK * ROW_SUB
    n_items = item_blk.shape[0]
    assert xs_rp.shape[0] % BS == 0 and xs_rp.shape[0] // BS >= max(EXPERT_IN_SLOTS, EXPERT_OUT_SLOTS)
    xs3 = xs_rp.reshape(xs_rp.shape[0] // ROW_SUB, ROW_SUB, LANES)

    def w_map(i, blk, exp, lo, hi, newexp, newblk, endblk):
        return (exp[i], 0, 0)

    grid_spec = pltpu.PrefetchScalarGridSpec(
        num_scalar_prefetch=7,
        grid=(n_items,),
        in_specs=[
            pl.BlockSpec(memory_space=pl.ANY),
            pl.BlockSpec((1, D, D_EXPERT), w_map),
            pl.BlockSpec((1, D, D_EXPERT), w_map),
            pl.BlockSpec((1, D_EXPERT, D), w_map),
        ],
        out_specs=pl.BlockSpec(memory_space=pl.ANY),
        scratch_shapes=[
            pltpu.VMEM((EXPERT_IN_SLOTS * EXPERT_BLOCK, D), jnp.float32),
            pltpu.VMEM((EXPERT_OUT_SLOTS * EXPERT_BLOCK, D), jnp.float32),
            pltpu.VMEM((D, 2 * D_EXPERT), jnp.bfloat16),
            pltpu.VMEM((D_EXPERT, D), jnp.bfloat16),
            pltpu.SemaphoreType.DMA((EXPERT_IN_SLOTS,)),
            pltpu.SemaphoreType.DMA((EXPERT_OUT_SLOTS,)),
        ],
    )
    return pl.pallas_call(
        _expert_kernel,
        grid_spec=grid_spec,
        out_shape=jax.ShapeDtypeStruct(xs3.shape, jnp.float32),
        compiler_params=pltpu.CompilerParams(
            dimension_semantics=("arbitrary",), vmem_limit_bytes=VMEM_LIMIT),
        name="moe_experts",
    )(item_blk, item_exp, item_lo, item_hi, item_newexp, item_newblk, item_endblk, xs3, w_eg, w_eu, w_ed
      ).reshape(xs_rp.shape)


def _combine_kernel(lsrc_ref, gdst_ref, cnt_ref, ld_ref, w_ref, x2_hbm, ys_ref, out_hbm,
                    stag_ref, acc_ref, sems, in_sems, out_sems):
    b = pl.program_id(0)
    nb = pl.num_programs(0)
    T = MIX_TILE
    slot = b & 1
    base = slot * TILE_ROWS
    aslot = lax.rem(b, COMBINE_ACC_SLOTS)

    def in_copies(tile):
        s_ = lax.rem(tile, COMBINE_ACC_SLOTS)
        return _tile_relayout_copies(x2_hbm, tile, acc_ref, s_, in_sems.at[s_], True)

    def out_copies(tile):
        s_ = lax.rem(tile, COMBINE_ACC_SLOTS)
        return _tile_relayout_copies(out_hbm, tile, acc_ref, s_, out_sems.at[s_], False)

    @pl.when(b >= COMBINE_ACC_SLOTS - 1)
    def _():
        for cp in out_copies(b - (COMBINE_ACC_SLOTS - 1)):
            cp.wait()

    @pl.when(b == 0)
    def _():
        for cp in in_copies(b):
            cp.start()

    @pl.when(b + 1 < nb)
    def _():
        for cp in in_copies(b + 1):
            cp.start()

    def fetch(tile, slot_):
        def seg(e, c):
            s = tile * N_EXPERTS + e
            _segment_copies(ys_ref, gdst_ref[s], stag_ref, slot_ * TILE_ROWS + lsrc_ref[s], cnt_ref[s],
                            sems.at[slot_])
            return c

        lax.fori_loop(0, N_EXPERTS, seg, 0)

    @pl.when(b == 0)
    def _():
        fetch(b, slot)

    @pl.when(b + 1 < nb)
    def _():
        fetch(b + 1, 1 - slot)

    _wait_tile(ys_ref, stag_ref, base, sems.at[slot], False)
    for cp in in_copies(b):
        cp.wait()

    def tok(t, c):
        a = acc_ref[aslot * T + t]
        for kk_ in range(TOP_K):
            r8 = ld_ref[0, 0, kk_ * T + t]
            w = w_ref[0, 0, kk_ * T + t]
            a = a + w * stag_ref[pl.ds(pl.multiple_of(r8, ROW_SUB), ROW_SUB), :]
        acc_ref[aslot * T + t] = a
        return c

    lax.fori_loop(0, T, tok, 0, unroll=16)
    for cp in out_copies(b):
        cp.start()

    @pl.when(b == nb - 1)
    def _():
        for back in range(COMBINE_ACC_SLOTS - 2, -1, -1):
            @pl.when(b >= back)
            def _(back=back):
                for cp in out_copies(b - back):
                    cp.wait()


def _combine_call(lsrc, gdst, cnt, ld_t, w_t, x2, ys_rp):
    N, D = x2.shape
    nt = ld_t.shape[0]
    T = MIX_TILE
    smem = pl.BlockSpec(memory_space=pltpu.SMEM)
    smem_blk = pl.BlockSpec((1, 1, TILE_ROWS), lambda i: (i, 0, 0), memory_space=pltpu.SMEM)
    return pl.pallas_call(
        _combine_kernel,
        grid=(nt,),
        in_specs=[
            smem, smem, smem, smem_blk, smem_blk,
            pl.BlockSpec(memory_space=pl.ANY),
            pl.BlockSpec(memory_space=pl.ANY),
        ],
        out_specs=pl.BlockSpec(memory_space=pl.ANY),
        out_shape=jax.ShapeDtypeStruct((N, D), jnp.float32),
        scratch_shapes=[
            pltpu.VMEM((2 * TILE_ROWS * ROW_SUB, LANES), jnp.float32),
            pltpu.VMEM((COMBINE_ACC_SLOTS * T, ROW_SUB, LANES), jnp.float32),
            pltpu.SemaphoreType.DMA((2,)),
            pltpu.SemaphoreType.DMA((COMBINE_ACC_SLOTS,)),
            pltpu.SemaphoreType.DMA((COMBINE_ACC_SLOTS,)),
        ],
        compiler_params=pltpu.CompilerParams(
            dimension_semantics=("arbitrary",), vmem_limit_bytes=VMEM_LIMIT),
        name="moe_combine",
    )(lsrc, gdst, cnt, ld_t, w_t, x2, ys_rp)


def _work_items(gstart, totals, n_items):
    BM = EXPERT_BLOCK
    i32 = jnp.int32
    gend = gstart + totals
    first_blk = gstart // BM
    last_blk = jnp.maximum(gend - 1, gstart) // BM
    n_e = jnp.where(totals > 0, last_blk - first_blk + 1, 0)
    item_end = jnp.cumsum(n_e)
    item_start = item_end - n_e
    n_real = item_end[-1]
    j = jnp.minimum(jnp.arange(n_items, dtype=i32), n_real - 1)
    exp = jnp.sum((item_end[None, :] <= j[:, None]).astype(i32), axis=1)
    onehot = exp[:, None] == jnp.arange(N_EXPERTS, dtype=i32)[None, :]

    def of_exp(v):
        return jnp.sum(jnp.where(onehot, v[None, :], 0), axis=1)

    blk = of_exp(first_blk) + (j - of_exp(item_start))
    lo = jnp.maximum(of_exp(gstart), blk * BM) - blk * BM
    hi = jnp.minimum(of_exp(gend), (blk + 1) * BM) - blk * BM
    real = jnp.arange(n_items, dtype=i32) < n_real
    hi = jnp.where(real, hi, lo)
    prev_exp = jnp.concatenate([jnp.full((1,), -1, i32), exp[:-1]])
    prev_blk = jnp.concatenate([jnp.full((1,), -1, i32), blk[:-1]])
    next_blk = jnp.concatenate([blk[1:], jnp.full((1,), -1, i32)])
    is_last = jnp.arange(n_items, dtype=i32) == n_real - 1
    newexp = jnp.logical_and(real, exp != prev_exp).astype(i32)
    newblk = jnp.logical_and(real, blk != prev_blk).astype(i32)
    endblk = jnp.logical_and(real, jnp.logical_or(blk != next_blk, is_last)).astype(i32)
    return blk.astype(i32), exp.astype(i32), lo.astype(i32), hi.astype(i32), newexp, newblk, endblk


def _layer(x, mix_norm_w, w_in, conv_w, q_norm_w, k_norm_w, sinks, conv_out_norm_w,
           attn_out_norm_w, w_out, ffn_norm_w, w_router, router_bias, w_eg, w_eu, w_ed,
           w_sg, w_su, w_sd):
    B, S, D = x.shape
    N = B * S
    T = MIX_TILE
    nt = N // T
    bf16 = jnp.bfloat16
    f32 = jnp.float32
    i32 = jnp.int32

    wr_t = w_router.astype(f32).T
    wr_hi = wr_t.astype(bf16)
    wr_hl = jnp.concatenate([wr_hi, (wr_t - wr_hi.astype(f32)).astype(bf16)], axis=0)
    sink_col = jnp.broadcast_to(
        jnp.repeat(sinks.astype(f32), BLK).reshape(N_KV_HEADS, GQA_GROUP * BLK, 1),
        (N_KV_HEADS, GQA_GROUP * BLK, LANES))
    tri = jnp.asarray(np.triu(np.ones((T, T), np.float32), k=1), dtype=bf16)
    head_ones = jnp.asarray(np.kron(np.eye(2 * LANES // HEAD_DIM, dtype=np.float32),
                                    np.ones((HEAD_DIM, HEAD_DIM), np.float32)), dtype=bf16)
    bias_tab = jnp.asarray(_attn_bias_table())

    x2, xn, wts, ld, cnt = _mixer_call(
        x, mix_norm_w.reshape(1, D), w_in.astype(bf16), conv_w,
        jnp.tile(q_norm_w, N_HEADS).reshape(1, D_ATTN), jnp.tile(k_norm_w, N_KV_HEADS).reshape(1, D_KV),
        sink_col, bias_tab, conv_out_norm_w.reshape(1, D_CONV), attn_out_norm_w.reshape(1, D_ATTN),
        w_out.astype(bf16), ffn_norm_w.reshape(1, D), w_sg.astype(bf16), w_su.astype(bf16),
        w_sd.astype(bf16), wr_hl, router_bias.astype(f32).reshape(N_EXPERTS, 1), tri, head_ones)

    cnt_te = cnt[:, 0].reshape(nt, N_EXPERTS)
    lsrc = jnp.cumsum(cnt_te, axis=1) - cnt_te
    before = jnp.cumsum(cnt_te, axis=0) - cnt_te
    totals = jnp.sum(cnt_te, axis=0)
    gstart = jnp.cumsum(totals) - totals
    gdst = gstart[None, :] + before
    n_items = (N * TOP_K) // EXPERT_BLOCK + N_EXPERTS
    items = _work_items(gstart.astype(i32), totals.astype(i32), n_items)

    def tile_major(a):
        return a.reshape(TOP_K, nt, T).transpose(1, 0, 2).reshape(nt, 1, TILE_ROWS)

    ld_t = tile_major(ld)
    seg_tabs = (lsrc.reshape(-1).astype(i32), gdst.reshape(-1).astype(i32), cnt_te.reshape(-1).astype(i32))
    xs_rp = _dispatch_call(*seg_tabs, ld_t, xn)
    ys_rp = _expert_call(*items, xs_rp, w_eg, w_eu, w_ed)
    out = _combine_call(*seg_tabs, ld_t, tile_major(wts), x2, ys_rp)
    return out.reshape(B, S, D)


def kernel(x, mix_norm_w, w_in, conv_w, q_norm_w, k_norm_w, sinks, conv_out_norm_w, attn_out_norm_w, w_out, ffn_norm_w, w_router, router_bias, w_exp_gate, w_exp_up, w_exp_down, w_sh_gate, w_sh_up, w_sh_down):
    B, S, D = x.shape
    assert D == D_MODEL and w_in.shape == (1, D_MODEL, D_IN_PROJ) and w_exp_gate.shape == (1, N_EXPERTS, D_MODEL, D_EXPERT)
    assert S % MIX_TILE == 0 and (B * S * TOP_K) % EXPERT_BLOCK == 0
    return _layer(x, mix_norm_w[0], w_in[0], conv_w[0], q_norm_w[0], k_norm_w[0], sinks[0],
                  conv_out_norm_w[0], attn_out_norm_w[0], w_out[0], ffn_norm_w[0], w_router[0],
                  router_bias[0], w_exp_gate[0], w_exp_up[0], w_exp_down[0], w_sh_gate[0],
                  w_sh_up[0], w_sh_down[0])
```

```python
import numpy as np
import jax
import jax.numpy as jnp
from jax import lax
from jax.experimental import pallas as pl
from jax.experimental.pallas import tpu as pltpu

D_MODEL = 1024
EPS = 1e-6
D_CONV = 512
HEAD_DIM = 64
N_HEADS = 8
N_KV_HEADS = 2
GQA_GROUP = 4
D_ATTN = 512
D_KV = 128
WINDOW = 128
BLK = 128
D_IN_PROJ = 2304
N_EXPERTS = 64
TOP_K = 8
N_GROUPS = 8
GROUP_SIZE = 8
TOPK_GROUPS = 4
D_EXPERT = 256
ROUTED_SCALE = 2.5

LANES = 128
ROW_SUB = 8
MIX_TILE = 512
TILE_ROWS = TOP_K * MIX_TILE
EXPERT_BLOCK = 1024
EXPERT_SUB = 256
EXPERT_IN_SLOTS = 4
EXPERT_OUT_SLOTS = 3
COMBINE_ACC_SLOTS = 3
SEG_CHUNK_LOG2 = 5
NEG_BIG = -1e30
VMEM_LIMIT = 58 * 1024 * 1024


def _alibi_slopes():
    return np.array([2.0 ** (-8.0 * (h + 1) / N_HEADS) for h in range(N_HEADS)], dtype=np.float32)


def _attn_bias_table():
    qi = np.arange(BLK)[:, None]
    kj = np.arange(2 * BLK)[None, :]
    dist = qi - kj + BLK
    inwin = (dist >= 0) & (dist < WINDOW)
    slopes = _alibi_slopes()
    out = np.zeros((N_KV_HEADS, GQA_GROUP * BLK, 2 * BLK), np.float32)
    for g in range(N_KV_HEADS):
        for i in range(GQA_GROUP):
            h = g * GQA_GROUP + i
            out[g, i * BLK:(i + 1) * BLK] = np.where(inwin, -slopes[h] * dist.astype(np.float32), NEG_BIG)
    return out


def _rms(x, w):
    ms = jnp.mean(x * x, axis=-1, keepdims=True)
    return x * lax.rsqrt(ms + EPS) * w


def _head_rms(x, w_tiled, head_ones):
    C = x.shape[1]
    W = min(C, 2 * LANES)
    sq = x * x
    hi = sq.astype(jnp.bfloat16)
    lo = (sq - hi.astype(jnp.float32)).astype(jnp.bfloat16)
    ones = head_ones[0:W, 0:W]
    ssq = jnp.concatenate(
        [_dot(hi[:, c:c + W], ones) + _dot(lo[:, c:c + W], ones) for c in range(0, C, W)], axis=1)
    return x * lax.rsqrt(ssq * (1.0 / HEAD_DIM) + EPS) * w_tiled


def _dot(a, b):
    return jnp.dot(a, b, preferred_element_type=jnp.float32)


def _dot_nt(a, b):
    return lax.dot_general(a, b, (((1,), (1,)), ((), ())), preferred_element_type=jnp.float32)


def _mixer_kernel(x_ref, mixw_ref, win_ref, convw_ref, qw_ref, kw_ref, sink_ref, bias_ref,
                  cnw_ref, anw_ref, wout_ref, fnw_ref, wsg_ref, wsu_ref, wsd_ref,
                  wrhl_ref, rbias_ref, tri_ref, hones_ref,
                  x2_ref, xn_ref, wts_ref, ld_ref, cnt_ref,
                  kc_ref, vc_ref, cc_ref):
    j = pl.program_id(1)
    T = MIX_TILE
    bf16 = jnp.bfloat16
    f32 = jnp.float32

    @pl.when(j == 0)
    def _():
        kc_ref[...] = jnp.zeros_like(kc_ref)
        vc_ref[...] = jnp.zeros_like(vc_ref)
        cc_ref[...] = jnp.zeros_like(cc_ref)

    x = x_ref[0]
    u = _rms(x, mixw_ref[...]).astype(bf16)
    qkv = _dot(u, win_ref[:, 3 * D_CONV:D_IN_PROJ])
    q = qkv[:, 0:D_ATTN]
    k = qkv[:, D_ATTN:D_ATTN + D_KV]
    v = qkv[:, D_ATTN + D_KV:D_ATTN + 2 * D_KV]
    conv_piece = 2 * LANES
    conv_pieces = []

    lane = lax.broadcasted_iota(jnp.int32, (1, LANES), 1)
    lane_lo = lane < HEAD_DIM
    qn = _head_rms(q, qw_ref[...], hones_ref) * (HEAD_DIM ** -0.5)
    kn = _head_rms(k, kw_ref[...], hones_ref)
    kfull = jnp.concatenate([kc_ref[...], kn], axis=0)
    vfull = jnp.concatenate([vc_ref[...], v], axis=0)
    kc_ref[...] = kn[T - BLK:T, :]
    vc_ref[...] = v[T - BLK:T, :]

    def _rep(a, g):
        r = pltpu.roll(a, HEAD_DIM, axis=1)
        two = jnp.where(lane_lo, a, r) if g == 0 else jnp.where(lane_lo, r, a)
        return jnp.concatenate([two, two], axis=1).astype(bf16)

    k_rep = [_rep(kfull, g) for g in range(N_KV_HEADS)]
    v_rep = [_rep(vfull, g) for g in range(N_KV_HEADS)]

    lane256 = lax.broadcasted_iota(jnp.int32, (1, 2 * LANES), 1)
    head_of_lane = lane256 // HEAD_DIM
    first_f = (j == 0).astype(f32)
    prev_key_mask = jnp.where(lane256 < BLK, first_f * NEG_BIG, 0.0)
    ones_cols = jnp.ones((2 * BLK, LANES), bf16)

    def lanes2(a):
        return jnp.concatenate([a, a], axis=1)

    attn_rows = []
    for i in range(T // BLK):
        grp_out = []
        for g in range(N_KV_HEADS):
            if len(conv_pieces) * conv_piece < 3 * D_CONV:
                c0 = len(conv_pieces) * conv_piece
                conv_pieces.append(_dot(u, win_ref[:, c0:c0 + conv_piece]))
            qg = qn[i * BLK:(i + 1) * BLK, g * 256:(g + 1) * 256]
            qst = jnp.concatenate(
                [jnp.where(head_of_lane == hi, qg, 0.0) for hi in range(GQA_GROUP)], axis=0).astype(bf16)
            kk = k_rep[g][i * BLK:i * BLK + 2 * BLK, :]
            vv = jnp.concatenate([v_rep[g][i * BLK:i * BLK + 2 * BLK, :], ones_cols], axis=1)
            s = _dot_nt(qst, kk) + bias_ref[g]
            if i == 0:
                s = s + prev_key_mask
            sink = sink_ref[g]
            m = jnp.maximum(jnp.max(s, axis=-1, keepdims=True), sink)
            e = jnp.exp(s - lanes2(m)).astype(bf16)
            r = _dot(e, vv)
            inv = 1.0 / (r[:, 2 * LANES:3 * LANES] + jnp.exp(sink - m))
            r = r[:, 0:2 * LANES] * lanes2(inv)
            o = jnp.where(head_of_lane == 0, r[0:BLK], 0.0)
            for hi in range(1, GQA_GROUP):
                o = jnp.where(head_of_lane == hi, r[hi * BLK:(hi + 1) * BLK], o)
            grp_out.append(o)
        attn_rows.append(jnp.concatenate(grp_out, axis=1))
    y_attn = jnp.concatenate(attn_rows, axis=0)

    assert len(conv_pieces) * conv_piece == 3 * D_CONV
    conv_proj = jnp.concatenate(conv_pieces, axis=1)
    b_gate = conv_proj[:, 0:D_CONV]
    c_gate = conv_proj[:, D_CONV:2 * D_CONV]
    hh = conv_proj[:, 2 * D_CONV:3 * D_CONV]
    ch = c_gate * hh
    prev = cc_ref[...]
    p6 = prev[6:7, :]
    p7 = prev[7:8, :]
    row = lax.broadcasted_iota(jnp.int32, (T, D_CONV), 0)
    ch_m1 = jnp.where(row == 0, p7, pltpu.roll(ch, 1, axis=0))
    ch_m2 = jnp.where(row == 0, p6, jnp.where(row == 1, p7, pltpu.roll(ch, 2, axis=0)))
    cw = convw_ref[...]
    y_conv = b_gate * (cw[0:1, :] * ch_m2 + cw[1:2, :] * ch_m1 + cw[2:3, :] * ch)
    cc_ref[...] = ch[T - 8:T, :]

    y_mix = jnp.concatenate([_rms(y_conv, cnw_ref[...]), _rms(y_attn, anw_ref[...])], axis=1)
    x1 = x + _dot(y_mix.astype(bf16), wout_ref[...])

    xn = _rms(x1, fnw_ref[...])
    xn_ref[...] = xn
    xh = xn.astype(bf16)

    xl = (xn - xh.astype(f32)).astype(bf16)
    w_hl = wrhl_ref[...]
    a_hl = _dot_nt(w_hl, xh)
    logits = a_hl[0:N_EXPERTS] + a_hl[N_EXPERTS:2 * N_EXPERTS] + _dot_nt(w_hl[0:N_EXPERTS], xl)

    gs = _dot(xh, wsg_ref[...])
    us = _dot(xh, wsu_ref[...])
    hs = (gs * jax.nn.sigmoid(gs) * us).astype(bf16)
    x2 = x1 + _dot(hs, wsd_ref[...])
    x2_ref[...] = x2

    scores = jax.nn.sigmoid(logits)
    biased = scores + rbias_ref[...]

    sub8 = lax.broadcasted_iota(jnp.int32, (GROUP_SIZE, T), 0)
    gscore = []
    for g in range(N_GROUPS):
        blk = biased[g * GROUP_SIZE:(g + 1) * GROUP_SIZE, :]
        m1 = jnp.max(blk, axis=0, keepdims=True)
        first = jnp.min(jnp.where(blk == m1, sub8, GROUP_SIZE), axis=0, keepdims=True)
        m2 = jnp.max(jnp.where(sub8 == first, -jnp.inf, blk), axis=0, keepdims=True)
        gscore.append(m1 + m2)
    masked_blocks = []
    for g in range(N_GROUPS):
        rank = jnp.zeros((1, T), jnp.int32)
        for o_ in range(N_GROUPS):
            if o_ == g:
                continue
            if o_ < g:
                ahead = gscore[o_] >= gscore[g]
            else:
                ahead = gscore[o_] > gscore[g]
            rank = rank + ahead.astype(jnp.int32)
        keep = rank < TOPK_GROUPS
        blk = biased[g * GROUP_SIZE:(g + 1) * GROUP_SIZE, :]
        masked_blocks.append(jnp.where(keep, blk, -jnp.inf))
    cur = jnp.concatenate(masked_blocks, axis=0)

    eiota = lax.broadcasted_iota(jnp.int32, (N_EXPERTS, T), 0)
    row8 = lax.broadcasted_iota(jnp.int32, (TOP_K, T), 0)
    w_out = jnp.zeros((TOP_K, T), f32)
    sel_dense = jnp.zeros((N_EXPERTS, T), f32)
    onehots = []
    for kk_ in range(TOP_K):
        mx = jnp.max(cur, axis=0, keepdims=True)
        sel_idx = jnp.min(jnp.where(cur == mx, eiota, N_EXPERTS), axis=0, keepdims=True)
        onehot = eiota == sel_idx
        w_k = jnp.sum(jnp.where(onehot, scores, 0.0), axis=0, keepdims=True)
        cur = jnp.where(onehot, -jnp.inf, cur)
        sel_dense = jnp.where(onehot, 1.0, sel_dense)
        w_out = jnp.where(row8 == kk_, w_k, w_out)
        onehots.append(onehot)
    wsum = jnp.sum(w_out, axis=0, keepdims=True)
    wts_ref[...] = w_out / wsum * ROUTED_SCALE

    cum = _dot(sel_dense.astype(bf16), tri_ref[...])
    cnt = jnp.broadcast_to(jnp.sum(sel_dense, axis=1, keepdims=True), (N_EXPERTS, LANES))
    erow = lax.broadcasted_iota(jnp.int32, (N_EXPERTS, LANES), 0)
    incl = cnt
    step = 1
    while step < N_EXPERTS:
        incl = incl + jnp.where(erow >= step, pltpu.roll(incl, step, axis=0), 0.0)
        step *= 2
    lstart = incl - cnt
    ld_dense = lstart[:, 0:1] + cum
    ld_out = jnp.zeros((TOP_K, T), f32)
    for kk_ in range(TOP_K):
        p_k = jnp.sum(jnp.where(onehots[kk_], ld_dense, 0.0), axis=0, keepdims=True)
        ld_out = jnp.where(row8 == kk_, p_k, ld_out)
    tile_idx = pl.program_id(0) * pl.num_programs(1) + j
    slot_base = ((tile_idx & 1) * TILE_ROWS).astype(f32)
    ld_ref[...] = ((ld_out + slot_base) * ROW_SUB).astype(jnp.int32)
    cnt_ref[...] = cnt.astype(jnp.int32)


def _mixer_call(x, mixw, win, convw, qw, kw, sink_col, bias_tab, cnw, anw, wout, fnw,
                wsg, wsu, wsd, wrhl, rbias, tri, head_ones):
    B, S, D = x.shape
    T = MIX_TILE
    nt = S // T
    N = B * S

    def full(a):
        nd = a.ndim
        return pl.BlockSpec(a.shape, lambda b, j, _nd=nd: (0,) * _nd)

    tok_spec = pl.BlockSpec((TOP_K, T), lambda b, j: (0, b * nt + j))
    in_arrays = [mixw, win, convw, qw, kw, sink_col, bias_tab, cnw, anw, wout, fnw,
                 wsg, wsu, wsd, wrhl, rbias, tri, head_ones]
    return pl.pallas_call(
        _mixer_kernel,
        grid=(B, nt),
        in_specs=[pl.BlockSpec((1, T, D), lambda b, j: (b, j, 0))] + [full(a) for a in in_arrays],
        out_specs=[
            pl.BlockSpec((T, D), lambda b, j: (b * nt + j, 0)),
            pl.BlockSpec((T, D), lambda b, j: (b * nt + j, 0)),
            tok_spec, tok_spec,
            pl.BlockSpec((N_EXPERTS, LANES), lambda b, j: (b * nt + j, 0)),
        ],
        out_shape=[
            jax.ShapeDtypeStruct((N, D), jnp.float32),
            jax.ShapeDtypeStruct((N, D), jnp.float32),
            jax.ShapeDtypeStruct((TOP_K, N), jnp.float32),
            jax.ShapeDtypeStruct((TOP_K, N), jnp.int32),
            jax.ShapeDtypeStruct((B * nt * N_EXPERTS, LANES), jnp.int32),
        ],
        scratch_shapes=[
            pltpu.VMEM((BLK, D_KV), jnp.float32),
            pltpu.VMEM((BLK, D_KV), jnp.float32),
            pltpu.VMEM((8, D_CONV), jnp.float32),
        ],
        compiler_params=pltpu.CompilerParams(
            dimension_semantics=("arbitrary", "arbitrary"), vmem_limit_bytes=VMEM_LIMIT),
        name="mixer_router",
    )(x, *in_arrays)


def _rows(ref, row, nrows):
    return ref.at[pl.ds(pl.multiple_of(row * ROW_SUB, ROW_SUB), nrows * ROW_SUB)]


def _segment_copies(src, src_row, dst, dst_row, count, sem):
    chunk = 1 << SEG_CHUNK_LOG2

    def big(i, c):
        o = i * chunk
        pltpu.make_async_copy(_rows(src, src_row + o, chunk), _rows(dst, dst_row + o, chunk), sem).start()
        return c

    lax.fori_loop(0, count >> SEG_CHUNK_LOG2, big, 0)
    for bit in range(SEG_CHUNK_LOG2 - 1, -1, -1):
        o = (count >> (bit + 1)) << (bit + 1)

        @pl.when(((count >> bit) & 1) == 1)
        def _():
            n = 1 << bit
            pltpu.make_async_copy(_rows(src, src_row + o, n), _rows(dst, dst_row + o, n), sem).start()


def _wait_tile(hbm_ref, stag_ref, slot_row, sem, to_hbm):
    vm = _rows(stag_ref, slot_row, TILE_ROWS)
    hb = _rows(hbm_ref, 0, TILE_ROWS)
    (pltpu.make_async_copy(vm, hb, sem) if to_hbm else pltpu.make_async_copy(hb, vm, sem)).wait()


def _tile_relayout_copies(std_hbm, tile, rpt_ref, slot, sem, to_vmem):
    T = MIX_TILE
    row0 = tile * T
    row0 = row0 if isinstance(row0, int) else pl.multiple_of(row0, T)
    slot0 = slot * T
    slot0 = slot0 if isinstance(slot0, int) else pl.multiple_of(slot0, T)
    copies = []
    for c in range(ROW_SUB):
        hb = std_hbm.at[pl.ds(row0, T), pl.ds(c * LANES, LANES)]
        vm = rpt_ref.at[pl.ds(slot0, T), c, :]
        copies.append(pltpu.make_async_copy(hb, vm, sem) if to_vmem else pltpu.make_async_copy(vm, hb, sem))
    return copies


def _dispatch_kernel(lsrc_ref, gdst_ref, cnt_ref, ld_ref, xn_hbm, xs_ref, stag_ref, xin_ref, sems, in_sems):
    b = pl.program_id(0)
    nb = pl.num_programs(0)
    T = MIX_TILE
    slot = b & 1
    base = slot * TILE_ROWS

    def fetch_rows(tile, slot_):
        return _tile_relayout_copies(xn_hbm, tile, xin_ref, slot_, in_sems.at[slot_], True)

    @pl.when(b == 0)
    def _():
        for cp in fetch_rows(b, slot):
            cp.start()

    @pl.when(b + 1 < nb)
    def _():
        for cp in fetch_rows(b + 1, 1 - slot):
            cp.start()

    for cp in fetch_rows(b, slot):
        cp.wait()

    for kk_ in range(TOP_K):
        def row(t, c, kk_=kk_):
            r8 = ld_ref[0, 0, kk_ * T + t]
            stag_ref[pl.ds(pl.multiple_of(r8, ROW_SUB), ROW_SUB), :] = xin_ref[slot * T + t]
            return c

        lax.fori_loop(0, T, row, 0, unroll=128)

    def seg(e, c):
        s = b * N_EXPERTS + e
        _segment_copies(stag_ref, base + lsrc_ref[s], xs_ref, gdst_ref[s], cnt_ref[s], sems.at[slot])
        return c

    lax.fori_loop(0, N_EXPERTS, seg, 0, unroll=4)

    @pl.when(b > 0)
    def _():
        _wait_tile(xs_ref, stag_ref, (1 - slot) * TILE_ROWS, sems.at[1 - slot], True)

    @pl.when(b == nb - 1)
    def _():
        _wait_tile(xs_ref, stag_ref, base, sems.at[slot], True)


def _dispatch_call(lsrc, gdst, cnt, ld_t, xn):
    nt = ld_t.shape[0]
    T = MIX_TILE
    smem = pl.BlockSpec(memory_space=pltpu.SMEM)
    return pl.pallas_call(
        _dispatch_kernel,
        grid=(nt,),
        in_specs=[
            smem, smem, smem,
            pl.BlockSpec((1, 1, TILE_ROWS), lambda i: (i, 0, 0), memory_space=pltpu.SMEM),
            pl.BlockSpec(memory_space=pl.ANY),
        ],
        out_specs=pl.BlockSpec(memory_space=pl.ANY),
        out_shape=jax.ShapeDtypeStruct((nt * TILE_ROWS * ROW_SUB, LANES), jnp.float32),
        scratch_shapes=[pltpu.VMEM((2 * TILE_ROWS * ROW_SUB, LANES), jnp.float32),
                        pltpu.VMEM((2 * T, ROW_SUB, LANES), jnp.float32),
                        pltpu.SemaphoreType.DMA((2,)),
                        pltpu.SemaphoreType.DMA((2,))],
        compiler_params=pltpu.CompilerParams(
            dimension_semantics=("arbitrary",), vmem_limit_bytes=VMEM_LIMIT),
        name="moe_dispatch",
    )(lsrc, gdst, cnt, ld_t, xn)


def _expert_kernel(blk_ref, exp_ref, lo_ref, hi_ref, newexp_ref, newblk_ref, endblk_ref,
                   xs_hbm, wg_ref, wu_ref, wd_ref, ys_hbm,
                   xbuf_ref, ybuf_ref, wgub_ref, wdb_ref, in_sems, out_sems):
    i = pl.program_id(0)
    n_items = pl.num_programs(0)
    bf16 = jnp.bfloat16
    SB = EXPERT_SUB
    n_blocks = xs_hbm.shape[0] // EXPERT_BLOCK
    blk = blk_ref[i]
    lo = lo_ref[i]
    hi = hi_ref[i]

    class _Group:
        def __init__(self, copies):
            self.copies = copies

        def start(self):
            for cp in self.copies:
                cp.start()

        def wait(self):
            for cp in self.copies:
                cp.wait()

    def hbm_chunk(ref, b, c):
        start = b * EXPERT_BLOCK
        start = start if isinstance(start, int) else pl.multiple_of(start, EXPERT_BLOCK)
        return ref.at[pl.ds(start, EXPERT_BLOCK), c, :]

    def vmem_chunk(ref, slot, c):
        start = slot * EXPERT_BLOCK
        start = start if isinstance(start, int) else pl.multiple_of(start, EXPERT_BLOCK)
        return ref.at[pl.ds(start, EXPERT_BLOCK), pl.ds(c * LANES, LANES)]

    def in_copy(b, slot):
        return _Group([pltpu.make_async_copy(hbm_chunk(xs_hbm, b, c), vmem_chunk(xbuf_ref, slot, c), in_sems.at[slot])
                       for c in range(ROW_SUB)])

    def out_copy(b, slot):
        return _Group([pltpu.make_async_copy(vmem_chunk(ybuf_ref, slot, c), hbm_chunk(ys_hbm, b, c), out_sems.at[slot])
                       for c in range(ROW_SUB)])

    @pl.when(i == 0)
    def _():
        for b0 in range(EXPERT_IN_SLOTS - 1):
            in_copy(b0, b0).start()

    @pl.when(newblk_ref[i] == 1)
    def _():
        ahead = blk + (EXPERT_IN_SLOTS - 1)

        @pl.when(ahead < n_blocks)
        def _():
            in_copy(ahead, lax.rem(ahead, EXPERT_IN_SLOTS)).start()

        in_copy(blk, lax.rem(blk, EXPERT_IN_SLOTS)).wait()

        @pl.when(blk >= EXPERT_OUT_SLOTS)
        def _():
            out_copy(blk - EXPERT_OUT_SLOTS, lax.rem(blk, EXPERT_OUT_SLOTS)).wait()

    @pl.when(newexp_ref[i] == 1)
    def _():
        wgub_ref[:, 0:D_EXPERT] = wg_ref[0].astype(bf16)
        wgub_ref[:, D_EXPERT:2 * D_EXPERT] = wu_ref[0].astype(bf16)
        wdb_ref[...] = wd_ref[0].astype(bf16)

    xbase = pl.multiple_of(lax.rem(blk, EXPERT_IN_SLOTS) * EXPERT_BLOCK, EXPERT_BLOCK)
    ybase = pl.multiple_of(lax.rem(blk, EXPERT_OUT_SLOTS) * EXPERT_BLOCK, EXPERT_BLOCK)

    def run(r0, nrows, merge_first):
        xb = xbuf_ref[pl.ds(xbase + r0, nrows), :].astype(bf16)
        gu = _dot(xb, wgub_ref[...])
        g = gu[:, 0:D_EXPERT]
        u = gu[:, D_EXPERT:2 * D_EXPERT]
        h = (g * jax.nn.sigmoid(g) * u).astype(bf16)
        y = _dot(h, wdb_ref[...])
        first = SB if merge_first else 0
        if merge_first:
            rows = r0 + lax.broadcasted_iota(jnp.int32, (SB, D_MODEL), 0)
            keep = jnp.logical_and(rows >= lo, rows < hi)
            old = ybuf_ref[pl.ds(ybase + r0, SB), :]
            ybuf_ref[pl.ds(ybase + r0, SB), :] = jnp.where(keep, y[0:SB], old)
        if nrows > first:
            ybuf_ref[pl.ds(ybase + r0 + first, nrows - first), :] = y[first:nrows]

    n_sub = EXPERT_BLOCK // SB
    from_top = lo == 0
    sub_aligned = (lo & (SB - 1)) == 0
    to_end = jnp.logical_and(jnp.logical_and(lo > 0, hi == EXPERT_BLOCK), jnp.logical_not(sub_aligned))
    subs_used = (hi + SB - 1) // SB
    first_sub = lo // SB
    for k in range(1, n_sub + 1):
        @pl.when(jnp.logical_and(from_top, subs_used == k))
        def _(k=k):
            run(0, k * SB, False)

    for q in range(n_sub):
        @pl.when(jnp.logical_and(to_end, first_sub == q))
        def _(q=q):
            run(q * SB, EXPERT_BLOCK - q * SB, True)

    for q in range(n_sub):
        r0 = q * SB
        inside = jnp.logical_and(lo > 0, jnp.logical_or(hi < EXPERT_BLOCK, sub_aligned))

        @pl.when(jnp.logical_and(inside, jnp.minimum(hi, r0 + SB) > jnp.maximum(lo, r0)))
        def _(r0=r0):
            @pl.when(lo <= r0)
            def _():
                run(r0, SB, False)

            @pl.when(lo > r0)
            def _():
                run(r0, SB, True)

    @pl.when(endblk_ref[i] == 1)
    def _():
        out_copy(blk, lax.rem(blk, EXPERT_OUT_SLOTS)).start()

    @pl.when(i == n_items - 1)
    def _():
        for b1 in range(n_blocks - EXPERT_OUT_SLOTS, n_blocks):
            out_copy(b1, b1 % EXPERT_OUT_SLOTS).wait()


def _expert_call(item_blk, item_exp, item_lo, item_hi, item_newexp, item_newblk, item_endblk,
                 xs_rp, w_eg, w_eu, w_ed):
    D = D_MODEL
    BS = EXPERT_BLOCK * ROW_SUB
    n_items = item_blk.shape[0]
    assert xs_rp.shape[0] % BS == 0 and xs_rp.shape[0] // BS >= max(EXPERT_IN_SLOTS, EXPERT_OUT_SLOTS)
    xs3 = xs_rp.reshape(xs_rp.shape[0] // ROW_SUB, ROW_SUB, LANES)

    def w_map(i, blk, exp, lo, hi, newexp, newblk, endblk):
        return (exp[i], 0, 0)

    grid_spec = pltpu.PrefetchScalarGridSpec(
        num_scalar_prefetch=7,
        grid=(n_items,),
        in_specs=[
            pl.BlockSpec(memory_space=pl.ANY),
            pl.BlockSpec((1, D, D_EXPERT), w_map),
            pl.BlockSpec((1, D, D_EXPERT), w_map),
            pl.BlockSpec((1, D_EXPERT, D), w_map),
        ],
        out_specs=pl.BlockSpec(memory_space=pl.ANY),
        scratch_shapes=[
            pltpu.VMEM((EXPERT_IN_SLOTS * EXPERT_BLOCK, D), jnp.float32),
            pltpu.VMEM((EXPERT_OUT_SLOTS * EXPERT_BLOCK, D), jnp.float32),
            pltpu.VMEM((D, 2 * D_EXPERT), jnp.bfloat16),
            pltpu.VMEM((D_EXPERT, D), jnp.bfloat16),
            pltpu.SemaphoreType.DMA((EXPERT_IN_SLOTS,)),
            pltpu.SemaphoreType.DMA((EXPERT_OUT_SLOTS,)),
        ],
    )
    return pl.pallas_call(
        _expert_kernel,
        grid_spec=grid_spec,
        out_shape=jax.ShapeDtypeStruct(xs3.shape, jnp.float32),
        compiler_params=pltpu.CompilerParams(
            dimension_semantics=("arbitrary",), vmem_limit_bytes=VMEM_LIMIT),
        name="moe_experts",
    )(item_blk, item_exp, item_lo, item_hi, item_newexp, item_newblk, item_endblk, xs3, w_eg, w_eu, w_ed
      ).reshape(xs_rp.shape)


def _combine_kernel(lsrc_ref, gdst_ref, cnt_ref, ld_ref, w_ref, x2_hbm, ys_ref, out_hbm,
                    stag_ref, acc_ref, sems, in_sems, out_sems):
    b = pl.program_id(0)
    nb = pl.num_programs(0)
    T = MIX_TILE
    slot = b & 1
    base = slot * TILE_ROWS
    aslot = lax.rem(b, COMBINE_ACC_SLOTS)

    def in_copies(tile):
        s_ = lax.rem(tile, COMBINE_ACC_SLOTS)
        return _tile_relayout_copies(x2_hbm, tile, acc_ref, s_, in_sems.at[s_], True)

    def out_copies(tile):
        s_ = lax.rem(tile, COMBINE_ACC_SLOTS)
        return _tile_relayout_copies(out_hbm, tile, acc_ref, s_, out_sems.at[s_], False)

    @pl.when(b >= COMBINE_ACC_SLOTS - 1)
    def _():
        for cp in out_copies(b - (COMBINE_ACC_SLOTS - 1)):
            cp.wait()

    @pl.when(b == 0)
    def _():
        for cp in in_copies(b):
            cp.start()

    @pl.when(b + 1 < nb)
    def _():
        for cp in in_copies(b + 1):
            cp.start()

    def fetch(tile, slot_):
        def seg(e, c):
            s = tile * N_EXPERTS + e
            _segment_copies(ys_ref, gdst_ref[s], stag_ref, slot_ * TILE_ROWS + lsrc_ref[s], cnt_ref[s],
                            sems.at[slot_])
            return c

        lax.fori_loop(0, N_EXPERTS, seg, 0, unroll=4)

    @pl.when(b == 0)
    def _():
        fetch(b, slot)

    @pl.when(b + 1 < nb)
    def _():
        fetch(b + 1, 1 - slot)

    _wait_tile(ys_ref, stag_ref, base, sems.at[slot], False)
    for cp in in_copies(b):
        cp.wait()

    def tok(t, c):
        a = acc_ref[aslot * T + t]
        for kk_ in range(TOP_K):
            r8 = ld_ref[0, 0, kk_ * T + t]
            w = w_ref[0, 0, kk_ * T + t]
            a = a + w * stag_ref[pl.ds(pl.multiple_of(r8, ROW_SUB), ROW_SUB), :]
        acc_ref[aslot * T + t] = a
        return c

    lax.fori_loop(0, T, tok, 0, unroll=32)
    for cp in out_copies(b):
        cp.start()

    @pl.when(b == nb - 1)
    def _():
        for back in range(COMBINE_ACC_SLOTS - 2, -1, -1):
            @pl.when(b >= back)
            def _(back=back):
                for cp in out_copies(b - back):
                    cp.wait()


def _combine_call(lsrc, gdst, cnt, ld_t, w_t, x2, ys_rp):
    N, D = x2.shape
    nt = ld_t.shape[0]
    T = MIX_TILE
    smem = pl.BlockSpec(memory_space=pltpu.SMEM)
    smem_blk = pl.BlockSpec((1, 1, TILE_ROWS), lambda i: (i, 0, 0), memory_space=pltpu.SMEM)
    return pl.pallas_call(
        _combine_kernel,
        grid=(nt,),
        in_specs=[
            smem, smem, smem, smem_blk, smem_blk,
            pl.BlockSpec(memory_space=pl.ANY),
            pl.BlockSpec(memory_space=pl.ANY),
        ],
        out_specs=pl.BlockSpec(memory_space=pl.ANY),
        out_shape=jax.ShapeDtypeStruct((N, D), jnp.float32),
        scratch_shapes=[
            pltpu.VMEM((2 * TILE_ROWS * ROW_SUB, LANES), jnp.float32),
            pltpu.VMEM((COMBINE_ACC_SLOTS * T, ROW_SUB, LANES), jnp.float32),
            pltpu.SemaphoreType.DMA((2,)),
            pltpu.SemaphoreType.DMA((COMBINE_ACC_SLOTS,)),
            pltpu.SemaphoreType.DMA((COMBINE_ACC_SLOTS,)),
        ],
        compiler_params=pltpu.CompilerParams(
            dimension_semantics=("arbitrary",), vmem_limit_bytes=VMEM_LIMIT),
        name="moe_combine",
    )(lsrc, gdst, cnt, ld_t, w_t, x2, ys_rp)


def _work_items(gstart, totals, n_items):
    BM = EXPERT_BLOCK
    i32 = jnp.int32
    gend = gstart + totals
    first_blk = gstart // BM
    last_blk = jnp.maximum(gend - 1, gstart) // BM
    n_e = jnp.where(totals > 0, last_blk - first_blk + 1, 0)
    item_end = jnp.cumsum(n_e)
    item_start = item_end - n_e
    n_real = item_end[-1]
    j = jnp.minimum(jnp.arange(n_items, dtype=i32), n_real - 1)
    exp = jnp.sum((item_end[None, :] <= j[:, None]).astype(i32), axis=1)
    onehot = exp[:, None] == jnp.arange(N_EXPERTS, dtype=i32)[None, :]

    def of_exp(v):
        return jnp.sum(jnp.where(onehot, v[None, :], 0), axis=1)

    blk = of_exp(first_blk) + (j - of_exp(item_start))
    lo = jnp.maximum(of_exp(gstart), blk * BM) - blk * BM
    hi = jnp.minimum(of_exp(gend), (blk + 1) * BM) - blk * BM
    real = jnp.arange(n_items, dtype=i32) < n_real
    hi = jnp.where(real, hi, lo)
    prev_exp = jnp.concatenate([jnp.full((1,), -1, i32), exp[:-1]])
    prev_blk = jnp.concatenate([jnp.full((1,), -1, i32), blk[:-1]])
    next_blk = jnp.concatenate([blk[1:], jnp.full((1,), -1, i32)])
    is_last = jnp.arange(n_items, dtype=i32) == n_real - 1
    newexp = jnp.logical_and(real, exp != prev_exp).astype(i32)
    newblk = jnp.logical_and(real, blk != prev_blk).astype(i32)
    endblk = jnp.logical_and(real, jnp.logical_or(blk != next_blk, is_last)).astype(i32)
    return blk.astype(i32), exp.astype(i32), lo.astype(i32), hi.astype(i32), newexp, newblk, endblk


def _layer(x, mix_norm_w, w_in, conv_w, q_norm_w, k_norm_w, sinks, conv_out_norm_w,
           attn_out_norm_w, w_out, ffn_norm_w, w_router, router_bias, w_eg, w_eu, w_ed,
           w_sg, w_su, w_sd):
    B, S, D = x.shape
    N = B * S
    T = MIX_TILE
    nt = N // T
    bf16 = jnp.bfloat16
    f32 = jnp.float32
    i32 = jnp.int32

    wr_t = w_router.astype(f32).T
    wr_hi = wr_t.astype(bf16)
    wr_hl = jnp.concatenate([wr_hi, (wr_t - wr_hi.astype(f32)).astype(bf16)], axis=0)
    sink_col = jnp.broadcast_to(
        jnp.repeat(sinks.astype(f32), BLK).reshape(N_KV_HEADS, GQA_GROUP * BLK, 1),
        (N_KV_HEADS, GQA_GROUP * BLK, LANES))
    tri = jnp.asarray(np.triu(np.ones((T, T), np.float32), k=1), dtype=bf16)
    head_ones = jnp.asarray(np.kron(np.eye(2 * LANES // HEAD_DIM, dtype=np.float32),
                                    np.ones((HEAD_DIM, HEAD_DIM), np.float32)), dtype=bf16)
    bias_tab = jnp.asarray(_attn_bias_table())

    x2, xn, wts, ld, cnt = _mixer_call(
        x, mix_norm_w.reshape(1, D), w_in.astype(bf16), conv_w,
        jnp.tile(q_norm_w, N_HEADS).reshape(1, D_ATTN), jnp.tile(k_norm_w, N_KV_HEADS).reshape(1, D_KV),
        sink_col, bias_tab, conv_out_norm_w.reshape(1, D_CONV), attn_out_norm_w.reshape(1, D_ATTN),
        w_out.astype(bf16), ffn_norm_w.reshape(1, D), w_sg.astype(bf16), w_su.astype(bf16),
        w_sd.astype(bf16), wr_hl, router_bias.astype(f32).reshape(N_EXPERTS, 1), tri, head_ones)

    cnt_te = cnt[:, 0].reshape(nt, N_EXPERTS)
    lsrc = jnp.cumsum(cnt_te, axis=1) - cnt_te
    before = jnp.cumsum(cnt_te, axis=0) - cnt_te
    totals = jnp.sum(cnt_te, axis=0)
    gstart = jnp.cumsum(totals) - totals
    gdst = gstart[None, :] + before
    n_items = (N * TOP_K) // EXPERT_BLOCK + N_EXPERTS
    items = _work_items(gstart.astype(i32), totals.astype(i32), n_items)

    def tile_major(a):
        return a.reshape(TOP_K, nt, T).transpose(1, 0, 2).reshape(nt, 1, TILE_ROWS)

    ld_t = tile_major(ld)
    seg_tabs = (lsrc.reshape(-1).astype(i32), gdst.reshape(-1).astype(i32), cnt_te.reshape(-1).astype(i32))
    xs_rp = _dispatch_call(*seg_tabs, ld_t, xn)
    ys_rp = _expert_call(*items, xs_rp, w_eg, w_eu, w_ed)
    out = _combine_call(*seg_tabs, ld_t, tile_major(wts), x2, ys_rp)
    return out.reshape(B, S, D)


def kernel(x, mix_norm_w, w_in, conv_w, q_norm_w, k_norm_w, sinks, conv_out_norm_w, attn_out_norm_w, w_out, ffn_norm_w, w_router, router_bias, w_exp_gate, w_exp_up, w_exp_down, w_sh_gate, w_sh_up, w_sh_down):
    B, S, D = x.shape
    assert D == D_MODEL and w_in.shape == (1, D_MODEL, D_IN_PROJ) and w_exp_gate.shape == (1, N_EXPERTS, D_MODEL, D_EXPERT)
    assert S % MIX_TILE == 0 and (B * S * TOP_K) % EXPERT_BLOCK == 0
    return _layer(x, mix_norm_w[0], w_in[0], conv_w[0], q_norm_w[0], k_norm_w[0], sinks[0],
                  conv_out_norm_w[0], attn_out_norm_w[0], w_out[0], ffn_norm_w[0], w_router[0],
                  router_bias[0], w_exp_gate[0], w_exp_up[0], w_exp_down[0], w_sh_gate[0],
                  w_sh_up[0], w_sh_down[0])
```

```python
import numpy as np
import jax
import jax.numpy as jnp
from jax import lax
from jax.experimental import pallas as pl
from jax.experimental.pallas import tpu as pltpu

D_MODEL = 1024
EPS = 1e-6
D_CONV = 512
HEAD_DIM = 64
N_HEADS = 8
N_KV_HEADS = 2
GQA_GROUP = 4
D_ATTN = 512
D_KV = 128
WINDOW = 128
BLK = 128
D_IN_PROJ = 2304
N_EXPERTS = 64
TOP_K = 8
N_GROUPS = 8
GROUP_SIZE = 8
TOPK_GROUPS = 4
D_EXPERT = 256
ROUTED_SCALE = 2.5

LANES = 128
ROW_SUB = 8
MIX_TILE = 512
TILE_ROWS = TOP_K * MIX_TILE
EXPERT_BLOCK = 1024
EXPERT_SUB = 128
EXPERT_IN_SLOTS = 4
EXPERT_OUT_SLOTS = 3
COMBINE_ACC_SLOTS = 3
SEG_CHUNK_LOG2 = 5
NEG_BIG = -1e30
VMEM_LIMIT = 58 * 1024 * 1024


def _alibi_slopes():
    return np.array([2.0 ** (-8.0 * (h + 1) / N_HEADS) for h in range(N_HEADS)], dtype=np.float32)


def _attn_bias_table():
    qi = np.arange(BLK)[:, None]
    kj = np.arange(2 * BLK)[None, :]
    dist = qi - kj + BLK
    inwin = (dist >= 0) & (dist < WINDOW)
    slopes = _alibi_slopes()
    out = np.zeros((N_KV_HEADS, GQA_GROUP * BLK, 2 * BLK), np.float32)
    for g in range(N_KV_HEADS):
        for i in range(GQA_GROUP):
            h = g * GQA_GROUP + i
            out[g, i * BLK:(i + 1) * BLK] = np.where(inwin, -slopes[h] * dist.astype(np.float32), NEG_BIG)
    return out


def _rms(x, w):
    ms = jnp.mean(x * x, axis=-1, keepdims=True)
    return x * lax.rsqrt(ms + EPS) * w


def _head_rms(x, w_tiled, head_ones):
    C = x.shape[1]
    W = min(C, 2 * LANES)
    sq = x * x
    hi = sq.astype(jnp.bfloat16)
    lo = (sq - hi.astype(jnp.float32)).astype(jnp.bfloat16)
    ones = head_ones[0:W, 0:W]
    ssq = jnp.concatenate(
        [_dot(hi[:, c:c + W], ones) + _dot(lo[:, c:c + W], ones) for c in range(0, C, W)], axis=1)
    return x * lax.rsqrt(ssq * (1.0 / HEAD_DIM) + EPS) * w_tiled


def _dot(a, b):
    return jnp.dot(a, b, preferred_element_type=jnp.float32)


def _dot_nt(a, b):
    return lax.dot_general(a, b, (((1,), (1,)), ((), ())), preferred_element_type=jnp.float32)


def _mixer_kernel(x_ref, mixw_ref, win_ref, convw_ref, qw_ref, kw_ref, sink_ref, bias_ref,
                  cnw_ref, anw_ref, wout_ref, fnw_ref, wsg_ref, wsu_ref, wsd_ref,
                  wrhl_ref, rbias_ref, tri_ref, hones_ref,
                  x2_ref, xn_ref, wts_ref, ld_ref, cnt_ref,
                  kc_ref, vc_ref, cc_ref):
    j = pl.program_id(1)
    T = MIX_TILE
    bf16 = jnp.bfloat16
    f32 = jnp.float32

    @pl.when(j == 0)
    def _():
        kc_ref[...] = jnp.zeros_like(kc_ref)
        vc_ref[...] = jnp.zeros_like(vc_ref)
        cc_ref[...] = jnp.zeros_like(cc_ref)

    x = x_ref[0]
    u = _rms(x, mixw_ref[...]).astype(bf16)
    qkv = _dot(u, win_ref[:, 3 * D_CONV:D_IN_PROJ])
    q = qkv[:, 0:D_ATTN]
    k = qkv[:, D_ATTN:D_ATTN + D_KV]
    v = qkv[:, D_ATTN + D_KV:D_ATTN + 2 * D_KV]
    conv_piece = 2 * LANES
    conv_pieces = []

    lane = lax.broadcasted_iota(jnp.int32, (1, LANES), 1)
    lane_lo = lane < HEAD_DIM
    qn = _head_rms(q, qw_ref[...], hones_ref) * (HEAD_DIM ** -0.5)
    kn = _head_rms(k, kw_ref[...], hones_ref)
    kfull = jnp.concatenate([kc_ref[...], kn], axis=0)
    vfull = jnp.concatenate([vc_ref[...], v], axis=0)
    kc_ref[...] = kn[T - BLK:T, :]
    vc_ref[...] = v[T - BLK:T, :]

    def _rep(a, g):
        r = pltpu.roll(a, HEAD_DIM, axis=1)
        two = jnp.where(lane_lo, a, r) if g == 0 else jnp.where(lane_lo, r, a)
        return jnp.concatenate([two, two], axis=1).astype(bf16)

    k_rep = [_rep(kfull, g) for g in range(N_KV_HEADS)]
    v_rep = [_rep(vfull, g) for g in range(N_KV_HEADS)]

    lane256 = lax.broadcasted_iota(jnp.int32, (1, 2 * LANES), 1)
    head_of_lane = lane256 // HEAD_DIM
    first_f = (j == 0).astype(f32)
    prev_key_mask = jnp.where(lane256 < BLK, first_f * NEG_BIG, 0.0)
    ones_cols = jnp.ones((2 * BLK, LANES), bf16)

    def lanes2(a):
        return jnp.concatenate([a, a], axis=1)

    attn_rows = []
    for i in range(T // BLK):
        grp_out = []
        for g in range(N_KV_HEADS):
            if len(conv_pieces) * conv_piece < 3 * D_CONV:
                c0 = len(conv_pieces) * conv_piece
                conv_pieces.append(_dot(u, win_ref[:, c0:c0 + conv_piece]))
            qg = qn[i * BLK:(i + 1) * BLK, g * 256:(g + 1) * 256]
            qst = jnp.concatenate(
                [jnp.where(head_of_lane == hi, qg, 0.0) for hi in range(GQA_GROUP)], axis=0).astype(bf16)
            kk = k_rep[g][i * BLK:i * BLK + 2 * BLK, :]
            vv = jnp.concatenate([v_rep[g][i * BLK:i * BLK + 2 * BLK, :], ones_cols], axis=1)
            s = _dot_nt(qst, kk) + bias_ref[g]
            if i == 0:
                s = s + prev_key_mask
            sink = sink_ref[g]
            m = jnp.maximum(jnp.max(s, axis=-1, keepdims=True), sink)
            e = jnp.exp(s - lanes2(m)).astype(bf16)
            r = _dot(e, vv)
            inv = 1.0 / (r[:, 2 * LANES:3 * LANES] + jnp.exp(sink - m))
            r = r[:, 0:2 * LANES] * lanes2(inv)
            o = jnp.where(head_of_lane == 0, r[0:BLK], 0.0)
            for hi in range(1, GQA_GROUP):
                o = jnp.where(head_of_lane == hi, r[hi * BLK:(hi + 1) * BLK], o)
            grp_out.append(o)
        attn_rows.append(jnp.concatenate(grp_out, axis=1))
    y_attn = jnp.concatenate(attn_rows, axis=0)

    assert len(conv_pieces) * conv_piece == 3 * D_CONV
    conv_proj = jnp.concatenate(conv_pieces, axis=1)
    b_gate = conv_proj[:, 0:D_CONV]
    c_gate = conv_proj[:, D_CONV:2 * D_CONV]
    hh = conv_proj[:, 2 * D_CONV:3 * D_CONV]
    ch = c_gate * hh
    prev = cc_ref[...]
    p6 = prev[6:7, :]
    p7 = prev[7:8, :]
    row = lax.broadcasted_iota(jnp.int32, (T, D_CONV), 0)
    ch_m1 = jnp.where(row == 0, p7, pltpu.roll(ch, 1, axis=0))
    ch_m2 = jnp.where(row == 0, p6, jnp.where(row == 1, p7, pltpu.roll(ch, 2, axis=0)))
    cw = convw_ref[...]
    y_conv = b_gate * (cw[0:1, :] * ch_m2 + cw[1:2, :] * ch_m1 + cw[2:3, :] * ch)
    cc_ref[...] = ch[T - 8:T, :]

    y_mix = jnp.concatenate([_rms(y_conv, cnw_ref[...]), _rms(y_attn, anw_ref[...])], axis=1)
    x1 = x + _dot(y_mix.astype(bf16), wout_ref[...])

    xn = _rms(x1, fnw_ref[...])
    xn_ref[...] = xn
    xh = xn.astype(bf16)

    xl = (xn - xh.astype(f32)).astype(bf16)
    w_hl = wrhl_ref[...]
    a_hl = _dot_nt(w_hl, xh)
    logits = a_hl[0:N_EXPERTS] + a_hl[N_EXPERTS:2 * N_EXPERTS] + _dot_nt(w_hl[0:N_EXPERTS], xl)

    gs = _dot(xh, wsg_ref[...])
    us = _dot(xh, wsu_ref[...])
    hs = (gs * jax.nn.sigmoid(gs) * us).astype(bf16)
    x2 = x1 + _dot(hs, wsd_ref[...])
    x2_ref[...] = x2

    scores = jax.nn.sigmoid(logits)
    biased = scores + rbias_ref[...]

    sub8 = lax.broadcasted_iota(jnp.int32, (GROUP_SIZE, T), 0)
    gscore = []
    for g in range(N_GROUPS):
        blk = biased[g * GROUP_SIZE:(g + 1) * GROUP_SIZE, :]
        m1 = jnp.max(blk, axis=0, keepdims=True)
        first = jnp.min(jnp.where(blk == m1, sub8, GROUP_SIZE), axis=0, keepdims=True)
        m2 = jnp.max(jnp.where(sub8 == first, -jnp.inf, blk), axis=0, keepdims=True)
        gscore.append(m1 + m2)
    masked_blocks = []
    for g in range(N_GROUPS):
        rank = jnp.zeros((1, T), jnp.int32)
        for o_ in range(N_GROUPS):
            if o_ == g:
                continue
            if o_ < g:
                ahead = gscore[o_] >= gscore[g]
            else:
                ahead = gscore[o_] > gscore[g]
            rank = rank + ahead.astype(jnp.int32)
        keep = rank < TOPK_GROUPS
        blk = biased[g * GROUP_SIZE:(g + 1) * GROUP_SIZE, :]
        masked_blocks.append(jnp.where(keep, blk, -jnp.inf))
    cur = jnp.concatenate(masked_blocks, axis=0)

    eiota = lax.broadcasted_iota(jnp.int32, (N_EXPERTS, T), 0)
    row8 = lax.broadcasted_iota(jnp.int32, (TOP_K, T), 0)
    w_out = jnp.zeros((TOP_K, T), f32)
    sel_dense = jnp.zeros((N_EXPERTS, T), f32)
    onehots = []
    for kk_ in range(TOP_K):
        mx = jnp.max(cur, axis=0, keepdims=True)
        sel_idx = jnp.min(jnp.where(cur == mx, eiota, N_EXPERTS), axis=0, keepdims=True)
        onehot = eiota == sel_idx
        w_k = jnp.sum(jnp.where(onehot, scores, 0.0), axis=0, keepdims=True)
        cur = jnp.where(onehot, -jnp.inf, cur)
        sel_dense = jnp.where(onehot, 1.0, sel_dense)
        w_out = jnp.where(row8 == kk_, w_k, w_out)
        onehots.append(onehot)
    wsum = jnp.sum(w_out, axis=0, keepdims=True)
    wts_ref[...] = w_out / wsum * ROUTED_SCALE

    cum = _dot(sel_dense.astype(bf16), tri_ref[...])
    cnt = jnp.broadcast_to(jnp.sum(sel_dense, axis=1, keepdims=True), (N_EXPERTS, LANES))
    erow = lax.broadcasted_iota(jnp.int32, (N_EXPERTS, LANES), 0)
    incl = cnt
    step = 1
    while step < N_EXPERTS:
        incl = incl + jnp.where(erow >= step, pltpu.roll(incl, step, axis=0), 0.0)
        step *= 2
    lstart = incl - cnt
    ld_dense = lstart[:, 0:1] + cum
    ld_out = jnp.zeros((TOP_K, T), f32)
    for kk_ in range(TOP_K):
        p_k = jnp.sum(jnp.where(onehots[kk_], ld_dense, 0.0), axis=0, keepdims=True)
        ld_out = jnp.where(row8 == kk_, p_k, ld_out)
    tile_idx = pl.program_id(0) * pl.num_programs(1) + j
    slot_base = ((tile_idx & 1) * TILE_ROWS).astype(f32)
    ld_ref[...] = ((ld_out + slot_base) * ROW_SUB).astype(jnp.int32)
    cnt_ref[...] = cnt.astype(jnp.int32)


def _mixer_call(x, mixw, win, convw, qw, kw, sink_col, bias_tab, cnw, anw, wout, fnw,
                wsg, wsu, wsd, wrhl, rbias, tri, head_ones):
    B, S, D = x.shape
    T = MIX_TILE
    nt = S // T
    N = B * S

    def full(a):
        nd = a.ndim
        return pl.BlockSpec(a.shape, lambda b, j, _nd=nd: (0,) * _nd)

    tok_spec = pl.BlockSpec((TOP_K, T), lambda b, j: (0, b * nt + j))
    in_arrays = [mixw, win, convw, qw, kw, sink_col, bias_tab, cnw, anw, wout, fnw,
                 wsg, wsu, wsd, wrhl, rbias, tri, head_ones]
    return pl.pallas_call(
        _mixer_kernel,
        grid=(B, nt),
        in_specs=[pl.BlockSpec((1, T, D), lambda b, j: (b, j, 0))] + [full(a) for a in in_arrays],
        out_specs=[
            pl.BlockSpec((T, D), lambda b, j: (b * nt + j, 0)),
            pl.BlockSpec((T, D), lambda b, j: (b * nt + j, 0)),
            tok_spec, tok_spec,
            pl.BlockSpec((N_EXPERTS, LANES), lambda b, j: (b * nt + j, 0)),
        ],
        out_shape=[
            jax.ShapeDtypeStruct((N, D), jnp.float32),
            jax.ShapeDtypeStruct((N, D), jnp.float32),
            jax.ShapeDtypeStruct((TOP_K, N), jnp.float32),
            jax.ShapeDtypeStruct((TOP_K, N), jnp.int32),
            jax.ShapeDtypeStruct((B * nt * N_EXPERTS, LANES), jnp.int32),
        ],
        scratch_shapes=[
            pltpu.VMEM((BLK, D_KV), jnp.float32),
            pltpu.VMEM((BLK, D_KV), jnp.float32),
            pltpu.VMEM((8, D_CONV), jnp.float32),
        ],
        compiler_params=pltpu.CompilerParams(
            dimension_semantics=("arbitrary", "arbitrary"), vmem_limit_bytes=VMEM_LIMIT),
        name="mixer_router",
    )(x, *in_arrays)


def _rows(ref, row, nrows):
    return ref.at[pl.ds(pl.multiple_of(row * ROW_SUB, ROW_SUB), nrows * ROW_SUB)]


def _segment_copies(src, src_row, dst, dst_row, count, sem):
    chunk = 1 << SEG_CHUNK_LOG2

    def big(i, c):
        o = i * chunk
        pltpu.make_async_copy(_rows(src, src_row + o, chunk), _rows(dst, dst_row + o, chunk), sem).start()
        return c

    lax.fori_loop(0, count >> SEG_CHUNK_LOG2, big, 0)
    for bit in range(SEG_CHUNK_LOG2 - 1, -1, -1):
        o = (count >> (bit + 1)) << (bit + 1)

        @pl.when(((count >> bit) & 1) == 1)
        def _():
            n = 1 << bit
            pltpu.make_async_copy(_rows(src, src_row + o, n), _rows(dst, dst_row + o, n), sem).start()


def _wait_tile(hbm_ref, stag_ref, slot_row, sem, to_hbm):
    vm = _rows(stag_ref, slot_row, TILE_ROWS)
    hb = _rows(hbm_ref, 0, TILE_ROWS)
    (pltpu.make_async_copy(vm, hb, sem) if to_hbm else pltpu.make_async_copy(hb, vm, sem)).wait()


def _tile_relayout_copies(std_hbm, tile, rpt_ref, slot, sem, to_vmem):
    T = MIX_TILE
    row0 = tile * T
    row0 = row0 if isinstance(row0, int) else pl.multiple_of(row0, T)
    slot0 = slot * T
    slot0 = slot0 if isinstance(slot0, int) else pl.multiple_of(slot0, T)
    copies = []
    for c in range(ROW_SUB):
        hb = std_hbm.at[pl.ds(row0, T), pl.ds(c * LANES, LANES)]
        vm = rpt_ref.at[pl.ds(slot0, T), c, :]
        copies.append(pltpu.make_async_copy(hb, vm, sem) if to_vmem else pltpu.make_async_copy(vm, hb, sem))
    return copies


def _dispatch_kernel(lsrc_ref, gdst_ref, cnt_ref, ld_ref, xn_hbm, xs_ref, stag_ref, xin_ref, sems, in_sems):
    b = pl.program_id(0)
    nb = pl.num_programs(0)
    T = MIX_TILE
    slot = b & 1
    base = slot * TILE_ROWS

    def fetch_rows(tile, slot_):
        return _tile_relayout_copies(xn_hbm, tile, xin_ref, slot_, in_sems.at[slot_], True)

    @pl.when(b == 0)
    def _():
        for cp in fetch_rows(b, slot):
            cp.start()

    @pl.when(b + 1 < nb)
    def _():
        for cp in fetch_rows(b + 1, 1 - slot):
            cp.start()

    for cp in fetch_rows(b, slot):
        cp.wait()

    for kk_ in range(TOP_K):
        def row(t, c, kk_=kk_):
            r8 = ld_ref[0, 0, kk_ * T + t]
            stag_ref[pl.ds(pl.multiple_of(r8, ROW_SUB), ROW_SUB), :] = xin_ref[slot * T + t]
            return c

        lax.fori_loop(0, T, row, 0, unroll=128)

    def seg(e, c):
        s = b * N_EXPERTS + e
        _segment_copies(stag_ref, base + lsrc_ref[s], xs_ref, gdst_ref[s], cnt_ref[s], sems.at[slot])
        return c

    lax.fori_loop(0, N_EXPERTS, seg, 0, unroll=4)

    @pl.when(b > 0)
    def _():
        _wait_tile(xs_ref, stag_ref, (1 - slot) * TILE_ROWS, sems.at[1 - slot], True)

    @pl.when(b == nb - 1)
    def _():
        _wait_tile(xs_ref, stag_ref, base, sems.at[slot], True)


def _dispatch_call(lsrc, gdst, cnt, ld_t, xn):
    nt = ld_t.shape[0]
    T = MIX_TILE
    smem = pl.BlockSpec(memory_space=pltpu.SMEM)
    return pl.pallas_call(
        _dispatch_kernel,
        grid=(nt,),
        in_specs=[
            smem, smem, smem,
            pl.BlockSpec((1, 1, TILE_ROWS), lambda i: (i, 0, 0), memory_space=pltpu.SMEM),
            pl.BlockSpec(memory_space=pl.ANY),
        ],
        out_specs=pl.BlockSpec(memory_space=pl.ANY),
        out_shape=jax.ShapeDtypeStruct((nt * TILE_ROWS * ROW_SUB, LANES), jnp.float32),
        scratch_shapes=[pltpu.VMEM((2 * TILE_ROWS * ROW_SUB, LANES), jnp.float32),
                        pltpu.VMEM((2 * T, ROW_SUB, LANES), jnp.float32),
                        pltpu.SemaphoreType.DMA((2,)),
                        pltpu.SemaphoreType.DMA((2,))],
        compiler_params=pltpu.CompilerParams(
            dimension_semantics=("arbitrary",), vmem_limit_bytes=VMEM_LIMIT),
        name="moe_dispatch",
    )(lsrc, gdst, cnt, ld_t, xn)


def _expert_kernel(blk_ref, exp_ref, lo_ref, hi_ref, newexp_ref, newblk_ref, endblk_ref,
                   xs_hbm, wg_ref, wu_ref, wd_ref, ys_hbm,
                   xbuf_ref, ybuf_ref, wgub_ref, wdb_ref, in_sems, out_sems):
    i = pl.program_id(0)
    n_items = pl.num_programs(0)
    bf16 = jnp.bfloat16
    SB = EXPERT_SUB
    n_blocks = xs_hbm.shape[0] // EXPERT_BLOCK
    blk = blk_ref[i]
    lo = lo_ref[i]
    hi = hi_ref[i]

    class _Group:
        def __init__(self, copies):
            self.copies = copies

        def start(self):
            for cp in self.copies:
                cp.start()

        def wait(self):
            for cp in self.copies:
                cp.wait()

    def hbm_chunk(ref, b, c):
        start = b * EXPERT_BLOCK
        start = start if isinstance(start, int) else pl.multiple_of(start, EXPERT_BLOCK)
        return ref.at[pl.ds(start, EXPERT_BLOCK), c, :]

    def vmem_chunk(ref, slot, c):
        start = slot * EXPERT_BLOCK
        start = start if isinstance(start, int) else pl.multiple_of(start, EXPERT_BLOCK)
        return ref.at[pl.ds(start, EXPERT_BLOCK), pl.ds(c * LANES, LANES)]

    def in_copy(b, slot):
        return _Group([pltpu.make_async_copy(hbm_chunk(xs_hbm, b, c), vmem_chunk(xbuf_ref, slot, c), in_sems.at[slot])
                       for c in range(ROW_SUB)])

    def out_copy(b, slot):
        return _Group([pltpu.make_async_copy(vmem_chunk(ybuf_ref, slot, c), hbm_chunk(ys_hbm, b, c), out_sems.at[slot])
                       for c in range(ROW_SUB)])

    @pl.when(i == 0)
    def _():
        for b0 in range(EXPERT_IN_SLOTS - 1):
            in_copy(b0, b0).start()

    @pl.when(newblk_ref[i] == 1)
    def _():
        ahead = blk + (EXPERT_IN_SLOTS - 1)

        @pl.when(ahead < n_blocks)
        def _():
            in_copy(ahead, lax.rem(ahead, EXPERT_IN_SLOTS)).start()

        in_copy(blk, lax.rem(blk, EXPERT_IN_SLOTS)).wait()

        @pl.when(blk >= EXPERT_OUT_SLOTS)
        def _():
            out_copy(blk - EXPERT_OUT_SLOTS, lax.rem(blk, EXPERT_OUT_SLOTS)).wait()

    @pl.when(newexp_ref[i] == 1)
    def _():
        wgub_ref[:, 0:D_EXPERT] = wg_ref[0].astype(bf16)
        wgub_ref[:, D_EXPERT:2 * D_EXPERT] = wu_ref[0].astype(bf16)
        wdb_ref[...] = wd_ref[0].astype(bf16)

    xbase = pl.multiple_of(lax.rem(blk, EXPERT_IN_SLOTS) * EXPERT_BLOCK, EXPERT_BLOCK)
    ybase = pl.multiple_of(lax.rem(blk, EXPERT_OUT_SLOTS) * EXPERT_BLOCK, EXPERT_BLOCK)

    def run(r0, nrows, merge_first):
        xb = xbuf_ref[pl.ds(xbase + r0, nrows), :].astype(bf16)
        gu = _dot(xb, wgub_ref[...])
        g = gu[:, 0:D_EXPERT]
        u = gu[:, D_EXPERT:2 * D_EXPERT]
        h = (g * jax.nn.sigmoid(g) * u).astype(bf16)
        y = _dot(h, wdb_ref[...])
        first = SB if merge_first else 0
        if merge_first:
            rows = r0 + lax.broadcasted_iota(jnp.int32, (SB, D_MODEL), 0)
            keep = jnp.logical_and(rows >= lo, rows < hi)
            old = ybuf_ref[pl.ds(ybase + r0, SB), :]
            ybuf_ref[pl.ds(ybase + r0, SB), :] = jnp.where(keep, y[0:SB], old)
        if nrows > first:
            ybuf_ref[pl.ds(ybase + r0 + first, nrows - first), :] = y[first:nrows]

    n_sub = EXPERT_BLOCK // SB
    from_top = lo == 0
    sub_aligned = (lo & (SB - 1)) == 0
    to_end = jnp.logical_and(jnp.logical_and(lo > 0, hi == EXPERT_BLOCK), jnp.logical_not(sub_aligned))
    subs_used = (hi + SB - 1) // SB
    first_sub = lo // SB
    for k in range(1, n_sub + 1):
        @pl.when(jnp.logical_and(from_top, subs_used == k))
        def _(k=k):
            run(0, k * SB, False)

    for q in range(n_sub):
        @pl.when(jnp.logical_and(to_end, first_sub == q))
        def _(q=q):
            run(q * SB, EXPERT_BLOCK - q * SB, True)

    for q in range(n_sub):
        r0 = q * SB
        inside = jnp.logical_and(lo > 0, jnp.logical_or(hi < EXPERT_BLOCK, sub_aligned))

        @pl.when(jnp.logical_and(inside, jnp.minimum(hi, r0 + SB) > jnp.maximum(lo, r0)))
        def _(r0=r0):
            @pl.when(lo <= r0)
            def _():
                run(r0, SB, False)

            @pl.when(lo > r0)
            def _():
                run(r0, SB, True)

    @pl.when(endblk_ref[i] == 1)
    def _():
        out_copy(blk, lax.rem(blk, EXPERT_OUT_SLOTS)).start()

    @pl.when(i == n_items - 1)
    def _():
        for b1 in range(n_blocks - EXPERT_OUT_SLOTS, n_blocks):
            out_copy(b1, b1 % EXPERT_OUT_SLOTS).wait()


def _expert_call(item_blk, item_exp, item_lo, item_hi, item_newexp, item_newblk, item_endblk,
                 xs_rp, w_eg, w_eu, w_ed):
    D = D_MODEL
    BS = EXPERT_BLOCK * ROW_SUB
    n_items = item_blk.shape[0]
    assert xs_rp.shape[0] % BS == 0 and xs_rp.shape[0] // BS >= max(EXPERT_IN_SLOTS, EXPERT_OUT_SLOTS)
    xs3 = xs_rp.reshape(xs_rp.shape[0] // ROW_SUB, ROW_SUB, LANES)

    def w_map(i, blk, exp, lo, hi, newexp, newblk, endblk):
        return (exp[i], 0, 0)

    grid_spec = pltpu.PrefetchScalarGridSpec(
        num_scalar_prefetch=7,
        grid=(n_items,),
        in_specs=[
            pl.BlockSpec(memory_space=pl.ANY),
            pl.BlockSpec((1, D, D_EXPERT), w_map),
            pl.BlockSpec((1, D, D_EXPERT), w_map),
            pl.BlockSpec((1, D_EXPERT, D), w_map),
        ],
        out_specs=pl.BlockSpec(memory_space=pl.ANY),
        scratch_shapes=[
            pltpu.VMEM((EXPERT_IN_SLOTS * EXPERT_BLOCK, D), jnp.float32),
            pltpu.VMEM((EXPERT_OUT_SLOTS * EXPERT_BLOCK, D), jnp.float32),
            pltpu.VMEM((D, 2 * D_EXPERT), jnp.bfloat16),
            pltpu.VMEM((D_EXPERT, D), jnp.bfloat16),
            pltpu.SemaphoreType.DMA((EXPERT_IN_SLOTS,)),
            pltpu.SemaphoreType.DMA((EXPERT_OUT_SLOTS,)),
        ],
    )
    return pl.pallas_call(
        _expert_kernel,
        grid_spec=grid_spec,
        out_shape=jax.ShapeDtypeStruct(xs3.shape, jnp.float32),
        compiler_params=pltpu.CompilerParams(
            dimension_semantics=("arbitrary",), vmem_limit_bytes=VMEM_LIMIT),
        name="moe_experts",
    )(item_blk, item_exp, item_lo, item_hi, item_newexp, item_newblk, item_endblk, xs3, w_eg, w_eu, w_ed
      ).reshape(xs_rp.shape)


def _combine_kernel(lsrc_ref, gdst_ref, cnt_ref, ld_ref, w_ref, x2_hbm, ys_ref, out_hbm,
                    stag_ref, acc_ref, sems, in_sems, out_sems):
    b = pl.program_id(0)
    nb = pl.num_programs(0)
    T = MIX_TILE
    slot = b & 1
    base = slot * TILE_ROWS
    aslot = lax.rem(b, COMBINE_ACC_SLOTS)

    def in_copies(tile):
        s_ = lax.rem(tile, COMBINE_ACC_SLOTS)
        return _tile_relayout_copies(x2_hbm, tile, acc_ref, s_, in_sems.at[s_], True)

    def out_copies(tile):
        s_ = lax.rem(tile, COMBINE_ACC_SLOTS)
        return _tile_relayout_copies(out_hbm, tile, acc_ref, s_, out_sems.at[s_], False)

    @pl.when(b >= COMBINE_ACC_SLOTS - 1)
    def _():
        for cp in out_copies(b - (COMBINE_ACC_SLOTS - 1)):
            cp.wait()

    @pl.when(b == 0)
    def _():
        for cp in in_copies(b):
            cp.start()

    @pl.when(b + 1 < nb)
    def _():
        for cp in in_copies(b + 1):
            cp.start()

    def fetch(tile, slot_):
        def seg(e, c):
            s = tile * N_EXPERTS + e
            _segment_copies(ys_ref, gdst_ref[s], stag_ref, slot_ * TILE_ROWS + lsrc_ref[s], cnt_ref[s],
                            sems.at[slot_])
            return c

        lax.fori_loop(0, N_EXPERTS, seg, 0, unroll=4)

    @pl.when(b == 0)
    def _():
        fetch(b, slot)

    @pl.when(b + 1 < nb)
    def _():
        fetch(b + 1, 1 - slot)

    _wait_tile(ys_ref, stag_ref, base, sems.at[slot], False)
    for cp in in_copies(b):
        cp.wait()

    def tok(t, c):
        a = acc_ref[aslot * T + t]
        for kk_ in range(TOP_K):
            r8 = ld_ref[0, 0, kk_ * T + t]
            w = w_ref[0, 0, kk_ * T + t]
            a = a + w * stag_ref[pl.ds(pl.multiple_of(r8, ROW_SUB), ROW_SUB), :]
        acc_ref[aslot * T + t] = a
        return c

    lax.fori_loop(0, T, tok, 0, unroll=32)
    for cp in out_copies(b):
        cp.start()

    @pl.when(b == nb - 1)
    def _():
        for back in range(COMBINE_ACC_SLOTS - 2, -1, -1):
            @pl.when(b >= back)
            def _(back=back):
                for cp in out_copies(b - back):
                    cp.wait()


def _combine_call(lsrc, gdst, cnt, ld_t, w_t, x2, ys_rp):
    N, D = x2.shape
    nt = ld_t.shape[0]
    T = MIX_TILE
    smem = pl.BlockSpec(memory_space=pltpu.SMEM)
    smem_blk = pl.BlockSpec((1, 1, TILE_ROWS), lambda i: (i, 0, 0), memory_space=pltpu.SMEM)
    return pl.pallas_call(
        _combine_kernel,
        grid=(nt,),
        in_specs=[
            smem, smem, smem, smem_blk, smem_blk,
            pl.BlockSpec(memory_space=pl.ANY),
            pl.BlockSpec(memory_space=pl.ANY),
        ],
        out_specs=pl.BlockSpec(memory_space=pl.ANY),
        out_shape=jax.ShapeDtypeStruct((N, D), jnp.float32),
        scratch_shapes=[
            pltpu.VMEM((2 * TILE_ROWS * ROW_SUB, LANES), jnp.float32),
            pltpu.VMEM((COMBINE_ACC_SLOTS * T, ROW_SUB, LANES), jnp.float32),
            pltpu.SemaphoreType.DMA((2,)),
            pltpu.SemaphoreType.DMA((COMBINE_ACC_SLOTS,)),
            pltpu.SemaphoreType.DMA((COMBINE_ACC_SLOTS,)),
        ],
        compiler_params=pltpu.CompilerParams(
            dimension_semantics=("arbitrary",), vmem_limit_bytes=VMEM_LIMIT),
        name="moe_combine",
    )(lsrc, gdst, cnt, ld_t, w_t, x2, ys_rp)


def _work_items(gstart, totals, n_items):
    BM = EXPERT_BLOCK
    i32 = jnp.int32
    gend = gstart + totals
    first_blk = gstart // BM
    last_blk = jnp.maximum(gend - 1, gstart) // BM
    n_e = jnp.where(totals > 0, last_blk - first_blk + 1, 0)
    item_end = jnp.cumsum(n_e)
    item_start = item_end - n_e
    n_real = item_end[-1]
    j = jnp.minimum(jnp.arange(n_items, dtype=i32), n_real - 1)
    exp = jnp.sum((item_end[None, :] <= j[:, None]).astype(i32), axis=1)
    onehot = exp[:, None] == jnp.arange(N_EXPERTS, dtype=i32)[None, :]

    def of_exp(v):
        return jnp.sum(jnp.where(onehot, v[None, :], 0), axis=1)

    blk = of_exp(first_blk) + (j - of_exp(item_start))
    lo = jnp.maximum(of_exp(gstart), blk * BM) - blk * BM
    hi = jnp.minimum(of_exp(gend), (blk + 1) * BM) - blk * BM
    real = jnp.arange(n_items, dtype=i32) < n_real
    hi = jnp.where(real, hi, lo)
    prev_exp = jnp.concatenate([jnp.full((1,), -1, i32), exp[:-1]])
    prev_blk = jnp.concatenate([jnp.full((1,), -1, i32), blk[:-1]])
    next_blk = jnp.concatenate([blk[1:], jnp.full((1,), -1, i32)])
    is_last = jnp.arange(n_items, dtype=i32) == n_real - 1
    newexp = jnp.logical_and(real, exp != prev_exp).astype(i32)
    newblk = jnp.logical_and(real, blk != prev_blk).astype(i32)
    endblk = jnp.logical_and(real, jnp.logical_or(blk != next_blk, is_last)).astype(i32)
    return blk.astype(i32), exp.astype(i32), lo.astype(i32), hi.astype(i32), newexp, newblk, endblk


def _layer(x, mix_norm_w, w_in, conv_w, q_norm_w, k_norm_w, sinks, conv_out_norm_w,
           attn_out_norm_w, w_out, ffn_norm_w, w_router, router_bias, w_eg, w_eu, w_ed,
           w_sg, w_su, w_sd):
    B, S, D = x.shape
    N = B * S
    T = MIX_TILE
    nt = N // T
    bf16 = jnp.bfloat16
    f32 = jnp.float32
    i32 = jnp.int32

    wr_t = w_router.astype(f32).T
    wr_hi = wr_t.astype(bf16)
    wr_hl = jnp.concatenate([wr_hi, (wr_t - wr_hi.astype(f32)).astype(bf16)], axis=0)
    sink_col = jnp.broadcast_to(
        jnp.repeat(sinks.astype(f32), BLK).reshape(N_KV_HEADS, GQA_GROUP * BLK, 1),
        (N_KV_HEADS, GQA_GROUP * BLK, LANES))
    tri = jnp.asarray(np.triu(np.ones((T, T), np.float32), k=1), dtype=bf16)
    head_ones = jnp.asarray(np.kron(np.eye(2 * LANES // HEAD_DIM, dtype=np.float32),
                                    np.ones((HEAD_DIM, HEAD_DIM), np.float32)), dtype=bf16)
    bias_tab = jnp.asarray(_attn_bias_table())

    x2, xn, wts, ld, cnt = _mixer_call(
        x, mix_norm_w.reshape(1, D), w_in.astype(bf16), conv_w,
        jnp.tile(q_norm_w, N_HEADS).reshape(1, D_ATTN), jnp.tile(k_norm_w, N_KV_HEADS).reshape(1, D_KV),
        sink_col, bias_tab, conv_out_norm_w.reshape(1, D_CONV), attn_out_norm_w.reshape(1, D_ATTN),
        w_out.astype(bf16), ffn_norm_w.reshape(1, D), w_sg.astype(bf16), w_su.astype(bf16),
        w_sd.astype(bf16), wr_hl, router_bias.astype(f32).reshape(N_EXPERTS, 1), tri, head_ones)

    cnt_te = cnt[:, 0].reshape(nt, N_EXPERTS)
    lsrc = jnp.cumsum(cnt_te, axis=1) - cnt_te
    before = jnp.cumsum(cnt_te, axis=0) - cnt_te
    totals = jnp.sum(cnt_te, axis=0)
    gstart = jnp.cumsum(totals) - totals
    gdst = gstart[None, :] + before
    n_items = (N * TOP_K) // EXPERT_BLOCK + N_EXPERTS
    items = _work_items(gstart.astype(i32), totals.astype(i32), n_items)

    def tile_major(a):
        return a.reshape(TOP_K, nt, T).transpose(1, 0, 2).reshape(nt, 1, TILE_ROWS)

    ld_t = tile_major(ld)
    seg_tabs = (lsrc.reshape(-1).astype(i32), gdst.reshape(-1).astype(i32), cnt_te.reshape(-1).astype(i32))
    xs_rp = _dispatch_call(*seg_tabs, ld_t, xn)
    ys_rp = _expert_call(*items, xs_rp, w_eg, w_eu, w_ed)
    out = _combine_call(*seg_tabs, ld_t, tile_major(wts), x2, ys_rp)
    return out.reshape(B, S, D)


def kernel(x, mix_norm_w, w_in, conv_w, q_norm_w, k_norm_w, sinks, conv_out_norm_w, attn_out_norm_w, w_out, ffn_norm_w, w_router, router_bias, w_exp_gate, w_exp_up, w_exp_down, w_sh_gate, w_sh_up, w_sh_down):
    B, S, D = x.shape
    assert D == D_MODEL and w_in.shape == (1, D_MODEL, D_IN_PROJ) and w_exp_gate.shape == (1, N_EXPERTS, D_MODEL, D_EXPERT)
    assert S % MIX_TILE == 0 and (B * S * TOP_K) % EXPERT_BLOCK == 0
    return _layer(x, mix_norm_w[0], w_in[0], conv_w[0], q_norm_w[0], k_norm_w[0], sinks[0],
                  conv_out_norm_w[0], attn_out_norm_w[0], w_out[0], ffn_norm_w[0], w_router[0],
                  router_bias[0], w_exp_gate[0], w_exp_up[0], w_exp_down[0], w_sh_gate[0],
                  w_sh_up[0], w_sh_down[0])
```

```python
import numpy as np
import jax
import jax.numpy as jnp
from jax import lax
from jax.experimental import pallas as pl
from jax.experimental.pallas import tpu as pltpu

D_MODEL = 1024
EPS = 1e-6
D_CONV = 512
HEAD_DIM = 64
N_HEADS = 8
N_KV_HEADS = 2
GQA_GROUP = 4
D_ATTN = 512
D_KV = 128
WINDOW = 128
BLK = 128
D_IN_PROJ = 2304
N_EXPERTS = 64
TOP_K = 8
N_GROUPS = 8
GROUP_SIZE = 8
TOPK_GROUPS = 4
D_EXPERT = 256
ROUTED_SCALE = 2.5

LANES = 128
ROW_SUB = 8
MIX_TILE = 512
TILE_ROWS = TOP_K * MIX_TILE
EXPERT_BLOCK = 1024
EXPERT_SUB = 256
EXPERT_IN_SLOTS = 5
EXPERT_OUT_SLOTS = 4
COMBINE_ACC_SLOTS = 3
SEG_CHUNK_LOG2 = 5
NEG_BIG = -1e30
VMEM_LIMIT = 58 * 1024 * 1024


def _alibi_slopes():
    return np.array([2.0 ** (-8.0 * (h + 1) / N_HEADS) for h in range(N_HEADS)], dtype=np.float32)


def _attn_bias_table():
    qi = np.arange(BLK)[:, None]
    kj = np.arange(2 * BLK)[None, :]
    dist = qi - kj + BLK
    inwin = (dist >= 0) & (dist < WINDOW)
    slopes = _alibi_slopes()
    out = np.zeros((N_KV_HEADS, GQA_GROUP * BLK, 2 * BLK), np.float32)
    for g in range(N_KV_HEADS):
        for i in range(GQA_GROUP):
            h = g * GQA_GROUP + i
            out[g, i * BLK:(i + 1) * BLK] = np.where(inwin, -slopes[h] * dist.astype(np.float32), NEG_BIG)
    return out


def _rms(x, w):
    ms = jnp.mean(x * x, axis=-1, keepdims=True)
    return x * lax.rsqrt(ms + EPS) * w


def _head_rms(x, w_tiled, head_ones):
    C = x.shape[1]
    W = min(C, 2 * LANES)
    sq = x * x
    hi = sq.astype(jnp.bfloat16)
    lo = (sq - hi.astype(jnp.float32)).astype(jnp.bfloat16)
    ones = head_ones[0:W, 0:W]
    ssq = jnp.concatenate(
        [_dot(hi[:, c:c + W], ones) + _dot(lo[:, c:c + W], ones) for c in range(0, C, W)], axis=1)
    return x * lax.rsqrt(ssq * (1.0 / HEAD_DIM) + EPS) * w_tiled


def _dot(a, b):
    return jnp.dot(a, b, preferred_element_type=jnp.float32)


def _dot_nt(a, b):
    return lax.dot_general(a, b, (((1,), (1,)), ((), ())), preferred_element_type=jnp.float32)


def _mixer_kernel(x_ref, mixw_ref, win_ref, convw_ref, qw_ref, kw_ref, sink_ref, bias_ref,
                  cnw_ref, anw_ref, wout_ref, fnw_ref, wsg_ref, wsu_ref, wsd_ref,
                  wrhl_ref, rbias_ref, tri_ref, hones_ref,
                  x2_ref, xn_ref, wts_ref, ld_ref, cnt_ref,
                  kc_ref, vc_ref, cc_ref):
    j = pl.program_id(1)
    T = MIX_TILE
    bf16 = jnp.bfloat16
    f32 = jnp.float32

    @pl.when(j == 0)
    def _():
        kc_ref[...] = jnp.zeros_like(kc_ref)
        vc_ref[...] = jnp.zeros_like(vc_ref)
        cc_ref[...] = jnp.zeros_like(cc_ref)

    x = x_ref[0]
    u = _rms(x, mixw_ref[...]).astype(bf16)
    qkv = _dot(u, win_ref[:, 3 * D_CONV:D_IN_PROJ])
    q = qkv[:, 0:D_ATTN]
    k = qkv[:, D_ATTN:D_ATTN + D_KV]
    v = qkv[:, D_ATTN + D_KV:D_ATTN + 2 * D_KV]
    conv_piece = 2 * LANES
    conv_pieces = []

    lane = lax.broadcasted_iota(jnp.int32, (1, LANES), 1)
    lane_lo = lane < HEAD_DIM
    qn = _head_rms(q, qw_ref[...], hones_ref) * (HEAD_DIM ** -0.5)
    kn = _head_rms(k, kw_ref[...], hones_ref)
    kfull = jnp.concatenate([kc_ref[...], kn], axis=0)
    vfull = jnp.concatenate([vc_ref[...], v], axis=0)
    kc_ref[...] = kn[T - BLK:T, :]
    vc_ref[...] = v[T - BLK:T, :]

    def _rep(a, g):
        r = pltpu.roll(a, HEAD_DIM, axis=1)
        two = jnp.where(lane_lo, a, r) if g == 0 else jnp.where(lane_lo, r, a)
        return jnp.concatenate([two, two], axis=1).astype(bf16)

    k_rep = [_rep(kfull, g) for g in range(N_KV_HEADS)]
    v_rep = [_rep(vfull, g) for g in range(N_KV_HEADS)]

    lane256 = lax.broadcasted_iota(jnp.int32, (1, 2 * LANES), 1)
    head_of_lane = lane256 // HEAD_DIM
    first_f = (j == 0).astype(f32)
    prev_key_mask = jnp.where(lane256 < BLK, first_f * NEG_BIG, 0.0)
    ones_cols = jnp.ones((2 * BLK, LANES), bf16)

    def lanes2(a):
        return jnp.concatenate([a, a], axis=1)

    attn_rows = []
    for i in range(T // BLK):
        grp_out = []
        for g in range(N_KV_HEADS):
            if len(conv_pieces) * conv_piece < 3 * D_CONV:
                c0 = len(conv_pieces) * conv_piece
                conv_pieces.append(_dot(u, win_ref[:, c0:c0 + conv_piece]))
            qg = qn[i * BLK:(i + 1) * BLK, g * 256:(g + 1) * 256]
            qst = jnp.concatenate(
                [jnp.where(head_of_lane == hi, qg, 0.0) for hi in range(GQA_GROUP)], axis=0).astype(bf16)
            kk = k_rep[g][i * BLK:i * BLK + 2 * BLK, :]
            vv = jnp.concatenate([v_rep[g][i * BLK:i * BLK + 2 * BLK, :], ones_cols], axis=1)
            s = _dot_nt(qst, kk) + bias_ref[g]
            if i == 0:
                s = s + prev_key_mask
            sink = sink_ref[g]
            m = jnp.maximum(jnp.max(s, axis=-1, keepdims=True), sink)
            e = jnp.exp(s - lanes2(m)).astype(bf16)
            r = _dot(e, vv)
            inv = 1.0 / (r[:, 2 * LANES:3 * LANES] + jnp.exp(sink - m))
            r = r[:, 0:2 * LANES] * lanes2(inv)
            o = jnp.where(head_of_lane == 0, r[0:BLK], 0.0)
            for hi in range(1, GQA_GROUP):
                o = jnp.where(head_of_lane == hi, r[hi * BLK:(hi + 1) * BLK], o)
            grp_out.append(o)
        attn_rows.append(jnp.concatenate(grp_out, axis=1))
    y_attn = jnp.concatenate(attn_rows, axis=0)

    assert len(conv_pieces) * conv_piece == 3 * D_CONV
    conv_proj = jnp.concatenate(conv_pieces, axis=1)
    b_gate = conv_proj[:, 0:D_CONV]
    c_gate = conv_proj[:, D_CONV:2 * D_CONV]
    hh = conv_proj[:, 2 * D_CONV:3 * D_CONV]
    ch = c_gate * hh
    prev = cc_ref[...]
    p6 = prev[6:7, :]
    p7 = prev[7:8, :]
    row = lax.broadcasted_iota(jnp.int32, (T, D_CONV), 0)
    ch_m1 = jnp.where(row == 0, p7, pltpu.roll(ch, 1, axis=0))
    ch_m2 = jnp.where(row == 0, p6, jnp.where(row == 1, p7, pltpu.roll(ch, 2, axis=0)))
    cw = convw_ref[...]
    y_conv = b_gate * (cw[0:1, :] * ch_m2 + cw[1:2, :] * ch_m1 + cw[2:3, :] * ch)
    cc_ref[...] = ch[T - 8:T, :]

    y_mix = jnp.concatenate([_rms(y_conv, cnw_ref[...]), _rms(y_attn, anw_ref[...])], axis=1)
    x1 = x + _dot(y_mix.astype(bf16), wout_ref[...])

    xn = _rms(x1, fnw_ref[...])
    xn_ref[...] = xn
    xh = xn.astype(bf16)

    xl = (xn - xh.astype(f32)).astype(bf16)
    w_hl = wrhl_ref[...]
    a_hl = _dot_nt(w_hl, xh)
    logits = a_hl[0:N_EXPERTS] + a_hl[N_EXPERTS:2 * N_EXPERTS] + _dot_nt(w_hl[0:N_EXPERTS], xl)

    gs = _dot(xh, wsg_ref[...])
    us = _dot(xh, wsu_ref[...])
    hs = (gs * jax.nn.sigmoid(gs) * us).astype(bf16)
    x2 = x1 + _dot(hs, wsd_ref[...])
    x2_ref[...] = x2

    scores = jax.nn.sigmoid(logits)
    biased = scores + rbias_ref[...]

    sub8 = lax.broadcasted_iota(jnp.int32, (GROUP_SIZE, T), 0)
    gscore = []
    for g in range(N_GROUPS):
        blk = biased[g * GROUP_SIZE:(g + 1) * GROUP_SIZE, :]
        m1 = jnp.max(blk, axis=0, keepdims=True)
        first = jnp.min(jnp.where(blk == m1, sub8, GROUP_SIZE), axis=0, keepdims=True)
        m2 = jnp.max(jnp.where(sub8 == first, -jnp.inf, blk), axis=0, keepdims=True)
        gscore.append(m1 + m2)
    masked_blocks = []
    for g in range(N_GROUPS):
        rank = jnp.zeros((1, T), jnp.int32)
        for o_ in range(N_GROUPS):
            if o_ == g:
                continue
            if o_ < g:
                ahead = gscore[o_] >= gscore[g]
            else:
                ahead = gscore[o_] > gscore[g]
            rank = rank + ahead.astype(jnp.int32)
        keep = rank < TOPK_GROUPS
        blk = biased[g * GROUP_SIZE:(g + 1) * GROUP_SIZE, :]
        masked_blocks.append(jnp.where(keep, blk, -jnp.inf))
    cur = jnp.concatenate(masked_blocks, axis=0)

    eiota = lax.broadcasted_iota(jnp.int32, (N_EXPERTS, T), 0)
    row8 = lax.broadcasted_iota(jnp.int32, (TOP_K, T), 0)
    w_out = jnp.zeros((TOP_K, T), f32)
    sel_dense = jnp.zeros((N_EXPERTS, T), f32)
    onehots = []
    for kk_ in range(TOP_K):
        mx = jnp.max(cur, axis=0, keepdims=True)
        sel_idx = jnp.min(jnp.where(cur == mx, eiota, N_EXPERTS), axis=0, keepdims=True)
        onehot = eiota == sel_idx
        w_k = jnp.sum(jnp.where(onehot, scores, 0.0), axis=0, keepdims=True)
        cur = jnp.where(onehot, -jnp.inf, cur)
        sel_dense = jnp.where(onehot, 1.0, sel_dense)
        w_out = jnp.where(row8 == kk_, w_k, w_out)
        onehots.append(onehot)
    wsum = jnp.sum(w_out, axis=0, keepdims=True)
    wts_ref[...] = w_out / wsum * ROUTED_SCALE

    cum = _dot(sel_dense.astype(bf16), tri_ref[...])
    cnt = jnp.broadcast_to(jnp.sum(sel_dense, axis=1, keepdims=True), (N_EXPERTS, LANES))
    erow = lax.broadcasted_iota(jnp.int32, (N_EXPERTS, LANES), 0)
    incl = cnt
    step = 1
    while step < N_EXPERTS:
        incl = incl + jnp.where(erow >= step, pltpu.roll(incl, step, axis=0), 0.0)
        step *= 2
    lstart = incl - cnt
    ld_dense = lstart[:, 0:1] + cum
    ld_out = jnp.zeros((TOP_K, T), f32)
    for kk_ in range(TOP_K):
        p_k = jnp.sum(jnp.where(onehots[kk_], ld_dense, 0.0), axis=0, keepdims=True)
        ld_out = jnp.where(row8 == kk_, p_k, ld_out)
    tile_idx = pl.program_id(0) * pl.num_programs(1) + j
    slot_base = ((tile_idx & 1) * TILE_ROWS).astype(f32)
    ld_ref[...] = ((ld_out + slot_base) * ROW_SUB).astype(jnp.int32)
    cnt_ref[...] = cnt.astype(jnp.int32)


def _mixer_call(x, mixw, win, convw, qw, kw, sink_col, bias_tab, cnw, anw, wout, fnw,
                wsg, wsu, wsd, wrhl, rbias, tri, head_ones):
    B, S, D = x.shape
    T = MIX_TILE
    nt = S // T
    N = B * S

    def full(a):
        nd = a.ndim
        return pl.BlockSpec(a.shape, lambda b, j, _nd=nd: (0,) * _nd)

    tok_spec = pl.BlockSpec((TOP_K, T), lambda b, j: (0, b * nt + j))
    in_arrays = [mixw, win, convw, qw, kw, sink_col, bias_tab, cnw, anw, wout, fnw,
                 wsg, wsu, wsd, wrhl, rbias, tri, head_ones]
    return pl.pallas_call(
        _mixer_kernel,
        grid=(B, nt),
        in_specs=[pl.BlockSpec((1, T, D), lambda b, j: (b, j, 0))] + [full(a) for a in in_arrays],
        out_specs=[
            pl.BlockSpec((T, D), lambda b, j: (b * nt + j, 0)),
            pl.BlockSpec((T, D), lambda b, j: (b * nt + j, 0)),
            tok_spec, tok_spec,
            pl.BlockSpec((N_EXPERTS, LANES), lambda b, j: (b * nt + j, 0)),
        ],
        out_shape=[
            jax.ShapeDtypeStruct((N, D), jnp.float32),
            jax.ShapeDtypeStruct((N, D), jnp.float32),
            jax.ShapeDtypeStruct((TOP_K, N), jnp.float32),
            jax.ShapeDtypeStruct((TOP_K, N), jnp.int32),
            jax.ShapeDtypeStruct((B * nt * N_EXPERTS, LANES), jnp.int32),
        ],
        scratch_shapes=[
            pltpu.VMEM((BLK, D_KV), jnp.float32),
            pltpu.VMEM((BLK, D_KV), jnp.float32),
            pltpu.VMEM((8, D_CONV), jnp.float32),
        ],
        compiler_params=pltpu.CompilerParams(
            dimension_semantics=("arbitrary", "arbitrary"), vmem_limit_bytes=VMEM_LIMIT),
        name="mixer_router",
    )(x, *in_arrays)


def _rows(ref, row, nrows):
    return ref.at[pl.ds(pl.multiple_of(row * ROW_SUB, ROW_SUB), nrows * ROW_SUB)]


def _segment_copies(src, src_row, dst, dst_row, count, sem):
    chunk = 1 << SEG_CHUNK_LOG2

    def big(i, c):
        o = i * chunk
        pltpu.make_async_copy(_rows(src, src_row + o, chunk), _rows(dst, dst_row + o, chunk), sem).start()
        return c

    lax.fori_loop(0, count >> SEG_CHUNK_LOG2, big, 0)
    for bit in range(SEG_CHUNK_LOG2 - 1, -1, -1):
        o = (count >> (bit + 1)) << (bit + 1)

        @pl.when(((count >> bit) & 1) == 1)
        def _():
            n = 1 << bit
            pltpu.make_async_copy(_rows(src, src_row + o, n), _rows(dst, dst_row + o, n), sem).start()


def _wait_tile(hbm_ref, stag_ref, slot_row, sem, to_hbm):
    vm = _rows(stag_ref, slot_row, TILE_ROWS)
    hb = _rows(hbm_ref, 0, TILE_ROWS)
    (pltpu.make_async_copy(vm, hb, sem) if to_hbm else pltpu.make_async_copy(hb, vm, sem)).wait()


def _tile_relayout_copies(std_hbm, tile, rpt_ref, slot, sem, to_vmem):
    T = MIX_TILE
    row0 = tile * T
    row0 = row0 if isinstance(row0, int) else pl.multiple_of(row0, T)
    slot0 = slot * T
    slot0 = slot0 if isinstance(slot0, int) else pl.multiple_of(slot0, T)
    copies = []
    for c in range(ROW_SUB):
        hb = std_hbm.at[pl.ds(row0, T), pl.ds(c * LANES, LANES)]
        vm = rpt_ref.at[pl.ds(slot0, T), c, :]
        copies.append(pltpu.make_async_copy(hb, vm, sem) if to_vmem else pltpu.make_async_copy(vm, hb, sem))
    return copies


def _dispatch_kernel(lsrc_ref, gdst_ref, cnt_ref, ld_ref, xn_hbm, xs_ref, stag_ref, xin_ref, sems, in_sems):
    b = pl.program_id(0)
    nb = pl.num_programs(0)
    T = MIX_TILE
    slot = b & 1
    base = slot * TILE_ROWS

    def fetch_rows(tile, slot_):
        return _tile_relayout_copies(xn_hbm, tile, xin_ref, slot_, in_sems.at[slot_], True)

    @pl.when(b == 0)
    def _():
        for cp in fetch_rows(b, slot):
            cp.start()

    @pl.when(b + 1 < nb)
    def _():
        for cp in fetch_rows(b + 1, 1 - slot):
            cp.start()

    for cp in fetch_rows(b, slot):
        cp.wait()

    for kk_ in range(TOP_K):
        def row(t, c, kk_=kk_):
            r8 = ld_ref[0, 0, kk_ * T + t]
            stag_ref[pl.ds(pl.multiple_of(r8, ROW_SUB), ROW_SUB), :] = xin_ref[slot * T + t]
            return c

        lax.fori_loop(0, T, row, 0, unroll=128)

    def seg(e, c):
        s = b * N_EXPERTS + e
        _segment_copies(stag_ref, base + lsrc_ref[s], xs_ref, gdst_ref[s], cnt_ref[s], sems.at[slot])
        return c

    lax.fori_loop(0, N_EXPERTS, seg, 0, unroll=4)

    @pl.when(b > 0)
    def _():
        _wait_tile(xs_ref, stag_ref, (1 - slot) * TILE_ROWS, sems.at[1 - slot], True)

    @pl.when(b == nb - 1)
    def _():
        _wait_tile(xs_ref, stag_ref, base, sems.at[slot], True)


def _dispatch_call(lsrc, gdst, cnt, ld_t, xn):
    nt = ld_t.shape[0]
    T = MIX_TILE
    smem = pl.BlockSpec(memory_space=pltpu.SMEM)
    return pl.pallas_call(
        _dispatch_kernel,
        grid=(nt,),
        in_specs=[
            smem, smem, smem,
            pl.BlockSpec((1, 1, TILE_ROWS), lambda i: (i, 0, 0), memory_space=pltpu.SMEM),
            pl.BlockSpec(memory_space=pl.ANY),
        ],
        out_specs=pl.BlockSpec(memory_space=pl.ANY),
        out_shape=jax.ShapeDtypeStruct((nt * TILE_ROWS * ROW_SUB, LANES), jnp.float32),
        scratch_shapes=[pltpu.VMEM((2 * TILE_ROWS * ROW_SUB, LANES), jnp.float32),
                        pltpu.VMEM((2 * T, ROW_SUB, LANES), jnp.float32),
                        pltpu.SemaphoreType.DMA((2,)),
                        pltpu.SemaphoreType.DMA((2,))],
        compiler_params=pltpu.CompilerParams(
            dimension_semantics=("arbitrary",), vmem_limit_bytes=VMEM_LIMIT),
        name="moe_dispatch",
    )(lsrc, gdst, cnt, ld_t, xn)


def _expert_kernel(blk_ref, exp_ref, lo_ref, hi_ref, newexp_ref, newblk_ref, endblk_ref,
                   xs_hbm, wg_ref, wu_ref, wd_ref, ys_hbm,
                   xbuf_ref, ybuf_ref, wgub_ref, wdb_ref, in_sems, out_sems):
    i = pl.program_id(0)
    n_items = pl.num_programs(0)
    bf16 = jnp.bfloat16
    SB = EXPERT_SUB
    n_blocks = xs_hbm.shape[0] // EXPERT_BLOCK
    blk = blk_ref[i]
    lo = lo_ref[i]
    hi = hi_ref[i]

    class _Group:
        def __init__(self, copies):
            self.copies = copies

        def start(self):
            for cp in self.copies:
                cp.start()

        def wait(self):
            for cp in self.copies:
                cp.wait()

    def hbm_chunk(ref, b, c):
        start = b * EXPERT_BLOCK
        start = start if isinstance(start, int) else pl.multiple_of(start, EXPERT_BLOCK)
        return ref.at[pl.ds(start, EXPERT_BLOCK), c, :]

    def vmem_chunk(ref, slot, c):
        start = slot * EXPERT_BLOCK
        start = start if isinstance(start, int) else pl.multiple_of(start, EXPERT_BLOCK)
        return ref.at[pl.ds(start, EXPERT_BLOCK), pl.ds(c * LANES, LANES)]

    def in_copy(b, slot):
        return _Group([pltpu.make_async_copy(hbm_chunk(xs_hbm, b, c), vmem_chunk(xbuf_ref, slot, c), in_sems.at[slot])
                       for c in range(ROW_SUB)])

    def out_copy(b, slot):
        return _Group([pltpu.make_async_copy(vmem_chunk(ybuf_ref, slot, c), hbm_chunk(ys_hbm, b, c), out_sems.at[slot])
                       for c in range(ROW_SUB)])

    @pl.when(i == 0)
    def _():
        for b0 in range(EXPERT_IN_SLOTS - 1):
            in_copy(b0, b0).start()

    @pl.when(newblk_ref[i] == 1)
    def _():
        ahead = blk + (EXPERT_IN_SLOTS - 1)

        @pl.when(ahead < n_blocks)
        def _():
            in_copy(ahead, lax.rem(ahead, EXPERT_IN_SLOTS)).start()

        in_copy(blk, lax.rem(blk, EXPERT_IN_SLOTS)).wait()

        @pl.when(blk >= EXPERT_OUT_SLOTS)
        def _():
            out_copy(blk - EXPERT_OUT_SLOTS, lax.rem(blk, EXPERT_OUT_SLOTS)).wait()

    @pl.when(newexp_ref[i] == 1)
    def _():
        wgub_ref[:, 0:D_EXPERT] = wg_ref[0].astype(bf16)
        wgub_ref[:, D_EXPERT:2 * D_EXPERT] = wu_ref[0].astype(bf16)
        wdb_ref[...] = wd_ref[0].astype(bf16)

    xbase = pl.multiple_of(lax.rem(blk, EXPERT_IN_SLOTS) * EXPERT_BLOCK, EXPERT_BLOCK)
    ybase = pl.multiple_of(lax.rem(blk, EXPERT_OUT_SLOTS) * EXPERT_BLOCK, EXPERT_BLOCK)

    def run(r0, nrows, merge_first):
        xb = xbuf_ref[pl.ds(xbase + r0, nrows), :].astype(bf16)
        gu = _dot(xb, wgub_ref[...])
        g = gu[:, 0:D_EXPERT]
        u = gu[:, D_EXPERT:2 * D_EXPERT]
        h = (g * jax.nn.sigmoid(g) * u).astype(bf16)
        y = _dot(h, wdb_ref[...])
        first = SB if merge_first else 0
        if merge_first:
            rows = r0 + lax.broadcasted_iota(jnp.int32, (SB, D_MODEL), 0)
            keep = jnp.logical_and(rows >= lo, rows < hi)
            old = ybuf_ref[pl.ds(ybase + r0, SB), :]
            ybuf_ref[pl.ds(ybase + r0, SB), :] = jnp.where(keep, y[0:SB], old)
        if nrows > first:
            ybuf_ref[pl.ds(ybase + r0 + first, nrows - first), :] = y[first:nrows]

    n_sub = EXPERT_BLOCK // SB
    from_top = lo == 0
    sub_aligned = (lo & (SB - 1)) == 0
    to_end = jnp.logical_and(jnp.logical_and(lo > 0, hi == EXPERT_BLOCK), jnp.logical_not(sub_aligned))
    subs_used = (hi + SB - 1) // SB
    first_sub = lo // SB
    for k in range(1, n_sub + 1):
        @pl.when(jnp.logical_and(from_top, subs_used == k))
        def _(k=k):
            run(0, k * SB, False)

    for q in range(n_sub):
        @pl.when(jnp.logical_and(to_end, first_sub == q))
        def _(q=q):
            run(q * SB, EXPERT_BLOCK - q * SB, True)

    for q in range(n_sub):
        r0 = q * SB
        inside = jnp.logical_and(lo > 0, jnp.logical_or(hi < EXPERT_BLOCK, sub_aligned))

        @pl.when(jnp.logical_and(inside, jnp.minimum(hi, r0 + SB) > jnp.maximum(lo, r0)))
        def _(r0=r0):
            @pl.when(lo <= r0)
            def _():
                run(r0, SB, False)

            @pl.when(lo > r0)
            def _():
                run(r0, SB, True)

    @pl.when(endblk_ref[i] == 1)
    def _():
        out_copy(blk, lax.rem(blk, EXPERT_OUT_SLOTS)).start()

    @pl.when(i == n_items - 1)
    def _():
        for b1 in range(n_blocks - EXPERT_OUT_SLOTS, n_blocks):
            out_copy(b1, b1 % EXPERT_OUT_SLOTS).wait()


def _expert_call(item_blk, item_exp, item_lo, item_hi, item_newexp, item_newblk, item_endblk,
                 xs_rp, w_eg, w_eu, w_ed):
    D = D_MODEL
    BS = EXPERT_BLOCK * ROW_SUB
    n_items = item_blk.shape[0]
    assert xs_rp.shape[0] % BS == 0 and xs_rp.shape[0] // BS >= max(EXPERT_IN_SLOTS, EXPERT_OUT_SLOTS)
    xs3 = xs_rp.reshape(xs_rp.shape[0] // ROW_SUB, ROW_SUB, LANES)

    def w_map(i, blk, exp, lo, hi, newexp, newblk, endblk):
        return (exp[i], 0, 0)

    grid_spec = pltpu.PrefetchScalarGridSpec(
        num_scalar_prefetch=7,
        grid=(n_items,),
        in_specs=[
            pl.BlockSpec(memory_space=pl.ANY),
            pl.BlockSpec((1, D, D_EXPERT), w_map),
            pl.BlockSpec((1, D, D_EXPERT), w_map),
            pl.BlockSpec((1, D_EXPERT, D), w_map),
        ],
        out_specs=pl.BlockSpec(memory_space=pl.ANY),
        scratch_shapes=[
            pltpu.VMEM((EXPERT_IN_SLOTS * EXPERT_BLOCK, D), jnp.float32),
            pltpu.VMEM((EXPERT_OUT_SLOTS * EXPERT_BLOCK, D), jnp.float32),
            pltpu.VMEM((D, 2 * D_EXPERT), jnp.bfloat16),
            pltpu.VMEM((D_EXPERT, D), jnp.bfloat16),
            pltpu.SemaphoreType.DMA((EXPERT_IN_SLOTS,)),
            pltpu.SemaphoreType.DMA((EXPERT_OUT_SLOTS,)),
        ],
    )
    return pl.pallas_call(
        _expert_kernel,
        grid_spec=grid_spec,
        out_shape=jax.ShapeDtypeStruct(xs3.shape, jnp.float32),
        compiler_params=pltpu.CompilerParams(
            dimension_semantics=("arbitrary",), vmem_limit_bytes=VMEM_LIMIT),
        name="moe_experts",
    )(item_blk, item_exp, item_lo, item_hi, item_newexp, item_newblk, item_endblk, xs3, w_eg, w_eu, w_ed
      ).reshape(xs_rp.shape)


def _combine_kernel(lsrc_ref, gdst_ref, cnt_ref, ld_ref, w_ref, x2_hbm, ys_ref, out_hbm,
                    stag_ref, acc_ref, sems, in_sems, out_sems):
    b = pl.program_id(0)
    nb = pl.num_programs(0)
    T = MIX_TILE
    slot = b & 1
    base = slot * TILE_ROWS
    aslot = lax.rem(b, COMBINE_ACC_SLOTS)

    def in_copies(tile):
        s_ = lax.rem(tile, COMBINE_ACC_SLOTS)
        return _tile_relayout_copies(x2_hbm, tile, acc_ref, s_, in_sems.at[s_], True)

    def out_copies(tile):
        s_ = lax.rem(tile, COMBINE_ACC_SLOTS)
        return _tile_relayout_copies(out_hbm, tile, acc_ref, s_, out_sems.at[s_], False)

    @pl.when(b >= COMBINE_ACC_SLOTS - 1)
    def _():
        for cp in out_copies(b - (COMBINE_ACC_SLOTS - 1)):
            cp.wait()

    @pl.when(b == 0)
    def _():
        for cp in in_copies(b):
            cp.start()

    @pl.when(b + 1 < nb)
    def _():
        for cp in in_copies(b + 1):
            cp.start()

    def fetch(tile, slot_):
        def seg(e, c):
            s = tile * N_EXPERTS + e
            _segment_copies(ys_ref, gdst_ref[s], stag_ref, slot_ * TILE_ROWS + lsrc_ref[s], cnt_ref[s],
                            sems.at[slot_])
            return c

        lax.fori_loop(0, N_EXPERTS, seg, 0, unroll=4)

    @pl.when(b == 0)
    def _():
        fetch(b, slot)

    @pl.when(b + 1 < nb)
    def _():
        fetch(b + 1, 1 - slot)

    _wait_tile(ys_ref, stag_ref, base, sems.at[slot], False)
    for cp in in_copies(b):
        cp.wait()

    def tok(t, c):
        a = acc_ref[aslot * T + t]
        for kk_ in range(TOP_K):
            r8 = ld_ref[0, 0, kk_ * T + t]
            w = w_ref[0, 0, kk_ * T + t]
            a = a + w * stag_ref[pl.ds(pl.multiple_of(r8, ROW_SUB), ROW_SUB), :]
        acc_ref[aslot * T + t] = a
        return c

    lax.fori_loop(0, T, tok, 0, unroll=32)
    for cp in out_copies(b):
        cp.start()

    @pl.when(b == nb - 1)
    def _():
        for back in range(COMBINE_ACC_SLOTS - 2, -1, -1):
            @pl.when(b >= back)
            def _(back=back):
                for cp in out_copies(b - back):
                    cp.wait()


def _combine_call(lsrc, gdst, cnt, ld_t, w_t, x2, ys_rp):
    N, D = x2.shape
    nt = ld_t.shape[0]
    T = MIX_TILE
    smem = pl.BlockSpec(memory_space=pltpu.SMEM)
    smem_blk = pl.BlockSpec((1, 1, TILE_ROWS), lambda i: (i, 0, 0), memory_space=pltpu.SMEM)
    return pl.pallas_call(
        _combine_kernel,
        grid=(nt,),
        in_specs=[
            smem, smem, smem, smem_blk, smem_blk,
            pl.BlockSpec(memory_space=pl.ANY),
            pl.BlockSpec(memory_space=pl.ANY),
        ],
        out_specs=pl.BlockSpec(memory_space=pl.ANY),
        out_shape=jax.ShapeDtypeStruct((N, D), jnp.float32),
        scratch_shapes=[
            pltpu.VMEM((2 * TILE_ROWS * ROW_SUB, LANES), jnp.float32),
            pltpu.VMEM((COMBINE_ACC_SLOTS * T, ROW_SUB, LANES), jnp.float32),
            pltpu.SemaphoreType.DMA((2,)),
            pltpu.SemaphoreType.DMA((COMBINE_ACC_SLOTS,)),
            pltpu.SemaphoreType.DMA((COMBINE_ACC_SLOTS,)),
        ],
        compiler_params=pltpu.CompilerParams(
            dimension_semantics=("arbitrary",), vmem_limit_bytes=VMEM_LIMIT),
        name="moe_combine",
    )(lsrc, gdst, cnt, ld_t, w_t, x2, ys_rp)


def _work_items(gstart, totals, n_items):
    BM = EXPERT_BLOCK
    i32 = jnp.int32
    gend = gstart + totals
    first_blk = gstart // BM
    last_blk = jnp.maximum(gend - 1, gstart) // BM
    n_e = jnp.where(totals > 0, last_blk - first_blk + 1, 0)
    item_end = jnp.cumsum(n_e)
    item_start = item_end - n_e
    n_real = item_end[-1]
    j = jnp.minimum(jnp.arange(n_items, dtype=i32), n_real - 1)
    exp = jnp.sum((item_end[None, :] <= j[:, None]).astype(i32), axis=1)
    onehot = exp[:, None] == jnp.arange(N_EXPERTS, dtype=i32)[None, :]

    def of_exp(v):
        return jnp.sum(jnp.where(onehot, v[None, :], 0), axis=1)

    blk = of_exp(first_blk) + (j - of_exp(item_start))
    lo = jnp.maximum(of_exp(gstart), blk * BM) - blk * BM
    hi = jnp.minimum(of_exp(gend), (blk + 1) * BM) - blk * BM
    real = jnp.arange(n_items, dtype=i32) < n_real
    hi = jnp.where(real, hi, lo)
    prev_exp = jnp.concatenate([jnp.full((1,), -1, i32), exp[:-1]])
    prev_blk = jnp.concatenate([jnp.full((1,), -1, i32), blk[:-1]])
    next_blk = jnp.concatenate([blk[1:], jnp.full((1,), -1, i32)])
    is_last = jnp.arange(n_items, dtype=i32) == n_real - 1
    newexp = jnp.logical_and(real, exp != prev_exp).astype(i32)
    newblk = jnp.logical_and(real, blk != prev_blk).astype(i32)
    endblk = jnp.logical_and(real, jnp.logical_or(blk != next_blk, is_last)).astype(i32)
    return blk.astype(i32), exp.astype(i32), lo.astype(i32), hi.astype(i32), newexp, newblk, endblk


def _layer(x, mix_norm_w, w_in, conv_w, q_norm_w, k_norm_w, sinks, conv_out_norm_w,
           attn_out_norm_w, w_out, ffn_norm_w, w_router, router_bias, w_eg, w_eu, w_ed,
           w_sg, w_su, w_sd):
    B, S, D = x.shape
    N = B * S
    T = MIX_TILE
    nt = N // T
    bf16 = jnp.bfloat16
    f32 = jnp.float32
    i32 = jnp.int32

    wr_t = w_router.astype(f32).T
    wr_hi = wr_t.astype(bf16)
    wr_hl = jnp.concatenate([wr_hi, (wr_t - wr_hi.astype(f32)).astype(bf16)], axis=0)
    sink_col = jnp.broadcast_to(
        jnp.repeat(sinks.astype(f32), BLK).reshape(N_KV_HEADS, GQA_GROUP * BLK, 1),
        (N_KV_HEADS, GQA_GROUP * BLK, LANES))
    tri = jnp.asarray(np.triu(np.ones((T, T), np.float32), k=1), dtype=bf16)
    head_ones = jnp.asarray(np.kron(np.eye(2 * LANES // HEAD_DIM, dtype=np.float32),
                                    np.ones((HEAD_DIM, HEAD_DIM), np.float32)), dtype=bf16)
    bias_tab = jnp.asarray(_attn_bias_table())

    x2, xn, wts, ld, cnt = _mixer_call(
        x, mix_norm_w.reshape(1, D), w_in.astype(bf16), conv_w,
        jnp.tile(q_norm_w, N_HEADS).reshape(1, D_ATTN), jnp.tile(k_norm_w, N_KV_HEADS).reshape(1, D_KV),
        sink_col, bias_tab, conv_out_norm_w.reshape(1, D_CONV), attn_out_norm_w.reshape(1, D_ATTN),
        w_out.astype(bf16), ffn_norm_w.reshape(1, D), w_sg.astype(bf16), w_su.astype(bf16),
        w_sd.astype(bf16), wr_hl, router_bias.astype(f32).reshape(N_EXPERTS, 1), tri, head_ones)

    cnt_te = cnt[:, 0].reshape(nt, N_EXPERTS)
    lsrc = jnp.cumsum(cnt_te, axis=1) - cnt_te
    before = jnp.cumsum(cnt_te, axis=0) - cnt_te
    totals = jnp.sum(cnt_te, axis=0)
    gstart = jnp.cumsum(totals) - totals
    gdst = gstart[None, :] + before
    n_items = (N * TOP_K) // EXPERT_BLOCK + N_EXPERTS
    items = _work_items(gstart.astype(i32), totals.astype(i32), n_items)

    def tile_major(a):
        return a.reshape(TOP_K, nt, T).transpose(1, 0, 2).reshape(nt, 1, TILE_ROWS)

    ld_t = tile_major(ld)
    seg_tabs = (lsrc.reshape(-1).astype(i32), gdst.reshape(-1).astype(i32), cnt_te.reshape(-1).astype(i32))
    xs_rp = _dispatch_call(*seg_tabs, ld_t, xn)
    ys_rp = _expert_call(*items, xs_rp, w_eg, w_eu, w_ed)
    out = _combine_call(*seg_tabs, ld_t, tile_major(wts), x2, ys_rp)
    return out.reshape(B, S, D)


def kernel(x, mix_norm_w, w_in, conv_w, q_norm_w, k_norm_w, sinks, conv_out_norm_w, attn_out_norm_w, w_out, ffn_norm_w, w_router, router_bias, w_exp_gate, w_exp_up, w_exp_down, w_sh_gate, w_sh_up, w_sh_down):
    B, S, D = x.shape
    assert D == D_MODEL and w_in.shape == (1, D_MODEL, D_IN_PROJ) and w_exp_gate.shape == (1, N_EXPERTS, D_MODEL, D_EXPERT)
    assert S % MIX_TILE == 0 and (B * S * TOP_K) % EXPERT_BLOCK == 0
    return _layer(x, mix_norm_w[0], w_in[0], conv_w[0], q_norm_w[0], k_norm_w[0], sinks[0],
                  conv_out_norm_w[0], attn_out_norm_w[0], w_out[0], ffn_norm_w[0], w_router[0],
                  router_bias[0], w_exp_gate[0], w_exp_up[0], w_exp_down[0], w_sh_gate[0],
                  w_sh_up[0], w_sh_down[0])
```

```python
import numpy as np
import jax
import jax.numpy as jnp
from jax import lax
from jax.experimental import pallas as pl
from jax.experimental.pallas import tpu as pltpu

D_MODEL = 1024
EPS = 1e-6
D_CONV = 512
HEAD_DIM = 64
N_HEADS = 8
N_KV_HEADS = 2
GQA_GROUP = 4
D_ATTN = 512
D_KV = 128
WINDOW = 128
BLK = 128
D_IN_PROJ = 2304
N_EXPERTS = 64
TOP_K = 8
N_GROUPS = 8
GROUP_SIZE = 8
TOPK_GROUPS = 4
D_EXPERT = 256
ROUTED_SCALE = 2.5

LANES = 128
ROW_SUB = 8
MIX_TILE = 512
TILE_ROWS = TOP_K * MIX_TILE
EXPERT_BLOCK = 1024
EXPERT_SUB = 256
EXPERT_IN_SLOTS = 5
EXPERT_OUT_SLOTS = 4
COMBINE_ACC_SLOTS = 3
SEG_CHUNK_LOG2 = 5
NEG_BIG = -1e30
VMEM_LIMIT = 58 * 1024 * 1024


def _alibi_slopes():
    return np.array([2.0 ** (-8.0 * (h + 1) / N_HEADS) for h in range(N_HEADS)], dtype=np.float32)


def _attn_bias_table():
    qi = np.arange(BLK)[:, None]
    kj = np.arange(2 * BLK)[None, :]
    dist = qi - kj + BLK
    inwin = (dist >= 0) & (dist < WINDOW)
    slopes = _alibi_slopes()
    out = np.zeros((N_KV_HEADS, GQA_GROUP * BLK, 2 * BLK), np.float32)
    for g in range(N_KV_HEADS):
        for i in range(GQA_GROUP):
            h = g * GQA_GROUP + i
            out[g, i * BLK:(i + 1) * BLK] = np.where(inwin, -slopes[h] * dist.astype(np.float32), NEG_BIG)
    return out


def _rms(x, w):
    ms = jnp.mean(x * x, axis=-1, keepdims=True)
    return x * lax.rsqrt(ms + EPS) * w


def _head_rms(x, w_tiled, head_ones):
    C = x.shape[1]
    W = min(C, 2 * LANES)
    sq = x * x
    hi = sq.astype(jnp.bfloat16)
    lo = (sq - hi.astype(jnp.float32)).astype(jnp.bfloat16)
    ones = head_ones[0:W, 0:W]
    ssq = jnp.concatenate(
        [_dot(hi[:, c:c + W], ones) + _dot(lo[:, c:c + W], ones) for c in range(0, C, W)], axis=1)
    return x * lax.rsqrt(ssq * (1.0 / HEAD_DIM) + EPS) * w_tiled


def _dot(a, b):
    return jnp.dot(a, b, preferred_element_type=jnp.float32)


def _dot_nt(a, b):
    return lax.dot_general(a, b, (((1,), (1,)), ((), ())), preferred_element_type=jnp.float32)


def _mixer_kernel(x_ref, mixw_ref, win_ref, convw_ref, qw_ref, kw_ref, sink_ref, bias_ref,
                  cnw_ref, anw_ref, wout_ref, fnw_ref, wsg_ref, wsu_ref, wsd_ref,
                  wrhl_ref, rbias_ref, tri_ref, hones_ref, weg_hbm, weu_hbm, wed_hbm,
                  x2_ref, xn_ref, wts_ref, ld_ref, cnt_ref, wgu_hbm, wdb_hbm,
                  kc_ref, vc_ref, cc_ref, wg_in_ref, wu_in_ref, wd_in_ref, wgu_out_ref, wd_out_ref,
                  win_sems, wout_sems):
    j = pl.program_id(1)
    T = MIX_TILE
    bf16 = jnp.bfloat16
    f32 = jnp.float32

    @pl.when(j == 0)
    def _():
        kc_ref[...] = jnp.zeros_like(kc_ref)
        vc_ref[...] = jnp.zeros_like(vc_ref)
        cc_ref[...] = jnp.zeros_like(cc_ref)

    lin = pl.program_id(0) * pl.num_programs(1) + j
    n_lin = pl.num_programs(0) * pl.num_programs(1)
    eps = wg_in_ref.shape[1]
    wslot = lin & 1

    def w_in_copies(step_, slot_):
        src = pl.ds(step_ * eps, eps)
        return [pltpu.make_async_copy(h.at[src], v.at[slot_], win_sems.at[slot_])
                for h, v in ((weg_hbm, wg_in_ref), (weu_hbm, wu_in_ref), (wed_hbm, wd_in_ref))]

    def w_out_copies(step_, slot_):
        dst = pl.ds(step_ * eps, eps)
        return [pltpu.make_async_copy(v.at[slot_], h.at[dst], wout_sems.at[slot_])
                for h, v in ((wgu_hbm, wgu_out_ref), (wdb_hbm, wd_out_ref))]

    def cast_piece(p):
        parts = TOP_K // eps
        e, part = divmod(p, parts)
        r0, r1 = part * (D_MODEL // parts), (part + 1) * (D_MODEL // parts)
        d0, d1 = part * (D_EXPERT // parts), (part + 1) * (D_EXPERT // parts)
        wgu_out_ref[wslot, e, r0:r1, 0:D_EXPERT] = wg_in_ref[wslot, e, r0:r1, :].astype(bf16)
        wgu_out_ref[wslot, e, r0:r1, D_EXPERT:2 * D_EXPERT] = wu_in_ref[wslot, e, r0:r1, :].astype(bf16)
        wd_out_ref[wslot, e, d0:d1, :] = wd_in_ref[wslot, e, d0:d1, :].astype(bf16)

    @pl.when(lin == 0)
    def _():
        for cp in w_in_copies(lin, wslot):
            cp.start()

    @pl.when(lin + 1 < n_lin)
    def _():
        for cp in w_in_copies(lin + 1, 1 - wslot):
            cp.start()

    x = x_ref[0]
    u = _rms(x, mixw_ref[...]).astype(bf16)
    qkv = _dot(u, win_ref[:, 3 * D_CONV:D_IN_PROJ])
    q = qkv[:, 0:D_ATTN]
    k = qkv[:, D_ATTN:D_ATTN + D_KV]
    v = qkv[:, D_ATTN + D_KV:D_ATTN + 2 * D_KV]
    conv_piece = 2 * LANES
    conv_pieces = []

    lane = lax.broadcasted_iota(jnp.int32, (1, LANES), 1)
    lane_lo = lane < HEAD_DIM
    qn = _head_rms(q, qw_ref[...], hones_ref) * (HEAD_DIM ** -0.5)
    kn = _head_rms(k, kw_ref[...], hones_ref)
    kfull = jnp.concatenate([kc_ref[...], kn], axis=0)
    vfull = jnp.concatenate([vc_ref[...], v], axis=0)
    kc_ref[...] = kn[T - BLK:T, :]
    vc_ref[...] = v[T - BLK:T, :]

    def _rep(a, g):
        r = pltpu.roll(a, HEAD_DIM, axis=1)
        two = jnp.where(lane_lo, a, r) if g == 0 else jnp.where(lane_lo, r, a)
        return jnp.concatenate([two, two], axis=1).astype(bf16)

    k_rep = [_rep(kfull, g) for g in range(N_KV_HEADS)]
    v_rep = [_rep(vfull, g) for g in range(N_KV_HEADS)]

    lane256 = lax.broadcasted_iota(jnp.int32, (1, 2 * LANES), 1)
    head_of_lane = lane256 // HEAD_DIM
    first_f = (j == 0).astype(f32)
    prev_key_mask = jnp.where(lane256 < BLK, first_f * NEG_BIG, 0.0)
    ones_cols = jnp.ones((2 * BLK, LANES), bf16)

    def lanes2(a):
        return jnp.concatenate([a, a], axis=1)

    attn_rows = []
    for i in range(T // BLK):
        grp_out = []
        for g in range(N_KV_HEADS):
            if len(conv_pieces) * conv_piece < 3 * D_CONV:
                c0 = len(conv_pieces) * conv_piece
                conv_pieces.append(_dot(u, win_ref[:, c0:c0 + conv_piece]))
            qg = qn[i * BLK:(i + 1) * BLK, g * 256:(g + 1) * 256]
            qst = jnp.concatenate(
                [jnp.where(head_of_lane == hi, qg, 0.0) for hi in range(GQA_GROUP)], axis=0).astype(bf16)
            kk = k_rep[g][i * BLK:i * BLK + 2 * BLK, :]
            vv = jnp.concatenate([v_rep[g][i * BLK:i * BLK + 2 * BLK, :], ones_cols], axis=1)
            s = _dot_nt(qst, kk) + bias_ref[g]
            if i == 0:
                s = s + prev_key_mask
            sink = sink_ref[g]
            m = jnp.maximum(jnp.max(s, axis=-1, keepdims=True), sink)
            e = jnp.exp(s - lanes2(m)).astype(bf16)
            r = _dot(e, vv)
            inv = 1.0 / (r[:, 2 * LANES:3 * LANES] + jnp.exp(sink - m))
            r = r[:, 0:2 * LANES] * lanes2(inv)
            o = jnp.where(head_of_lane == 0, r[0:BLK], 0.0)
            for hi in range(1, GQA_GROUP):
                o = jnp.where(head_of_lane == hi, r[hi * BLK:(hi + 1) * BLK], o)
            grp_out.append(o)
        attn_rows.append(jnp.concatenate(grp_out, axis=1))
    y_attn = jnp.concatenate(attn_rows, axis=0)

    assert len(conv_pieces) * conv_piece == 3 * D_CONV
    conv_proj = jnp.concatenate(conv_pieces, axis=1)
    b_gate = conv_proj[:, 0:D_CONV]
    c_gate = conv_proj[:, D_CONV:2 * D_CONV]
    hh = conv_proj[:, 2 * D_CONV:3 * D_CONV]
    ch = c_gate * hh
    prev = cc_ref[...]
    p6 = prev[6:7, :]
    p7 = prev[7:8, :]
    row = lax.broadcasted_iota(jnp.int32, (T, D_CONV), 0)
    ch_m1 = jnp.where(row == 0, p7, pltpu.roll(ch, 1, axis=0))
    ch_m2 = jnp.where(row == 0, p6, jnp.where(row == 1, p7, pltpu.roll(ch, 2, axis=0)))
    cw = convw_ref[...]
    y_conv = b_gate * (cw[0:1, :] * ch_m2 + cw[1:2, :] * ch_m1 + cw[2:3, :] * ch)
    cc_ref[...] = ch[T - 8:T, :]

    y_mix = jnp.concatenate([_rms(y_conv, cnw_ref[...]), _rms(y_attn, anw_ref[...])], axis=1)
    x1 = x + _dot(y_mix.astype(bf16), wout_ref[...])

    xn = _rms(x1, fnw_ref[...])
    xn_ref[...] = xn
    xh = xn.astype(bf16)

    xl = (xn - xh.astype(f32)).astype(bf16)
    w_hl = wrhl_ref[...]
    a_hl = _dot_nt(w_hl, xh)
    logits = a_hl[0:N_EXPERTS] + a_hl[N_EXPERTS:2 * N_EXPERTS] + _dot_nt(w_hl[0:N_EXPERTS], xl)

    gs = _dot(xh, wsg_ref[...])
    us = _dot(xh, wsu_ref[...])
    hs = (gs * jax.nn.sigmoid(gs) * us).astype(bf16)
    x2 = x1 + _dot(hs, wsd_ref[...])
    x2_ref[...] = x2

    scores = jax.nn.sigmoid(logits)
    biased = scores + rbias_ref[...]

    sub8 = lax.broadcasted_iota(jnp.int32, (GROUP_SIZE, T), 0)
    gscore = []
    for g in range(N_GROUPS):
        blk = biased[g * GROUP_SIZE:(g + 1) * GROUP_SIZE, :]
        m1 = jnp.max(blk, axis=0, keepdims=True)
        first = jnp.min(jnp.where(blk == m1, sub8, GROUP_SIZE), axis=0, keepdims=True)
        m2 = jnp.max(jnp.where(sub8 == first, -jnp.inf, blk), axis=0, keepdims=True)
        gscore.append(m1 + m2)
    masked_blocks = []
    for g in range(N_GROUPS):
        rank = jnp.zeros((1, T), jnp.int32)
        for o_ in range(N_GROUPS):
            if o_ == g:
                continue
            if o_ < g:
                ahead = gscore[o_] >= gscore[g]
            else:
                ahead = gscore[o_] > gscore[g]
            rank = rank + ahead.astype(jnp.int32)
        keep = rank < TOPK_GROUPS
        blk = biased[g * GROUP_SIZE:(g + 1) * GROUP_SIZE, :]
        masked_blocks.append(jnp.where(keep, blk, -jnp.inf))
    cur = jnp.concatenate(masked_blocks, axis=0)

    eiota = lax.broadcasted_iota(jnp.int32, (N_EXPERTS, T), 0)
    row8 = lax.broadcasted_iota(jnp.int32, (TOP_K, T), 0)
    w_out = jnp.zeros((TOP_K, T), f32)
    sel_dense = jnp.zeros((N_EXPERTS, T), f32)
    onehots = []
    for cp in w_in_copies(lin, wslot):
        cp.wait()

    @pl.when(lin >= 2)
    def _():
        for cp in w_out_copies(lin - 2, wslot):
            cp.wait()

    for kk_ in range(TOP_K):
        cast_piece(kk_)
        mx = jnp.max(cur, axis=0, keepdims=True)
        sel_idx = jnp.min(jnp.where(cur == mx, eiota, N_EXPERTS), axis=0, keepdims=True)
        onehot = eiota == sel_idx
        w_k = jnp.sum(jnp.where(onehot, scores, 0.0), axis=0, keepdims=True)
        cur = jnp.where(onehot, -jnp.inf, cur)
        sel_dense = jnp.where(onehot, 1.0, sel_dense)
        w_out = jnp.where(row8 == kk_, w_k, w_out)
        onehots.append(onehot)
    for cp in w_out_copies(lin, wslot):
        cp.start()
    wsum = jnp.sum(w_out, axis=0, keepdims=True)
    wts_ref[...] = w_out / wsum * ROUTED_SCALE

    cum = _dot(sel_dense.astype(bf16), tri_ref[...])
    cnt = jnp.broadcast_to(jnp.sum(sel_dense, axis=1, keepdims=True), (N_EXPERTS, LANES))
    erow = lax.broadcasted_iota(jnp.int32, (N_EXPERTS, LANES), 0)
    incl = cnt
    step = 1
    while step < N_EXPERTS:
        incl = incl + jnp.where(erow >= step, pltpu.roll(incl, step, axis=0), 0.0)
        step *= 2
    lstart = incl - cnt
    ld_dense = lstart[:, 0:1] + cum
    ld_out = jnp.zeros((TOP_K, T), f32)
    for kk_ in range(TOP_K):
        p_k = jnp.sum(jnp.where(onehots[kk_], ld_dense, 0.0), axis=0, keepdims=True)
        ld_out = jnp.where(row8 == kk_, p_k, ld_out)
    tile_idx = pl.program_id(0) * pl.num_programs(1) + j
    slot_base = ((tile_idx & 1) * TILE_ROWS).astype(f32)
    ld_ref[...] = ((ld_out + slot_base) * ROW_SUB).astype(jnp.int32)
    cnt_ref[...] = cnt.astype(jnp.int32)

    @pl.when(lin == n_lin - 1)
    def _():
        @pl.when(lin >= 1)
        def _():
            for cp in w_out_copies(lin - 1, 1 - wslot):
                cp.wait()

        for cp in w_out_copies(lin, wslot):
            cp.wait()


def _mixer_call(x, mixw, win, convw, qw, kw, sink_col, bias_tab, cnw, anw, wout, fnw,
                wsg, wsu, wsd, wrhl, rbias, tri, head_ones, w_eg, w_eu, w_ed):
    B, S, D = x.shape
    T = MIX_TILE
    nt = S // T
    N = B * S
    assert N_EXPERTS % (B * nt) == 0
    eps = N_EXPERTS // (B * nt)
    assert TOP_K % eps == 0 and D_EXPERT % (16 * (TOP_K // eps)) == 0
    any_spec = pl.BlockSpec(memory_space=pl.ANY)

    def full(a):
        nd = a.ndim
        return pl.BlockSpec(a.shape, lambda b, j, _nd=nd: (0,) * _nd)

    tok_spec = pl.BlockSpec((TOP_K, T), lambda b, j: (0, b * nt + j))
    in_arrays = [mixw, win, convw, qw, kw, sink_col, bias_tab, cnw, anw, wout, fnw,
                 wsg, wsu, wsd, wrhl, rbias, tri, head_ones]
    return pl.pallas_call(
        _mixer_kernel,
        grid=(B, nt),
        in_specs=([pl.BlockSpec((1, T, D), lambda b, j: (b, j, 0))] + [full(a) for a in in_arrays]
                  + [any_spec, any_spec, any_spec]),
        out_specs=[
            pl.BlockSpec((T, D), lambda b, j: (b * nt + j, 0)),
            pl.BlockSpec((T, D), lambda b, j: (b * nt + j, 0)),
            tok_spec, tok_spec,
            pl.BlockSpec((N_EXPERTS, LANES), lambda b, j: (b * nt + j, 0)),
            any_spec, any_spec,
        ],
        out_shape=[
            jax.ShapeDtypeStruct((N, D), jnp.float32),
            jax.ShapeDtypeStruct((N, D), jnp.float32),
            jax.ShapeDtypeStruct((TOP_K, N), jnp.float32),
            jax.ShapeDtypeStruct((TOP_K, N), jnp.int32),
            jax.ShapeDtypeStruct((B * nt * N_EXPERTS, LANES), jnp.int32),
            jax.ShapeDtypeStruct((N_EXPERTS, D, 2 * D_EXPERT), jnp.bfloat16),
            jax.ShapeDtypeStruct((N_EXPERTS, D_EXPERT, D), jnp.bfloat16),
        ],
        scratch_shapes=[
            pltpu.VMEM((BLK, D_KV), jnp.float32),
            pltpu.VMEM((BLK, D_KV), jnp.float32),
            pltpu.VMEM((8, D_CONV), jnp.float32),
            pltpu.VMEM((2, eps, D, D_EXPERT), jnp.float32),
            pltpu.VMEM((2, eps, D, D_EXPERT), jnp.float32),
            pltpu.VMEM((2, eps, D_EXPERT, D), jnp.float32),
            pltpu.VMEM((2, eps, D, 2 * D_EXPERT), jnp.bfloat16),
            pltpu.VMEM((2, eps, D_EXPERT, D), jnp.bfloat16),
            pltpu.SemaphoreType.DMA((2,)),
            pltpu.SemaphoreType.DMA((2,)),
        ],
        compiler_params=pltpu.CompilerParams(
            dimension_semantics=("arbitrary", "arbitrary"), vmem_limit_bytes=VMEM_LIMIT),
        name="mixer_router",
    )(x, *in_arrays, w_eg, w_eu, w_ed)


def _rows(ref, row, nrows):
    return ref.at[pl.ds(pl.multiple_of(row * ROW_SUB, ROW_SUB), nrows * ROW_SUB)]


def _segment_copies(src, src_row, dst, dst_row, count, sem):
    chunk = 1 << SEG_CHUNK_LOG2

    def big(i, c):
        o = i * chunk
        pltpu.make_async_copy(_rows(src, src_row + o, chunk), _rows(dst, dst_row + o, chunk), sem).start()
        return c

    lax.fori_loop(0, count >> SEG_CHUNK_LOG2, big, 0)
    for bit in range(SEG_CHUNK_LOG2 - 1, -1, -1):
        o = (count >> (bit + 1)) << (bit + 1)

        @pl.when(((count >> bit) & 1) == 1)
        def _():
            n = 1 << bit
            pltpu.make_async_copy(_rows(src, src_row + o, n), _rows(dst, dst_row + o, n), sem).start()


def _wait_tile(hbm_ref, stag_ref, slot_row, sem, to_hbm):
    vm = _rows(stag_ref, slot_row, TILE_ROWS)
    hb = _rows(hbm_ref, 0, TILE_ROWS)
    (pltpu.make_async_copy(vm, hb, sem) if to_hbm else pltpu.make_async_copy(hb, vm, sem)).wait()


def _tile_relayout_copies(std_hbm, tile, rpt_ref, slot, sem, to_vmem):
    T = MIX_TILE
    row0 = tile * T
    row0 = row0 if isinstance(row0, int) else pl.multiple_of(row0, T)
    slot0 = slot * T
    slot0 = slot0 if isinstance(slot0, int) else pl.multiple_of(slot0, T)
    copies = []
    for c in range(ROW_SUB):
        hb = std_hbm.at[pl.ds(row0, T), pl.ds(c * LANES, LANES)]
        vm = rpt_ref.at[pl.ds(slot0, T), c, :]
        copies.append(pltpu.make_async_copy(hb, vm, sem) if to_vmem else pltpu.make_async_copy(vm, hb, sem))
    return copies


def _dispatch_kernel(lsrc_ref, gdst_ref, cnt_ref, ld_ref, xn_hbm, xs_ref, stag_ref, xin_ref, sems, in_sems):
    b = pl.program_id(0)
    nb = pl.num_programs(0)
    T = MIX_TILE
    slot = b & 1
    base = slot * TILE_ROWS

    def fetch_rows(tile, slot_):
        return _tile_relayout_copies(xn_hbm, tile, xin_ref, slot_, in_sems.at[slot_], True)

    @pl.when(b == 0)
    def _():
        for cp in fetch_rows(b, slot):
            cp.start()

    @pl.when(b + 1 < nb)
    def _():
        for cp in fetch_rows(b + 1, 1 - slot):
            cp.start()

    for cp in fetch_rows(b, slot):
        cp.wait()

    for kk_ in range(TOP_K):
        def row(t, c, kk_=kk_):
            r8 = ld_ref[0, 0, kk_ * T + t]
            stag_ref[pl.ds(pl.multiple_of(r8, ROW_SUB), ROW_SUB), :] = xin_ref[slot * T + t]
            return c

        lax.fori_loop(0, T, row, 0, unroll=128)

    def seg(e, c):
        s = b * N_EXPERTS + e
        _segment_copies(stag_ref, base + lsrc_ref[s], xs_ref, gdst_ref[s], cnt_ref[s], sems.at[slot])
        return c

    lax.fori_loop(0, N_EXPERTS, seg, 0, unroll=4)

    @pl.when(b > 0)
    def _():
        _wait_tile(xs_ref, stag_ref, (1 - slot) * TILE_ROWS, sems.at[1 - slot], True)

    @pl.when(b == nb - 1)
    def _():
        _wait_tile(xs_ref, stag_ref, base, sems.at[slot], True)


def _dispatch_call(lsrc, gdst, cnt, ld_t, xn):
    nt = ld_t.shape[0]
    T = MIX_TILE
    smem = pl.BlockSpec(memory_space=pltpu.SMEM)
    return pl.pallas_call(
        _dispatch_kernel,
        grid=(nt,),
        in_specs=[
            smem, smem, smem,
            pl.BlockSpec((1, 1, TILE_ROWS), lambda i: (i, 0, 0), memory_space=pltpu.SMEM),
            pl.BlockSpec(memory_space=pl.ANY),
        ],
        out_specs=pl.BlockSpec(memory_space=pl.ANY),
        out_shape=jax.ShapeDtypeStruct((nt * TILE_ROWS * ROW_SUB, LANES), jnp.float32),
        scratch_shapes=[pltpu.VMEM((2 * TILE_ROWS * ROW_SUB, LANES), jnp.float32),
                        pltpu.VMEM((2 * T, ROW_SUB, LANES), jnp.float32),
                        pltpu.SemaphoreType.DMA((2,)),
                        pltpu.SemaphoreType.DMA((2,))],
        compiler_params=pltpu.CompilerParams(
            dimension_semantics=("arbitrary",), vmem_limit_bytes=VMEM_LIMIT),
        name="moe_dispatch",
    )(lsrc, gdst, cnt, ld_t, xn)


def _expert_kernel(blk_ref, exp_ref, lo_ref, hi_ref, newblk_ref, endblk_ref,
                   xs_hbm, wgu_ref, wd_ref, ys_hbm,
                   xbuf_ref, ybuf_ref, in_sems, out_sems):
    i = pl.program_id(0)
    n_items = pl.num_programs(0)
    bf16 = jnp.bfloat16
    SB = EXPERT_SUB
    n_blocks = xs_hbm.shape[0] // EXPERT_BLOCK
    blk = blk_ref[i]
    lo = lo_ref[i]
    hi = hi_ref[i]

    class _Group:
        def __init__(self, copies):
            self.copies = copies

        def start(self):
            for cp in self.copies:
                cp.start()

        def wait(self):
            for cp in self.copies:
                cp.wait()

    def hbm_chunk(ref, b, c):
        start = b * EXPERT_BLOCK
        start = start if isinstance(start, int) else pl.multiple_of(start, EXPERT_BLOCK)
        return ref.at[pl.ds(start, EXPERT_BLOCK), c, :]

    def vmem_chunk(ref, slot, c):
        start = slot * EXPERT_BLOCK
        start = start if isinstance(start, int) else pl.multiple_of(start, EXPERT_BLOCK)
        return ref.at[pl.ds(start, EXPERT_BLOCK), pl.ds(c * LANES, LANES)]

    def in_copy(b, slot):
        return _Group([pltpu.make_async_copy(hbm_chunk(xs_hbm, b, c), vmem_chunk(xbuf_ref, slot, c), in_sems.at[slot])
                       for c in range(ROW_SUB)])

    def out_copy(b, slot):
        return _Group([pltpu.make_async_copy(vmem_chunk(ybuf_ref, slot, c), hbm_chunk(ys_hbm, b, c), out_sems.at[slot])
                       for c in range(ROW_SUB)])

    @pl.when(i == 0)
    def _():
        for b0 in range(EXPERT_IN_SLOTS - 1):
            in_copy(b0, b0).start()

    @pl.when(newblk_ref[i] == 1)
    def _():
        ahead = blk + (EXPERT_IN_SLOTS - 1)

        @pl.when(ahead < n_blocks)
        def _():
            in_copy(ahead, lax.rem(ahead, EXPERT_IN_SLOTS)).start()

        in_copy(blk, lax.rem(blk, EXPERT_IN_SLOTS)).wait()

        @pl.when(blk >= EXPERT_OUT_SLOTS)
        def _():
            out_copy(blk - EXPERT_OUT_SLOTS, lax.rem(blk, EXPERT_OUT_SLOTS)).wait()

    xbase = pl.multiple_of(lax.rem(blk, EXPERT_IN_SLOTS) * EXPERT_BLOCK, EXPERT_BLOCK)
    ybase = pl.multiple_of(lax.rem(blk, EXPERT_OUT_SLOTS) * EXPERT_BLOCK, EXPERT_BLOCK)

    def run(r0, nrows, merge_first):
        xb = xbuf_ref[pl.ds(xbase + r0, nrows), :].astype(bf16)
        gu = _dot(xb, wgu_ref[0])
        g = gu[:, 0:D_EXPERT]
        u = gu[:, D_EXPERT:2 * D_EXPERT]
        h = (g * jax.nn.sigmoid(g) * u).astype(bf16)
        y = _dot(h, wd_ref[0])
        first = SB if merge_first else 0
        if merge_first:
            rows = r0 + lax.broadcasted_iota(jnp.int32, (SB, D_MODEL), 0)
            keep = jnp.logical_and(rows >= lo, rows < hi)
            old = ybuf_ref[pl.ds(ybase + r0, SB), :]
            ybuf_ref[pl.ds(ybase + r0, SB), :] = jnp.where(keep, y[0:SB], old)
        if nrows > first:
            ybuf_ref[pl.ds(ybase + r0 + first, nrows - first), :] = y[first:nrows]

    n_sub = EXPERT_BLOCK // SB
    from_top = lo == 0
    sub_aligned = (lo & (SB - 1)) == 0
    to_end = jnp.logical_and(jnp.logical_and(lo > 0, hi == EXPERT_BLOCK), jnp.logical_not(sub_aligned))
    subs_used = (hi + SB - 1) // SB
    first_sub = lo // SB
    for k in range(1, n_sub + 1):
        @pl.when(jnp.logical_and(from_top, subs_used == k))
        def _(k=k):
            run(0, k * SB, False)

    for q in range(n_sub):
        @pl.when(jnp.logical_and(to_end, first_sub == q))
        def _(q=q):
            run(q * SB, EXPERT_BLOCK - q * SB, True)

    for q in range(n_sub):
        r0 = q * SB
        inside = jnp.logical_and(lo > 0, jnp.logical_or(hi < EXPERT_BLOCK, sub_aligned))

        @pl.when(jnp.logical_and(inside, jnp.minimum(hi, r0 + SB) > jnp.maximum(lo, r0)))
        def _(r0=r0):
            @pl.when(lo <= r0)
            def _():
                run(r0, SB, False)

            @pl.when(lo > r0)
            def _():
                run(r0, SB, True)

    @pl.when(endblk_ref[i] == 1)
    def _():
        out_copy(blk, lax.rem(blk, EXPERT_OUT_SLOTS)).start()

    @pl.when(i == n_items - 1)
    def _():
        for b1 in range(n_blocks - EXPERT_OUT_SLOTS, n_blocks):
            out_copy(b1, b1 % EXPERT_OUT_SLOTS).wait()


def _expert_call(item_blk, item_exp, item_lo, item_hi, item_newblk, item_endblk, xs_rp, w_gu, w_d):
    D = D_MODEL
    BS = EXPERT_BLOCK * ROW_SUB
    n_items = item_blk.shape[0]
    assert xs_rp.shape[0] % BS == 0 and xs_rp.shape[0] // BS >= max(EXPERT_IN_SLOTS, EXPERT_OUT_SLOTS)
    xs3 = xs_rp.reshape(xs_rp.shape[0] // ROW_SUB, ROW_SUB, LANES)

    def w_map(i, blk, exp, lo, hi, newblk, endblk):
        return (exp[i], 0, 0)

    grid_spec = pltpu.PrefetchScalarGridSpec(
        num_scalar_prefetch=6,
        grid=(n_items,),
        in_specs=[
            pl.BlockSpec(memory_space=pl.ANY),
            pl.BlockSpec((1, D, 2 * D_EXPERT), w_map),
            pl.BlockSpec((1, D_EXPERT, D), w_map),
        ],
        out_specs=pl.BlockSpec(memory_space=pl.ANY),
        scratch_shapes=[
            pltpu.VMEM((EXPERT_IN_SLOTS * EXPERT_BLOCK, D), jnp.float32),
            pltpu.VMEM((EXPERT_OUT_SLOTS * EXPERT_BLOCK, D), jnp.float32),
            pltpu.SemaphoreType.DMA((EXPERT_IN_SLOTS,)),
            pltpu.SemaphoreType.DMA((EXPERT_OUT_SLOTS,)),
        ],
    )
    return pl.pallas_call(
        _expert_kernel,
        grid_spec=grid_spec,
        out_shape=jax.ShapeDtypeStruct(xs3.shape, jnp.float32),
        compiler_params=pltpu.CompilerParams(
            dimension_semantics=("arbitrary",), vmem_limit_bytes=VMEM_LIMIT),
        name="moe_experts",
    )(item_blk, item_exp, item_lo, item_hi, item_newblk, item_endblk, xs3, w_gu, w_d).reshape(xs_rp.shape)


def _combine_kernel(lsrc_ref, gdst_ref, cnt_ref, ld_ref, w_ref, x2_hbm, ys_ref, out_hbm,
                    stag_ref, acc_ref, sems, in_sems, out_sems):
    b = pl.program_id(0)
    nb = pl.num_programs(0)
    T = MIX_TILE
    slot = b & 1
    base = slot * TILE_ROWS
    aslot = lax.rem(b, COMBINE_ACC_SLOTS)

    def in_copies(tile):
        s_ = lax.rem(tile, COMBINE_ACC_SLOTS)
        return _tile_relayout_copies(x2_hbm, tile, acc_ref, s_, in_sems.at[s_], True)

    def out_copies(tile):
        s_ = lax.rem(tile, COMBINE_ACC_SLOTS)
        return _tile_relayout_copies(out_hbm, tile, acc_ref, s_, out_sems.at[s_], False)

    @pl.when(b >= COMBINE_ACC_SLOTS - 1)
    def _():
        for cp in out_copies(b - (COMBINE_ACC_SLOTS - 1)):
            cp.wait()

    @pl.when(b == 0)
    def _():
        for cp in in_copies(b):
            cp.start()

    @pl.when(b + 1 < nb)
    def _():
        for cp in in_copies(b + 1):
            cp.start()

    def fetch(tile, slot_):
        def seg(e, c):
            s = tile * N_EXPERTS + e
            _segment_copies(ys_ref, gdst_ref[s], stag_ref, slot_ * TILE_ROWS + lsrc_ref[s], cnt_ref[s],
                            sems.at[slot_])
            return c

        lax.fori_loop(0, N_EXPERTS, seg, 0, unroll=4)

    @pl.when(b == 0)
    def _():
        fetch(b, slot)

    @pl.when(b + 1 < nb)
    def _():
        fetch(b + 1, 1 - slot)

    _wait_tile(ys_ref, stag_ref, base, sems.at[slot], False)
    for cp in in_copies(b):
        cp.wait()

    def tok(t, c):
        a = acc_ref[aslot * T + t]
        for kk_ in range(TOP_K):
            r8 = ld_ref[0, 0, kk_ * T + t]
            w = w_ref[0, 0, kk_ * T + t]
            a = a + w * stag_ref[pl.ds(pl.multiple_of(r8, ROW_SUB), ROW_SUB), :]
        acc_ref[aslot * T + t] = a
        return c

    lax.fori_loop(0, T, tok, 0, unroll=32)
    for cp in out_copies(b):
        cp.start()

    @pl.when(b == nb - 1)
    def _():
        for back in range(COMBINE_ACC_SLOTS - 2, -1, -1):
            @pl.when(b >= back)
            def _(back=back):
                for cp in out_copies(b - back):
                    cp.wait()


def _combine_call(lsrc, gdst, cnt, ld_t, w_t, x2, ys_rp):
    N, D = x2.shape
    nt = ld_t.shape[0]
    T = MIX_TILE
    smem = pl.BlockSpec(memory_space=pltpu.SMEM)
    smem_blk = pl.BlockSpec((1, 1, TILE_ROWS), lambda i: (i, 0, 0), memory_space=pltpu.SMEM)
    return pl.pallas_call(
        _combine_kernel,
        grid=(nt,),
        in_specs=[
            smem, smem, smem, smem_blk, smem_blk,
            pl.BlockSpec(memory_space=pl.ANY),
            pl.BlockSpec(memory_space=pl.ANY),
        ],
        out_specs=pl.BlockSpec(memory_space=pl.ANY),
        out_shape=jax.ShapeDtypeStruct((N, D), jnp.float32),
        scratch_shapes=[
            pltpu.VMEM((2 * TILE_ROWS * ROW_SUB, LANES), jnp.float32),
            pltpu.VMEM((COMBINE_ACC_SLOTS * T, ROW_SUB, LANES), jnp.float32),
            pltpu.SemaphoreType.DMA((2,)),
            pltpu.SemaphoreType.DMA((COMBINE_ACC_SLOTS,)),
            pltpu.SemaphoreType.DMA((COMBINE_ACC_SLOTS,)),
        ],
        compiler_params=pltpu.CompilerParams(
            dimension_semantics=("arbitrary",), vmem_limit_bytes=VMEM_LIMIT),
        name="moe_combine",
    )(lsrc, gdst, cnt, ld_t, w_t, x2, ys_rp)


def _work_items(gstart, totals, n_items):
    BM = EXPERT_BLOCK
    i32 = jnp.int32
    gend = gstart + totals
    first_blk = gstart // BM
    last_blk = jnp.maximum(gend - 1, gstart) // BM
    n_e = jnp.where(totals > 0, last_blk - first_blk + 1, 0)
    item_end = jnp.cumsum(n_e)
    item_start = item_end - n_e
    n_real = item_end[-1]
    j = jnp.minimum(jnp.arange(n_items, dtype=i32), n_real - 1)
    exp = jnp.sum((item_end[None, :] <= j[:, None]).astype(i32), axis=1)
    onehot = exp[:, None] == jnp.arange(N_EXPERTS, dtype=i32)[None, :]

    def of_exp(v):
        return jnp.sum(jnp.where(onehot, v[None, :], 0), axis=1)

    blk = of_exp(first_blk) + (j - of_exp(item_start))
    lo = jnp.maximum(of_exp(gstart), blk * BM) - blk * BM
    hi = jnp.minimum(of_exp(gend), (blk + 1) * BM) - blk * BM
    real = jnp.arange(n_items, dtype=i32) < n_real
    hi = jnp.where(real, hi, lo)
    prev_blk = jnp.concatenate([jnp.full((1,), -1, i32), blk[:-1]])
    next_blk = jnp.concatenate([blk[1:], jnp.full((1,), -1, i32)])
    is_last = jnp.arange(n_items, dtype=i32) == n_real - 1
    newblk = jnp.logical_and(real, blk != prev_blk).astype(i32)
    endblk = jnp.logical_and(real, jnp.logical_or(blk != next_blk, is_last)).astype(i32)
    return blk.astype(i32), exp.astype(i32), lo.astype(i32), hi.astype(i32), newblk, endblk


def _layer(x, mix_norm_w, w_in, conv_w, q_norm_w, k_norm_w, sinks, conv_out_norm_w,
           attn_out_norm_w, w_out, ffn_norm_w, w_router, router_bias, w_eg, w_eu, w_ed,
           w_sg, w_su, w_sd):
    B, S, D = x.shape
    N = B * S
    T = MIX_TILE
    nt = N // T
    bf16 = jnp.bfloat16
    f32 = jnp.float32
    i32 = jnp.int32

    wr_t = w_router.astype(f32).T
    wr_hi = wr_t.astype(bf16)
    wr_hl = jnp.concatenate([wr_hi, (wr_t - wr_hi.astype(f32)).astype(bf16)], axis=0)
    sink_col = jnp.broadcast_to(
        jnp.repeat(sinks.astype(f32), BLK).reshape(N_KV_HEADS, GQA_GROUP * BLK, 1),
        (N_KV_HEADS, GQA_GROUP * BLK, LANES))
    tri = jnp.asarray(np.triu(np.ones((T, T), np.float32), k=1), dtype=bf16)
    head_ones = jnp.asarray(np.kron(np.eye(2 * LANES // HEAD_DIM, dtype=np.float32),
                                    np.ones((HEAD_DIM, HEAD_DIM), np.float32)), dtype=bf16)
    bias_tab = jnp.asarray(_attn_bias_table())

    x2, xn, wts, ld, cnt, w_gu, w_d = _mixer_call(
        x, mix_norm_w.reshape(1, D), w_in.astype(bf16), conv_w,
        jnp.tile(q_norm_w, N_HEADS).reshape(1, D_ATTN), jnp.tile(k_norm_w, N_KV_HEADS).reshape(1, D_KV),
        sink_col, bias_tab, conv_out_norm_w.reshape(1, D_CONV), attn_out_norm_w.reshape(1, D_ATTN),
        w_out.astype(bf16), ffn_norm_w.reshape(1, D), w_sg.astype(bf16), w_su.astype(bf16),
        w_sd.astype(bf16), wr_hl, router_bias.astype(f32).reshape(N_EXPERTS, 1), tri, head_ones,
        w_eg.astype(f32), w_eu.astype(f32), w_ed.astype(f32))

    cnt_te = cnt[:, 0].reshape(nt, N_EXPERTS)
    lsrc = jnp.cumsum(cnt_te, axis=1) - cnt_te
    before = jnp.cumsum(cnt_te, axis=0) - cnt_te
    totals = jnp.sum(cnt_te, axis=0)
    gstart = jnp.cumsum(totals) - totals
    gdst = gstart[None, :] + before
    n_items = (N * TOP_K) // EXPERT_BLOCK + N_EXPERTS
    items = _work_items(gstart.astype(i32), totals.astype(i32), n_items)

    def tile_major(a):
        return a.reshape(TOP_K, nt, T).transpose(1, 0, 2).reshape(nt, 1, TILE_ROWS)

    ld_t = tile_major(ld)
    seg_tabs = (lsrc.reshape(-1).astype(i32), gdst.reshape(-1).astype(i32), cnt_te.reshape(-1).astype(i32))
    xs_rp = _dispatch_call(*seg_tabs, ld_t, xn)
    ys_rp = _expert_call(*items, xs_rp, w_gu, w_d)
    out = _combine_call(*seg_tabs, ld_t, tile_major(wts), x2, ys_rp)
    return out.reshape(B, S, D)


def kernel(x, mix_norm_w, w_in, conv_w, q_norm_w, k_norm_w, sinks, conv_out_norm_w, attn_out_norm_w, w_out, ffn_norm_w, w_router, router_bias, w_exp_gate, w_exp_up, w_exp_down, w_sh_gate, w_sh_up, w_sh_down):
    B, S, D = x.shape
    assert D == D_MODEL and w_in.shape == (1, D_MODEL, D_IN_PROJ) and w_exp_gate.shape == (1, N_EXPERTS, D_MODEL, D_EXPERT)
    assert S % MIX_TILE == 0 and (B * S * TOP_K) % EXPERT_BLOCK == 0
    return _layer(x, mix_norm_w[0], w_in[0], conv_w[0], q_norm_w[0], k_norm_w[0], sinks[0],
                  conv_out_norm_w[0], attn_out_norm_w[0], w_out[0], ffn_norm_w[0], w_router[0],
                  router_bias[0], w_exp_gate[0], w_exp_up[0], w_exp_down[0], w_sh_gate[0],
                  w_sh_up[0], w_sh_down[0])
```

```python
import numpy as np
import jax
import jax.numpy as jnp
from jax import lax
from jax.experimental import pallas as pl
from jax.experimental.pallas import tpu as pltpu

D_MODEL = 1024
EPS = 1e-6
D_CONV = 512
HEAD_DIM = 64
N_HEADS = 8
N_KV_HEADS = 2
GQA_GROUP = 4
D_ATTN = 512
D_KV = 128
WINDOW = 128
BLK = 128
D_IN_PROJ = 2304
N_EXPERTS = 64
TOP_K = 8
N_GROUPS = 8
GROUP_SIZE = 8
TOPK_GROUPS = 4
D_EXPERT = 256
ROUTED_SCALE = 2.5

LANES = 128
ROW_SUB = 8
MIX_TILE = 512
TILE_ROWS = TOP_K * MIX_TILE
EXPERT_BLOCK = 1024
EXPERT_SUB = 256
EXPERT_IN_SLOTS = 5
EXPERT_OUT_SLOTS = 4
COMBINE_ACC_SLOTS = 3
SEG_CHUNK_LOG2 = 5
NEG_BIG = -1e30
VMEM_LIMIT = 58 * 1024 * 1024


def _alibi_slopes():
    return np.array([2.0 ** (-8.0 * (h + 1) / N_HEADS) for h in range(N_HEADS)], dtype=np.float32)


def _attn_bias_table():
    qi = np.arange(BLK)[:, None]
    kj = np.arange(2 * BLK)[None, :]
    dist = qi - kj + BLK
    inwin = (dist >= 0) & (dist < WINDOW)
    slopes = _alibi_slopes()
    out = np.zeros((N_KV_HEADS, GQA_GROUP * BLK, 2 * BLK), np.float32)
    for g in range(N_KV_HEADS):
        for i in range(GQA_GROUP):
            h = g * GQA_GROUP + i
            out[g, i * BLK:(i + 1) * BLK] = np.where(inwin, -slopes[h] * dist.astype(np.float32), NEG_BIG)
    return out


def _rms(x, w):
    ms = jnp.mean(x * x, axis=-1, keepdims=True)
    return x * lax.rsqrt(ms + EPS) * w


def _head_rms(x, w_tiled, head_ones):
    C = x.shape[1]
    W = min(C, 2 * LANES)
    sq = x * x
    hi = sq.astype(jnp.bfloat16)
    lo = (sq - hi.astype(jnp.float32)).astype(jnp.bfloat16)
    ones = head_ones[0:W, 0:W]
    ssq = jnp.concatenate(
        [_dot(hi[:, c:c + W], ones) + _dot(lo[:, c:c + W], ones) for c in range(0, C, W)], axis=1)
    return x * lax.rsqrt(ssq * (1.0 / HEAD_DIM) + EPS) * w_tiled


def _dot(a, b):
    return jnp.dot(a, b, preferred_element_type=jnp.float32)


def _dot_nt(a, b):
    return lax.dot_general(a, b, (((1,), (1,)), ((), ())), preferred_element_type=jnp.float32)


def _mixer_kernel(x_ref, mixw_ref, win_ref, convw_ref, qw_ref, kw_ref, sink_ref, bias_ref,
                  cnw_ref, anw_ref, wout_ref, fnw_ref, wsg_ref, wsu_ref, wsd_ref,
                  wrhl_ref, rbias_ref, tri_ref, hones_ref, weg_hbm, weu_hbm, wed_hbm,
                  x2_ref, xn_ref, wts_ref, ld_ref, cnt_ref, wgu_hbm, wdb_hbm,
                  kc_ref, vc_ref, cc_ref, wg_in_ref, wu_in_ref, wd_in_ref, wgu_out_ref, wd_out_ref,
                  win_sems, wout_sems):
    j = pl.program_id(1)
    T = MIX_TILE
    bf16 = jnp.bfloat16
    f32 = jnp.float32

    @pl.when(j == 0)
    def _():
        kc_ref[...] = jnp.zeros_like(kc_ref)
        vc_ref[...] = jnp.zeros_like(vc_ref)
        cc_ref[...] = jnp.zeros_like(cc_ref)

    lin = pl.program_id(0) * pl.num_programs(1) + j
    n_lin = pl.num_programs(0) * pl.num_programs(1)
    eps = wg_in_ref.shape[1]
    wslot = lin & 1

    def w_in_copies(step_, slot_):
        src = pl.ds(step_ * eps, eps)
        return [pltpu.make_async_copy(h.at[src], v.at[slot_], win_sems.at[slot_])
                for h, v in ((weg_hbm, wg_in_ref), (weu_hbm, wu_in_ref), (wed_hbm, wd_in_ref))]

    def w_out_copies(step_, slot_):
        dst = pl.ds(step_ * eps, eps)
        return [pltpu.make_async_copy(v.at[slot_], h.at[dst], wout_sems.at[slot_])
                for h, v in ((wgu_hbm, wgu_out_ref), (wdb_hbm, wd_out_ref))]

    def cast_weights():
        for e in range(eps):
            wgu_out_ref[wslot, e, :, 0:D_EXPERT] = wg_in_ref[wslot, e].astype(bf16)
            wgu_out_ref[wslot, e, :, D_EXPERT:2 * D_EXPERT] = wu_in_ref[wslot, e].astype(bf16)
            wd_out_ref[wslot, e] = wd_in_ref[wslot, e].astype(bf16)

    @pl.when(lin == 0)
    def _():
        for cp in w_in_copies(lin, wslot):
            cp.start()

    @pl.when(lin + 1 < n_lin)
    def _():
        for cp in w_in_copies(lin + 1, 1 - wslot):
            cp.start()

    for cp in w_in_copies(lin, wslot):
        cp.wait()

    @pl.when(lin >= 2)
    def _():
        for cp in w_out_copies(lin - 2, wslot):
            cp.wait()

    x = x_ref[0]
    u = _rms(x, mixw_ref[...]).astype(bf16)
    qkv = _dot(u, win_ref[:, 3 * D_CONV:D_IN_PROJ])
    q = qkv[:, 0:D_ATTN]
    k = qkv[:, D_ATTN:D_ATTN + D_KV]
    v = qkv[:, D_ATTN + D_KV:D_ATTN + 2 * D_KV]
    conv_piece = 2 * LANES
    conv_pieces = []

    lane = lax.broadcasted_iota(jnp.int32, (1, LANES), 1)
    lane_lo = lane < HEAD_DIM
    qn = _head_rms(q, qw_ref[...], hones_ref) * (HEAD_DIM ** -0.5)
    kn = _head_rms(k, kw_ref[...], hones_ref)
    kfull = jnp.concatenate([kc_ref[...], kn], axis=0)
    vfull = jnp.concatenate([vc_ref[...], v], axis=0)
    kc_ref[...] = kn[T - BLK:T, :]
    vc_ref[...] = v[T - BLK:T, :]

    def _rep(a, g):
        r = pltpu.roll(a, HEAD_DIM, axis=1)
        two = jnp.where(lane_lo, a, r) if g == 0 else jnp.where(lane_lo, r, a)
        return jnp.concatenate([two, two], axis=1).astype(bf16)

    k_rep = [_rep(kfull, g) for g in range(N_KV_HEADS)]
    v_rep = [_rep(vfull, g) for g in range(N_KV_HEADS)]

    lane256 = lax.broadcasted_iota(jnp.int32, (1, 2 * LANES), 1)
    head_of_lane = lane256 // HEAD_DIM
    first_f = (j == 0).astype(f32)
    prev_key_mask = jnp.where(lane256 < BLK, first_f * NEG_BIG, 0.0)
    ones_cols = jnp.ones((2 * BLK, LANES), bf16)

    def lanes2(a):
        return jnp.concatenate([a, a], axis=1)

    attn_rows = []
    for i in range(T // BLK):
        grp_out = []
        for g in range(N_KV_HEADS):
            if len(conv_pieces) * conv_piece < 3 * D_CONV:
                c0 = len(conv_pieces) * conv_piece
                conv_pieces.append(_dot(u, win_ref[:, c0:c0 + conv_piece]))
            qg = qn[i * BLK:(i + 1) * BLK, g * 256:(g + 1) * 256]
            qst = jnp.concatenate(
                [jnp.where(head_of_lane == hi, qg, 0.0) for hi in range(GQA_GROUP)], axis=0).astype(bf16)
            kk = k_rep[g][i * BLK:i * BLK + 2 * BLK, :]
            vv = jnp.concatenate([v_rep[g][i * BLK:i * BLK + 2 * BLK, :], ones_cols], axis=1)
            s = _dot_nt(qst, kk) + bias_ref[g]
            if i == 0:
                s = s + prev_key_mask
            sink = sink_ref[g]
            m = jnp.maximum(jnp.max(s, axis=-1, keepdims=True), sink)
            e = jnp.exp(s - lanes2(m)).astype(bf16)
            r = _dot(e, vv)
            inv = 1.0 / (r[:, 2 * LANES:3 * LANES] + jnp.exp(sink - m))
            r = r[:, 0:2 * LANES] * lanes2(inv)
            o = jnp.where(head_of_lane == 0, r[0:BLK], 0.0)
            for hi in range(1, GQA_GROUP):
                o = jnp.where(head_of_lane == hi, r[hi * BLK:(hi + 1) * BLK], o)
            grp_out.append(o)
        attn_rows.append(jnp.concatenate(grp_out, axis=1))
    y_attn = jnp.concatenate(attn_rows, axis=0)
    cast_weights()

    assert len(conv_pieces) * conv_piece == 3 * D_CONV
    conv_proj = jnp.concatenate(conv_pieces, axis=1)
    b_gate = conv_proj[:, 0:D_CONV]
    c_gate = conv_proj[:, D_CONV:2 * D_CONV]
    hh = conv_proj[:, 2 * D_CONV:3 * D_CONV]
    ch = c_gate * hh
    prev = cc_ref[...]
    p6 = prev[6:7, :]
    p7 = prev[7:8, :]
    row = lax.broadcasted_iota(jnp.int32, (T, D_CONV), 0)
    ch_m1 = jnp.where(row == 0, p7, pltpu.roll(ch, 1, axis=0))
    ch_m2 = jnp.where(row == 0, p6, jnp.where(row == 1, p7, pltpu.roll(ch, 2, axis=0)))
    cw = convw_ref[...]
    y_conv = b_gate * (cw[0:1, :] * ch_m2 + cw[1:2, :] * ch_m1 + cw[2:3, :] * ch)
    cc_ref[...] = ch[T - 8:T, :]

    y_mix = jnp.concatenate([_rms(y_conv, cnw_ref[...]), _rms(y_attn, anw_ref[...])], axis=1)
    x1 = x + _dot(y_mix.astype(bf16), wout_ref[...])

    xn = _rms(x1, fnw_ref[...])
    xn_ref[...] = xn
    xh = xn.astype(bf16)

    xl = (xn - xh.astype(f32)).astype(bf16)
    w_hl = wrhl_ref[...]
    a_hl = _dot_nt(w_hl, xh)
    logits = a_hl[0:N_EXPERTS] + a_hl[N_EXPERTS:2 * N_EXPERTS] + _dot_nt(w_hl[0:N_EXPERTS], xl)

    gs = _dot(xh, wsg_ref[...])
    us = _dot(xh, wsu_ref[...])
    hs = (gs * jax.nn.sigmoid(gs) * us).astype(bf16)
    x2 = x1 + _dot(hs, wsd_ref[...])
    x2_ref[...] = x2

    scores = jax.nn.sigmoid(logits)
    biased = scores + rbias_ref[...]

    sub8 = lax.broadcasted_iota(jnp.int32, (GROUP_SIZE, T), 0)
    gscore = []
    for g in range(N_GROUPS):
        blk = biased[g * GROUP_SIZE:(g + 1) * GROUP_SIZE, :]
        m1 = jnp.max(blk, axis=0, keepdims=True)
        first = jnp.min(jnp.where(blk == m1, sub8, GROUP_SIZE), axis=0, keepdims=True)
        m2 = jnp.max(jnp.where(sub8 == first, -jnp.inf, blk), axis=0, keepdims=True)
        gscore.append(m1 + m2)
    masked_blocks = []
    for g in range(N_GROUPS):
        rank = jnp.zeros((1, T), jnp.int32)
        for o_ in range(N_GROUPS):
            if o_ == g:
                continue
            if o_ < g:
                ahead = gscore[o_] >= gscore[g]
            else:
                ahead = gscore[o_] > gscore[g]
            rank = rank + ahead.astype(jnp.int32)
        keep = rank < TOPK_GROUPS
        blk = biased[g * GROUP_SIZE:(g + 1) * GROUP_SIZE, :]
        masked_blocks.append(jnp.where(keep, blk, -jnp.inf))
    cur = jnp.concatenate(masked_blocks, axis=0)

    eiota = lax.broadcasted_iota(jnp.int32, (N_EXPERTS, T), 0)
    row8 = lax.broadcasted_iota(jnp.int32, (TOP_K, T), 0)
    w_out = jnp.zeros((TOP_K, T), f32)
    sel_dense = jnp.zeros((N_EXPERTS, T), f32)
    onehots = []
    for kk_ in range(TOP_K):
        mx = jnp.max(cur, axis=0, keepdims=True)
        sel_idx = jnp.min(jnp.where(cur == mx, eiota, N_EXPERTS), axis=0, keepdims=True)
        onehot = eiota == sel_idx
        w_k = jnp.sum(jnp.where(onehot, scores, 0.0), axis=0, keepdims=True)
        cur = jnp.where(onehot, -jnp.inf, cur)
        sel_dense = jnp.where(onehot, 1.0, sel_dense)
        w_out = jnp.where(row8 == kk_, w_k, w_out)
        onehots.append(onehot)
    wsum = jnp.sum(w_out, axis=0, keepdims=True)
    wts_ref[...] = w_out / wsum * ROUTED_SCALE

    cum = _dot(sel_dense.astype(bf16), tri_ref[...])
    cnt = jnp.broadcast_to(jnp.sum(sel_dense, axis=1, keepdims=True), (N_EXPERTS, LANES))
    erow = lax.broadcasted_iota(jnp.int32, (N_EXPERTS, LANES), 0)
    incl = cnt
    step = 1
    while step < N_EXPERTS:
        incl = incl + jnp.where(erow >= step, pltpu.roll(incl, step, axis=0), 0.0)
        step *= 2
    lstart = incl - cnt
    ld_dense = lstart[:, 0:1] + cum
    ld_out = jnp.zeros((TOP_K, T), f32)
    for kk_ in range(TOP_K):
        p_k = jnp.sum(jnp.where(onehots[kk_], ld_dense, 0.0), axis=0, keepdims=True)
        ld_out = jnp.where(row8 == kk_, p_k, ld_out)
    tile_idx = pl.program_id(0) * pl.num_programs(1) + j
    slot_base = ((tile_idx & 1) * TILE_ROWS).astype(f32)
    ld_ref[...] = ((ld_out + slot_base) * ROW_SUB).astype(jnp.int32)
    cnt_ref[...] = cnt.astype(jnp.int32)

    for cp in w_out_copies(lin, wslot):
        cp.start()

    @pl.when(lin == n_lin - 1)
    def _():
        @pl.when(lin >= 1)
        def _():
            for cp in w_out_copies(lin - 1, 1 - wslot):
                cp.wait()

        for cp in w_out_copies(lin, wslot):
            cp.wait()


def _mixer_call(x, mixw, win, convw, qw, kw, sink_col, bias_tab, cnw, anw, wout, fnw,
                wsg, wsu, wsd, wrhl, rbias, tri, head_ones, w_eg, w_eu, w_ed):
    B, S, D = x.shape
    T = MIX_TILE
    nt = S // T
    N = B * S
    assert N_EXPERTS % (B * nt) == 0
    eps = N_EXPERTS // (B * nt)
    any_spec = pl.BlockSpec(memory_space=pl.ANY)

    def full(a):
        nd = a.ndim
        return pl.BlockSpec(a.shape, lambda b, j, _nd=nd: (0,) * _nd)

    tok_spec = pl.BlockSpec((TOP_K, T), lambda b, j: (0, b * nt + j))
    in_arrays = [mixw, win, convw, qw, kw, sink_col, bias_tab, cnw, anw, wout, fnw,
                 wsg, wsu, wsd, wrhl, rbias, tri, head_ones]
    return pl.pallas_call(
        _mixer_kernel,
        grid=(B, nt),
        in_specs=([pl.BlockSpec((1, T, D), lambda b, j: (b, j, 0))] + [full(a) for a in in_arrays]
                  + [any_spec, any_spec, any_spec]),
        out_specs=[
            pl.BlockSpec((T, D), lambda b, j: (b * nt + j, 0)),
            pl.BlockSpec((T, D), lambda b, j: (b * nt + j, 0)),
            tok_spec, tok_spec,
            pl.BlockSpec((N_EXPERTS, LANES), lambda b, j: (b * nt + j, 0)),
            any_spec, any_spec,
        ],
        out_shape=[
            jax.ShapeDtypeStruct((N, D), jnp.float32),
            jax.ShapeDtypeStruct((N, D), jnp.float32),
            jax.ShapeDtypeStruct((TOP_K, N), jnp.float32),
            jax.ShapeDtypeStruct((TOP_K, N), jnp.int32),
            jax.ShapeDtypeStruct((B * nt * N_EXPERTS, LANES), jnp.int32),
            jax.ShapeDtypeStruct((N_EXPERTS, D, 2 * D_EXPERT), jnp.bfloat16),
            jax.ShapeDtypeStruct((N_EXPERTS, D_EXPERT, D), jnp.bfloat16),
        ],
        scratch_shapes=[
            pltpu.VMEM((BLK, D_KV), jnp.float32),
            pltpu.VMEM((BLK, D_KV), jnp.float32),
            pltpu.VMEM((8, D_CONV), jnp.float32),
            pltpu.VMEM((2, eps, D, D_EXPERT), jnp.float32),
            pltpu.VMEM((2, eps, D, D_EXPERT), jnp.float32),
            pltpu.VMEM((2, eps, D_EXPERT, D), jnp.float32),
            pltpu.VMEM((2, eps, D, 2 * D_EXPERT), jnp.bfloat16),
            pltpu.VMEM((2, eps, D_EXPERT, D), jnp.bfloat16),
            pltpu.SemaphoreType.DMA((2,)),
            pltpu.SemaphoreType.DMA((2,)),
        ],
        compiler_params=pltpu.CompilerParams(
            dimension_semantics=("arbitrary", "arbitrary"), vmem_limit_bytes=VMEM_LIMIT),
        name="mixer_router",
    )(x, *in_arrays, w_eg, w_eu, w_ed)


def _rows(ref, row, nrows):
    return ref.at[pl.ds(pl.multiple_of(row * ROW_SUB, ROW_SUB), nrows * ROW_SUB)]


def _segment_copies(src, src_row, dst, dst_row, count, sem):
    chunk = 1 << SEG_CHUNK_LOG2

    def big(i, c):
        o = i * chunk
        pltpu.make_async_copy(_rows(src, src_row + o, chunk), _rows(dst, dst_row + o, chunk), sem).start()
        return c

    lax.fori_loop(0, count >> SEG_CHUNK_LOG2, big, 0)
    for bit in range(SEG_CHUNK_LOG2 - 1, -1, -1):
        o = (count >> (bit + 1)) << (bit + 1)

        @pl.when(((count >> bit) & 1) == 1)
        def _():
            n = 1 << bit
            pltpu.make_async_copy(_rows(src, src_row + o, n), _rows(dst, dst_row + o, n), sem).start()


def _wait_tile(hbm_ref, stag_ref, slot_row, sem, to_hbm):
    vm = _rows(stag_ref, slot_row, TILE_ROWS)
    hb = _rows(hbm_ref, 0, TILE_ROWS)
    (pltpu.make_async_copy(vm, hb, sem) if to_hbm else pltpu.make_async_copy(hb, vm, sem)).wait()


def _tile_relayout_copies(std_hbm, tile, rpt_ref, slot, sem, to_vmem):
    T = MIX_TILE
    row0 = tile * T
    row0 = row0 if isinstance(row0, int) else pl.multiple_of(row0, T)
    slot0 = slot * T
    slot0 = slot0 if isinstance(slot0, int) else pl.multiple_of(slot0, T)
    copies = []
    for c in range(ROW_SUB):
        hb = std_hbm.at[pl.ds(row0, T), pl.ds(c * LANES, LANES)]
        vm = rpt_ref.at[pl.ds(slot0, T), c, :]
        copies.append(pltpu.make_async_copy(hb, vm, sem) if to_vmem else pltpu.make_async_copy(vm, hb, sem))
    return copies


def _dispatch_kernel(lsrc_ref, gdst_ref, cnt_ref, ld_ref, xn_hbm, xs_ref, stag_ref, xin_ref, sems, in_sems):
    b = pl.program_id(0)
    nb = pl.num_programs(0)
    T = MIX_TILE
    slot = b & 1
    base = slot * TILE_ROWS

    def fetch_rows(tile, slot_):
        return _tile_relayout_copies(xn_hbm, tile, xin_ref, slot_, in_sems.at[slot_], True)

    @pl.when(b == 0)
    def _():
        for cp in fetch_rows(b, slot):
            cp.start()

    @pl.when(b + 1 < nb)
    def _():
        for cp in fetch_rows(b + 1, 1 - slot):
            cp.start()

    for cp in fetch_rows(b, slot):
        cp.wait()

    for kk_ in range(TOP_K):
        def row(t, c, kk_=kk_):
            r8 = ld_ref[0, 0, kk_ * T + t]
            stag_ref[pl.ds(pl.multiple_of(r8, ROW_SUB), ROW_SUB), :] = xin_ref[slot * T + t]
            return c

        lax.fori_loop(0, T, row, 0, unroll=128)

    def seg(e, c):
        s = b * N_EXPERTS + e
        _segment_copies(stag_ref, base + lsrc_ref[s], xs_ref, gdst_ref[s], cnt_ref[s], sems.at[slot])
        return c

    lax.fori_loop(0, N_EXPERTS, seg, 0, unroll=4)

    @pl.when(b > 0)
    def _():
        _wait_tile(xs_ref, stag_ref, (1 - slot) * TILE_ROWS, sems.at[1 - slot], True)

    @pl.when(b == nb - 1)
    def _():
        _wait_tile(xs_ref, stag_ref, base, sems.at[slot], True)


def _dispatch_call(lsrc, gdst, cnt, ld_t, xn):
    nt = ld_t.shape[0]
    T = MIX_TILE
    smem = pl.BlockSpec(memory_space=pltpu.SMEM)
    return pl.pallas_call(
        _dispatch_kernel,
        grid=(nt,),
        in_specs=[
            smem, smem, smem,
            pl.BlockSpec((1, 1, TILE_ROWS), lambda i: (i, 0, 0), memory_space=pltpu.SMEM),
            pl.BlockSpec(memory_space=pl.ANY),
        ],
        out_specs=pl.BlockSpec(memory_space=pl.ANY),
        out_shape=jax.ShapeDtypeStruct((nt * TILE_ROWS * ROW_SUB, LANES), jnp.float32),
        scratch_shapes=[pltpu.VMEM((2 * TILE_ROWS * ROW_SUB, LANES), jnp.float32),
                        pltpu.VMEM((2 * T, ROW_SUB, LANES), jnp.float32),
                        pltpu.SemaphoreType.DMA((2,)),
                        pltpu.SemaphoreType.DMA((2,))],
        compiler_params=pltpu.CompilerParams(
            dimension_semantics=("arbitrary",), vmem_limit_bytes=VMEM_LIMIT),
        name="moe_dispatch",
    )(lsrc, gdst, cnt, ld_t, xn)


def _expert_kernel(blk_ref, exp_ref, lo_ref, hi_ref, newblk_ref, endblk_ref,
                   xs_hbm, wgu_ref, wd_ref, ys_hbm,
                   xbuf_ref, ybuf_ref, in_sems, out_sems):
    i = pl.program_id(0)
    n_items = pl.num_programs(0)
    bf16 = jnp.bfloat16
    SB = EXPERT_SUB
    n_blocks = xs_hbm.shape[0] // EXPERT_BLOCK
    blk = blk_ref[i]
    lo = lo_ref[i]
    hi = hi_ref[i]

    class _Group:
        def __init__(self, copies):
            self.copies = copies

        def start(self):
            for cp in self.copies:
                cp.start()

        def wait(self):
            for cp in self.copies:
                cp.wait()

    def hbm_chunk(ref, b, c):
        start = b * EXPERT_BLOCK
        start = start if isinstance(start, int) else pl.multiple_of(start, EXPERT_BLOCK)
        return ref.at[pl.ds(start, EXPERT_BLOCK), c, :]

    def vmem_chunk(ref, slot, c):
        start = slot * EXPERT_BLOCK
        start = start if isinstance(start, int) else pl.multiple_of(start, EXPERT_BLOCK)
        return ref.at[pl.ds(start, EXPERT_BLOCK), pl.ds(c * LANES, LANES)]

    def in_copy(b, slot):
        return _Group([pltpu.make_async_copy(hbm_chunk(xs_hbm, b, c), vmem_chunk(xbuf_ref, slot, c), in_sems.at[slot])
                       for c in range(ROW_SUB)])

    def out_copy(b, slot):
        return _Group([pltpu.make_async_copy(vmem_chunk(ybuf_ref, slot, c), hbm_chunk(ys_hbm, b, c), out_sems.at[slot])
                       for c in range(ROW_SUB)])

    @pl.when(i == 0)
    def _():
        for b0 in range(EXPERT_IN_SLOTS - 1):
            in_copy(b0, b0).start()

    @pl.when(newblk_ref[i] == 1)
    def _():
        ahead = blk + (EXPERT_IN_SLOTS - 1)

        @pl.when(ahead < n_blocks)
        def _():
            in_copy(ahead, lax.rem(ahead, EXPERT_IN_SLOTS)).start()

        in_copy(blk, lax.rem(blk, EXPERT_IN_SLOTS)).wait()

        @pl.when(blk >= EXPERT_OUT_SLOTS)
        def _():
            out_copy(blk - EXPERT_OUT_SLOTS, lax.rem(blk, EXPERT_OUT_SLOTS)).wait()

    xbase = pl.multiple_of(lax.rem(blk, EXPERT_IN_SLOTS) * EXPERT_BLOCK, EXPERT_BLOCK)
    ybase = pl.multiple_of(lax.rem(blk, EXPERT_OUT_SLOTS) * EXPERT_BLOCK, EXPERT_BLOCK)

    def run(r0, nrows, merge_first):
        xb = xbuf_ref[pl.ds(xbase + r0, nrows), :].astype(bf16)
        gu = _dot(xb, wgu_ref[0])
        g = gu[:, 0:D_EXPERT]
        u = gu[:, D_EXPERT:2 * D_EXPERT]
        h = (g * jax.nn.sigmoid(g) * u).astype(bf16)
        y = _dot(h, wd_ref[0])
        first = SB if merge_first else 0
        if merge_first:
            rows = r0 + lax.broadcasted_iota(jnp.int32, (SB, D_MODEL), 0)
            keep = jnp.logical_and(rows >= lo, rows < hi)
            old = ybuf_ref[pl.ds(ybase + r0, SB), :]
            ybuf_ref[pl.ds(ybase + r0, SB), :] = jnp.where(keep, y[0:SB], old)
        if nrows > first:
            ybuf_ref[pl.ds(ybase + r0 + first, nrows - first), :] = y[first:nrows]

    n_sub = EXPERT_BLOCK // SB
    from_top = lo == 0
    sub_aligned = (lo & (SB - 1)) == 0
    to_end = jnp.logical_and(jnp.logical_and(lo > 0, hi == EXPERT_BLOCK), jnp.logical_not(sub_aligned))
    subs_used = (hi + SB - 1) // SB
    first_sub = lo // SB
    for k in range(1, n_sub + 1):
        @pl.when(jnp.logical_and(from_top, subs_used == k))
        def _(k=k):
            run(0, k * SB, False)

    for q in range(n_sub):
        @pl.when(jnp.logical_and(to_end, first_sub == q))
        def _(q=q):
            run(q * SB, EXPERT_BLOCK - q * SB, True)

    for q in range(n_sub):
        r0 = q * SB
        inside = jnp.logical_and(lo > 0, jnp.logical_or(hi < EXPERT_BLOCK, sub_aligned))

        @pl.when(jnp.logical_and(inside, jnp.minimum(hi, r0 + SB) > jnp.maximum(lo, r0)))
        def _(r0=r0):
            @pl.when(lo <= r0)
            def _():
                run(r0, SB, False)

            @pl.when(lo > r0)
            def _():
                run(r0, SB, True)

    @pl.when(endblk_ref[i] == 1)
    def _():
        out_copy(blk, lax.rem(blk, EXPERT_OUT_SLOTS)).start()

    @pl.when(i == n_items - 1)
    def _():
        for b1 in range(n_blocks - EXPERT_OUT_SLOTS, n_blocks):
            out_copy(b1, b1 % EXPERT_OUT_SLOTS).wait()


def _expert_call(item_blk, item_exp, item_lo, item_hi, item_newblk, item_endblk, xs_rp, w_gu, w_d):
    D = D_MODEL
    BS = EXPERT_BLOCK * ROW_SUB
    n_items = item_blk.shape[0]
    assert xs_rp.shape[0] % BS == 0 and xs_rp.shape[0] // BS >= max(EXPERT_IN_SLOTS, EXPERT_OUT_SLOTS)
    xs3 = xs_rp.reshape(xs_rp.shape[0] // ROW_SUB, ROW_SUB, LANES)

    def w_map(i, blk, exp, lo, hi, newblk, endblk):
        return (exp[i], 0, 0)

    grid_spec = pltpu.PrefetchScalarGridSpec(
        num_scalar_prefetch=6,
        grid=(n_items,),
        in_specs=[
            pl.BlockSpec(memory_space=pl.ANY),
            pl.BlockSpec((1, D, 2 * D_EXPERT), w_map),
            pl.BlockSpec((1, D_EXPERT, D), w_map),
        ],
        out_specs=pl.BlockSpec(memory_space=pl.ANY),
        scratch_shapes=[
            pltpu.VMEM((EXPERT_IN_SLOTS * EXPERT_BLOCK, D), jnp.float32),
            pltpu.VMEM((EXPERT_OUT_SLOTS * EXPERT_BLOCK, D), jnp.float32),
            pltpu.SemaphoreType.DMA((EXPERT_IN_SLOTS,)),
            pltpu.SemaphoreType.DMA((EXPERT_OUT_SLOTS,)),
        ],
    )
    return pl.pallas_call(
        _expert_kernel,
        grid_spec=grid_spec,
        out_shape=jax.ShapeDtypeStruct(xs3.shape, jnp.float32),
        compiler_params=pltpu.CompilerParams(
            dimension_semantics=("arbitrary",), vmem_limit_bytes=VMEM_LIMIT),
        name="moe_experts",
    )(item_blk, item_exp, item_lo, item_hi, item_newblk, item_endblk, xs3, w_gu, w_d).reshape(xs_rp.shape)


def _combine_kernel(lsrc_ref, gdst_ref, cnt_ref, ld_ref, w_ref, x2_hbm, ys_ref, out_hbm,
                    stag_ref, acc_ref, sems, in_sems, out_sems):
    b = pl.program_id(0)
    nb = pl.num_programs(0)
    T = MIX_TILE
    slot = b & 1
    base = slot * TILE_ROWS
    aslot = lax.rem(b, COMBINE_ACC_SLOTS)

    def in_copies(tile):
        s_ = lax.rem(tile, COMBINE_ACC_SLOTS)
        return _tile_relayout_copies(x2_hbm, tile, acc_ref, s_, in_sems.at[s_], True)

    def out_copies(tile):
        s_ = lax.rem(tile, COMBINE_ACC_SLOTS)
        return _tile_relayout_copies(out_hbm, tile, acc_ref, s_, out_sems.at[s_], False)

    @pl.when(b >= COMBINE_ACC_SLOTS - 1)
    def _():
        for cp in out_copies(b - (COMBINE_ACC_SLOTS - 1)):
            cp.wait()

    @pl.when(b == 0)
    def _():
        for cp in in_copies(b):
            cp.start()

    @pl.when(b + 1 < nb)
    def _():
        for cp in in_copies(b + 1):
            cp.start()

    def fetch(tile, slot_):
        def seg(e, c):
            s = tile * N_EXPERTS + e
            _segment_copies(ys_ref, gdst_ref[s], stag_ref, slot_ * TILE_ROWS + lsrc_ref[s], cnt_ref[s],
                            sems.at[slot_])
            return c

        lax.fori_loop(0, N_EXPERTS, seg, 0, unroll=4)

    @pl.when(b == 0)
    def _():
        fetch(b, slot)

    @pl.when(b + 1 < nb)
    def _():
        fetch(b + 1, 1 - slot)

    _wait_tile(ys_ref, stag_ref, base, sems.at[slot], False)
    for cp in in_copies(b):
        cp.wait()

    def tok(t, c):
        a = acc_ref[aslot * T + t]
        for kk_ in range(TOP_K):
            r8 = ld_ref[0, 0, kk_ * T + t]
            w = w_ref[0, 0, kk_ * T + t]
            a = a + w * stag_ref[pl.ds(pl.multiple_of(r8, ROW_SUB), ROW_SUB), :]
        acc_ref[aslot * T + t] = a
        return c

    lax.fori_loop(0, T, tok, 0, unroll=32)
    for cp in out_copies(b):
        cp.start()

    @pl.when(b == nb - 1)
    def _():
        for back in range(COMBINE_ACC_SLOTS - 2, -1, -1):
            @pl.when(b >= back)
            def _(back=back):
                for cp in out_copies(b - back):
                    cp.wait()


def _combine_call(lsrc, gdst, cnt, ld_t, w_t, x2, ys_rp):
    N, D = x2.shape
    nt = ld_t.shape[0]
    T = MIX_TILE
    smem = pl.BlockSpec(memory_space=pltpu.SMEM)
    smem_blk = pl.BlockSpec((1, 1, TILE_ROWS), lambda i: (i, 0, 0), memory_space=pltpu.SMEM)
    return pl.pallas_call(
        _combine_kernel,
        grid=(nt,),
        in_specs=[
            smem, smem, smem, smem_blk, smem_blk,
            pl.BlockSpec(memory_space=pl.ANY),
            pl.BlockSpec(memory_space=pl.ANY),
        ],
        out_specs=pl.BlockSpec(memory_space=pl.ANY),
        out_shape=jax.ShapeDtypeStruct((N, D), jnp.float32),
        scratch_shapes=[
            pltpu.VMEM((2 * TILE_ROWS * ROW_SUB, LANES), jnp.float32),
            pltpu.VMEM((COMBINE_ACC_SLOTS * T, ROW_SUB, LANES), jnp.float32),
            pltpu.SemaphoreType.DMA((2,)),
            pltpu.SemaphoreType.DMA((COMBINE_ACC_SLOTS,)),
            pltpu.SemaphoreType.DMA((COMBINE_ACC_SLOTS,)),
        ],
        compiler_params=pltpu.CompilerParams(
            dimension_semantics=("arbitrary",), vmem_limit_bytes=VMEM_LIMIT),
        name="moe_combine",
    )(lsrc, gdst, cnt, ld_t, w_t, x2, ys_rp)


def _work_items(gstart, totals, n_items):
    BM = EXPERT_BLOCK
    i32 = jnp.int32
    gend = gstart + totals
    first_blk = gstart // BM
    last_blk = jnp.maximum(gend - 1, gstart) // BM
    n_e = jnp.where(totals > 0, last_blk - first_blk + 1, 0)
    item_end = jnp.cumsum(n_e)
    item_start = item_end - n_e
    n_real = item_end[-1]
    j = jnp.minimum(jnp.arange(n_items, dtype=i32), n_real - 1)
    exp = jnp.sum((item_end[None, :] <= j[:, None]).astype(i32), axis=1)
    onehot = exp[:, None] == jnp.arange(N_EXPERTS, dtype=i32)[None, :]

    def of_exp(v):
        return jnp.sum(jnp.where(onehot, v[None, :], 0), axis=1)

    blk = of_exp(first_blk) + (j - of_exp(item_start))
    lo = jnp.maximum(of_exp(gstart), blk * BM) - blk * BM
    hi = jnp.minimum(of_exp(gend), (blk + 1) * BM) - blk * BM
    real = jnp.arange(n_items, dtype=i32) < n_real
    hi = jnp.where(real, hi, lo)
    prev_blk = jnp.concatenate([jnp.full((1,), -1, i32), blk[:-1]])
    next_blk = jnp.concatenate([blk[1:], jnp.full((1,), -1, i32)])
    is_last = jnp.arange(n_items, dtype=i32) == n_real - 1
    newblk = jnp.logical_and(real, blk != prev_blk).astype(i32)
    endblk = jnp.logical_and(real, jnp.logical_or(blk != next_blk, is_last)).astype(i32)
    return blk.astype(i32), exp.astype(i32), lo.astype(i32), hi.astype(i32), newblk, endblk


def _layer(x, mix_norm_w, w_in, conv_w, q_norm_w, k_norm_w, sinks, conv_out_norm_w,
           attn_out_norm_w, w_out, ffn_norm_w, w_router, router_bias, w_eg, w_eu, w_ed,
           w_sg, w_su, w_sd):
    B, S, D = x.shape
    N = B * S
    T = MIX_TILE
    nt = N // T
    bf16 = jnp.bfloat16
    f32 = jnp.float32
    i32 = jnp.int32

    wr_t = w_router.astype(f32).T
    wr_hi = wr_t.astype(bf16)
    wr_hl = jnp.concatenate([wr_hi, (wr_t - wr_hi.astype(f32)).astype(bf16)], axis=0)
    sink_col = jnp.broadcast_to(
        jnp.repeat(sinks.astype(f32), BLK).reshape(N_KV_HEADS, GQA_GROUP * BLK, 1),
        (N_KV_HEADS, GQA_GROUP * BLK, LANES))
    tri = jnp.asarray(np.triu(np.ones((T, T), np.float32), k=1), dtype=bf16)
    head_ones = jnp.asarray(np.kron(np.eye(2 * LANES // HEAD_DIM, dtype=np.float32),
                                    np.ones((HEAD_DIM, HEAD_DIM), np.float32)), dtype=bf16)
    bias_tab = jnp.asarray(_attn_bias_table())

    x2, xn, wts, ld, cnt, w_gu, w_d = _mixer_call(
        x, mix_norm_w.reshape(1, D), w_in.astype(bf16), conv_w,
        jnp.tile(q_norm_w, N_HEADS).reshape(1, D_ATTN), jnp.tile(k_norm_w, N_KV_HEADS).reshape(1, D_KV),
        sink_col, bias_tab, conv_out_norm_w.reshape(1, D_CONV), attn_out_norm_w.reshape(1, D_ATTN),
        w_out.astype(bf16), ffn_norm_w.reshape(1, D), w_sg.astype(bf16), w_su.astype(bf16),
        w_sd.astype(bf16), wr_hl, router_bias.astype(f32).reshape(N_EXPERTS, 1), tri, head_ones,
        w_eg.astype(f32), w_eu.astype(f32), w_ed.astype(f32))

    cnt_te = cnt[:, 0].reshape(nt, N_EXPERTS)
    lsrc = jnp.cumsum(cnt_te, axis=1) - cnt_te
    before = jnp.cumsum(cnt_te, axis=0) - cnt_te
    totals = jnp.sum(cnt_te, axis=0)
    gstart = jnp.cumsum(totals) - totals
    gdst = gstart[None, :] + before
    n_items = (N * TOP_K) // EXPERT_BLOCK + N_EXPERTS
    items = _work_items(gstart.astype(i32), totals.astype(i32), n_items)

    def tile_major(a):
        return a.reshape(TOP_K, nt, T).transpose(1, 0, 2).reshape(nt, 1, TILE_ROWS)

    ld_t = tile_major(ld)
    seg_tabs = (lsrc.reshape(-1).astype(i32), gdst.reshape(-1).astype(i32), cnt_te.reshape(-1).astype(i32))
    xs_rp = _dispatch_call(*seg_tabs, ld_t, xn)
    ys_rp = _expert_call(*items, xs_rp, w_gu, w_d)
    out = _combine_call(*seg_tabs, ld_t, tile_major(wts), x2, ys_rp)
    return out.reshape(B, S, D)


def kernel(x, mix_norm_w, w_in, conv_w, q_norm_w, k_norm_w, sinks, conv_out_norm_w, attn_out_norm_w, w_out, ffn_norm_w, w_router, router_bias, w_exp_gate, w_exp_up, w_exp_down, w_sh_gate, w_sh_up, w_sh_down):
    B, S, D = x.shape
    assert D == D_MODEL and w_in.shape == (1, D_MODEL, D_IN_PROJ) and w_exp_gate.shape == (1, N_EXPERTS, D_MODEL, D_EXPERT)
    assert S % MIX_TILE == 0 and (B * S * TOP_K) % EXPERT_BLOCK == 0
    return _layer(x, mix_norm_w[0], w_in[0], conv_w[0], q_norm_w[0], k_norm_w[0], sinks[0],
                  conv_out_norm_w[0], attn_out_norm_w[0], w_out[0], ffn_norm_w[0], w_router[0],
                  router_bias[0], w_exp_gate[0], w_exp_up[0], w_exp_down[0], w_sh_gate[0],
                  w_sh_up[0], w_sh_down[0])
```

```python
import numpy as np
import jax
import jax.numpy as jnp
from jax import lax
from jax.experimental import pallas as pl
from jax.experimental.pallas import tpu as pltpu

D_MODEL = 1024
EPS = 1e-6
D_CONV = 512
HEAD_DIM = 64
N_HEADS = 8
N_KV_HEADS = 2
GQA_GROUP = 4
D_ATTN = 512
D_KV = 128
WINDOW = 128
BLK = 128
D_IN_PROJ = 2304
N_EXPERTS = 64
TOP_K = 8
N_GROUPS = 8
GROUP_SIZE = 8
TOPK_GROUPS = 4
D_EXPERT = 256
ROUTED_SCALE = 2.5

LANES = 128
ROW_SUB = 8
MIX_TILE = 512
TILE_ROWS = TOP_K * MIX_TILE
EXPERT_BLOCK = 1024
EXPERT_SUB = 256
EXPERT_IN_SLOTS = 5
EXPERT_OUT_SLOTS = 4
COMBINE_ACC_SLOTS = 3
SEG_CHUNK_LOG2 = 5
NEG_BIG = -1e30
VMEM_LIMIT = 58 * 1024 * 1024


def _alibi_slopes():
    return np.array([2.0 ** (-8.0 * (h + 1) / N_HEADS) for h in range(N_HEADS)], dtype=np.float32)


def _attn_bias_table():
    qi = np.arange(BLK)[:, None]
    kj = np.arange(2 * BLK)[None, :]
    dist = qi - kj + BLK
    inwin = (dist >= 0) & (dist < WINDOW)
    slopes = _alibi_slopes()
    out = np.zeros((N_KV_HEADS, GQA_GROUP * BLK, 2 * BLK), np.float32)
    for g in range(N_KV_HEADS):
        for i in range(GQA_GROUP):
            h = g * GQA_GROUP + i
            out[g, i * BLK:(i + 1) * BLK] = np.where(inwin, -slopes[h] * dist.astype(np.float32), NEG_BIG)
    return out


def _rms(x, w):
    ms = jnp.mean(x * x, axis=-1, keepdims=True)
    return x * lax.rsqrt(ms + EPS) * w


def _head_rms(x, w_tiled, head_ones):
    C = x.shape[1]
    W = min(C, 2 * LANES)
    sq = x * x
    hi = sq.astype(jnp.bfloat16)
    lo = (sq - hi.astype(jnp.float32)).astype(jnp.bfloat16)
    ones = head_ones[0:W, 0:W]
    ssq = jnp.concatenate(
        [_dot(hi[:, c:c + W], ones) + _dot(lo[:, c:c + W], ones) for c in range(0, C, W)], axis=1)
    return x * lax.rsqrt(ssq * (1.0 / HEAD_DIM) + EPS) * w_tiled


def _dot(a, b):
    return jnp.dot(a, b, preferred_element_type=jnp.float32)


def _dot_nt(a, b):
    return lax.dot_general(a, b, (((1,), (1,)), ((), ())), preferred_element_type=jnp.float32)


def _mixer_kernel(x_ref, mixw_ref, win_ref, convw_ref, qw_ref, kw_ref, sink_ref, bias_ref,
                  cnw_ref, anw_ref, wout_ref, fnw_ref, wsg_ref, wsu_ref, wsd_ref,
                  wrhl_ref, rbias_ref, tri_ref, hones_ref, weg_hbm, weu_hbm, wed_hbm,
                  x2_ref, xn_ref, wts_ref, ld_ref, cnt_ref, wgu_hbm, wdb_hbm,
                  kc_ref, vc_ref, cc_ref, wg_in_ref, wu_in_ref, wd_in_ref, wgu_out_ref, wd_out_ref,
                  win_sems, wout_sems):
    j = pl.program_id(1)
    T = MIX_TILE
    bf16 = jnp.bfloat16
    f32 = jnp.float32

    @pl.when(j == 0)
    def _():
        kc_ref[...] = jnp.zeros_like(kc_ref)
        vc_ref[...] = jnp.zeros_like(vc_ref)
        cc_ref[...] = jnp.zeros_like(cc_ref)

    lin = pl.program_id(0) * pl.num_programs(1) + j
    n_lin = pl.num_programs(0) * pl.num_programs(1)
    eps = wg_in_ref.shape[1]
    wslot = lin & 1

    def w_in_copies(step_, slot_):
        src = pl.ds(step_ * eps, eps)
        return [pltpu.make_async_copy(h.at[src], v.at[slot_], win_sems.at[slot_])
                for h, v in ((weg_hbm, wg_in_ref), (weu_hbm, wu_in_ref), (wed_hbm, wd_in_ref))]

    def w_out_copies(step_, slot_):
        dst = pl.ds(step_ * eps, eps)
        return [pltpu.make_async_copy(v.at[slot_], h.at[dst], wout_sems.at[slot_])
                for h, v in ((wgu_hbm, wgu_out_ref), (wdb_hbm, wd_out_ref))]

    def cast_weights():
        for e in range(eps):
            wgu_out_ref[wslot, e, :, 0:D_EXPERT] = wg_in_ref[wslot, e].astype(bf16)
            wgu_out_ref[wslot, e, :, D_EXPERT:2 * D_EXPERT] = wu_in_ref[wslot, e].astype(bf16)
            wd_out_ref[wslot, e] = wd_in_ref[wslot, e].astype(bf16)

    @pl.when(lin == 0)
    def _():
        for cp in w_in_copies(lin, wslot):
            cp.start()

    @pl.when(lin + 1 < n_lin)
    def _():
        for cp in w_in_copies(lin + 1, 1 - wslot):
            cp.start()

    for cp in w_in_copies(lin, wslot):
        cp.wait()

    @pl.when(lin >= 2)
    def _():
        for cp in w_out_copies(lin - 2, wslot):
            cp.wait()

    x = x_ref[0]
    u = _rms(x, mixw_ref[...]).astype(bf16)
    qkv = _dot(u, win_ref[:, 3 * D_CONV:D_IN_PROJ])
    q = qkv[:, 0:D_ATTN]
    k = qkv[:, D_ATTN:D_ATTN + D_KV]
    v = qkv[:, D_ATTN + D_KV:D_ATTN + 2 * D_KV]
    conv_piece = 2 * LANES
    conv_pieces = []

    lane = lax.broadcasted_iota(jnp.int32, (1, LANES), 1)
    lane_lo = lane < HEAD_DIM
    qn = _head_rms(q, qw_ref[...], hones_ref) * (HEAD_DIM ** -0.5)
    kn = _head_rms(k, kw_ref[...], hones_ref)
    kfull = jnp.concatenate([kc_ref[...], kn], axis=0)
    vfull = jnp.concatenate([vc_ref[...], v], axis=0)
    kc_ref[...] = kn[T - BLK:T, :]
    vc_ref[...] = v[T - BLK:T, :]

    def _rep(a, g):
        r = pltpu.roll(a, HEAD_DIM, axis=1)
        two = jnp.where(lane_lo, a, r) if g == 0 else jnp.where(lane_lo, r, a)
        return jnp.concatenate([two, two], axis=1).astype(bf16)

    k_rep = [_rep(kfull, g) for g in range(N_KV_HEADS)]
    v_rep = [_rep(vfull, g) for g in range(N_KV_HEADS)]

    lane256 = lax.broadcasted_iota(jnp.int32, (1, 2 * LANES), 1)
    head_of_lane = lane256 // HEAD_DIM
    first_f = (j == 0).astype(f32)
    prev_key_mask = jnp.where(lane256 < BLK, first_f * NEG_BIG, 0.0)
    ones_cols = jnp.ones((2 * BLK, LANES), bf16)

    def lanes2(a):
        return jnp.concatenate([a, a], axis=1)

    attn_rows = []
    for i in range(T // BLK):
        grp_out = []
        for g in range(N_KV_HEADS):
            if len(conv_pieces) * conv_piece < 3 * D_CONV:
                c0 = len(conv_pieces) * conv_piece
                conv_pieces.append(_dot(u, win_ref[:, c0:c0 + conv_piece]))
            qg = qn[i * BLK:(i + 1) * BLK, g * 256:(g + 1) * 256]
            qst = jnp.concatenate(
                [jnp.where(head_of_lane == hi, qg, 0.0) for hi in range(GQA_GROUP)], axis=0).astype(bf16)
            kk = k_rep[g][i * BLK:i * BLK + 2 * BLK, :]
            vv = jnp.concatenate([v_rep[g][i * BLK:i * BLK + 2 * BLK, :], ones_cols], axis=1)
            s = _dot_nt(qst, kk) + bias_ref[g]
            if i == 0:
                s = s + prev_key_mask
            sink = sink_ref[g]
            m = jnp.maximum(jnp.max(s, axis=-1, keepdims=True), sink)
            e = jnp.exp(s - lanes2(m)).astype(bf16)
            r = _dot(e, vv)
            inv = 1.0 / (r[:, 2 * LANES:3 * LANES] + jnp.exp(sink - m))
            r = r[:, 0:2 * LANES] * lanes2(inv)
            o = jnp.where(head_of_lane == 0, r[0:BLK], 0.0)
            for hi in range(1, GQA_GROUP):
                o = jnp.where(head_of_lane == hi, r[hi * BLK:(hi + 1) * BLK], o)
            grp_out.append(o)
        attn_rows.append(jnp.concatenate(grp_out, axis=1))
    y_attn = jnp.concatenate(attn_rows, axis=0)
    cast_weights()

    assert len(conv_pieces) * conv_piece == 3 * D_CONV
    conv_proj = jnp.concatenate(conv_pieces, axis=1)
    b_gate = conv_proj[:, 0:D_CONV]
    c_gate = conv_proj[:, D_CONV:2 * D_CONV]
    hh = conv_proj[:, 2 * D_CONV:3 * D_CONV]
    ch = c_gate * hh
    prev = cc_ref[...]
    p6 = prev[6:7, :]
    p7 = prev[7:8, :]
    row = lax.broadcasted_iota(jnp.int32, (T, D_CONV), 0)
    ch_m1 = jnp.where(row == 0, p7, pltpu.roll(ch, 1, axis=0))
    ch_m2 = jnp.where(row == 0, p6, jnp.where(row == 1, p7, pltpu.roll(ch, 2, axis=0)))
    cw = convw_ref[...]
    y_conv = b_gate * (cw[0:1, :] * ch_m2 + cw[1:2, :] * ch_m1 + cw[2:3, :] * ch)
    cc_ref[...] = ch[T - 8:T, :]

    y_mix = jnp.concatenate([_rms(y_conv, cnw_ref[...]), _rms(y_attn, anw_ref[...])], axis=1)
    x1 = x + _dot(y_mix.astype(bf16), wout_ref[...])

    xn = _rms(x1, fnw_ref[...])
    xn_ref[...] = xn
    xh = xn.astype(bf16)

    xl = (xn - xh.astype(f32)).astype(bf16)
    w_hl = wrhl_ref[...]
    a_hl = _dot_nt(w_hl, xh)
    logits = a_hl[0:N_EXPERTS] + a_hl[N_EXPERTS:2 * N_EXPERTS] + _dot_nt(w_hl[0:N_EXPERTS], xl)

    gs = _dot(xh, wsg_ref[...])
    us = _dot(xh, wsu_ref[...])
    hs = (gs * jax.nn.sigmoid(gs) * us).astype(bf16)
    x2 = x1 + _dot(hs, wsd_ref[...])
    x2_ref[...] = x2

    scores = jax.nn.sigmoid(logits)
    biased = scores + rbias_ref[...]

    sub8 = lax.broadcasted_iota(jnp.int32, (GROUP_SIZE, T), 0)
    gscore = []
    for g in range(N_GROUPS):
        blk = biased[g * GROUP_SIZE:(g + 1) * GROUP_SIZE, :]
        m1 = jnp.max(blk, axis=0, keepdims=True)
        first = jnp.min(jnp.where(blk == m1, sub8, GROUP_SIZE), axis=0, keepdims=True)
        m2 = jnp.max(jnp.where(sub8 == first, -jnp.inf, blk), axis=0, keepdims=True)
        gscore.append(m1 + m2)
    masked_blocks = []
    for g in range(N_GROUPS):
        rank = jnp.zeros((1, T), jnp.int32)
        for o_ in range(N_GROUPS):
            if o_ == g:
                continue
            if o_ < g:
                ahead = gscore[o_] >= gscore[g]
            else:
                ahead = gscore[o_] > gscore[g]
            rank = rank + ahead.astype(jnp.int32)
        keep = rank < TOPK_GROUPS
        blk = biased[g * GROUP_SIZE:(g + 1) * GROUP_SIZE, :]
        masked_blocks.append(jnp.where(keep, blk, -jnp.inf))
    cur = jnp.concatenate(masked_blocks, axis=0)

    eiota = lax.broadcasted_iota(jnp.int32, (N_EXPERTS, T), 0)
    row8 = lax.broadcasted_iota(jnp.int32, (TOP_K, T), 0)
    w_out = jnp.zeros((TOP_K, T), f32)
    sel_dense = jnp.zeros((N_EXPERTS, T), f32)
    onehots = []
    for kk_ in range(TOP_K):
        mx = jnp.max(cur, axis=0, keepdims=True)
        sel_idx = jnp.min(jnp.where(cur == mx, eiota, N_EXPERTS), axis=0, keepdims=True)
        onehot = eiota == sel_idx
        w_k = jnp.sum(jnp.where(onehot, scores, 0.0), axis=0, keepdims=True)
        cur = jnp.where(onehot, -jnp.inf, cur)
        sel_dense = jnp.where(onehot, 1.0, sel_dense)
        w_out = jnp.where(row8 == kk_, w_k, w_out)
        onehots.append(onehot)
    wsum = jnp.sum(w_out, axis=0, keepdims=True)
    wts = w_out / wsum * ROUTED_SCALE
    for kk_ in range(TOP_K):
        wts_ref[0, :, kk_ * T:(kk_ + 1) * T] = wts[kk_:kk_ + 1, :]

    cum = _dot(sel_dense.astype(bf16), tri_ref[...])
    cnt = jnp.broadcast_to(jnp.sum(sel_dense, axis=1, keepdims=True), (N_EXPERTS, LANES))
    erow = lax.broadcasted_iota(jnp.int32, (N_EXPERTS, LANES), 0)
    incl = cnt
    step = 1
    while step < N_EXPERTS:
        incl = incl + jnp.where(erow >= step, pltpu.roll(incl, step, axis=0), 0.0)
        step *= 2
    lstart = incl - cnt
    ld_dense = lstart[:, 0:1] + cum
    ld_out = jnp.zeros((TOP_K, T), f32)
    for kk_ in range(TOP_K):
        p_k = jnp.sum(jnp.where(onehots[kk_], ld_dense, 0.0), axis=0, keepdims=True)
        ld_out = jnp.where(row8 == kk_, p_k, ld_out)
    tile_idx = pl.program_id(0) * pl.num_programs(1) + j
    slot_base = ((tile_idx & 1) * TILE_ROWS).astype(f32)
    ld_int = ((ld_out + slot_base) * ROW_SUB).astype(jnp.int32)
    for kk_ in range(TOP_K):
        ld_ref[0, :, kk_ * T:(kk_ + 1) * T] = ld_int[kk_:kk_ + 1, :]
    cnt_ref[...] = cnt.astype(jnp.int32)

    for cp in w_out_copies(lin, wslot):
        cp.start()

    @pl.when(lin == n_lin - 1)
    def _():
        @pl.when(lin >= 1)
        def _():
            for cp in w_out_copies(lin - 1, 1 - wslot):
                cp.wait()

        for cp in w_out_copies(lin, wslot):
            cp.wait()


def _mixer_call(x, mixw, win, convw, qw, kw, sink_col, bias_tab, cnw, anw, wout, fnw,
                wsg, wsu, wsd, wrhl, rbias, tri, head_ones, w_eg, w_eu, w_ed):
    B, S, D = x.shape
    T = MIX_TILE
    nt = S // T
    N = B * S
    assert N_EXPERTS % (B * nt) == 0
    eps = N_EXPERTS // (B * nt)
    any_spec = pl.BlockSpec(memory_space=pl.ANY)

    def full(a):
        nd = a.ndim
        return pl.BlockSpec(a.shape, lambda b, j, _nd=nd: (0,) * _nd)

    tok_spec = pl.BlockSpec((1, 1, TILE_ROWS), lambda b, j: (b * nt + j, 0, 0))
    in_arrays = [mixw, win, convw, qw, kw, sink_col, bias_tab, cnw, anw, wout, fnw,
                 wsg, wsu, wsd, wrhl, rbias, tri, head_ones]
    return pl.pallas_call(
        _mixer_kernel,
        grid=(B, nt),
        in_specs=([pl.BlockSpec((1, T, D), lambda b, j: (b, j, 0))] + [full(a) for a in in_arrays]
                  + [any_spec, any_spec, any_spec]),
        out_specs=[
            pl.BlockSpec((T, D), lambda b, j: (b * nt + j, 0)),
            pl.BlockSpec((T, D), lambda b, j: (b * nt + j, 0)),
            tok_spec, tok_spec,
            pl.BlockSpec((N_EXPERTS, LANES), lambda b, j: (b * nt + j, 0)),
            any_spec, any_spec,
        ],
        out_shape=[
            jax.ShapeDtypeStruct((N, D), jnp.float32),
            jax.ShapeDtypeStruct((N, D), jnp.float32),
            jax.ShapeDtypeStruct((B * nt, 1, TILE_ROWS), jnp.float32),
            jax.ShapeDtypeStruct((B * nt, 1, TILE_ROWS), jnp.int32),
            jax.ShapeDtypeStruct((B * nt * N_EXPERTS, LANES), jnp.int32),
            jax.ShapeDtypeStruct((N_EXPERTS, D, 2 * D_EXPERT), jnp.bfloat16),
            jax.ShapeDtypeStruct((N_EXPERTS, D_EXPERT, D), jnp.bfloat16),
        ],
        scratch_shapes=[
            pltpu.VMEM((BLK, D_KV), jnp.float32),
            pltpu.VMEM((BLK, D_KV), jnp.float32),
            pltpu.VMEM((8, D_CONV), jnp.float32),
            pltpu.VMEM((2, eps, D, D_EXPERT), jnp.float32),
            pltpu.VMEM((2, eps, D, D_EXPERT), jnp.float32),
            pltpu.VMEM((2, eps, D_EXPERT, D), jnp.float32),
            pltpu.VMEM((2, eps, D, 2 * D_EXPERT), jnp.bfloat16),
            pltpu.VMEM((2, eps, D_EXPERT, D), jnp.bfloat16),
            pltpu.SemaphoreType.DMA((2,)),
            pltpu.SemaphoreType.DMA((2,)),
        ],
        compiler_params=pltpu.CompilerParams(
            dimension_semantics=("arbitrary", "arbitrary"), vmem_limit_bytes=VMEM_LIMIT),
        name="mixer_router",
    )(x, *in_arrays, w_eg, w_eu, w_ed)


def _rows(ref, row, nrows):
    return ref.at[pl.ds(pl.multiple_of(row * ROW_SUB, ROW_SUB), nrows * ROW_SUB)]


def _segment_copies(src, src_row, dst, dst_row, count, sem):
    chunk = 1 << SEG_CHUNK_LOG2

    def big(i, c):
        o = i * chunk
        pltpu.make_async_copy(_rows(src, src_row + o, chunk), _rows(dst, dst_row + o, chunk), sem).start()
        return c

    lax.fori_loop(0, count >> SEG_CHUNK_LOG2, big, 0)
    for bit in range(SEG_CHUNK_LOG2 - 1, -1, -1):
        o = (count >> (bit + 1)) << (bit + 1)

        @pl.when(((count >> bit) & 1) == 1)
        def _():
            n = 1 << bit
            pltpu.make_async_copy(_rows(src, src_row + o, n), _rows(dst, dst_row + o, n), sem).start()


def _wait_tile(hbm_ref, stag_ref, slot_row, sem, to_hbm):
    vm = _rows(stag_ref, slot_row, TILE_ROWS)
    hb = _rows(hbm_ref, 0, TILE_ROWS)
    (pltpu.make_async_copy(vm, hb, sem) if to_hbm else pltpu.make_async_copy(hb, vm, sem)).wait()


def _tile_relayout_copies(std_hbm, tile, rpt_ref, slot, sem, to_vmem):
    T = MIX_TILE
    row0 = tile * T
    row0 = row0 if isinstance(row0, int) else pl.multiple_of(row0, T)
    slot0 = slot * T
    slot0 = slot0 if isinstance(slot0, int) else pl.multiple_of(slot0, T)
    copies = []
    for c in range(ROW_SUB):
        hb = std_hbm.at[pl.ds(row0, T), pl.ds(c * LANES, LANES)]
        vm = rpt_ref.at[pl.ds(slot0, T), c, :]
        copies.append(pltpu.make_async_copy(hb, vm, sem) if to_vmem else pltpu.make_async_copy(vm, hb, sem))
    return copies


def _dispatch_kernel(lsrc_ref, gdst_ref, cnt_ref, ld_ref, xn_hbm, xs_ref, stag_ref, xin_ref, sems, in_sems):
    b = pl.program_id(0)
    nb = pl.num_programs(0)
    T = MIX_TILE
    slot = b & 1
    base = slot * TILE_ROWS

    def fetch_rows(tile, slot_):
        return _tile_relayout_copies(xn_hbm, tile, xin_ref, slot_, in_sems.at[slot_], True)

    @pl.when(b == 0)
    def _():
        for cp in fetch_rows(b, slot):
            cp.start()

    @pl.when(b + 1 < nb)
    def _():
        for cp in fetch_rows(b + 1, 1 - slot):
            cp.start()

    for cp in fetch_rows(b, slot):
        cp.wait()

    for kk_ in range(TOP_K):
        def row(t, c, kk_=kk_):
            r8 = ld_ref[0, 0, kk_ * T + t]
            stag_ref[pl.ds(pl.multiple_of(r8, ROW_SUB), ROW_SUB), :] = xin_ref[slot * T + t]
            return c

        lax.fori_loop(0, T, row, 0, unroll=128)

    def seg(e, c):
        s = b * N_EXPERTS + e
        _segment_copies(stag_ref, base + lsrc_ref[s], xs_ref, gdst_ref[s], cnt_ref[s], sems.at[slot])
        return c

    lax.fori_loop(0, N_EXPERTS, seg, 0, unroll=4)

    @pl.when(b > 0)
    def _():
        _wait_tile(xs_ref, stag_ref, (1 - slot) * TILE_ROWS, sems.at[1 - slot], True)

    @pl.when(b == nb - 1)
    def _():
        _wait_tile(xs_ref, stag_ref, base, sems.at[slot], True)


def _dispatch_call(lsrc, gdst, cnt, ld_t, xn):
    nt = ld_t.shape[0]
    T = MIX_TILE
    smem = pl.BlockSpec(memory_space=pltpu.SMEM)
    return pl.pallas_call(
        _dispatch_kernel,
        grid=(nt,),
        in_specs=[
            smem, smem, smem,
            pl.BlockSpec((1, 1, TILE_ROWS), lambda i: (i, 0, 0), memory_space=pltpu.SMEM),
            pl.BlockSpec(memory_space=pl.ANY),
        ],
        out_specs=pl.BlockSpec(memory_space=pl.ANY),
        out_shape=jax.ShapeDtypeStruct((nt * TILE_ROWS * ROW_SUB, LANES), jnp.float32),
        scratch_shapes=[pltpu.VMEM((2 * TILE_ROWS * ROW_SUB, LANES), jnp.float32),
                        pltpu.VMEM((2 * T, ROW_SUB, LANES), jnp.float32),
                        pltpu.SemaphoreType.DMA((2,)),
                        pltpu.SemaphoreType.DMA((2,))],
        compiler_params=pltpu.CompilerParams(
            dimension_semantics=("arbitrary",), vmem_limit_bytes=VMEM_LIMIT),
        name="moe_dispatch",
    )(lsrc, gdst, cnt, ld_t, xn)


def _expert_kernel(blk_ref, exp_ref, lo_ref, hi_ref, newblk_ref, endblk_ref,
                   xs_hbm, wgu_ref, wd_ref, ys_hbm,
                   xbuf_ref, ybuf_ref, in_sems, out_sems):
    i = pl.program_id(0)
    n_items = pl.num_programs(0)
    bf16 = jnp.bfloat16
    SB = EXPERT_SUB
    n_blocks = xs_hbm.shape[0] // EXPERT_BLOCK
    blk = blk_ref[i]
    lo = lo_ref[i]
    hi = hi_ref[i]

    class _Group:
        def __init__(self, copies):
            self.copies = copies

        def start(self):
            for cp in self.copies:
                cp.start()

        def wait(self):
            for cp in self.copies:
                cp.wait()

    def hbm_chunk(ref, b, c):
        start = b * EXPERT_BLOCK
        start = start if isinstance(start, int) else pl.multiple_of(start, EXPERT_BLOCK)
        return ref.at[pl.ds(start, EXPERT_BLOCK), c, :]

    def vmem_chunk(ref, slot, c):
        start = slot * EXPERT_BLOCK
        start = start if isinstance(start, int) else pl.multiple_of(start, EXPERT_BLOCK)
        return ref.at[pl.ds(start, EXPERT_BLOCK), pl.ds(c * LANES, LANES)]

    def in_copy(b, slot):
        return _Group([pltpu.make_async_copy(hbm_chunk(xs_hbm, b, c), vmem_chunk(xbuf_ref, slot, c), in_sems.at[slot])
                       for c in range(ROW_SUB)])

    def out_copy(b, slot):
        return _Group([pltpu.make_async_copy(vmem_chunk(ybuf_ref, slot, c), hbm_chunk(ys_hbm, b, c), out_sems.at[slot])
                       for c in range(ROW_SUB)])

    @pl.when(i == 0)
    def _():
        for b0 in range(EXPERT_IN_SLOTS - 1):
            in_copy(b0, b0).start()

    @pl.when(newblk_ref[i] == 1)
    def _():
        ahead = blk + (EXPERT_IN_SLOTS - 1)

        @pl.when(ahead < n_blocks)
        def _():
            in_copy(ahead, lax.rem(ahead, EXPERT_IN_SLOTS)).start()

        in_copy(blk, lax.rem(blk, EXPERT_IN_SLOTS)).wait()

        @pl.when(blk >= EXPERT_OUT_SLOTS)
        def _():
            out_copy(blk - EXPERT_OUT_SLOTS, lax.rem(blk, EXPERT_OUT_SLOTS)).wait()

    xbase = pl.multiple_of(lax.rem(blk, EXPERT_IN_SLOTS) * EXPERT_BLOCK, EXPERT_BLOCK)
    ybase = pl.multiple_of(lax.rem(blk, EXPERT_OUT_SLOTS) * EXPERT_BLOCK, EXPERT_BLOCK)

    def run(r0, nrows, merge_first):
        xb = xbuf_ref[pl.ds(xbase + r0, nrows), :].astype(bf16)
        gu = _dot(xb, wgu_ref[0])
        g = gu[:, 0:D_EXPERT]
        u = gu[:, D_EXPERT:2 * D_EXPERT]
        h = (g * jax.nn.sigmoid(g) * u).astype(bf16)
        y = _dot(h, wd_ref[0])
        first = SB if merge_first else 0
        if merge_first:
            rows = r0 + lax.broadcasted_iota(jnp.int32, (SB, D_MODEL), 0)
            keep = jnp.logical_and(rows >= lo, rows < hi)
            old = ybuf_ref[pl.ds(ybase + r0, SB), :]
            ybuf_ref[pl.ds(ybase + r0, SB), :] = jnp.where(keep, y[0:SB], old)
        if nrows > first:
            ybuf_ref[pl.ds(ybase + r0 + first, nrows - first), :] = y[first:nrows]

    n_sub = EXPERT_BLOCK // SB
    from_top = lo == 0
    sub_aligned = (lo & (SB - 1)) == 0
    to_end = jnp.logical_and(jnp.logical_and(lo > 0, hi == EXPERT_BLOCK), jnp.logical_not(sub_aligned))
    subs_used = (hi + SB - 1) // SB
    first_sub = lo // SB
    for k in range(1, n_sub + 1):
        @pl.when(jnp.logical_and(from_top, subs_used == k))
        def _(k=k):
            run(0, k * SB, False)

    for q in range(n_sub):
        @pl.when(jnp.logical_and(to_end, first_sub == q))
        def _(q=q):
            run(q * SB, EXPERT_BLOCK - q * SB, True)

    for q in range(n_sub):
        r0 = q * SB
        inside = jnp.logical_and(lo > 0, jnp.logical_or(hi < EXPERT_BLOCK, sub_aligned))

        @pl.when(jnp.logical_and(inside, jnp.minimum(hi, r0 + SB) > jnp.maximum(lo, r0)))
        def _(r0=r0):
            @pl.when(lo <= r0)
            def _():
                run(r0, SB, False)

            @pl.when(lo > r0)
            def _():
                run(r0, SB, True)

    @pl.when(endblk_ref[i] == 1)
    def _():
        out_copy(blk, lax.rem(blk, EXPERT_OUT_SLOTS)).start()

    @pl.when(i == n_items - 1)
    def _():
        for b1 in range(n_blocks - EXPERT_OUT_SLOTS, n_blocks):
            out_copy(b1, b1 % EXPERT_OUT_SLOTS).wait()


def _expert_call(item_blk, item_exp, item_lo, item_hi, item_newblk, item_endblk, xs_rp, w_gu, w_d):
    D = D_MODEL
    BS = EXPERT_BLOCK * ROW_SUB
    n_items = item_blk.shape[0]
    assert xs_rp.shape[0] % BS == 0 and xs_rp.shape[0] // BS >= max(EXPERT_IN_SLOTS, EXPERT_OUT_SLOTS)
    xs3 = xs_rp.reshape(xs_rp.shape[0] // ROW_SUB, ROW_SUB, LANES)

    def w_map(i, blk, exp, lo, hi, newblk, endblk):
        return (exp[i], 0, 0)

    grid_spec = pltpu.PrefetchScalarGridSpec(
        num_scalar_prefetch=6,
        grid=(n_items,),
        in_specs=[
            pl.BlockSpec(memory_space=pl.ANY),
            pl.BlockSpec((1, D, 2 * D_EXPERT), w_map),
            pl.BlockSpec((1, D_EXPERT, D), w_map),
        ],
        out_specs=pl.BlockSpec(memory_space=pl.ANY),
        scratch_shapes=[
            pltpu.VMEM((EXPERT_IN_SLOTS * EXPERT_BLOCK, D), jnp.float32),
            pltpu.VMEM((EXPERT_OUT_SLOTS * EXPERT_BLOCK, D), jnp.float32),
            pltpu.SemaphoreType.DMA((EXPERT_IN_SLOTS,)),
            pltpu.SemaphoreType.DMA((EXPERT_OUT_SLOTS,)),
        ],
    )
    return pl.pallas_call(
        _expert_kernel,
        grid_spec=grid_spec,
        out_shape=jax.ShapeDtypeStruct(xs3.shape, jnp.float32),
        compiler_params=pltpu.CompilerParams(
            dimension_semantics=("arbitrary",), vmem_limit_bytes=VMEM_LIMIT),
        name="moe_experts",
    )(item_blk, item_exp, item_lo, item_hi, item_newblk, item_endblk, xs3, w_gu, w_d).reshape(xs_rp.shape)


def _combine_kernel(lsrc_ref, gdst_ref, cnt_ref, ld_ref, w_ref, x2_hbm, ys_ref, out_hbm,
                    stag_ref, acc_ref, sems, in_sems, out_sems):
    b = pl.program_id(0)
    nb = pl.num_programs(0)
    T = MIX_TILE
    slot = b & 1
    base = slot * TILE_ROWS
    aslot = lax.rem(b, COMBINE_ACC_SLOTS)

    def in_copies(tile):
        s_ = lax.rem(tile, COMBINE_ACC_SLOTS)
        return _tile_relayout_copies(x2_hbm, tile, acc_ref, s_, in_sems.at[s_], True)

    def out_copies(tile):
        s_ = lax.rem(tile, COMBINE_ACC_SLOTS)
        return _tile_relayout_copies(out_hbm, tile, acc_ref, s_, out_sems.at[s_], False)

    @pl.when(b >= COMBINE_ACC_SLOTS - 1)
    def _():
        for cp in out_copies(b - (COMBINE_ACC_SLOTS - 1)):
            cp.wait()

    @pl.when(b == 0)
    def _():
        for cp in in_copies(b):
            cp.start()

    @pl.when(b + 1 < nb)
    def _():
        for cp in in_copies(b + 1):
            cp.start()

    def fetch(tile, slot_):
        def seg(e, c):
            s = tile * N_EXPERTS + e
            _segment_copies(ys_ref, gdst_ref[s], stag_ref, slot_ * TILE_ROWS + lsrc_ref[s], cnt_ref[s],
                            sems.at[slot_])
            return c

        lax.fori_loop(0, N_EXPERTS, seg, 0, unroll=4)

    @pl.when(b == 0)
    def _():
        fetch(b, slot)

    @pl.when(b + 1 < nb)
    def _():
        fetch(b + 1, 1 - slot)

    _wait_tile(ys_ref, stag_ref, base, sems.at[slot], False)
    for cp in in_copies(b):
        cp.wait()

    def tok(t, c):
        a = acc_ref[aslot * T + t]
        for kk_ in range(TOP_K):
            r8 = ld_ref[0, 0, kk_ * T + t]
            w = w_ref[0, 0, kk_ * T + t]
            a = a + w * stag_ref[pl.ds(pl.multiple_of(r8, ROW_SUB), ROW_SUB), :]
        acc_ref[aslot * T + t] = a
        return c

    lax.fori_loop(0, T, tok, 0, unroll=32)
    for cp in out_copies(b):
        cp.start()

    @pl.when(b == nb - 1)
    def _():
        for back in range(COMBINE_ACC_SLOTS - 2, -1, -1):
            @pl.when(b >= back)
            def _(back=back):
                for cp in out_copies(b - back):
                    cp.wait()


def _combine_call(lsrc, gdst, cnt, ld_t, w_t, x2, ys_rp):
    N, D = x2.shape
    nt = ld_t.shape[0]
    T = MIX_TILE
    smem = pl.BlockSpec(memory_space=pltpu.SMEM)
    smem_blk = pl.BlockSpec((1, 1, TILE_ROWS), lambda i: (i, 0, 0), memory_space=pltpu.SMEM)
    return pl.pallas_call(
        _combine_kernel,
        grid=(nt,),
        in_specs=[
            smem, smem, smem, smem_blk, smem_blk,
            pl.BlockSpec(memory_space=pl.ANY),
            pl.BlockSpec(memory_space=pl.ANY),
        ],
        out_specs=pl.BlockSpec(memory_space=pl.ANY),
        out_shape=jax.ShapeDtypeStruct((N, D), jnp.float32),
        scratch_shapes=[
            pltpu.VMEM((2 * TILE_ROWS * ROW_SUB, LANES), jnp.float32),
            pltpu.VMEM((COMBINE_ACC_SLOTS * T, ROW_SUB, LANES), jnp.float32),
            pltpu.SemaphoreType.DMA((2,)),
            pltpu.SemaphoreType.DMA((COMBINE_ACC_SLOTS,)),
            pltpu.SemaphoreType.DMA((COMBINE_ACC_SLOTS,)),
        ],
        compiler_params=pltpu.CompilerParams(
            dimension_semantics=("arbitrary",), vmem_limit_bytes=VMEM_LIMIT),
        name="moe_combine",
    )(lsrc, gdst, cnt, ld_t, w_t, x2, ys_rp)


def _work_items(gstart, totals, n_items):
    BM = EXPERT_BLOCK
    i32 = jnp.int32
    gend = gstart + totals
    first_blk = gstart // BM
    last_blk = jnp.maximum(gend - 1, gstart) // BM
    n_e = jnp.where(totals > 0, last_blk - first_blk + 1, 0)
    item_end = jnp.cumsum(n_e)
    item_start = item_end - n_e
    n_real = item_end[-1]
    j = jnp.minimum(jnp.arange(n_items, dtype=i32), n_real - 1)
    exp = jnp.sum((item_end[None, :] <= j[:, None]).astype(i32), axis=1)
    onehot = exp[:, None] == jnp.arange(N_EXPERTS, dtype=i32)[None, :]

    def of_exp(v):
        return jnp.sum(jnp.where(onehot, v[None, :], 0), axis=1)

    blk = of_exp(first_blk) + (j - of_exp(item_start))
    lo = jnp.maximum(of_exp(gstart), blk * BM) - blk * BM
    hi = jnp.minimum(of_exp(gend), (blk + 1) * BM) - blk * BM
    real = jnp.arange(n_items, dtype=i32) < n_real
    hi = jnp.where(real, hi, lo)
    prev_blk = jnp.concatenate([jnp.full((1,), -1, i32), blk[:-1]])
    next_blk = jnp.concatenate([blk[1:], jnp.full((1,), -1, i32)])
    is_last = jnp.arange(n_items, dtype=i32) == n_real - 1
    newblk = jnp.logical_and(real, blk != prev_blk).astype(i32)
    endblk = jnp.logical_and(real, jnp.logical_or(blk != next_blk, is_last)).astype(i32)
    return blk.astype(i32), exp.astype(i32), lo.astype(i32), hi.astype(i32), newblk, endblk


def _layer(x, mix_norm_w, w_in, conv_w, q_norm_w, k_norm_w, sinks, conv_out_norm_w,
           attn_out_norm_w, w_out, ffn_norm_w, w_router, router_bias, w_eg, w_eu, w_ed,
           w_sg, w_su, w_sd):
    B, S, D = x.shape
    N = B * S
    T = MIX_TILE
    nt = N // T
    bf16 = jnp.bfloat16
    f32 = jnp.float32
    i32 = jnp.int32

    wr_t = w_router.astype(f32).T
    wr_hi = wr_t.astype(bf16)
    wr_hl = jnp.concatenate([wr_hi, (wr_t - wr_hi.astype(f32)).astype(bf16)], axis=0)
    sink_col = jnp.broadcast_to(
        jnp.repeat(sinks.astype(f32), BLK).reshape(N_KV_HEADS, GQA_GROUP * BLK, 1),
        (N_KV_HEADS, GQA_GROUP * BLK, LANES))
    tri = jnp.asarray(np.triu(np.ones((T, T), np.float32), k=1), dtype=bf16)
    head_ones = jnp.asarray(np.kron(np.eye(2 * LANES // HEAD_DIM, dtype=np.float32),
                                    np.ones((HEAD_DIM, HEAD_DIM), np.float32)), dtype=bf16)
    bias_tab = jnp.asarray(_attn_bias_table())

    x2, xn, wts, ld, cnt, w_gu, w_d = _mixer_call(
        x, mix_norm_w.reshape(1, D), w_in.astype(bf16), conv_w,
        jnp.tile(q_norm_w, N_HEADS).reshape(1, D_ATTN), jnp.tile(k_norm_w, N_KV_HEADS).reshape(1, D_KV),
        sink_col, bias_tab, conv_out_norm_w.reshape(1, D_CONV), attn_out_norm_w.reshape(1, D_ATTN),
        w_out.astype(bf16), ffn_norm_w.reshape(1, D), w_sg.astype(bf16), w_su.astype(bf16),
        w_sd.astype(bf16), wr_hl, router_bias.astype(f32).reshape(N_EXPERTS, 1), tri, head_ones,
        w_eg.astype(f32), w_eu.astype(f32), w_ed.astype(f32))

    cnt_te = cnt[:, 0].reshape(nt, N_EXPERTS)
    lsrc = jnp.cumsum(cnt_te, axis=1) - cnt_te
    before = jnp.cumsum(cnt_te, axis=0) - cnt_te
    totals = jnp.sum(cnt_te, axis=0)
    gstart = jnp.cumsum(totals) - totals
    gdst = gstart[None, :] + before
    n_items = (N * TOP_K) // EXPERT_BLOCK + N_EXPERTS
    items = _work_items(gstart.astype(i32), totals.astype(i32), n_items)

    seg_tabs = (lsrc.reshape(-1).astype(i32), gdst.reshape(-1).astype(i32), cnt_te.reshape(-1).astype(i32))
    xs_rp = _dispatch_call(*seg_tabs, ld, xn)
    ys_rp = _expert_call(*items, xs_rp, w_gu, w_d)
    out = _combine_call(*seg_tabs, ld, wts, x2, ys_rp)
    return out.reshape(B, S, D)


def kernel(x, mix_norm_w, w_in, conv_w, q_norm_w, k_norm_w, sinks, conv_out_norm_w, attn_out_norm_w, w_out, ffn_norm_w, w_router, router_bias, w_exp_gate, w_exp_up, w_exp_down, w_sh_gate, w_sh_up, w_sh_down):
    B, S, D = x.shape
    assert D == D_MODEL and w_in.shape == (1, D_MODEL, D_IN_PROJ) and w_exp_gate.shape == (1, N_EXPERTS, D_MODEL, D_EXPERT)
    assert S % MIX_TILE == 0 and (B * S * TOP_K) % EXPERT_BLOCK == 0
    return _layer(x, mix_norm_w[0], w_in[0], conv_w[0], q_norm_w[0], k_norm_w[0], sinks[0],
                  conv_out_norm_w[0], attn_out_norm_w[0], w_out[0], ffn_norm_w[0], w_router[0],
                  router_bias[0], w_exp_gate[0], w_exp_up[0], w_exp_down[0], w_sh_gate[0],
                  w_sh_up[0], w_sh_down[0])
```

```python
import numpy as np
import jax
import jax.numpy as jnp
from jax import lax
from jax.experimental import pallas as pl
from jax.experimental.pallas import tpu as pltpu

D_MODEL = 1024
EPS = 1e-6
D_CONV = 512
HEAD_DIM = 64
N_HEADS = 8
N_KV_HEADS = 2
GQA_GROUP = 4
D_ATTN = 512
D_KV = 128
WINDOW = 128
BLK = 128
D_IN_PROJ = 2304
N_EXPERTS = 64
TOP_K = 8
N_GROUPS = 8
GROUP_SIZE = 8
TOPK_GROUPS = 4
D_EXPERT = 256
ROUTED_SCALE = 2.5

LANES = 128
ROW_SUB = 8
MIX_TILE = 512
TILE_ROWS = TOP_K * MIX_TILE
EXPERT_BLOCK = 1024
EXPERT_SUB = 256
EXPERT_IN_SLOTS = 5
EXPERT_OUT_SLOTS = 4
COMBINE_ACC_SLOTS = 3
SEG_CHUNK_LOG2 = 5
NEG_BIG = -1e30
VMEM_LIMIT = 58 * 1024 * 1024


def _alibi_slopes():
    return np.array([2.0 ** (-8.0 * (h + 1) / N_HEADS) for h in range(N_HEADS)], dtype=np.float32)


def _attn_bias_table():
    qi = np.arange(BLK)[:, None]
    kj = np.arange(2 * BLK)[None, :]
    dist = qi - kj + BLK
    inwin = (dist >= 0) & (dist < WINDOW)
    slopes = _alibi_slopes()
    out = np.zeros((N_KV_HEADS, GQA_GROUP * BLK, 2 * BLK), np.float32)
    for g in range(N_KV_HEADS):
        for i in range(GQA_GROUP):
            h = g * GQA_GROUP + i
            out[g, i * BLK:(i + 1) * BLK] = np.where(inwin, -slopes[h] * dist.astype(np.float32), NEG_BIG)
    return out


def _rms(x, w):
    ms = jnp.mean(x * x, axis=-1, keepdims=True)
    return x * lax.rsqrt(ms + EPS) * w


def _head_rms(x, w_tiled, head_ones):
    C = x.shape[1]
    W = min(C, 2 * LANES)
    sq = x * x
    hi = sq.astype(jnp.bfloat16)
    lo = (sq - hi.astype(jnp.float32)).astype(jnp.bfloat16)
    ones = head_ones[0:W, 0:W]
    ssq = jnp.concatenate(
        [_dot(hi[:, c:c + W], ones) + _dot(lo[:, c:c + W], ones) for c in range(0, C, W)], axis=1)
    return x * lax.rsqrt(ssq * (1.0 / HEAD_DIM) + EPS) * w_tiled


def _dot(a, b):
    return jnp.dot(a, b, preferred_element_type=jnp.float32)


def _dot_nt(a, b):
    return lax.dot_general(a, b, (((1,), (1,)), ((), ())), preferred_element_type=jnp.float32)


def _mixer_kernel(x_ref, mixw_ref, win_ref, convw_ref, qw_ref, kw_ref, sink_ref, bias_ref,
                  cnw_ref, anw_ref, wout_ref, fnw_ref, wsg_ref, wsu_ref, wsd_ref,
                  wrhl_ref, rbias_ref, tri_ref, hones_ref, weg_hbm, weu_hbm, wed_hbm,
                  x2_ref, xn_ref, wts_ref, ld_ref, cnt_ref, wgu_hbm, wdb_hbm,
                  kc_ref, vc_ref, cc_ref, wg_in_ref, wu_in_ref, wd_in_ref, wgu_out_ref, wd_out_ref,
                  win_sems, wout_sems):
    j = pl.program_id(1)
    T = MIX_TILE
    bf16 = jnp.bfloat16
    f32 = jnp.float32

    @pl.when(j == 0)
    def _():
        kc_ref[...] = jnp.zeros_like(kc_ref)
        vc_ref[...] = jnp.zeros_like(vc_ref)
        cc_ref[...] = jnp.zeros_like(cc_ref)

    lin = pl.program_id(0) * pl.num_programs(1) + j
    n_lin = pl.num_programs(0) * pl.num_programs(1)
    eps = wg_in_ref.shape[1]
    wslot = lin & 1

    def w_in_copies(step_, slot_):
        src = pl.ds(step_ * eps, eps)
        return [pltpu.make_async_copy(h.at[src], v.at[slot_], win_sems.at[slot_])
                for h, v in ((weg_hbm, wg_in_ref), (weu_hbm, wu_in_ref), (wed_hbm, wd_in_ref))]

    def w_out_copies(step_, slot_):
        dst = pl.ds(step_ * eps, eps)
        return [pltpu.make_async_copy(v.at[slot_], h.at[dst], wout_sems.at[slot_])
                for h, v in ((wgu_hbm, wgu_out_ref), (wdb_hbm, wd_out_ref))]

    def cast_weights():
        for e in range(eps):
            wgu_out_ref[wslot, e, :, 0:D_EXPERT] = wg_in_ref[wslot, e].astype(bf16)
            wgu_out_ref[wslot, e, :, D_EXPERT:2 * D_EXPERT] = wu_in_ref[wslot, e].astype(bf16)
            wd_out_ref[wslot, e] = wd_in_ref[wslot, e].astype(bf16)

    @pl.when(lin == 0)
    def _():
        for cp in w_in_copies(lin, wslot):
            cp.start()

    @pl.when(lin + 1 < n_lin)
    def _():
        for cp in w_in_copies(lin + 1, 1 - wslot):
            cp.start()

    for cp in w_in_copies(lin, wslot):
        cp.wait()

    @pl.when(lin >= 2)
    def _():
        for cp in w_out_copies(lin - 2, wslot):
            cp.wait()

    x = x_ref[0]
    u = _rms(x, mixw_ref[...]).astype(bf16)
    qkv = _dot(u, win_ref[:, 3 * D_CONV:D_IN_PROJ])
    q = qkv[:, 0:D_ATTN]
    k = qkv[:, D_ATTN:D_ATTN + D_KV]
    v = qkv[:, D_ATTN + D_KV:D_ATTN + 2 * D_KV]
    conv_piece = 2 * LANES
    conv_pieces = []

    lane = lax.broadcasted_iota(jnp.int32, (1, LANES), 1)
    lane_lo = lane < HEAD_DIM
    qn = _head_rms(q, qw_ref[...], hones_ref) * (HEAD_DIM ** -0.5)
    kn = _head_rms(k, kw_ref[...], hones_ref)
    kfull = jnp.concatenate([kc_ref[...], kn], axis=0)
    vfull = jnp.concatenate([vc_ref[...], v], axis=0)
    kc_ref[...] = kn[T - BLK:T, :]
    vc_ref[...] = v[T - BLK:T, :]

    def _rep(a, g):
        r = pltpu.roll(a, HEAD_DIM, axis=1)
        two = jnp.where(lane_lo, a, r) if g == 0 else jnp.where(lane_lo, r, a)
        return jnp.concatenate([two, two], axis=1).astype(bf16)

    k_rep = [_rep(kfull, g) for g in range(N_KV_HEADS)]
    v_rep = [_rep(vfull, g) for g in range(N_KV_HEADS)]

    lane256 = lax.broadcasted_iota(jnp.int32, (1, 2 * LANES), 1)
    head_of_lane = lane256 // HEAD_DIM
    first_f = (j == 0).astype(f32)
    prev_key_mask = jnp.where(lane256 < BLK, first_f * NEG_BIG, 0.0)
    ones_cols = jnp.ones((2 * BLK, LANES), bf16)

    def lanes2(a):
        return jnp.concatenate([a, a], axis=1)

    attn_rows = []
    for i in range(T // BLK):
        grp_out = []
        for g in range(N_KV_HEADS):
            if len(conv_pieces) * conv_piece < 3 * D_CONV:
                c0 = len(conv_pieces) * conv_piece
                conv_pieces.append(_dot(u, win_ref[:, c0:c0 + conv_piece]))
            qg = qn[i * BLK:(i + 1) * BLK, g * 256:(g + 1) * 256]
            qst = jnp.concatenate(
                [jnp.where(head_of_lane == hi, qg, 0.0) for hi in range(GQA_GROUP)], axis=0).astype(bf16)
            kk = k_rep[g][i * BLK:i * BLK + 2 * BLK, :]
            vv = jnp.concatenate([v_rep[g][i * BLK:i * BLK + 2 * BLK, :], ones_cols], axis=1)
            s = _dot_nt(qst, kk) + bias_ref[g]
            if i == 0:
                s = s + prev_key_mask
            sink = sink_ref[g]
            m = jnp.maximum(jnp.max(s, axis=-1, keepdims=True), sink)
            e = jnp.exp(s - lanes2(m)).astype(bf16)
            r = _dot(e, vv)
            inv = 1.0 / (r[:, 2 * LANES:3 * LANES] + jnp.exp(sink - m))
            r = r[:, 0:2 * LANES] * lanes2(inv)
            o = jnp.where(head_of_lane == 0, r[0:BLK], 0.0)
            for hi in range(1, GQA_GROUP):
                o = jnp.where(head_of_lane == hi, r[hi * BLK:(hi + 1) * BLK], o)
            grp_out.append(o)
        attn_rows.append(jnp.concatenate(grp_out, axis=1))
    y_attn = jnp.concatenate(attn_rows, axis=0)
    cast_weights()

    assert len(conv_pieces) * conv_piece == 3 * D_CONV
    conv_proj = jnp.concatenate(conv_pieces, axis=1)
    b_gate = conv_proj[:, 0:D_CONV]
    c_gate = conv_proj[:, D_CONV:2 * D_CONV]
    hh = conv_proj[:, 2 * D_CONV:3 * D_CONV]
    ch = c_gate * hh
    prev = cc_ref[...]
    p6 = prev[6:7, :]
    p7 = prev[7:8, :]
    row = lax.broadcasted_iota(jnp.int32, (T, D_CONV), 0)
    ch_m1 = jnp.where(row == 0, p7, pltpu.roll(ch, 1, axis=0))
    ch_m2 = jnp.where(row == 0, p6, jnp.where(row == 1, p7, pltpu.roll(ch, 2, axis=0)))
    cw = convw_ref[...]
    y_conv = b_gate * (cw[0:1, :] * ch_m2 + cw[1:2, :] * ch_m1 + cw[2:3, :] * ch)
    cc_ref[...] = ch[T - 8:T, :]

    y_mix = jnp.concatenate([_rms(y_conv, cnw_ref[...]), _rms(y_attn, anw_ref[...])], axis=1)
    x1 = x + _dot(y_mix.astype(bf16), wout_ref[...])

    xn = _rms(x1, fnw_ref[...])
    xn_ref[...] = xn
    xh = xn.astype(bf16)

    xl = (xn - xh.astype(f32)).astype(bf16)
    w_hl = wrhl_ref[...]
    a_hl = _dot_nt(w_hl, xh)
    logits = a_hl[0:N_EXPERTS] + a_hl[N_EXPERTS:2 * N_EXPERTS] + _dot_nt(w_hl[0:N_EXPERTS], xl)

    gs = _dot(xh, wsg_ref[...])
    us = _dot(xh, wsu_ref[...])
    hs = (gs * jax.nn.sigmoid(gs) * us).astype(bf16)
    x2 = x1 + _dot(hs, wsd_ref[...])
    x2_ref[...] = x2

    scores = jax.nn.sigmoid(logits)
    biased = scores + rbias_ref[...]

    sub8 = lax.broadcasted_iota(jnp.int32, (GROUP_SIZE, T), 0)
    gscore = []
    for g in range(N_GROUPS):
        blk = biased[g * GROUP_SIZE:(g + 1) * GROUP_SIZE, :]
        m1 = jnp.max(blk, axis=0, keepdims=True)
        first = jnp.min(jnp.where(blk == m1, sub8, GROUP_SIZE), axis=0, keepdims=True)
        m2 = jnp.max(jnp.where(sub8 == first, -jnp.inf, blk), axis=0, keepdims=True)
        gscore.append(m1 + m2)
    masked_blocks = []
    for g in range(N_GROUPS):
        rank = jnp.zeros((1, T), jnp.int32)
        for o_ in range(N_GROUPS):
            if o_ == g:
                continue
            if o_ < g:
                ahead = gscore[o_] >= gscore[g]
            else:
                ahead = gscore[o_] > gscore[g]
            rank = rank + ahead.astype(jnp.int32)
        keep = rank < TOPK_GROUPS
        blk = biased[g * GROUP_SIZE:(g + 1) * GROUP_SIZE, :]
        masked_blocks.append(jnp.where(keep, blk, -jnp.inf))
    cur = jnp.concatenate(masked_blocks, axis=0)

    eiota = lax.broadcasted_iota(jnp.int32, (N_EXPERTS, T), 0)
    row8 = lax.broadcasted_iota(jnp.int32, (TOP_K, T), 0)
    w_out = jnp.zeros((TOP_K, T), f32)
    sel_dense = jnp.zeros((N_EXPERTS, T), f32)
    onehots = []
    for kk_ in range(TOP_K):
        mx = jnp.max(cur, axis=0, keepdims=True)
        sel_idx = jnp.min(jnp.where(cur == mx, eiota, N_EXPERTS), axis=0, keepdims=True)
        onehot = eiota == sel_idx
        w_k = jnp.sum(jnp.where(onehot, scores, 0.0), axis=0, keepdims=True)
        cur = jnp.where(onehot, -jnp.inf, cur)
        sel_dense = jnp.where(onehot, 1.0, sel_dense)
        w_out = jnp.where(row8 == kk_, w_k, w_out)
        onehots.append(onehot)
    wsum = jnp.sum(w_out, axis=0, keepdims=True)
    wts = w_out / wsum * ROUTED_SCALE
    for kk_ in range(TOP_K):
        wts_ref[0, :, kk_ * T:(kk_ + 1) * T] = wts[kk_:kk_ + 1, :]

    cum = _dot(sel_dense.astype(bf16), tri_ref[...])
    cnt = jnp.broadcast_to(jnp.sum(sel_dense, axis=1, keepdims=True), (N_EXPERTS, LANES))
    erow = lax.broadcasted_iota(jnp.int32, (N_EXPERTS, LANES), 0)
    incl = cnt
    step = 1
    while step < N_EXPERTS:
        incl = incl + jnp.where(erow >= step, pltpu.roll(incl, step, axis=0), 0.0)
        step *= 2
    lstart = incl - cnt
    ld_dense = lstart[:, 0:1] + cum
    ld_out = jnp.zeros((TOP_K, T), f32)
    for kk_ in range(TOP_K):
        p_k = jnp.sum(jnp.where(onehots[kk_], ld_dense, 0.0), axis=0, keepdims=True)
        ld_out = jnp.where(row8 == kk_, p_k, ld_out)
    tile_idx = pl.program_id(0) * pl.num_programs(1) + j
    slot_base = ((tile_idx & 1) * TILE_ROWS).astype(f32)
    ld_int = ((ld_out + slot_base) * ROW_SUB).astype(jnp.int32)
    for kk_ in range(TOP_K):
        ld_ref[0, :, kk_ * T:(kk_ + 1) * T] = ld_int[kk_:kk_ + 1, :]
    cnt_ref[...] = cnt.astype(jnp.int32)

    for cp in w_out_copies(lin, wslot):
        cp.start()

    @pl.when(lin == n_lin - 1)
    def _():
        @pl.when(lin >= 1)
        def _():
            for cp in w_out_copies(lin - 1, 1 - wslot):
                cp.wait()

        for cp in w_out_copies(lin, wslot):
            cp.wait()


def _mixer_call(x, mixw, win, convw, qw, kw, sink_col, bias_tab, cnw, anw, wout, fnw,
                wsg, wsu, wsd, wrhl, rbias, tri, head_ones, w_eg, w_eu, w_ed):
    B, S, D = x.shape
    T = MIX_TILE
    nt = S // T
    N = B * S
    assert N_EXPERTS % (B * nt) == 0
    eps = N_EXPERTS // (B * nt)
    any_spec = pl.BlockSpec(memory_space=pl.ANY)

    def full(a):
        nd = a.ndim
        return pl.BlockSpec(a.shape, lambda b, j, _nd=nd: (0,) * _nd)

    tok_spec = pl.BlockSpec((1, 1, TILE_ROWS), lambda b, j: (b * nt + j, 0, 0))
    in_arrays = [mixw, win, convw, qw, kw, sink_col, bias_tab, cnw, anw, wout, fnw,
                 wsg, wsu, wsd, wrhl, rbias, tri, head_ones]
    return pl.pallas_call(
        _mixer_kernel,
        grid=(B, nt),
        in_specs=([pl.BlockSpec((1, T, D), lambda b, j: (b, j, 0))] + [full(a) for a in in_arrays]
                  + [any_spec, any_spec, any_spec]),
        out_specs=[
            pl.BlockSpec((T, D), lambda b, j: (b * nt + j, 0)),
            pl.BlockSpec((T, D), lambda b, j: (b * nt + j, 0)),
            tok_spec, tok_spec,
            pl.BlockSpec((N_EXPERTS, LANES), lambda b, j: (b * nt + j, 0)),
            any_spec, any_spec,
        ],
        out_shape=[
            jax.ShapeDtypeStruct((N, D), jnp.float32),
            jax.ShapeDtypeStruct((N, D), jnp.float32),
            jax.ShapeDtypeStruct((B * nt, 1, TILE_ROWS), jnp.float32),
            jax.ShapeDtypeStruct((B * nt, 1, TILE_ROWS), jnp.int32),
            jax.ShapeDtypeStruct((B * nt * N_EXPERTS, LANES), jnp.int32),
            jax.ShapeDtypeStruct((N_EXPERTS, D, 2 * D_EXPERT), jnp.bfloat16),
            jax.ShapeDtypeStruct((N_EXPERTS, D_EXPERT, D), jnp.bfloat16),
        ],
        scratch_shapes=[
            pltpu.VMEM((BLK, D_KV), jnp.float32),
            pltpu.VMEM((BLK, D_KV), jnp.float32),
            pltpu.VMEM((8, D_CONV), jnp.float32),
            pltpu.VMEM((2, eps, D, D_EXPERT), jnp.float32),
            pltpu.VMEM((2, eps, D, D_EXPERT), jnp.float32),
            pltpu.VMEM((2, eps, D_EXPERT, D), jnp.float32),
            pltpu.VMEM((2, eps, D, 2 * D_EXPERT), jnp.bfloat16),
            pltpu.VMEM((2, eps, D_EXPERT, D), jnp.bfloat16),
            pltpu.SemaphoreType.DMA((2,)),
            pltpu.SemaphoreType.DMA((2,)),
        ],
        compiler_params=pltpu.CompilerParams(
            dimension_semantics=("arbitrary", "arbitrary"), vmem_limit_bytes=VMEM_LIMIT),
        name="mixer_router",
    )(x, *in_arrays, w_eg, w_eu, w_ed)


def _rows(ref, row, nrows):
    return ref.at[pl.ds(pl.multiple_of(row * ROW_SUB, ROW_SUB), nrows * ROW_SUB)]


def _segment_copies(src, src_row, dst, dst_row, count, sem):
    chunk = 1 << SEG_CHUNK_LOG2

    def big(i, c):
        o = i * chunk
        pltpu.make_async_copy(_rows(src, src_row + o, chunk), _rows(dst, dst_row + o, chunk), sem).start()
        return c

    lax.fori_loop(0, count >> SEG_CHUNK_LOG2, big, 0)
    for bit in range(SEG_CHUNK_LOG2 - 1, -1, -1):
        o = (count >> (bit + 1)) << (bit + 1)

        @pl.when(((count >> bit) & 1) == 1)
        def _():
            n = 1 << bit
            pltpu.make_async_copy(_rows(src, src_row + o, n), _rows(dst, dst_row + o, n), sem).start(
                priority=(bit + 1) % 2)


def _wait_tile(hbm_ref, stag_ref, slot_row, sem, to_hbm):
    vm = _rows(stag_ref, slot_row, TILE_ROWS)
    hb = _rows(hbm_ref, 0, TILE_ROWS)
    (pltpu.make_async_copy(vm, hb, sem) if to_hbm else pltpu.make_async_copy(hb, vm, sem)).wait()


def _tile_relayout_copies(std_hbm, tile, rpt_ref, slot, sem, to_vmem):
    T = MIX_TILE
    row0 = tile * T
    row0 = row0 if isinstance(row0, int) else pl.multiple_of(row0, T)
    slot0 = slot * T
    slot0 = slot0 if isinstance(slot0, int) else pl.multiple_of(slot0, T)
    copies = []
    for c in range(ROW_SUB):
        hb = std_hbm.at[pl.ds(row0, T), pl.ds(c * LANES, LANES)]
        vm = rpt_ref.at[pl.ds(slot0, T), c, :]
        copies.append(pltpu.make_async_copy(hb, vm, sem) if to_vmem else pltpu.make_async_copy(vm, hb, sem))
    return copies


def _dispatch_kernel(lsrc_ref, gdst_ref, cnt_ref, ld_ref, xn_hbm, xs_ref, stag_ref, xin_ref, sems, in_sems):
    b = pl.program_id(0)
    nb = pl.num_programs(0)
    T = MIX_TILE
    slot = b & 1
    base = slot * TILE_ROWS

    def fetch_rows(tile, slot_):
        return _tile_relayout_copies(xn_hbm, tile, xin_ref, slot_, in_sems.at[slot_], True)

    @pl.when(b == 0)
    def _():
        for cp in fetch_rows(b, slot):
            cp.start()

    @pl.when(b + 1 < nb)
    def _():
        for cp in fetch_rows(b + 1, 1 - slot):
            cp.start()

    for cp in fetch_rows(b, slot):
        cp.wait()

    for kk_ in range(TOP_K):
        def row(t, c, kk_=kk_):
            r8 = ld_ref[0, 0, kk_ * T + t]
            stag_ref[pl.ds(pl.multiple_of(r8, ROW_SUB), ROW_SUB), :] = xin_ref[slot * T + t]
            return c

        lax.fori_loop(0, T, row, 0, unroll=128)

    def seg(e, c):
        s = b * N_EXPERTS + e
        _segment_copies(stag_ref, base + lsrc_ref[s], xs_ref, gdst_ref[s], cnt_ref[s], sems.at[slot])
        return c

    lax.fori_loop(0, N_EXPERTS, seg, 0, unroll=4)

    @pl.when(b > 0)
    def _():
        _wait_tile(xs_ref, stag_ref, (1 - slot) * TILE_ROWS, sems.at[1 - slot], True)

    @pl.when(b == nb - 1)
    def _():
        _wait_tile(xs_ref, stag_ref, base, sems.at[slot], True)


def _dispatch_call(lsrc, gdst, cnt, ld_t, xn):
    nt = ld_t.shape[0]
    T = MIX_TILE
    smem = pl.BlockSpec(memory_space=pltpu.SMEM)
    return pl.pallas_call(
        _dispatch_kernel,
        grid=(nt,),
        in_specs=[
            smem, smem, smem,
            pl.BlockSpec((1, 1, TILE_ROWS), lambda i: (i, 0, 0), memory_space=pltpu.SMEM),
            pl.BlockSpec(memory_space=pl.ANY),
        ],
        out_specs=pl.BlockSpec(memory_space=pl.ANY),
        out_shape=jax.ShapeDtypeStruct((nt * TILE_ROWS * ROW_SUB, LANES), jnp.float32),
        scratch_shapes=[pltpu.VMEM((2 * TILE_ROWS * ROW_SUB, LANES), jnp.float32),
                        pltpu.VMEM((2 * T, ROW_SUB, LANES), jnp.float32),
                        pltpu.SemaphoreType.DMA((2,)),
                        pltpu.SemaphoreType.DMA((2,))],
        compiler_params=pltpu.CompilerParams(
            dimension_semantics=("arbitrary",), vmem_limit_bytes=VMEM_LIMIT),
        name="moe_dispatch",
    )(lsrc, gdst, cnt, ld_t, xn)


def _expert_kernel(blk_ref, exp_ref, lo_ref, hi_ref, newblk_ref, endblk_ref,
                   xs_hbm, wgu_ref, wd_ref, ys_hbm,
                   xbuf_ref, ybuf_ref, in_sems, out_sems):
    i = pl.program_id(0)
    n_items = pl.num_programs(0)
    bf16 = jnp.bfloat16
    SB = EXPERT_SUB
    n_blocks = xs_hbm.shape[0] // EXPERT_BLOCK
    blk = blk_ref[i]
    lo = lo_ref[i]
    hi = hi_ref[i]

    class _Group:
        def __init__(self, copies):
            self.copies = copies

        def start(self):
            for c, cp in enumerate(self.copies):
                cp.start(priority=c % 2)

        def wait(self):
            for cp in self.copies:
                cp.wait()

    def hbm_chunk(ref, b, c):
        start = b * EXPERT_BLOCK
        start = start if isinstance(start, int) else pl.multiple_of(start, EXPERT_BLOCK)
        return ref.at[pl.ds(start, EXPERT_BLOCK), c, :]

    def vmem_chunk(ref, slot, c):
        start = slot * EXPERT_BLOCK
        start = start if isinstance(start, int) else pl.multiple_of(start, EXPERT_BLOCK)
        return ref.at[pl.ds(start, EXPERT_BLOCK), pl.ds(c * LANES, LANES)]

    def in_copy(b, slot):
        return _Group([pltpu.make_async_copy(hbm_chunk(xs_hbm, b, c), vmem_chunk(xbuf_ref, slot, c), in_sems.at[slot])
                       for c in range(ROW_SUB)])

    def out_copy(b, slot):
        return _Group([pltpu.make_async_copy(vmem_chunk(ybuf_ref, slot, c), hbm_chunk(ys_hbm, b, c), out_sems.at[slot])
                       for c in range(ROW_SUB)])

    @pl.when(i == 0)
    def _():
        for b0 in range(EXPERT_IN_SLOTS - 1):
            in_copy(b0, b0).start()

    @pl.when(newblk_ref[i] == 1)
    def _():
        ahead = blk + (EXPERT_IN_SLOTS - 1)

        @pl.when(ahead < n_blocks)
        def _():
            in_copy(ahead, lax.rem(ahead, EXPERT_IN_SLOTS)).start()

        in_copy(blk, lax.rem(blk, EXPERT_IN_SLOTS)).wait()

        @pl.when(blk >= EXPERT_OUT_SLOTS)
        def _():
            out_copy(blk - EXPERT_OUT_SLOTS, lax.rem(blk, EXPERT_OUT_SLOTS)).wait()

    xbase = pl.multiple_of(lax.rem(blk, EXPERT_IN_SLOTS) * EXPERT_BLOCK, EXPERT_BLOCK)
    ybase = pl.multiple_of(lax.rem(blk, EXPERT_OUT_SLOTS) * EXPERT_BLOCK, EXPERT_BLOCK)

    def run(r0, nrows, merge_first):
        xb = xbuf_ref[pl.ds(xbase + r0, nrows), :].astype(bf16)
        gu = _dot(xb, wgu_ref[0])
        g = gu[:, 0:D_EXPERT]
        u = gu[:, D_EXPERT:2 * D_EXPERT]
        h = (g * jax.nn.sigmoid(g) * u).astype(bf16)
        y = _dot(h, wd_ref[0])
        first = SB if merge_first else 0
        if merge_first:
            rows = r0 + lax.broadcasted_iota(jnp.int32, (SB, D_MODEL), 0)
            keep = jnp.logical_and(rows >= lo, rows < hi)
            old = ybuf_ref[pl.ds(ybase + r0, SB), :]
            ybuf_ref[pl.ds(ybase + r0, SB), :] = jnp.where(keep, y[0:SB], old)
        if nrows > first:
            ybuf_ref[pl.ds(ybase + r0 + first, nrows - first), :] = y[first:nrows]

    n_sub = EXPERT_BLOCK // SB
    from_top = lo == 0
    sub_aligned = (lo & (SB - 1)) == 0
    to_end = jnp.logical_and(jnp.logical_and(lo > 0, hi == EXPERT_BLOCK), jnp.logical_not(sub_aligned))
    subs_used = (hi + SB - 1) // SB
    first_sub = lo // SB
    for k in range(1, n_sub + 1):
        @pl.when(jnp.logical_and(from_top, subs_used == k))
        def _(k=k):
            run(0, k * SB, False)

    for q in range(n_sub):
        @pl.when(jnp.logical_and(to_end, first_sub == q))
        def _(q=q):
            run(q * SB, EXPERT_BLOCK - q * SB, True)

    for q in range(n_sub):
        r0 = q * SB
        inside = jnp.logical_and(lo > 0, jnp.logical_or(hi < EXPERT_BLOCK, sub_aligned))

        @pl.when(jnp.logical_and(inside, jnp.minimum(hi, r0 + SB) > jnp.maximum(lo, r0)))
        def _(r0=r0):
            @pl.when(lo <= r0)
            def _():
                run(r0, SB, False)

            @pl.when(lo > r0)
            def _():
                run(r0, SB, True)

    @pl.when(endblk_ref[i] == 1)
    def _():
        out_copy(blk, lax.rem(blk, EXPERT_OUT_SLOTS)).start()

    @pl.when(i == n_items - 1)
    def _():
        for b1 in range(n_blocks - EXPERT_OUT_SLOTS, n_blocks):
            out_copy(b1, b1 % EXPERT_OUT_SLOTS).wait()


def _expert_call(item_blk, item_exp, item_lo, item_hi, item_newblk, item_endblk, xs_rp, w_gu, w_d):
    D = D_MODEL
    BS = EXPERT_BLOCK * ROW_SUB
    n_items = item_blk.shape[0]
    assert xs_rp.shape[0] % BS == 0 and xs_rp.shape[0] // BS >= max(EXPERT_IN_SLOTS, EXPERT_OUT_SLOTS)
    xs3 = xs_rp.reshape(xs_rp.shape[0] // ROW_SUB, ROW_SUB, LANES)

    def w_map(i, blk, exp, lo, hi, newblk, endblk):
        return (exp[i], 0, 0)

    grid_spec = pltpu.PrefetchScalarGridSpec(
        num_scalar_prefetch=6,
        grid=(n_items,),
        in_specs=[
            pl.BlockSpec(memory_space=pl.ANY),
            pl.BlockSpec((1, D, 2 * D_EXPERT), w_map),
            pl.BlockSpec((1, D_EXPERT, D), w_map),
        ],
        out_specs=pl.BlockSpec(memory_space=pl.ANY),
        scratch_shapes=[
            pltpu.VMEM((EXPERT_IN_SLOTS * EXPERT_BLOCK, D), jnp.float32),
            pltpu.VMEM((EXPERT_OUT_SLOTS * EXPERT_BLOCK, D), jnp.float32),
            pltpu.SemaphoreType.DMA((EXPERT_IN_SLOTS,)),
            pltpu.SemaphoreType.DMA((EXPERT_OUT_SLOTS,)),
        ],
    )
    return pl.pallas_call(
        _expert_kernel,
        grid_spec=grid_spec,
        out_shape=jax.ShapeDtypeStruct(xs3.shape, jnp.float32),
        compiler_params=pltpu.CompilerParams(
            dimension_semantics=("arbitrary",), vmem_limit_bytes=VMEM_LIMIT),
        name="moe_experts",
    )(item_blk, item_exp, item_lo, item_hi, item_newblk, item_endblk, xs3, w_gu, w_d).reshape(xs_rp.shape)


def _combine_kernel(lsrc_ref, gdst_ref, cnt_ref, ld_ref, w_ref, x2_hbm, ys_ref, out_hbm,
                    stag_ref, acc_ref, sems, in_sems, out_sems):
    b = pl.program_id(0)
    nb = pl.num_programs(0)
    T = MIX_TILE
    slot = b & 1
    base = slot * TILE_ROWS
    aslot = lax.rem(b, COMBINE_ACC_SLOTS)

    def in_copies(tile):
        s_ = lax.rem(tile, COMBINE_ACC_SLOTS)
        return _tile_relayout_copies(x2_hbm, tile, acc_ref, s_, in_sems.at[s_], True)

    def out_copies(tile):
        s_ = lax.rem(tile, COMBINE_ACC_SLOTS)
        return _tile_relayout_copies(out_hbm, tile, acc_ref, s_, out_sems.at[s_], False)

    @pl.when(b >= COMBINE_ACC_SLOTS - 1)
    def _():
        for cp in out_copies(b - (COMBINE_ACC_SLOTS - 1)):
            cp.wait()

    @pl.when(b == 0)
    def _():
        for cp in in_copies(b):
            cp.start()

    @pl.when(b + 1 < nb)
    def _():
        for cp in in_copies(b + 1):
            cp.start()

    def fetch(tile, slot_):
        def seg(e, c):
            s = tile * N_EXPERTS + e
            _segment_copies(ys_ref, gdst_ref[s], stag_ref, slot_ * TILE_ROWS + lsrc_ref[s], cnt_ref[s],
                            sems.at[slot_])
            return c

        lax.fori_loop(0, N_EXPERTS, seg, 0, unroll=4)

    @pl.when(b == 0)
    def _():
        fetch(b, slot)

    @pl.when(b + 1 < nb)
    def _():
        fetch(b + 1, 1 - slot)

    _wait_tile(ys_ref, stag_ref, base, sems.at[slot], False)
    for cp in in_copies(b):
        cp.wait()

    def tok(t, c):
        a = acc_ref[aslot * T + t]
        for kk_ in range(TOP_K):
            r8 = ld_ref[0, 0, kk_ * T + t]
            w = w_ref[0, 0, kk_ * T + t]
            a = a + w * stag_ref[pl.ds(pl.multiple_of(r8, ROW_SUB), ROW_SUB), :]
        acc_ref[aslot * T + t] = a
        return c

    lax.fori_loop(0, T, tok, 0, unroll=32)
    for cp in out_copies(b):
        cp.start()

    @pl.when(b == nb - 1)
    def _():
        for back in range(COMBINE_ACC_SLOTS - 2, -1, -1):
            @pl.when(b >= back)
            def _(back=back):
                for cp in out_copies(b - back):
                    cp.wait()


def _combine_call(lsrc, gdst, cnt, ld_t, w_t, x2, ys_rp):
    N, D = x2.shape
    nt = ld_t.shape[0]
    T = MIX_TILE
    smem = pl.BlockSpec(memory_space=pltpu.SMEM)
    smem_blk = pl.BlockSpec((1, 1, TILE_ROWS), lambda i: (i, 0, 0), memory_space=pltpu.SMEM)
    return pl.pallas_call(
        _combine_kernel,
        grid=(nt,),
        in_specs=[
            smem, smem, smem, smem_blk, smem_blk,
            pl.BlockSpec(memory_space=pl.ANY),
            pl.BlockSpec(memory_space=pl.ANY),
        ],
        out_specs=pl.BlockSpec(memory_space=pl.ANY),
        out_shape=jax.ShapeDtypeStruct((N, D), jnp.float32),
        scratch_shapes=[
            pltpu.VMEM((2 * TILE_ROWS * ROW_SUB, LANES), jnp.float32),
            pltpu.VMEM((COMBINE_ACC_SLOTS * T, ROW_SUB, LANES), jnp.float32),
            pltpu.SemaphoreType.DMA((2,)),
            pltpu.SemaphoreType.DMA((COMBINE_ACC_SLOTS,)),
            pltpu.SemaphoreType.DMA((COMBINE_ACC_SLOTS,)),
        ],
        compiler_params=pltpu.CompilerParams(
            dimension_semantics=("arbitrary",), vmem_limit_bytes=VMEM_LIMIT),
        name="moe_combine",
    )(lsrc, gdst, cnt, ld_t, w_t, x2, ys_rp)


def _work_items(gstart, totals, n_items):
    BM = EXPERT_BLOCK
    i32 = jnp.int32
    gend = gstart + totals
    first_blk = gstart // BM
    last_blk = jnp.maximum(gend - 1, gstart) // BM
    n_e = jnp.where(totals > 0, last_blk - first_blk + 1, 0)
    item_end = jnp.cumsum(n_e)
    item_start = item_end - n_e
    n_real = item_end[-1]
    j = jnp.minimum(jnp.arange(n_items, dtype=i32), n_real - 1)
    exp = jnp.sum((item_end[None, :] <= j[:, None]).astype(i32), axis=1)
    onehot = exp[:, None] == jnp.arange(N_EXPERTS, dtype=i32)[None, :]

    def of_exp(v):
        return jnp.sum(jnp.where(onehot, v[None, :], 0), axis=1)

    blk = of_exp(first_blk) + (j - of_exp(item_start))
    lo = jnp.maximum(of_exp(gstart), blk * BM) - blk * BM
    hi = jnp.minimum(of_exp(gend), (blk + 1) * BM) - blk * BM
    real = jnp.arange(n_items, dtype=i32) < n_real
    hi = jnp.where(real, hi, lo)
    prev_blk = jnp.concatenate([jnp.full((1,), -1, i32), blk[:-1]])
    next_blk = jnp.concatenate([blk[1:], jnp.full((1,), -1, i32)])
    is_last = jnp.arange(n_items, dtype=i32) == n_real - 1
    newblk = jnp.logical_and(real, blk != prev_blk).astype(i32)
    endblk = jnp.logical_and(real, jnp.logical_or(blk != next_blk, is_last)).astype(i32)
    return blk.astype(i32), exp.astype(i32), lo.astype(i32), hi.astype(i32), newblk, endblk


def _layer(x, mix_norm_w, w_in, conv_w, q_norm_w, k_norm_w, sinks, conv_out_norm_w,
           attn_out_norm_w, w_out, ffn_norm_w, w_router, router_bias, w_eg, w_eu, w_ed,
           w_sg, w_su, w_sd):
    B, S, D = x.shape
    N = B * S
    T = MIX_TILE
    nt = N // T
    bf16 = jnp.bfloat16
    f32 = jnp.float32
    i32 = jnp.int32

    wr_t = w_router.astype(f32).T
    wr_hi = wr_t.astype(bf16)
    wr_hl = jnp.concatenate([wr_hi, (wr_t - wr_hi.astype(f32)).astype(bf16)], axis=0)
    sink_col = jnp.broadcast_to(
        jnp.repeat(sinks.astype(f32), BLK).reshape(N_KV_HEADS, GQA_GROUP * BLK, 1),
        (N_KV_HEADS, GQA_GROUP * BLK, LANES))
    tri = jnp.asarray(np.triu(np.ones((T, T), np.float32), k=1), dtype=bf16)
    head_ones = jnp.asarray(np.kron(np.eye(2 * LANES // HEAD_DIM, dtype=np.float32),
                                    np.ones((HEAD_DIM, HEAD_DIM), np.float32)), dtype=bf16)
    bias_tab = jnp.asarray(_attn_bias_table())

    x2, xn, wts, ld, cnt, w_gu, w_d = _mixer_call(
        x, mix_norm_w.reshape(1, D), w_in.astype(bf16), conv_w,
        jnp.tile(q_norm_w, N_HEADS).reshape(1, D_ATTN), jnp.tile(k_norm_w, N_KV_HEADS).reshape(1, D_KV),
        sink_col, bias_tab, conv_out_norm_w.reshape(1, D_CONV), attn_out_norm_w.reshape(1, D_ATTN),
        w_out.astype(bf16), ffn_norm_w.reshape(1, D), w_sg.astype(bf16), w_su.astype(bf16),
        w_sd.astype(bf16), wr_hl, router_bias.astype(f32).reshape(N_EXPERTS, 1), tri, head_ones,
        w_eg.astype(f32), w_eu.astype(f32), w_ed.astype(f32))

    cnt_te = cnt[:, 0].reshape(nt, N_EXPERTS)
    lsrc = jnp.cumsum(cnt_te, axis=1) - cnt_te
    before = jnp.cumsum(cnt_te, axis=0) - cnt_te
    totals = jnp.sum(cnt_te, axis=0)
    gstart = jnp.cumsum(totals) - totals
    gdst = gstart[None, :] + before
    n_items = (N * TOP_K) // EXPERT_BLOCK + N_EXPERTS
    items = _work_items(gstart.astype(i32), totals.astype(i32), n_items)

    seg_tabs = (lsrc.reshape(-1).astype(i32), gdst.reshape(-1).astype(i32), cnt_te.reshape(-1).astype(i32))
    xs_rp = _dispatch_call(*seg_tabs, ld, xn)
    ys_rp = _expert_call(*items, xs_rp, w_gu, w_d)
    out = _combine_call(*seg_tabs, ld, wts, x2, ys_rp)
    return out.reshape(B, S, D)


def kernel(x, mix_norm_w, w_in, conv_w, q_norm_w, k_norm_w, sinks, conv_out_norm_w, attn_out_norm_w, w_out, ffn_norm_w, w_router, router_bias, w_exp_gate, w_exp_up, w_exp_down, w_sh_gate, w_sh_up, w_sh_down):
    B, S, D = x.shape
    assert D == D_MODEL and w_in.shape == (1, D_MODEL, D_IN_PROJ) and w_exp_gate.shape == (1, N_EXPERTS, D_MODEL, D_EXPERT)
    assert S % MIX_TILE == 0 and (B * S * TOP_K) % EXPERT_BLOCK == 0
    return _layer(x, mix_norm_w[0], w_in[0], conv_w[0], q_norm_w[0], k_norm_w[0], sinks[0],
                  conv_out_norm_w[0], attn_out_norm_w[0], w_out[0], ffn_norm_w[0], w_router[0],
                  router_bias[0], w_exp_gate[0], w_exp_up[0], w_exp_down[0], w_sh_gate[0],
                  w_sh_up[0], w_sh_down[0])
```
